```python
import jax, jax.numpy as jnp
from jax import lax
import numpy as np

D_MODEL = 1024
BATCH = 32
SEQ = 256
DEPTH = 2
DEC_BATCH = 2
DEC_SEQ = 1024
PAST_LEN = 256

GRID_W = 64
N_EVEN = (DEPTH + 1) // 2
N_ODD = DEPTH // 2
A_WIDTH = D_MODEL // 2
A_GROUPS = 4
A_GROUP_DIM = A_WIDTH // A_GROUPS
CHUNK = 128
B_HEADS = 4
B_DK = (D_MODEL // 2) // B_HEADS
B_DV = (D_MODEL // 2) // B_HEADS
RET_CHUNK = 128
ROPE_BASE = 10000.0
ROPE_PAIRS_AXIS = B_DK // 4
POOL_WINDOWS = (2, 4, 8, 16)
POOL_GROUPS = len(POOL_WINDOWS)
POOL_DIM = D_MODEL // POOL_GROUPS
D_FF = 4 * D_MODEL
EPS = 1e-6
QK_W = B_HEADS * B_DK
V_W = B_HEADS * B_DV
IN_SPLITS = (A_WIDTH, 2 * A_WIDTH, 2 * A_WIDTH + QK_W, 2 * A_WIDTH + 2 * QK_W,
             2 * A_WIDTH + 2 * QK_W + V_W, 2 * A_WIDTH + 2 * QK_W + 2 * V_W)
IN_WIDTH = 2 * A_WIDTH + 2 * QK_W + 3 * V_W
OUT_WIDTH = A_WIDTH + V_W

kernel_name = "hybrid_gmlp_retention_pool_diffusion_step"


def _rms_norm(x, gain):
    xf = x.astype(jnp.float32)
    y = xf * lax.rsqrt(jnp.mean(xf * xf, axis=-1, keepdims=True) + EPS)
    return (y * gain.astype(jnp.float32)).astype(x.dtype)


def _layer_norm(x):
    xf = x.astype(jnp.float32)
    mu = jnp.mean(xf, axis=-1, keepdims=True)
    var = jnp.mean(jnp.square(xf - mu), axis=-1, keepdims=True)
    return (xf - mu) * lax.rsqrt(var + EPS)


def _rope_tables(n_tokens):
    rows = n_tokens // GRID_W
    pos_r = jnp.repeat(jnp.arange(rows, dtype=jnp.float32), GRID_W)
    pos_c = jnp.tile(jnp.arange(GRID_W, dtype=jnp.float32), rows)
    inv = ROPE_BASE ** (-jnp.arange(ROPE_PAIRS_AXIS, dtype=jnp.float32) / ROPE_PAIRS_AXIS)
    ang = jnp.concatenate([pos_r[:, None] * inv, pos_c[:, None] * inv], axis=-1)
    return jnp.cos(ang), jnp.sin(ang)


def _apply_rope(x, cos, sin):
    half = B_DK // 2
    x1, x2 = x[..., :half], x[..., half:]
    return jnp.concatenate([x1 * cos - x2 * sin, x1 * sin + x2 * cos], axis=-1)


def _chunk_gmlp(u, v, w_s, b_s, vnorm_gain):
    bsz, length, _ = u.shape
    n = length // CHUNK
    ug = u.reshape(bsz, n, CHUNK, A_GROUPS, A_GROUP_DIM)
    vg = _layer_norm(v.reshape(bsz, n, CHUNK, A_GROUPS, A_GROUP_DIM)).astype(u.dtype) * vnorm_gain
    sp = jnp.einsum('gij,bnjgc->bnigc', w_s, vg) + b_s.T[:, :, None]
    return (ug * sp).reshape(bsz, length, A_WIDTH)


def _retention_dir(q, k, v, log_gamma, s0):
    bsz, nh, length, dk = q.shape
    n = length // RET_CHUNK
    qc = q.reshape(bsz, nh, n, RET_CHUNK, dk)
    kc = k.reshape(bsz, nh, n, RET_CHUNK, dk)
    vc = v.reshape(bsz, nh, n, RET_CHUNK, B_DV)
    idx = jnp.arange(RET_CHUNK, dtype=jnp.float32)
    diff = idx[:, None] - idx[None, :]
    mask = diff >= 0
    decay = jnp.where(mask, jnp.exp(log_gamma[:, None, None] * jnp.where(mask, diff, 0.0)), 0.0)
    scores = jnp.einsum('bhnid,bhnjd->bhnij', qc, kc) * decay[None, :, None]
    y_inner = jnp.einsum('bhnij,bhnje->bhnie', scores, vc)
    q_decay = jnp.exp(log_gamma[:, None] * (idx + 1.0))
    k_decay = jnp.exp(log_gamma[:, None] * (RET_CHUNK - 1.0 - idx))
    chunk_decay = jnp.exp(log_gamma * RET_CHUNK)[None, :, None, None]
    kv = jnp.einsum('bhnjd,hj,bhnje->nbhde', kc, k_decay, vc)

    def step(s, kv_c):
        return chunk_decay * s + kv_c, s

    s_final, s_prev = lax.scan(step, s0.astype(jnp.float32), kv)
    y_cross = jnp.einsum('bhnid,hi,nbhde->bhnie', qc, q_decay, s_prev)
    return (y_inner + y_cross).reshape(bsz, nh, length, B_DV), s_final


def _retention_bidir(q, k, v, decay_param, s0):
    log_g = -jnp.exp(decay_param.astype(jnp.float32))
    y_f, s_f = _retention_dir(q, k, v, log_g[0], s0[:, 0])
    y_b, s_b = _retention_dir(q[:, :, ::-1], k[:, :, ::-1], v[:, :, ::-1], log_g[1], s0[:, 1])
    return y_f, y_b[:, :, ::-1], jnp.stack([s_f, s_b], axis=1)


def _even_mixer(h, rope, s0, w_in, w_s, b_s, vnorm_gain, decay_param, w_out):
    bsz, length, _ = h.shape
    z = h @ w_in
    u, v, q, k, val, g_f, g_b = jnp.split(z, IN_SPLITS, axis=-1)
    a_out = _chunk_gmlp(jax.nn.gelu(u), jax.nn.gelu(v), w_s, b_s, vnorm_gain)

    def heads(t, d):
        return t.reshape(bsz, length, B_HEADS, d).transpose(0, 2, 1, 3).astype(jnp.float32)

    qh = heads(q, B_DK)
    kh = heads(k, B_DK) * (B_DK ** -0.5)
    if rope is not None:
        cos, sin = rope
        qh = _apply_rope(qh, cos, sin)
        kh = _apply_rope(kh, cos, sin)
    y_f, y_b, s_new = _retention_bidir(qh, kh, heads(val, B_DV), decay_param, s0)

    def merge(y):
        return _layer_norm(y).transpose(0, 2, 1, 3).reshape(bsz, length, V_W).astype(h.dtype)

    b_out = jax.nn.silu(g_f) * merge(y_f) + jax.nn.silu(g_b) * merge(y_b)
    return jnp.concatenate([a_out, b_out], axis=-1) @ w_out, s_new


def _pool_mixer(h, w_pool, pool_scale):
    bsz, length, _ = h.shape
    hg = h.reshape(bsz, length, POOL_GROUPS, POOL_DIM).astype(jnp.float32)
    csum = jnp.concatenate([jnp.zeros((bsz, 1, POOL_GROUPS, POOL_DIM), jnp.float32),
                            jnp.cumsum(hg, axis=1)], axis=1)
    t = jnp.arange(length)
    outs = []
    for g, w in enumerate(POOL_WINDOWS):
        lo = jnp.clip(t - w // 2, 0, length)
        hi = jnp.clip(t + w // 2, 0, length)
        mean = (csum[:, hi, g] - csum[:, lo, g]) / (hi - lo).astype(jnp.float32)[None, :, None]
        outs.append(mean - hg[:, :, g])
    pooled = jnp.stack(outs, axis=2).astype(h.dtype)
    y = jnp.einsum('blgc,gcd->blgd', pooled, w_pool).reshape(bsz, length, D_MODEL)
    return y * pool_scale


def _trunk(x, cond, rope, s_init, ada_w, ada_b, norm_mix, norm_mlp, mlp_w1, mlp_w2,
           ev_w_in, ev_w_s, ev_b_s, ev_vnorm, ev_decay, ev_w_out, od_w_pool, od_pool_scale, final_norm):
    states = []
    for i in range(DEPTH):
        mod = (jax.nn.silu(cond) @ ada_w[i] + ada_b[i])[:, None, :]
        sh1, sc1, g1, sh2, sc2, g2 = jnp.split(mod, 6, axis=-1)
        h = _rms_norm(x, norm_mix[i]) * (1.0 + sc1) + sh1
        if i % 2 == 0:
            e = i // 2
            out, s_new = _even_mixer(h, rope, s_init[:, e], ev_w_in[e], ev_w_s[e], ev_b_s[e],
                                     ev_vnorm[e], ev_decay[e], ev_w_out[e])
            states.append(s_new)
        else:
            o = i // 2
            out = _pool_mixer(h, od_w_pool[o], od_pool_scale[o])
        x = x + g1 * out
        h = _rms_norm(x, norm_mlp[i]) * (1.0 + sc2) + sh2
        x = x + g2 * (jnp.square(jax.nn.relu(h @ mlp_w1[i])) @ mlp_w2[i])
    return _rms_norm(x, final_norm), jnp.stack(states, axis=1)


def setup_inputs(seed: int = 0) -> dict:
    key = jax.random.key(seed)
    ks = jax.random.split(key, 24)
    f32 = jnp.float32
    nrm = lambda k, shape, s: jax.random.normal(k, shape, f32) * s
    gammas = 1.0 - 2.0 ** (-5.0 - jnp.arange(B_HEADS, dtype=f32))
    decay_base = jnp.log(-jnp.log(gammas))
    return {
        "x_prompt": nrm(ks[0], (BATCH, SEQ, D_MODEL), 1.0),
        "x_sample": nrm(ks[1], (DEC_BATCH, DEC_SEQ, D_MODEL), 1.0),
        "state_ret": nrm(ks[2], (DEC_BATCH, N_EVEN, 2, B_HEADS, B_DK, B_DV), 0.5),
        "c": nrm(ks[3], (DEC_BATCH, D_MODEL), 1.0),
        "c_ctx": nrm(ks[4], (D_MODEL,), 1.0),
        "ada_w": nrm(ks[5], (DEPTH, D_MODEL, 6 * D_MODEL), 0.5 * D_MODEL ** -0.5),
        "ada_b": nrm(ks[6], (DEPTH, 6 * D_MODEL), 0.01),
        "norm_mix": 1.0 + nrm(ks[7], (DEPTH, D_MODEL), 0.05),
        "norm_mlp": 1.0 + nrm(ks[8], (DEPTH, D_MODEL), 0.05),
        "mlp_w1": nrm(ks[9], (DEPTH, D_MODEL, D_FF), D_MODEL ** -0.5),
        "mlp_w2": nrm(ks[10], (DEPTH, D_FF, D_MODEL), D_FF ** -0.5),
        "ev_w_in": nrm(ks[11], (N_EVEN, D_MODEL, IN_WIDTH), D_MODEL ** -0.5),
        "ev_w_s": nrm(ks[12], (N_EVEN, A_GROUPS, CHUNK, CHUNK), CHUNK ** -0.5),
        "ev_b_s": 1.0 + nrm(ks[13], (N_EVEN, A_GROUPS, CHUNK), 0.1),
        "ev_vnorm": 1.0 + nrm(ks[14], (N_EVEN, A_GROUPS, A_GROUP_DIM), 0.05),
        "ev_decay": decay_base[None, None, :] + nrm(ks[15], (N_EVEN, 2, B_HEADS), 0.1),
        "ev_w_out": nrm(ks[16], (N_EVEN, OUT_WIDTH, D_MODEL), OUT_WIDTH ** -0.5),
        "od_w_pool": nrm(ks[17], (N_ODD, POOL_GROUPS, POOL_DIM, POOL_DIM), POOL_DIM ** -0.5),
        "od_pool_scale": 1.0 + nrm(ks[18], (N_ODD, D_MODEL), 0.1),
        "final_norm": 1.0 + nrm(ks[19], (D_MODEL,), 0.05),
    }


def reference(x_prompt, x_sample, state_ret, c, c_ctx, ada_w, ada_b, norm_mix, norm_mlp, mlp_w1, mlp_w2,
              ev_w_in, ev_w_s, ev_b_s, ev_vnorm, ev_decay, ev_w_out, od_w_pool, od_pool_scale, final_norm):
    weights = (ada_w, ada_b, norm_mix, norm_mlp, mlp_w1, mlp_w2, ev_w_in, ev_w_s, ev_b_s, ev_vnorm,
               ev_decay, ev_w_out, od_w_pool, od_pool_scale, final_norm)
    s_zero = jnp.zeros((x_prompt.shape[0], N_EVEN, 2, B_HEADS, B_DK, B_DV), jnp.float32)
    y_prompt, new_state_ret = _trunk(x_prompt, c_ctx[None, :], None, s_zero, *weights)
    rope = _rope_tables(x_sample.shape[1])
    y_sample, _ = _trunk(x_sample, c, rope, state_ret, *weights)
    return (y_prompt, y_sample, new_state_ret.astype(x_prompt.dtype))
```

```python
import functools

import jax
import jax.numpy as jnp
from jax import lax
from jax.experimental import pallas as pl
from jax.experimental.pallas import tpu as pltpu

D_MODEL = 1024
DEPTH = 2
GRID_W = 64
A_WIDTH = D_MODEL // 2
A_GROUPS = 4
A_GROUP_DIM = A_WIDTH // A_GROUPS
CHUNK = 128
HEADS = 4
HEAD_DIM = (D_MODEL // 2) // HEADS
ROPE_BASE = 10000.0
ROPE_PAIRS_AXIS = HEAD_DIM // 4
POOL_WINDOWS = (2, 4, 8, 16)
POOL_DIM = D_MODEL // len(POOL_WINDOWS)
D_FF = 4 * D_MODEL
EPS = 1e-6
QK_W = HEADS * HEAD_DIM
V_W = HEADS * HEAD_DIM
IN_WIDTH = 2 * A_WIDTH + 2 * QK_W + 3 * V_W
OUT_WIDTH = A_WIDTH + V_W
N_MOD = 6 * D_MODEL

FF_CHUNK = 1024
ADA_BLOCK_N = 1536
COND_ROWS = 8
V7X_VMEM_LIMIT_BYTES = 60 * 1024 * 1024

F32 = jnp.float32
BF16 = jnp.bfloat16


def _dot(a, b):
    return jnp.dot(a, b, preferred_element_type=F32)


def _dot_nt(a, b):
    return lax.dot_general(a, b, (((1,), (1,)), ((), ())), preferred_element_type=F32)


def _silu(x):
    return x * jax.nn.sigmoid(x)


def _gelu_tanh(x):
    return 0.5 * x * (1.0 + jnp.tanh(0.7978845608028654 * (x + 0.044715 * (x * x * x))))


def _rms_mod(x, gain, shift, scale):
    y = x * lax.rsqrt(jnp.mean(x * x, axis=-1, keepdims=True) + EPS)
    return (y * gain) * (1.0 + scale) + shift


def _layer_norm(x):
    mu = jnp.mean(x, axis=-1, keepdims=True)
    d = x - mu
    return d * lax.rsqrt(jnp.mean(d * d, axis=-1, keepdims=True) + EPS)


def _ada_kernel(cond_ref, w_ref, b_ref, out_ref):
    s = _silu(cond_ref[...]).astype(BF16)
    out_ref[0] = _dot(s, w_ref[0].astype(BF16)) + b_ref[0]


def _ada_call(cond, ada_w, ada_b):
    return pl.pallas_call(
        _ada_kernel,
        grid=(DEPTH, N_MOD // ADA_BLOCK_N),
        in_specs=[
            pl.BlockSpec((COND_ROWS, D_MODEL), lambda i, j: (0, 0)),
            pl.BlockSpec((1, D_MODEL, ADA_BLOCK_N), lambda i, j: (i, 0, j)),
            pl.BlockSpec((1, 1, ADA_BLOCK_N), lambda i, j: (i, 0, j)),
        ],
        out_specs=pl.BlockSpec((1, COND_ROWS, ADA_BLOCK_N), lambda i, j: (i, 0, j)),
        out_shape=jax.ShapeDtypeStruct((DEPTH, COND_ROWS, N_MOD), F32),
        compiler_params=pltpu.CompilerParams(
            dimension_semantics=("arbitrary", "arbitrary"),
            vmem_limit_bytes=V7X_VMEM_LIMIT_BYTES),
        name="ada_mod",
    )(cond, ada_w, ada_b.reshape(DEPTH, 1, N_MOD))


def _split_mod(mod):
    return [mod[:, i * D_MODEL:(i + 1) * D_MODEL] for i in range(6)]


def _mlp_residual(x, gain, shift, scale, gate, w1_ref, w2_ref):
    h = _rms_mod(x, gain, shift, scale).astype(BF16)
    acc = None
    for j in range(D_FF // FF_CHUNK):
        a = _dot(h, w1_ref[:, j * FF_CHUNK:(j + 1) * FF_CHUNK])
        a = jnp.square(jnp.maximum(a, 0.0)).astype(BF16)
        p = _dot(a, w2_ref[j * FF_CHUNK:(j + 1) * FF_CHUNK, :])
        acc = p if acc is None else acc + p
    return x + gate * acc


def _retention_tables(decay_row):
    return jnp.broadcast_to(-jnp.exp(decay_row), (CHUNK, CHUNK))


def _even_layer_kernel(*refs, n_seq, n_chunk, use_rope, has_state_in, emit_state):
    it = iter(refs)
    x_ref = next(it)
    mod_ref = next(it)
    s0_ref = next(it) if has_state_in else None
    cos_ref = next(it) if use_rope else None
    sin_ref = next(it) if use_rope else None
    nmix_ref, nmlp_ref = next(it), next(it)
    w_in_ref, w_s_ref, b_s_ref, vg_ref, dec_ref, w_out_ref = (next(it) for _ in range(6))
    w1_ref, w2_ref = next(it), next(it)
    y_ref = next(it)
    st_ref = next(it) if emit_state else None
    q_s, k_s, v_s, yf_s, yb_s, cat_s, kv_s, sp_s = (next(it) for _ in range(8))

    seq_len = n_chunk * CHUNK
    x = x_ref[...]
    sh1, sc1, g1, sh2, sc2, g2 = _split_mod(mod_ref[0])
    hb = _rms_mod(x, nmix_ref[...], sh1, sc1).astype(BF16)

    zu = _gelu_tanh(_dot(hb, w_in_ref[:, 0:A_WIDTH]))
    zv = _gelu_tanh(_dot(hb, w_in_ref[:, A_WIDTH:2 * A_WIDTH]))
    for g in range(A_GROUPS):
        lo = g * A_GROUP_DIM
        vg = (_layer_norm(zv[:, lo:lo + A_GROUP_DIM]) * vg_ref[:, lo:lo + A_GROUP_DIM]).astype(BF16)
        w_g = w_s_ref[g]
        b_g = b_s_ref[g]
        for c in range(n_seq * n_chunk):
            r0 = c * CHUNK
            sp = _dot(w_g, vg[r0:r0 + CHUNK, :]) + b_g
            cat_s[r0:r0 + CHUNK, lo:lo + A_GROUP_DIM] = (zu[r0:r0 + CHUNK, lo:lo + A_GROUP_DIM] * sp).astype(BF16)

    base = 2 * A_WIDTH
    zq = _dot(hb, w_in_ref[:, base:base + QK_W])
    zk = _dot(hb, w_in_ref[:, base + QK_W:base + 2 * QK_W]) * (HEAD_DIM ** -0.5)
    zval = _dot(hb, w_in_ref[:, base + 2 * QK_W:base + 2 * QK_W + V_W])
    if use_rope:
        for hd in range(HEADS):
            lo = hd * HEAD_DIM
            cos2, sin2 = cos_ref[...], sin_ref[...]
            qh = zq[:, lo:lo + HEAD_DIM]
            kh = zk[:, lo:lo + HEAD_DIM]
            q_s[:, lo:lo + HEAD_DIM] = (qh * cos2 + pltpu.roll(qh, HEAD_DIM // 2, axis=1) * sin2).astype(BF16)
            k_s[:, lo:lo + HEAD_DIM] = kh * cos2 + pltpu.roll(kh, HEAD_DIM // 2, axis=1) * sin2
    else:
        q_s[...] = zq.astype(BF16)
        k_s[...] = zk
    v_s[...] = zval.astype(BF16)

    row = lax.broadcasted_iota(jnp.int32, (CHUNK, CHUNK), 0).astype(F32)
    col = lax.broadcasted_iota(jnp.int32, (CHUNK, CHUNK), 1).astype(F32)
    for hd in range(HEADS):
        lo = hd * HEAD_DIM
        lg_f = _retention_tables(dec_ref[hd:hd + 1, :])
        lg_b = _retention_tables(dec_ref[HEADS + hd:HEADS + hd + 1, :])
        d_f = row - col
        d_b = col - row
        decay_f = jnp.where(d_f >= 0, jnp.exp(lg_f * jnp.maximum(d_f, 0.0)), 0.0)
        decay_b = jnp.where(d_b >= 0, jnp.exp(lg_b * jnp.maximum(d_b, 0.0)), 0.0)
        qd_f = jnp.exp(lg_f * (row + 1.0))
        qd_b = jnp.exp(lg_b * (CHUNK - row))
        kd_f = jnp.exp(lg_f * (CHUNK - 1.0 - row))
        kd_b = jnp.exp(lg_b * row)
        cd_f = jnp.exp(lg_f * CHUNK)
        cd_b = jnp.exp(lg_b * CHUNK)
        for s in range(n_seq):
            for c in range(n_chunk):
                r0 = s * seq_len + c * CHUNK
                kc = k_s[r0:r0 + CHUNK, lo:lo + HEAD_DIM]
                vc = v_s[r0:r0 + CHUNK, lo:lo + HEAD_DIM]
                kv_s[0, c] = _dot((kc * kd_f).T.astype(BF16), vc)
                kv_s[1, c] = _dot((kc * kd_b).T.astype(BF16), vc)
            if has_state_in:
                st_f = s0_ref[s, 0, 0, hd]
                st_b = s0_ref[s, 0, 1, hd]
            else:
                st_f = jnp.zeros((HEAD_DIM, HEAD_DIM), F32)
                st_b = jnp.zeros((HEAD_DIM, HEAD_DIM), F32)
            for c in range(n_chunk):
                sp_s[0, c] = st_f.astype(BF16)
                st_f = cd_f * st_f + kv_s[0, c]
            for c in reversed(range(n_chunk)):
                sp_s[1, c] = st_b.astype(BF16)
                st_b = cd_b * st_b + kv_s[1, c]
            if emit_state:
                st_ref[s, 0, 0, hd] = st_f
                st_ref[s, 0, 1, hd] = st_b
            for c in range(n_chunk):
                r0 = s * seq_len + c * CHUNK
                qc = q_s[r0:r0 + CHUNK, lo:lo + HEAD_DIM]
                kc = k_s[r0:r0 + CHUNK, lo:lo + HEAD_DIM].astype(BF16)
                vc = v_s[r0:r0 + CHUNK, lo:lo + HEAD_DIM]
                scores = _dot_nt(qc, kc)
                qf = qc.astype(F32)
                y_f = _dot((scores * decay_f).astype(BF16), vc) + _dot((qf * qd_f).astype(BF16), sp_s[0, c])
                y_b = _dot((scores * decay_b).astype(BF16), vc) + _dot((qf * qd_b).astype(BF16), sp_s[1, c])
                yf_s[r0:r0 + CHUNK, lo:lo + HEAD_DIM] = _layer_norm(y_f)
                yb_s[r0:r0 + CHUNK, lo:lo + HEAD_DIM] = _layer_norm(y_b)

    gbase = base + 2 * QK_W + V_W
    gate_f = _silu(_dot(hb, w_in_ref[:, gbase:gbase + V_W]))
    gate_b = _silu(_dot(hb, w_in_ref[:, gbase + V_W:gbase + 2 * V_W]))
    cat_s[:, A_WIDTH:OUT_WIDTH] = (gate_f * yf_s[...] + gate_b * yb_s[...]).astype(BF16)

    x = x + g1 * _dot(cat_s[...], w_out_ref[...])
    y_ref[...] = _mlp_residual(x, nmlp_ref[...], sh2, sc2, g2, w1_ref, w2_ref)


def _odd_layer_kernel(x_ref, mod_ref, nmix_ref, nmlp_ref, fin_ref, w_pool_ref, pscale_ref, w1_ref, w2_ref,
                      y_ref, *, seq_len):
    x = x_ref[...]
    n_rows = x.shape[0]
    sh1, sc1, g1, sh2, sc2, g2 = _split_mod(mod_ref[0])
    h = _rms_mod(x, nmix_ref[...], sh1, sc1)

    t = lax.broadcasted_iota(jnp.int32, (n_rows, 1), 0) & (seq_len - 1)

    def shift_down(a, k):
        return jnp.where(t >= k, pltpu.roll(a, k, axis=0), 0.0)

    def shift_up(a, k):
        return jnp.where(t < seq_len - k, pltpu.roll(a, n_rows - k, axis=0), 0.0)

    outs = []
    for g, w in enumerate(POOL_WINDOWS):
        hg = h[:, g * POOL_DIM:(g + 1) * POOL_DIM]
        half = w // 2
        left = shift_down(hg, 1)
        right = hg
        span = 1
        while span < half:
            left = left + shift_down(left, span)
            right = right + shift_up(right, span)
            span *= 2
        cnt = (jnp.minimum(t + half, seq_len) - jnp.maximum(t - half, 0)).astype(F32)
        pooled = ((left + right) / cnt - hg).astype(BF16)
        outs.append(_dot(pooled, w_pool_ref[g]))
    y = jnp.concatenate(outs, axis=-1) * pscale_ref[...]
    x = x + g1 * y
    x = _mlp_residual(x, nmlp_ref[...], sh2, sc2, g2, w1_ref, w2_ref)
    y_ref[...] = x * lax.rsqrt(jnp.mean(x * x, axis=-1, keepdims=True) + EPS) * fin_ref[...]


def _resident(shape):
    return pl.BlockSpec(shape, lambda i: (0,) * len(shape), pipeline_mode=pl.Buffered(1))


def _even_layer_call(x, mod, s0, rope, params, *, seq_len, n_seq, emit_state):
    batch = x.shape[0]
    tile = n_seq * seq_len
    n_chunk = seq_len // CHUNK
    x2 = x.reshape(batch * seq_len, D_MODEL)
    has_state_in = s0 is not None
    use_rope = rope is not None
    rows_per_mod = mod.shape[0]

    x_mode = pl.Buffered(1) if tile >= 1024 else None
    in_specs = [pl.BlockSpec((tile, D_MODEL), lambda i: (i, 0), pipeline_mode=x_mode),
                pl.BlockSpec((1, 1, N_MOD), (lambda i: (0, 0, 0)) if rows_per_mod == 1 else (lambda i: (i, 0, 0)))]
    args = [x2, mod]
    if has_state_in:
        in_specs.append(pl.BlockSpec((n_seq, 1, 2, HEADS, HEAD_DIM, HEAD_DIM), lambda i: (i, 0, 0, 0, 0, 0)))
        args.append(s0)
    if use_rope:
        in_specs += [_resident((seq_len, HEAD_DIM)), _resident((seq_len, HEAD_DIM))]
        args += list(rope)
    for p in params:
        in_specs.append(_resident(p.shape))
        args.append(p)

    out_shape = [jax.ShapeDtypeStruct((batch * seq_len, D_MODEL), F32)]
    out_specs = [pl.BlockSpec((tile, D_MODEL), lambda i: (i, 0))]
    if emit_state:
        out_shape.append(jax.ShapeDtypeStruct((batch, 1, 2, HEADS, HEAD_DIM, HEAD_DIM), F32))
        out_specs.append(pl.BlockSpec((n_seq, 1, 2, HEADS, HEAD_DIM, HEAD_DIM), lambda i: (i, 0, 0, 0, 0, 0)))

    scratch = [
        pltpu.VMEM((tile, QK_W), BF16),
        pltpu.VMEM((tile, QK_W), F32),
        pltpu.VMEM((tile, V_W), BF16),
        pltpu.VMEM((tile, V_W), F32),
        pltpu.VMEM((tile, V_W), F32),
        pltpu.VMEM((tile, OUT_WIDTH), BF16),
        pltpu.VMEM((2, n_chunk, HEAD_DIM, HEAD_DIM), F32),
        pltpu.VMEM((2, n_chunk, HEAD_DIM, HEAD_DIM), BF16),
    ]
    kern = functools.partial(_even_layer_kernel, n_seq=n_seq, n_chunk=n_chunk, use_rope=use_rope,
                             has_state_in=has_state_in, emit_state=emit_state)
    outs = pl.pallas_call(
        kern,
        grid=(batch // n_seq,),
        in_specs=in_specs,
        out_specs=out_specs,
        out_shape=out_shape,
        scratch_shapes=scratch,
        compiler_params=pltpu.CompilerParams(
            dimension_semantics=("arbitrary",),
            vmem_limit_bytes=V7X_VMEM_LIMIT_BYTES),
        name="even_layer_rope" if use_rope else "even_layer",
    )(*args)
    y = outs[0].reshape(batch, seq_len, D_MODEL)
    return (y, outs[1]) if emit_state else (y, None)


def _odd_layer_call(x, mod, params, *, seq_len, n_seq):
    batch = x.shape[0]
    tile = n_seq * seq_len
    x2 = x.reshape(batch * seq_len, D_MODEL)
    rows_per_mod = mod.shape[0]
    in_specs = [pl.BlockSpec((tile, D_MODEL), lambda i: (i, 0)),
                pl.BlockSpec((1, 1, N_MOD), (lambda i: (0, 0, 0)) if rows_per_mod == 1 else (lambda i: (i, 0, 0)))]
    in_specs += [_resident(p.shape) for p in params]
    out = pl.pallas_call(
        functools.partial(_odd_layer_kernel, seq_len=seq_len),
        grid=(batch // n_seq,),
        in_specs=in_specs,
        out_specs=pl.BlockSpec((tile, D_MODEL), lambda i: (i, 0)),
        out_shape=jax.ShapeDtypeStruct((batch * seq_len, D_MODEL), F32),
        compiler_params=pltpu.CompilerParams(
            dimension_semantics=("arbitrary",),
            vmem_limit_bytes=V7X_VMEM_LIMIT_BYTES),
        name="odd_layer",
    )(x2, mod, *params)
    return out.reshape(batch, seq_len, D_MODEL)


def _rope_tables(n_tokens):
    rows = n_tokens // GRID_W
    pos_r = jnp.repeat(jnp.arange(rows, dtype=F32), GRID_W)
    pos_c = jnp.tile(jnp.arange(GRID_W, dtype=F32), rows)
    inv = ROPE_BASE ** (-jnp.arange(ROPE_PAIRS_AXIS, dtype=F32) / ROPE_PAIRS_AXIS)
    ang = jnp.concatenate([pos_r[:, None] * inv, pos_c[:, None] * inv], axis=-1)
    cos, sin = jnp.cos(ang), jnp.sin(ang)
    return jnp.concatenate([cos, cos], axis=-1), jnp.concatenate([-sin, sin], axis=-1)


def kernel(x_prompt, x_sample, state_ret, c, c_ctx, ada_w, ada_b, norm_mix, norm_mlp, mlp_w1, mlp_w2,
           ev_w_in, ev_w_s, ev_b_s, ev_vnorm, ev_decay, ev_w_out, od_w_pool, od_pool_scale, final_norm):
    n_prompt, prompt_len, _ = x_prompt.shape
    n_sample, sample_len, _ = x_sample.shape
    assert DEPTH == 2 and n_sample + 1 <= COND_ROWS

    cond = jnp.zeros((COND_ROWS, D_MODEL), F32).at[0].set(c_ctx).at[1:1 + n_sample].set(c)
    mod = _ada_call(cond, ada_w, ada_b)
    mod_prompt = [mod[i, 0:1].reshape(1, 1, N_MOD) for i in range(DEPTH)]
    mod_sample = [mod[i, 1:1 + n_sample].reshape(n_sample, 1, N_MOD) for i in range(DEPTH)]

    row = lambda v: v.reshape(1, -1)
    even_params = (
        row(norm_mix[0]), row(norm_mlp[0]),
        ev_w_in[0].astype(BF16), ev_w_s[0].astype(BF16),
        jnp.broadcast_to(ev_b_s[0][:, :, None], (A_GROUPS, CHUNK, A_GROUP_DIM)),
        row(ev_vnorm[0]),
        jnp.broadcast_to(ev_decay[0].reshape(2 * HEADS, 1), (2 * HEADS, CHUNK)),
        ev_w_out[0].astype(BF16),
        mlp_w1[0].astype(BF16), mlp_w2[0].astype(BF16),
    )
    odd_params = (
        row(norm_mix[1]), row(norm_mlp[1]), row(final_norm),
        od_w_pool[0].astype(BF16), row(od_pool_scale[0]),
        mlp_w1[1].astype(BF16), mlp_w2[1].astype(BF16),
    )

    xp, new_state = _even_layer_call(x_prompt, mod_prompt[0], None, None, even_params,
                                     seq_len=prompt_len, n_seq=2, emit_state=True)
    y_prompt = _odd_layer_call(xp, mod_prompt[1], odd_params, seq_len=prompt_len, n_seq=4)

    xs, _ = _even_layer_call(x_sample, mod_sample[0], state_ret, _rope_tables(sample_len), even_params,
                             seq_len=sample_len, n_seq=1, emit_state=False)
    y_sample = _odd_layer_call(xs, mod_sample[1], odd_params, seq_len=sample_len, n_seq=1)
    return y_prompt, y_sample, new_state
```

```python
import functools

import jax
import jax.numpy as jnp
from jax import lax
from jax.experimental import pallas as pl
from jax.experimental.pallas import tpu as pltpu

D_MODEL = 1024
DEPTH = 2
GRID_W = 64
A_WIDTH = D_MODEL // 2
A_GROUPS = 4
A_GROUP_DIM = A_WIDTH // A_GROUPS
CHUNK = 128
HEADS = 4
HEAD_DIM = (D_MODEL // 2) // HEADS
ROPE_BASE = 10000.0
ROPE_PAIRS_AXIS = HEAD_DIM // 4
POOL_WINDOWS = (2, 4, 8, 16)
POOL_DIM = D_MODEL // len(POOL_WINDOWS)
D_FF = 4 * D_MODEL
EPS = 1e-6
QK_W = HEADS * HEAD_DIM
V_W = HEADS * HEAD_DIM
IN_WIDTH = 2 * A_WIDTH + 2 * QK_W + 3 * V_W
OUT_WIDTH = A_WIDTH + V_W
N_MOD = 6 * D_MODEL

FF_CHUNK = 1024
ADA_BLOCK_N = 1536
COND_ROWS = 8
V7X_VMEM_LIMIT_BYTES = 60 * 1024 * 1024

F32 = jnp.float32
BF16 = jnp.bfloat16


def _dot(a, b):
    return jnp.dot(a, b, preferred_element_type=F32)


def _dot_nt(a, b):
    return lax.dot_general(a, b, (((1,), (1,)), ((), ())), preferred_element_type=F32)


def _silu(x):
    return x * jax.nn.sigmoid(x)


def _gelu_tanh(x):
    return 0.5 * x * (1.0 + jnp.tanh(0.7978845608028654 * (x + 0.044715 * (x * x * x))))


def _rms_mod(x, gain, shift, scale):
    y = x * lax.rsqrt(jnp.mean(x * x, axis=-1, keepdims=True) + EPS)
    return (y * gain) * (1.0 + scale) + shift


def _layer_norm(x):
    mu = jnp.mean(x, axis=-1, keepdims=True)
    d = x - mu
    return d * lax.rsqrt(jnp.mean(d * d, axis=-1, keepdims=True) + EPS)


def _ada_kernel(cond_ref, w_ref, b_ref, out_ref):
    s = _silu(cond_ref[...]).astype(BF16)
    out_ref[0] = _dot(s, w_ref[0].astype(BF16)) + b_ref[0]


def _ada_call(cond, ada_w, ada_b):
    return pl.pallas_call(
        _ada_kernel,
        grid=(DEPTH, N_MOD // ADA_BLOCK_N),
        in_specs=[
            pl.BlockSpec((COND_ROWS, D_MODEL), lambda i, j: (0, 0)),
            pl.BlockSpec((1, D_MODEL, ADA_BLOCK_N), lambda i, j: (i, 0, j)),
            pl.BlockSpec((1, 1, ADA_BLOCK_N), lambda i, j: (i, 0, j)),
        ],
        out_specs=pl.BlockSpec((1, COND_ROWS, ADA_BLOCK_N), lambda i, j: (i, 0, j)),
        out_shape=jax.ShapeDtypeStruct((DEPTH, COND_ROWS, N_MOD), F32),
        compiler_params=pltpu.CompilerParams(
            dimension_semantics=("arbitrary", "arbitrary"),
            vmem_limit_bytes=V7X_VMEM_LIMIT_BYTES),
        name="ada_mod",
    )(cond, ada_w, ada_b.reshape(DEPTH, 1, N_MOD))


def _split_mod(mod):
    return [mod[:, i * D_MODEL:(i + 1) * D_MODEL] for i in range(6)]


def _mlp_residual(x, gain, shift, scale, gate, w1_ref, w2_ref, acc_ref):
    h = _rms_mod(x, gain, shift, scale).astype(BF16)
    acc_ref[...] = jnp.zeros_like(acc_ref)

    def step(j, carry):
        a = _dot(h, w1_ref[j])
        a = jnp.square(jnp.maximum(a, 0.0)).astype(BF16)
        acc_ref[...] += _dot(a, w2_ref[j])
        return carry

    lax.fori_loop(0, D_FF // FF_CHUNK, step, 0)
    return x + gate * acc_ref[...]


def _retention_tables(decay_row):
    return jnp.broadcast_to(-jnp.exp(decay_row), (CHUNK, CHUNK))


def _even_layer_kernel(*refs, n_seq, n_chunk, use_rope, has_state_in, emit_state):
    it = iter(refs)
    x_ref = next(it)
    mod_ref = next(it)
    s0_ref = next(it) if has_state_in else None
    cos_ref = next(it) if use_rope else None
    sin_ref = next(it) if use_rope else None
    nmix_ref, nmlp_ref = next(it), next(it)
    w_in_ref, w_s_ref, b_s_ref, vg_ref, dec_ref, w_out_ref = (next(it) for _ in range(6))
    w1_ref, w2_ref = next(it), next(it)
    y_ref = next(it)
    st_ref = next(it) if emit_state else None
    q_s, k_s, v_s, yf_s, yb_s, cat_s, kv_s, sp_s, acc_s = (next(it) for _ in range(9))

    seq_len = n_chunk * CHUNK
    x = x_ref[...]
    sh1, sc1, g1, sh2, sc2, g2 = _split_mod(mod_ref[0])
    hb = _rms_mod(x, nmix_ref[...], sh1, sc1).astype(BF16)

    zu = _gelu_tanh(_dot(hb, w_in_ref[:, 0:A_WIDTH]))
    zv = _gelu_tanh(_dot(hb, w_in_ref[:, A_WIDTH:2 * A_WIDTH]))
    for g in range(A_GROUPS):
        lo = g * A_GROUP_DIM
        vg = (_layer_norm(zv[:, lo:lo + A_GROUP_DIM]) * vg_ref[:, lo:lo + A_GROUP_DIM]).astype(BF16)
        w_g = w_s_ref[g]
        b_g = b_s_ref[g]
        for c in range(n_seq * n_chunk):
            r0 = c * CHUNK
            sp = _dot(w_g, vg[r0:r0 + CHUNK, :]) + b_g
            cat_s[r0:r0 + CHUNK, lo:lo + A_GROUP_DIM] = (zu[r0:r0 + CHUNK, lo:lo + A_GROUP_DIM] * sp).astype(BF16)

    base = 2 * A_WIDTH
    zq = _dot(hb, w_in_ref[:, base:base + QK_W])
    zk = _dot(hb, w_in_ref[:, base + QK_W:base + 2 * QK_W]) * (HEAD_DIM ** -0.5)
    zval = _dot(hb, w_in_ref[:, base + 2 * QK_W:base + 2 * QK_W + V_W])
    if use_rope:
        for hd in range(HEADS):
            lo = hd * HEAD_DIM
            cos2, sin2 = cos_ref[...], sin_ref[...]
            qh = zq[:, lo:lo + HEAD_DIM]
            kh = zk[:, lo:lo + HEAD_DIM]
            q_s[:, lo:lo + HEAD_DIM] = (qh * cos2 + pltpu.roll(qh, HEAD_DIM // 2, axis=1) * sin2).astype(BF16)
            k_s[:, lo:lo + HEAD_DIM] = kh * cos2 + pltpu.roll(kh, HEAD_DIM // 2, axis=1) * sin2
    else:
        q_s[...] = zq.astype(BF16)
        k_s[...] = zk
    v_s[...] = zval.astype(BF16)

    row = lax.broadcasted_iota(jnp.int32, (CHUNK, CHUNK), 0).astype(F32)
    col = lax.broadcasted_iota(jnp.int32, (CHUNK, CHUNK), 1).astype(F32)
    for hd in range(HEADS):
        lo = hd * HEAD_DIM
        lg_f = _retention_tables(dec_ref[hd:hd + 1, :])
        lg_b = _retention_tables(dec_ref[HEADS + hd:HEADS + hd + 1, :])
        d_f = row - col
        d_b = col - row
        decay_f = jnp.where(d_f >= 0, jnp.exp(lg_f * jnp.maximum(d_f, 0.0)), 0.0)
        decay_b = jnp.where(d_b >= 0, jnp.exp(lg_b * jnp.maximum(d_b, 0.0)), 0.0)
        qd_f = jnp.exp(lg_f * (row + 1.0))
        qd_b = jnp.exp(lg_b * (CHUNK - row))
        kd_f = jnp.exp(lg_f * (CHUNK - 1.0 - row))
        kd_b = jnp.exp(lg_b * row)
        cd_f = jnp.exp(lg_f * CHUNK)
        cd_b = jnp.exp(lg_b * CHUNK)
        for s in range(n_seq):
            for c in range(n_chunk):
                r0 = s * seq_len + c * CHUNK
                kc = k_s[r0:r0 + CHUNK, lo:lo + HEAD_DIM]
                vc = v_s[r0:r0 + CHUNK, lo:lo + HEAD_DIM]
                kv_s[0, c] = _dot((kc * kd_f).T.astype(BF16), vc)
                kv_s[1, c] = _dot((kc * kd_b).T.astype(BF16), vc)
            if has_state_in:
                st_f = s0_ref[s, 0, 0, hd]
                st_b = s0_ref[s, 0, 1, hd]
            else:
                st_f = jnp.zeros((HEAD_DIM, HEAD_DIM), F32)
                st_b = jnp.zeros((HEAD_DIM, HEAD_DIM), F32)
            for c in range(n_chunk):
                sp_s[0, c] = st_f.astype(BF16)
                st_f = cd_f * st_f + kv_s[0, c]
            for c in reversed(range(n_chunk)):
                sp_s[1, c] = st_b.astype(BF16)
                st_b = cd_b * st_b + kv_s[1, c]
            if emit_state:
                st_ref[s, 0, 0, hd] = st_f
                st_ref[s, 0, 1, hd] = st_b
            for c in range(n_chunk):
                r0 = s * seq_len + c * CHUNK
                qc = q_s[r0:r0 + CHUNK, lo:lo + HEAD_DIM]
                kc = k_s[r0:r0 + CHUNK, lo:lo + HEAD_DIM].astype(BF16)
                vc = v_s[r0:r0 + CHUNK, lo:lo + HEAD_DIM]
                scores = _dot_nt(qc, kc)
                qf = qc.astype(F32)
                y_f = _dot((scores * decay_f).astype(BF16), vc) + _dot((qf * qd_f).astype(BF16), sp_s[0, c])
                y_b = _dot((scores * decay_b).astype(BF16), vc) + _dot((qf * qd_b).astype(BF16), sp_s[1, c])
                yf_s[r0:r0 + CHUNK, lo:lo + HEAD_DIM] = _layer_norm(y_f)
                yb_s[r0:r0 + CHUNK, lo:lo + HEAD_DIM] = _layer_norm(y_b)

    gbase = base + 2 * QK_W + V_W
    gate_f = _silu(_dot(hb, w_in_ref[:, gbase:gbase + V_W]))
    gate_b = _silu(_dot(hb, w_in_ref[:, gbase + V_W:gbase + 2 * V_W]))
    cat_s[:, A_WIDTH:OUT_WIDTH] = (gate_f * yf_s[...] + gate_b * yb_s[...]).astype(BF16)

    x = x + g1 * _dot(cat_s[...], w_out_ref[...])
    y_ref[...] = _mlp_residual(x, nmlp_ref[...], sh2, sc2, g2, w1_ref, w2_ref, acc_s)


def _odd_layer_kernel(x_ref, mod_ref, nmix_ref, nmlp_ref, fin_ref, w_pool_ref, pscale_ref, w1_ref, w2_ref,
                      y_ref, acc_s, *, seq_len):
    x = x_ref[...]
    n_rows = x.shape[0]
    sh1, sc1, g1, sh2, sc2, g2 = _split_mod(mod_ref[0])
    h = _rms_mod(x, nmix_ref[...], sh1, sc1)

    t = lax.broadcasted_iota(jnp.int32, (n_rows, 1), 0) & (seq_len - 1)

    def shift_down(a, k):
        return jnp.where(t >= k, pltpu.roll(a, k, axis=0), 0.0)

    def shift_up(a, k):
        return jnp.where(t < seq_len - k, pltpu.roll(a, n_rows - k, axis=0), 0.0)

    outs = []
    for g, w in enumerate(POOL_WINDOWS):
        hg = h[:, g * POOL_DIM:(g + 1) * POOL_DIM]
        half = w // 2
        left = shift_down(hg, 1)
        right = hg
        span = 1
        while span < half:
            left = left + shift_down(left, span)
            right = right + shift_up(right, span)
            span *= 2
        cnt = (jnp.minimum(t + half, seq_len) - jnp.maximum(t - half, 0)).astype(F32)
        pooled = ((left + right) / cnt - hg).astype(BF16)
        outs.append(_dot(pooled, w_pool_ref[g]))
    y = jnp.concatenate(outs, axis=-1) * pscale_ref[...]
    x = x + g1 * y
    x = _mlp_residual(x, nmlp_ref[...], sh2, sc2, g2, w1_ref, w2_ref, acc_s)
    y_ref[...] = x * lax.rsqrt(jnp.mean(x * x, axis=-1, keepdims=True) + EPS) * fin_ref[...]


def _resident(shape):
    return pl.BlockSpec(shape, lambda i: (0,) * len(shape), pipeline_mode=pl.Buffered(1))


def _even_layer_call(x, mod, s0, rope, params, *, seq_len, n_seq, emit_state):
    batch = x.shape[0]
    tile = n_seq * seq_len
    n_chunk = seq_len // CHUNK
    x2 = x.reshape(batch * seq_len, D_MODEL)
    has_state_in = s0 is not None
    use_rope = rope is not None
    rows_per_mod = mod.shape[0]

    x_mode = pl.Buffered(1) if tile >= 1024 else None
    in_specs = [pl.BlockSpec((tile, D_MODEL), lambda i: (i, 0), pipeline_mode=x_mode),
                pl.BlockSpec((1, 1, N_MOD), (lambda i: (0, 0, 0)) if rows_per_mod == 1 else (lambda i: (i, 0, 0)))]
    args = [x2, mod]
    if has_state_in:
        in_specs.append(pl.BlockSpec((n_seq, 1, 2, HEADS, HEAD_DIM, HEAD_DIM), lambda i: (i, 0, 0, 0, 0, 0)))
        args.append(s0)
    if use_rope:
        in_specs += [_resident((seq_len, HEAD_DIM)), _resident((seq_len, HEAD_DIM))]
        args += list(rope)
    for p in params:
        in_specs.append(_resident(p.shape))
        args.append(p)

    out_shape = [jax.ShapeDtypeStruct((batch * seq_len, D_MODEL), F32)]
    out_specs = [pl.BlockSpec((tile, D_MODEL), lambda i: (i, 0), pipeline_mode=x_mode)]
    if emit_state:
        out_shape.append(jax.ShapeDtypeStruct((batch, 1, 2, HEADS, HEAD_DIM, HEAD_DIM), F32))
        out_specs.append(pl.BlockSpec((n_seq, 1, 2, HEADS, HEAD_DIM, HEAD_DIM), lambda i: (i, 0, 0, 0, 0, 0)))

    scratch = [
        pltpu.VMEM((tile, QK_W), BF16),
        pltpu.VMEM((tile, QK_W), F32),
        pltpu.VMEM((tile, V_W), BF16),
        pltpu.VMEM((tile, V_W), F32),
        pltpu.VMEM((tile, V_W), F32),
        pltpu.VMEM((tile, OUT_WIDTH), BF16),
        pltpu.VMEM((2, n_chunk, HEAD_DIM, HEAD_DIM), F32),
        pltpu.VMEM((2, n_chunk, HEAD_DIM, HEAD_DIM), BF16),
        pltpu.VMEM((tile, D_MODEL), F32),
    ]
    kern = functools.partial(_even_layer_kernel, n_seq=n_seq, n_chunk=n_chunk, use_rope=use_rope,
                             has_state_in=has_state_in, emit_state=emit_state)
    outs = pl.pallas_call(
        kern,
        grid=(batch // n_seq,),
        in_specs=in_specs,
        out_specs=out_specs,
        out_shape=out_shape,
        scratch_shapes=scratch,
        compiler_params=pltpu.CompilerParams(
            dimension_semantics=("arbitrary",),
            vmem_limit_bytes=V7X_VMEM_LIMIT_BYTES),
        name="even_layer_rope" if use_rope else "even_layer",
    )(*args)
    y = outs[0].reshape(batch, seq_len, D_MODEL)
    return (y, outs[1]) if emit_state else (y, None)


def _odd_layer_call(x, mod, params, *, seq_len, n_seq):
    batch = x.shape[0]
    tile = n_seq * seq_len
    x2 = x.reshape(batch * seq_len, D_MODEL)
    rows_per_mod = mod.shape[0]
    in_specs = [pl.BlockSpec((tile, D_MODEL), lambda i: (i, 0)),
                pl.BlockSpec((1, 1, N_MOD), (lambda i: (0, 0, 0)) if rows_per_mod == 1 else (lambda i: (i, 0, 0)))]
    in_specs += [_resident(p.shape) for p in params]
    out = pl.pallas_call(
        functools.partial(_odd_layer_kernel, seq_len=seq_len),
        grid=(batch // n_seq,),
        in_specs=in_specs,
        out_specs=pl.BlockSpec((tile, D_MODEL), lambda i: (i, 0)),
        out_shape=jax.ShapeDtypeStruct((batch * seq_len, D_MODEL), F32),
        scratch_shapes=[pltpu.VMEM((tile, D_MODEL), F32)],
        compiler_params=pltpu.CompilerParams(
            dimension_semantics=("arbitrary",),
            vmem_limit_bytes=V7X_VMEM_LIMIT_BYTES),
        name="odd_layer",
    )(x2, mod, *params)
    return out.reshape(batch, seq_len, D_MODEL)


def _rope_tables(n_tokens):
    rows = n_tokens // GRID_W
    pos_r = jnp.repeat(jnp.arange(rows, dtype=F32), GRID_W)
    pos_c = jnp.tile(jnp.arange(GRID_W, dtype=F32), rows)
    inv = ROPE_BASE ** (-jnp.arange(ROPE_PAIRS_AXIS, dtype=F32) / ROPE_PAIRS_AXIS)
    ang = jnp.concatenate([pos_r[:, None] * inv, pos_c[:, None] * inv], axis=-1)
    cos, sin = jnp.cos(ang), jnp.sin(ang)
    return jnp.concatenate([cos, cos], axis=-1), jnp.concatenate([-sin, sin], axis=-1)


def kernel(x_prompt, x_sample, state_ret, c, c_ctx, ada_w, ada_b, norm_mix, norm_mlp, mlp_w1, mlp_w2,
           ev_w_in, ev_w_s, ev_b_s, ev_vnorm, ev_decay, ev_w_out, od_w_pool, od_pool_scale, final_norm):
    n_prompt, prompt_len, _ = x_prompt.shape
    n_sample, sample_len, _ = x_sample.shape
    assert DEPTH == 2 and n_sample + 1 <= COND_ROWS

    cond = jnp.zeros((COND_ROWS, D_MODEL), F32).at[0].set(c_ctx).at[1:1 + n_sample].set(c)
    mod = _ada_call(cond, ada_w, ada_b)
    mod_prompt = [mod[i, 0:1].reshape(1, 1, N_MOD) for i in range(DEPTH)]
    mod_sample = [mod[i, 1:1 + n_sample].reshape(n_sample, 1, N_MOD) for i in range(DEPTH)]

    row = lambda v: v.reshape(1, -1)
    n_ff = D_FF // FF_CHUNK
    w1_slabs = lambda w: w.astype(BF16).reshape(D_MODEL, n_ff, FF_CHUNK).transpose(1, 0, 2)
    w2_slabs = lambda w: w.astype(BF16).reshape(n_ff, FF_CHUNK, D_MODEL)
    even_params = (
        row(norm_mix[0]), row(norm_mlp[0]),
        ev_w_in[0].astype(BF16), ev_w_s[0].astype(BF16),
        jnp.broadcast_to(ev_b_s[0][:, :, None], (A_GROUPS, CHUNK, A_GROUP_DIM)),
        row(ev_vnorm[0]),
        jnp.broadcast_to(ev_decay[0].reshape(2 * HEADS, 1), (2 * HEADS, CHUNK)),
        ev_w_out[0].astype(BF16),
        w1_slabs(mlp_w1[0]), w2_slabs(mlp_w2[0]),
    )
    odd_params = (
        row(norm_mix[1]), row(norm_mlp[1]), row(final_norm),
        od_w_pool[0].astype(BF16), row(od_pool_scale[0]),
        w1_slabs(mlp_w1[1]), w2_slabs(mlp_w2[1]),
    )

    xp, new_state = _even_layer_call(x_prompt, mod_prompt[0], None, None, even_params,
                                     seq_len=prompt_len, n_seq=2, emit_state=True)
    y_prompt = _odd_layer_call(xp, mod_prompt[1], odd_params, seq_len=prompt_len, n_seq=4)

    xs, _ = _even_layer_call(x_sample, mod_sample[0], state_ret, _rope_tables(sample_len), even_params,
                             seq_len=sample_len, n_seq=1, emit_state=False)
    y_sample = _odd_layer_call(xs, mod_sample[1], odd_params, seq_len=sample_len, n_seq=1)
    return y_prompt, y_sample, new_state
```

```python
import functools

import jax
import jax.numpy as jnp
from jax import lax
from jax.experimental import pallas as pl
from jax.experimental.pallas import tpu as pltpu

D_MODEL = 1024
DEPTH = 2
GRID_W = 64
A_WIDTH = D_MODEL // 2
A_GROUPS = 4
A_GROUP_DIM = A_WIDTH // A_GROUPS
CHUNK = 128
HEADS = 4
HEAD_DIM = (D_MODEL // 2) // HEADS
ROPE_BASE = 10000.0
ROPE_PAIRS_AXIS = HEAD_DIM // 4
POOL_WINDOWS = (2, 4, 8, 16)
POOL_DIM = D_MODEL // len(POOL_WINDOWS)
POOL_HALO = max(POOL_WINDOWS) // 2
D_FF = 4 * D_MODEL
EPS = 1e-6
QK_W = HEADS * HEAD_DIM
V_W = HEADS * HEAD_DIM
IN_WIDTH = 2 * A_WIDTH + 2 * QK_W + 3 * V_W
OUT_WIDTH = A_WIDTH + V_W
N_MOD = 6 * D_MODEL
K_COL = 2 * A_WIDTH + QK_W
V_COL = K_COL + QK_W

UNIT = 2 * CHUNK
UNIT_CHUNKS = UNIT // CHUNK
FF_CHUNK = 1024
ADA_BLOCK_N = 1536
COND_ROWS = 8
SUBLANES = 8
V7X_VMEM_LIMIT_BYTES = 60 * 1024 * 1024

F32 = jnp.float32
BF16 = jnp.bfloat16


def _dot(a, b):
    return jnp.dot(a, b, preferred_element_type=F32)


def _dot_nt(a, b):
    return lax.dot_general(a, b, (((1,), (1,)), ((), ())), preferred_element_type=F32)


def _silu(x):
    return x * jax.nn.sigmoid(x)


def _gelu_tanh(x):
    return 0.5 * x * (1.0 + jnp.tanh(0.7978845608028654 * (x + 0.044715 * (x * x * x))))


def _rms(x):
    return x * lax.rsqrt(jnp.mean(x * x, axis=-1, keepdims=True) + EPS)


def _rms_mod(x, gain, shift, scale):
    return (_rms(x) * gain) * (1.0 + scale) + shift


def _layer_norm(x):
    mu = jnp.mean(x, axis=-1, keepdims=True)
    d = x - mu
    return d * lax.rsqrt(jnp.mean(d * d, axis=-1, keepdims=True) + EPS)


def _split_mod(mod):
    return [mod[:, i * D_MODEL:(i + 1) * D_MODEL] for i in range(6)]


def _rope(x, cos2, sin2):
    return x * cos2 + pltpu.roll(x, HEAD_DIM // 2, axis=1) * sin2


def _interleave(*gens):
    live = list(gens)
    while live:
        for g in list(live):
            try:
                next(g)
            except StopIteration:
                live.remove(g)


def _pipeline_steps(step, n_units, stage_first, stage_mid, stage_last):
    @pl.when(step == 0)
    def _():
        _interleave(stage_first())

    @pl.when(step == 1)
    def _():
        _interleave(stage_first(), stage_mid())

    @pl.when((step >= 2) & (step < n_units))
    def _():
        _interleave(stage_last(), stage_first(), stage_mid())

    @pl.when(step == n_units)
    def _():
        _interleave(stage_last(), stage_mid())

    @pl.when(step == n_units + 1)
    def _():
        _interleave(stage_last())


def _mod_row_map(mod, units_per_seq):
    if mod.shape[0] == 1:
        return lambda unit: 0
    return lambda unit: unit // units_per_seq


def _resident(shape):
    return pl.BlockSpec(shape, lambda i: (0,) * len(shape), pipeline_mode=pl.Buffered(1))


def _ada_kernel(cond_ref, w_ref, b_ref, out_ref):
    s = _silu(cond_ref[...]).astype(BF16)
    out_ref[0] = _dot(s, w_ref[0].astype(BF16)) + b_ref[0]


def _ada_call(cond, ada_w, ada_b):
    return pl.pallas_call(
        _ada_kernel,
        grid=(DEPTH, N_MOD // ADA_BLOCK_N),
        in_specs=[
            pl.BlockSpec((COND_ROWS, D_MODEL), lambda i, j: (0, 0)),
            pl.BlockSpec((1, D_MODEL, ADA_BLOCK_N), lambda i, j: (i, 0, j)),
            pl.BlockSpec((1, 1, ADA_BLOCK_N), lambda i, j: (i, 0, j)),
        ],
        out_specs=pl.BlockSpec((1, COND_ROWS, ADA_BLOCK_N), lambda i, j: (i, 0, j)),
        out_shape=jax.ShapeDtypeStruct((DEPTH, COND_ROWS, N_MOD), F32),
        compiler_params=pltpu.CompilerParams(
            dimension_semantics=("arbitrary", "arbitrary"),
            vmem_limit_bytes=V7X_VMEM_LIMIT_BYTES),
        name="ada_mod",
    )(cond, ada_w, ada_b.reshape(DEPTH, 1, N_MOD))


def _decay_tables(dec_ref, hd):
    row = lax.broadcasted_iota(jnp.int32, (CHUNK, CHUNK), 0).astype(F32)
    col = lax.broadcasted_iota(jnp.int32, (CHUNK, CHUNK), 1).astype(F32)
    lg_f = jnp.broadcast_to(-jnp.exp(dec_ref[hd:hd + 1, :]), (CHUNK, CHUNK))
    lg_b = jnp.broadcast_to(-jnp.exp(dec_ref[HEADS + hd:HEADS + hd + 1, :]), (CHUNK, CHUNK))
    d_f = row - col
    d_b = col - row
    return dict(
        decay_f=jnp.where(d_f >= 0, jnp.exp(lg_f * jnp.maximum(d_f, 0.0)), 0.0),
        decay_b=jnp.where(d_b >= 0, jnp.exp(lg_b * jnp.maximum(d_b, 0.0)), 0.0),
        qd_f=jnp.exp(lg_f * (row + 1.0)),
        qd_b=jnp.exp(lg_b * (CHUNK - row)),
        kd_f=jnp.exp(lg_f * (CHUNK - 1.0 - row)),
        kd_b=jnp.exp(lg_b * row),
        cd_f=jnp.exp(lg_f * CHUNK),
        cd_b=jnp.exp(lg_b * CHUNK),
    )


TABLE_NAMES = ("decay_f", "decay_b", "qd_f", "qd_b", "kd_f", "kd_b", "cd_f", "cd_b")


def _ret_state_kernel(x_ref, mod_ref, s0_ref, cos_ref, sin_ref, nmix_ref, wk_ref, wv_ref, dec_ref, sp_ref,
                      *, n_chunk):
    sh1, sc1 = _split_mod(mod_ref[0])[:2]
    hb = _rms_mod(x_ref[...], nmix_ref[...], sh1, sc1).astype(BF16)
    zk = _dot(hb, wk_ref[...]) * (HEAD_DIM ** -0.5)
    zv = _dot(hb, wv_ref[...]).astype(BF16)
    cos2, sin2 = cos_ref[...], sin_ref[...]
    for hd in range(HEADS):
        lo = hd * HEAD_DIM
        tb = _decay_tables(dec_ref, hd)
        kh = _rope(zk[:, lo:lo + HEAD_DIM], cos2, sin2)
        kv_f, kv_b = [], []
        for c in range(n_chunk):
            kc = kh[c * CHUNK:(c + 1) * CHUNK]
            vc = zv[c * CHUNK:(c + 1) * CHUNK, lo:lo + HEAD_DIM]
            kv_f.append(_dot((kc * tb["kd_f"]).T.astype(BF16), vc))
            kv_b.append(_dot((kc * tb["kd_b"]).T.astype(BF16), vc))
        st = s0_ref[0, 0, 0, hd]
        for c in range(n_chunk):
            sp_ref[c // UNIT_CHUNKS, c % UNIT_CHUNKS, 0, hd] = st.astype(BF16)
            st = tb["cd_f"] * st + kv_f[c]
        st = s0_ref[0, 0, 1, hd]
        for c in reversed(range(n_chunk)):
            sp_ref[c // UNIT_CHUNKS, c % UNIT_CHUNKS, 1, hd] = st.astype(BF16)
            st = tb["cd_b"] * st + kv_b[c]


def _ret_state_call(x2, mod, s0, rope, nmix, w_in, dec, *, n_seq, seq_len):
    n_chunk = seq_len // CHUNK
    units = seq_len // UNIT
    return pl.pallas_call(
        functools.partial(_ret_state_kernel, n_chunk=n_chunk),
        grid=(n_seq,),
        in_specs=[
            pl.BlockSpec((seq_len, D_MODEL), lambda i: (i, 0)),
            pl.BlockSpec((1, 1, N_MOD), lambda i: (i, 0, 0)),
            pl.BlockSpec((1, 1, 2, HEADS, HEAD_DIM, HEAD_DIM), lambda i: (i, 0, 0, 0, 0, 0)),
            _resident((seq_len, HEAD_DIM)), _resident((seq_len, HEAD_DIM)),
            _resident((1, D_MODEL)),
            pl.BlockSpec((D_MODEL, QK_W), lambda i: (0, K_COL // QK_W), pipeline_mode=pl.Buffered(1)),
            pl.BlockSpec((D_MODEL, V_W), lambda i: (0, V_COL // V_W), pipeline_mode=pl.Buffered(1)),
            _resident(dec.shape),
        ],
        out_specs=pl.BlockSpec((units, UNIT_CHUNKS, 2, HEADS, HEAD_DIM, HEAD_DIM), lambda i: (i, 0, 0, 0, 0, 0)),
        out_shape=jax.ShapeDtypeStruct((n_seq * units, UNIT_CHUNKS, 2, HEADS, HEAD_DIM, HEAD_DIM), BF16),
        compiler_params=pltpu.CompilerParams(
            dimension_semantics=("arbitrary",),
            vmem_limit_bytes=V7X_VMEM_LIMIT_BYTES),
        name="ret_state",
    )(x2, mod, s0, rope[0], rope[1], nmix, w_in, w_in, dec)


def _even_kernel(*refs, n_units, use_rope, scanned_states):
    it = iter(refs)
    xa_ref, xc_ref, moda_ref, modc_ref = (next(it) for _ in range(4))
    sp_ref = next(it) if scanned_states else None
    cos_ref = next(it) if use_rope else None
    sin_ref = next(it) if use_rope else None
    nmix_ref, nmlp_ref = next(it), next(it)
    w_in_ref, w_s_ref, b_s_ref, vgain_ref, dec_ref, w_out_ref, w1_ref, w2_ref = (next(it) for _ in range(8))
    y_ref = next(it)
    st_ref = None if scanned_states else next(it)
    ug_s, vg_s, q_s, k_s, v_s, gf_s, gb_s, cat_s, tab_s = (next(it) for _ in range(9))

    step = pl.program_id(0)
    slot_a = lax.rem(step, 2)
    slot_b = 1 - slot_a

    @pl.when(step == 0)
    def _():
        for hd in range(HEADS):
            tb = _decay_tables(dec_ref, hd)
            for n, name in enumerate(TABLE_NAMES):
                tab_s[hd, n] = tb[name]

    def in_proj(col, width):
        return w_in_ref[:, col:col + width]

    def stage_in():
        sh1, sc1 = _split_mod(moda_ref[0])[:2]
        hb = _rms_mod(xa_ref[...], nmix_ref[...], sh1, sc1).astype(BF16)
        yield
        ug_s[slot_a] = _gelu_tanh(_dot(hb, in_proj(0, A_WIDTH)))
        yield
        zv = _gelu_tanh(_dot(hb, in_proj(A_WIDTH, A_WIDTH)))
        vgn = [_layer_norm(zv[:, g * A_GROUP_DIM:(g + 1) * A_GROUP_DIM]) for g in range(A_GROUPS)]
        vg_s[slot_a] = (jnp.concatenate(vgn, axis=-1) * vgain_ref[...]).astype(BF16)
        yield
        zq = _dot(hb, in_proj(2 * A_WIDTH, QK_W))
        zk = _dot(hb, in_proj(K_COL, QK_W)) * (HEAD_DIM ** -0.5)
        if use_rope:
            cos2, sin2 = cos_ref[...], sin_ref[...]
            zq = jnp.concatenate([_rope(zq[:, h * HEAD_DIM:(h + 1) * HEAD_DIM], cos2, sin2) for h in range(HEADS)], -1)
            yield
            zk = jnp.concatenate([_rope(zk[:, h * HEAD_DIM:(h + 1) * HEAD_DIM], cos2, sin2) for h in range(HEADS)], -1)
        q_s[slot_a] = zq.astype(BF16)
        k_s[slot_a] = zk
        yield
        v_s[slot_a] = _dot(hb, in_proj(V_COL, V_W)).astype(BF16)
        yield
        gf_s[slot_a] = _silu(_dot(hb, in_proj(V_COL + V_W, V_W)))
        yield
        gb_s[slot_a] = _silu(_dot(hb, in_proj(V_COL + 2 * V_W, V_W)))

    def stage_mix():
        s = slot_b
        for g in range(A_GROUPS):
            lo = g * A_GROUP_DIM
            vg2 = jnp.concatenate([vg_s[s, c * CHUNK:(c + 1) * CHUNK, lo:lo + A_GROUP_DIM]
                                   for c in range(UNIT_CHUNKS)], axis=-1)
            sp2 = _dot(w_s_ref[g], vg2)
            for c in range(UNIT_CHUNKS):
                rows = slice(c * CHUNK, (c + 1) * CHUNK)
                gate = sp2[:, c * CHUNK:(c + 1) * CHUNK] + b_s_ref[g]
                cat_s[s, rows, lo:lo + A_GROUP_DIM] = (ug_s[s, rows, lo:lo + A_GROUP_DIM] * gate).astype(BF16)
            yield
        for hd in range(HEADS):
            lo = hd * HEAD_DIM
            tb = {name: tab_s[hd, n] for n, name in enumerate(TABLE_NAMES)}
            rows = [slice(c * CHUNK, (c + 1) * CHUNK) for c in range(UNIT_CHUNKS)]
            qc = [q_s[s, r, lo:lo + HEAD_DIM] for r in rows]
            kc = [k_s[s, r, lo:lo + HEAD_DIM] for r in rows]
            vc = [v_s[s, r, lo:lo + HEAD_DIM] for r in rows]
            if scanned_states:
                enter_f = [sp_ref[0, c, 0, hd] for c in range(UNIT_CHUNKS)]
                enter_b = [sp_ref[0, c, 1, hd] for c in range(UNIT_CHUNKS)]
            else:
                kv_f = [_dot((kc[c] * tb["kd_f"]).T.astype(BF16), vc[c]) for c in range(UNIT_CHUNKS)]
                kv_b = [_dot((kc[c] * tb["kd_b"]).T.astype(BF16), vc[c]) for c in range(UNIT_CHUNKS)]
                enter_f, enter_b = [None], [None] * UNIT_CHUNKS
                st = kv_f[0]
                for c in range(1, UNIT_CHUNKS):
                    enter_f.append(st.astype(BF16))
                    st = tb["cd_f"] * st + kv_f[c]
                st_ref[0, 0, 0, hd] = st
                st = kv_b[UNIT_CHUNKS - 1]
                for c in reversed(range(UNIT_CHUNKS - 1)):
                    enter_b[c] = st.astype(BF16)
                    st = tb["cd_b"] * st + kv_b[c]
                st_ref[0, 0, 1, hd] = st
                yield
            for c in range(UNIT_CHUNKS):
                scores = _dot_nt(qc[c], kc[c].astype(BF16))
                qf = qc[c].astype(F32)

                def direction(decay, qd, enter):
                    p = (scores * decay).astype(BF16)
                    if enter is None:
                        return _layer_norm(_dot(p, vc[c]))
                    lhs = jnp.concatenate([p, (qf * qd).astype(BF16)], axis=-1)
                    return _layer_norm(_dot(lhs, jnp.concatenate([vc[c], enter], axis=0)))

                y_f = direction(tb["decay_f"], tb["qd_f"], enter_f[c])
                y_b = direction(tb["decay_b"], tb["qd_b"], enter_b[c])
                cat_s[s, rows[c], A_WIDTH + lo:A_WIDTH + lo + HEAD_DIM] = (
                    gf_s[s, rows[c], lo:lo + HEAD_DIM] * y_f + gb_s[s, rows[c], lo:lo + HEAD_DIM] * y_b).astype(BF16)
                yield

    def stage_mlp():
        _, _, g1, sh2, sc2, g2 = _split_mod(modc_ref[0])
        x = xc_ref[...] + g1 * _dot(cat_s[slot_a], w_out_ref[...])
        hb = _rms_mod(x, nmlp_ref[...], sh2, sc2).astype(BF16)
        yield
        acc = None
        for j in range(D_FF // FF_CHUNK):
            a = _dot(hb, w1_ref[:, j * FF_CHUNK:(j + 1) * FF_CHUNK])
            yield
            p = _dot(jnp.square(jnp.maximum(a, 0.0)).astype(BF16), w2_ref[j * FF_CHUNK:(j + 1) * FF_CHUNK, :])
            acc = p if acc is None else acc + p
            yield
        y_ref[...] = x + g2 * acc

    _pipeline_steps(step, n_units, stage_in, stage_mix, stage_mlp)


def _even_call(x2, mod, sp, rope, params, *, seq_len):
    n_units = x2.shape[0] // UNIT
    units_per_seq = seq_len // UNIT
    scanned = sp is not None
    use_rope = rope is not None
    assert scanned or units_per_seq == 1
    last = n_units - 1
    unit_a = lambda i: jnp.minimum(i, last)
    unit_b = lambda i: jnp.clip(i - 1, 0, last)
    unit_c = lambda i: jnp.clip(i - 2, 0, last)
    mod_row = _mod_row_map(mod, units_per_seq)
    state_block = (1, UNIT_CHUNKS if scanned else 1, 2, HEADS, HEAD_DIM, HEAD_DIM)

    in_specs = [pl.BlockSpec((UNIT, D_MODEL), lambda i: (unit_a(i), 0)),
                pl.BlockSpec((UNIT, D_MODEL), lambda i: (unit_c(i), 0)),
                pl.BlockSpec((1, 1, N_MOD), lambda i: (mod_row(unit_a(i)), 0, 0)),
                pl.BlockSpec((1, 1, N_MOD), lambda i: (mod_row(unit_c(i)), 0, 0))]
    args = [x2, x2, mod, mod]
    if scanned:
        in_specs.append(pl.BlockSpec(state_block, lambda i: (unit_b(i), 0, 0, 0, 0, 0)))
        args.append(sp)
    if use_rope:
        in_specs += [pl.BlockSpec((UNIT, HEAD_DIM), lambda i: (unit_a(i) % units_per_seq, 0))] * 2
        args += list(rope)
    in_specs += [_resident(p.shape) for p in params]
    args += list(params)

    out_shape = [jax.ShapeDtypeStruct(x2.shape, F32)]
    out_specs = [pl.BlockSpec((UNIT, D_MODEL), lambda i: (unit_c(i), 0))]
    if not scanned:
        out_shape.append(jax.ShapeDtypeStruct((n_units, 1, 2, HEADS, HEAD_DIM, HEAD_DIM), F32))
        out_specs.append(pl.BlockSpec(state_block, lambda i: (unit_b(i), 0, 0, 0, 0, 0)))

    scratch = [
        pltpu.VMEM((2, UNIT, A_WIDTH), F32),
        pltpu.VMEM((2, UNIT, A_WIDTH), BF16),
        pltpu.VMEM((2, UNIT, QK_W), BF16),
        pltpu.VMEM((2, UNIT, QK_W), F32),
        pltpu.VMEM((2, UNIT, V_W), BF16),
        pltpu.VMEM((2, UNIT, V_W), F32),
        pltpu.VMEM((2, UNIT, V_W), F32),
        pltpu.VMEM((2, UNIT, OUT_WIDTH), BF16),
        pltpu.VMEM((HEADS, len(TABLE_NAMES), CHUNK, CHUNK), F32),
    ]
    outs = pl.pallas_call(
        functools.partial(_even_kernel, n_units=n_units, use_rope=use_rope, scanned_states=scanned),
        grid=(n_units + 2,),
        in_specs=in_specs,
        out_specs=out_specs,
        out_shape=out_shape,
        scratch_shapes=scratch,
        compiler_params=pltpu.CompilerParams(
            dimension_semantics=("arbitrary",),
            vmem_limit_bytes=V7X_VMEM_LIMIT_BYTES),
        name="even_layer_rope" if use_rope else "even_layer",
    )(*args)
    return outs


def _odd_kernel(xp_ref, x_ref, xn_ref, moda_ref, modc_ref, nmix_ref, nmlp_ref, fin_ref, w_pool_ref, pscale_ref,
                w1_ref, w2_ref, y_ref, x1_s, hb_s, *, n_units, seq_len):
    step = pl.program_id(0)
    slot_a = lax.rem(step, 2)
    slot_b = 1 - slot_a
    rows_w = UNIT + 2 * POOL_HALO

    def stage_pool():
        sh1, sc1, g1, sh2, sc2, _ = _split_mod(moda_ref[0])
        xw = jnp.concatenate([xp_ref[...], x_ref[...], xn_ref[...]], axis=0)
        h = _rms_mod(xw, nmix_ref[...], sh1, sc1)
        first = jnp.minimum(step, n_units - 1) * UNIT - POOL_HALO
        t = (lax.broadcasted_iota(jnp.int32, (rows_w, 1), 0) + first) & (seq_len - 1)
        yield

        def shift_down(a, k):
            return jnp.where(t >= k, pltpu.roll(a, k, axis=0), 0.0)

        def shift_up(a, k):
            return jnp.where(t < seq_len - k, pltpu.roll(a, rows_w - k, axis=0), 0.0)

        outs = []
        for g, w in enumerate(POOL_WINDOWS):
            hg = h[:, g * POOL_DIM:(g + 1) * POOL_DIM]
            half = w // 2
            left = shift_down(hg, 1)
            right = hg
            span = 1
            while span < half:
                left = left + shift_down(left, span)
                right = right + shift_up(right, span)
                span *= 2
            cnt = (jnp.minimum(t + half, seq_len) - jnp.maximum(t - half, 0)).astype(F32)
            pooled = ((left + right) / cnt - hg)[POOL_HALO:POOL_HALO + UNIT].astype(BF16)
            outs.append(_dot(pooled, w_pool_ref[g]))
            yield
        x1 = x_ref[...] + g1 * (jnp.concatenate(outs, axis=-1) * pscale_ref[...])
        x1_s[slot_a] = x1
        hb_s[slot_a] = _rms_mod(x1, nmlp_ref[...], sh2, sc2).astype(BF16)

    def stage_mlp():
        g2 = _split_mod(modc_ref[0])[5]
        hb = hb_s[slot_b]
        acc = None
        for j in range(D_FF // FF_CHUNK):
            a = _dot(hb, w1_ref[:, j * FF_CHUNK:(j + 1) * FF_CHUNK])
            yield
            p = _dot(jnp.square(jnp.maximum(a, 0.0)).astype(BF16), w2_ref[j * FF_CHUNK:(j + 1) * FF_CHUNK, :])
            acc = p if acc is None else acc + p
            yield
        y_ref[...] = _rms(x1_s[slot_b] + g2 * acc) * fin_ref[...]

    @pl.when(step == 0)
    def _():
        _interleave(stage_pool())

    @pl.when((step >= 1) & (step < n_units))
    def _():
        _interleave(stage_mlp(), stage_pool())

    @pl.when(step == n_units)
    def _():
        _interleave(stage_mlp())


def _odd_call(x2, mod, params, *, seq_len):
    n_units = x2.shape[0] // UNIT
    units_per_seq = seq_len // UNIT
    halo_per_unit = UNIT // POOL_HALO
    last = n_units - 1
    last_halo = x2.shape[0] // POOL_HALO - 1
    unit_a = lambda i: jnp.minimum(i, last)
    unit_c = lambda i: jnp.clip(i - 1, 0, last)
    mod_row = _mod_row_map(mod, units_per_seq)
    in_specs =[pl.BlockSpec((POOL_HALO, D_MODEL), lambda i: (jnp.maximum(unit_a(i) * halo_per_unit - 1, 0), 0)),
                pl.BlockSpec((UNIT, D_MODEL), lambda i: (unit_a(i), 0)),
                pl.BlockSpec((POOL_HALO, D_MODEL),
                             lambda i: (jnp.minimum((unit_a(i) + 1) * halo_per_unit, last_halo), 0)),
                pl.BlockSpec((1, 1, N_MOD), lambda i: (mod_row(unit_a(i)), 0, 0)),
                pl.BlockSpec((1, 1, N_MOD), lambda i: (mod_row(unit_c(i)), 0, 0))]
    in_specs += [_resident(p.shape) for p in params]
    return pl.pallas_call(
        functools.partial(_odd_kernel, n_units=n_units, seq_len=seq_len),
        grid=(n_units + 1,),
        in_specs=in_specs,
        out_specs=pl.BlockSpec((UNIT, D_MODEL), lambda i: (unit_c(i), 0)),
        out_shape=jax.ShapeDtypeStruct(x2.shape, F32),
        scratch_shapes=[pltpu.VMEM((2, UNIT, D_MODEL), F32),
                        pltpu.VMEM((2, UNIT, D_MODEL), BF16)],
        compiler_params=pltpu.CompilerParams(
            dimension_semantics=("arbitrary",),
            vmem_limit_bytes=V7X_VMEM_LIMIT_BYTES),
        name="odd_layer",
    )(x2, x2, x2, mod, mod, *params)


def _rope_tables(n_tokens):
    rows = n_tokens // GRID_W
    pos_r = jnp.repeat(jnp.arange(rows, dtype=F32), GRID_W)
    pos_c = jnp.tile(jnp.arange(GRID_W, dtype=F32), rows)
    inv = ROPE_BASE ** (-jnp.arange(ROPE_PAIRS_AXIS, dtype=F32) / ROPE_PAIRS_AXIS)
    ang = jnp.concatenate([pos_r[:, None] * inv, pos_c[:, None] * inv], axis=-1)
    cos, sin = jnp.cos(ang), jnp.sin(ang)
    return jnp.concatenate([cos, cos], axis=-1), jnp.concatenate([-sin, sin], axis=-1)


def kernel(x_prompt, x_sample, state_ret, c, c_ctx, ada_w, ada_b, norm_mix, norm_mlp, mlp_w1, mlp_w2,
           ev_w_in, ev_w_s, ev_b_s, ev_vnorm, ev_decay, ev_w_out, od_w_pool, od_pool_scale, final_norm):
    n_prompt, prompt_len, _ = x_prompt.shape
    n_sample, sample_len, _ = x_sample.shape
    assert DEPTH == 2 and n_sample + 1 <= COND_ROWS
    assert prompt_len == UNIT and sample_len % UNIT == 0

    cond = jnp.zeros((COND_ROWS, D_MODEL), F32).at[0].set(c_ctx).at[1:1 + n_sample].set(c)
    mod = _ada_call(cond, ada_w, ada_b)
    mod_prompt = [mod[i, 0:1].reshape(1, 1, N_MOD) for i in range(DEPTH)]
    mod_sample = [mod[i, 1:1 + n_sample].reshape(n_sample, 1, N_MOD) for i in range(DEPTH)]

    row = lambda v: v.reshape(1, -1)
    w_in = ev_w_in[0].astype(BF16)
    dec = jnp.broadcast_to(ev_decay[0].reshape(2 * HEADS, 1), (2 * HEADS, CHUNK))
    even_params = (
        row(norm_mix[0]), row(norm_mlp[0]),
        w_in, ev_w_s[0].astype(BF16),
        jnp.broadcast_to(ev_b_s[0][:, :, None], (A_GROUPS, CHUNK, A_GROUP_DIM)),
        row(ev_vnorm[0]), dec,
        ev_w_out[0].astype(BF16),
        mlp_w1[0].astype(BF16), mlp_w2[0].astype(BF16),
    )
    odd_params = (
        row(norm_mix[1]), row(norm_mlp[1]), row(final_norm),
        od_w_pool[0].astype(BF16), row(od_pool_scale[0]),
        mlp_w1[1].astype(BF16), mlp_w2[1].astype(BF16),
    )

    xp2 = x_prompt.reshape(n_prompt * prompt_len, D_MODEL)
    xp2, new_state = _even_call(xp2, mod_prompt[0], None, None, even_params, seq_len=prompt_len)
    y_prompt = _odd_call(xp2, mod_prompt[1], odd_params, seq_len=prompt_len)

    xs2 = x_sample.reshape(n_sample * sample_len, D_MODEL)
    rope = _rope_tables(sample_len)
    sp = _ret_state_call(xs2, mod_sample[0], state_ret, rope, row(norm_mix[0]), w_in, dec,
                         n_seq=n_sample, seq_len=sample_len)
    (xs2,) = _even_call(xs2, mod_sample[0], sp, rope, even_params, seq_len=sample_len)
    y_sample = _odd_call(xs2, mod_sample[1], odd_params, seq_len=sample_len)

    return (y_prompt.reshape(n_prompt, prompt_len, D_MODEL),
            y_sample.reshape(n_sample, sample_len, D_MODEL),
            new_state)
```

```python
import functools

import jax
import jax.numpy as jnp
from jax import lax
from jax.experimental import pallas as pl
from jax.experimental.pallas import tpu as pltpu

D_MODEL = 1024
DEPTH = 2
GRID_W = 64
A_WIDTH = D_MODEL // 2
A_GROUPS = 4
A_GROUP_DIM = A_WIDTH // A_GROUPS
CHUNK = 128
HEADS = 4
HEAD_DIM = (D_MODEL // 2) // HEADS
ROPE_BASE = 10000.0
ROPE_PAIRS_AXIS = HEAD_DIM // 4
POOL_WINDOWS = (2, 4, 8, 16)
POOL_DIM = D_MODEL // len(POOL_WINDOWS)
POOL_HALO = max(POOL_WINDOWS) // 2
D_FF = 4 * D_MODEL
EPS = 1e-6
QK_W = HEADS * HEAD_DIM
V_W = HEADS * HEAD_DIM
IN_WIDTH = 2 * A_WIDTH + 2 * QK_W + 3 * V_W
OUT_WIDTH = A_WIDTH + V_W
N_MOD = 6 * D_MODEL
K_COL = 2 * A_WIDTH + QK_W
V_COL = K_COL + QK_W

UNIT = 2 * CHUNK
UNIT_CHUNKS = UNIT // CHUNK
FF_CHUNK = 1024
ADA_BLOCK_N = 1536
COND_ROWS = 8
SUBLANES = 8
V7X_VMEM_LIMIT_BYTES = 60 * 1024 * 1024

F32 = jnp.float32
BF16 = jnp.bfloat16


def _dot(a, b):
    return jnp.dot(a, b, preferred_element_type=F32)


def _dot_nt(a, b):
    return lax.dot_general(a, b, (((1,), (1,)), ((), ())), preferred_element_type=F32)


def _silu(x):
    return x * jax.nn.sigmoid(x)


def _gelu_tanh(x):
    return 0.5 * x * (1.0 + jnp.tanh(0.7978845608028654 * (x + 0.044715 * (x * x * x))))


def _rms(x):
    return x * lax.rsqrt(jnp.mean(x * x, axis=-1, keepdims=True) + EPS)


def _rms_mod(x, gain, shift, scale):
    return (_rms(x) * gain) * (1.0 + scale) + shift


def _layer_norm(x):
    mu = jnp.mean(x, axis=-1, keepdims=True)
    d = x - mu
    return d * lax.rsqrt(jnp.mean(d * d, axis=-1, keepdims=True) + EPS)


def _split_mod(mod):
    return [mod[:, i * D_MODEL:(i + 1) * D_MODEL] for i in range(6)]


def _rope(x, cos2, sin2):
    return x * cos2 + pltpu.roll(x, HEAD_DIM // 2, axis=1) * sin2


def _run(gen, n=None):
    done = 0
    while n is None or done < n:
        try:
            next(gen)
        except StopIteration:
            break
        done += 1
    return done


def _weave(heavy, lights, n_light):
    heavy = list(heavy)
    pending = _roundrobin(lights)
    ran = 0
    for k, piece in enumerate(heavy):
        piece()
        want = ((k + 1) * n_light + len(heavy) - 1) // len(heavy)
        ran += _run(pending, want - ran)
    _run(pending)


def _pieces(gen, n):
    return [functools.partial(next, gen, None) for _ in range(n)]


def _roundrobin(gens):
    live = list(gens)
    while live:
        for g in list(live):
            try:
                next(g)
            except StopIteration:
                live.remove(g)
                continue
            yield


MLP_DOT_PIECES = 2 * (D_FF // FF_CHUNK)


def _mlp_pieces(hb, w1_ref, w2_ref):
    n_ff = D_FF // FF_CHUNK
    up = lambda j: _dot(hb, w1_ref[:, j * FF_CHUNK:(j + 1) * FF_CHUNK])
    a = up(0)
    yield
    acc = None
    for j in range(n_ff):
        a_next = None
        if j + 1 < n_ff:
            a_next = up(j + 1)
            yield
        p = _dot(jnp.square(jnp.maximum(a, 0.0)).astype(BF16), w2_ref[j * FF_CHUNK:(j + 1) * FF_CHUNK, :])
        acc = p if acc is None else acc + p
        a = a_next
        yield
    return acc


def _mod_row_map(mod, units_per_seq):
    if mod.shape[0] == 1:
        return lambda unit: 0
    return lambda unit: unit // units_per_seq


def _resident(shape):
    return pl.BlockSpec(shape, lambda i: (0,) * len(shape), pipeline_mode=pl.Buffered(1))


def _ada_kernel(cond_ref, w_ref, b_ref, out_ref):
    s = _silu(cond_ref[...]).astype(BF16)
    out_ref[0] = _dot(s, w_ref[0].astype(BF16)) + b_ref[0]


def _ada_call(cond, ada_w, ada_b):
    return pl.pallas_call(
        _ada_kernel,
        grid=(DEPTH, N_MOD // ADA_BLOCK_N),
        in_specs=[
            pl.BlockSpec((COND_ROWS, D_MODEL), lambda i, j: (0, 0)),
            pl.BlockSpec((1, D_MODEL, ADA_BLOCK_N), lambda i, j: (i, 0, j)),
            pl.BlockSpec((1, 1, ADA_BLOCK_N), lambda i, j: (i, 0, j)),
        ],
        out_specs=pl.BlockSpec((1, COND_ROWS, ADA_BLOCK_N), lambda i, j: (i, 0, j)),
        out_shape=jax.ShapeDtypeStruct((DEPTH, COND_ROWS, N_MOD), F32),
        compiler_params=pltpu.CompilerParams(
            dimension_semantics=("arbitrary", "arbitrary"),
            vmem_limit_bytes=V7X_VMEM_LIMIT_BYTES),
        name="ada_mod",
    )(cond, ada_w, ada_b.reshape(DEPTH, 1, N_MOD))


def _decay_tables(dec_ref, hd):
    row = lax.broadcasted_iota(jnp.int32, (CHUNK, CHUNK), 0).astype(F32)
    col = lax.broadcasted_iota(jnp.int32, (CHUNK, CHUNK), 1).astype(F32)
    lg_f = jnp.broadcast_to(-jnp.exp(dec_ref[hd:hd + 1, :]), (CHUNK, CHUNK))
    lg_b = jnp.broadcast_to(-jnp.exp(dec_ref[HEADS + hd:HEADS + hd + 1, :]), (CHUNK, CHUNK))
    d_f = row - col
    d_b = col - row
    return dict(
        decay_f=jnp.where(d_f >= 0, jnp.exp(lg_f * jnp.maximum(d_f, 0.0)), 0.0),
        decay_b=jnp.where(d_b >= 0, jnp.exp(lg_b * jnp.maximum(d_b, 0.0)), 0.0),
        qd_f=jnp.exp(lg_f * (row + 1.0)),
        qd_b=jnp.exp(lg_b * (CHUNK - row)),
        kd_f=jnp.exp(lg_f * (CHUNK - 1.0 - row)),
        kd_b=jnp.exp(lg_b * row),
        cd_f=jnp.exp(lg_f * CHUNK),
        cd_b=jnp.exp(lg_b * CHUNK),
    )


TABLE_NAMES = ("decay_f", "decay_b", "qd_f", "qd_b", "kd_f", "kd_b", "cd_f", "cd_b")


def _ret_state_kernel(x_ref, mod_ref, s0_ref, cos_ref, sin_ref, nmix_ref, wk_ref, wv_ref, dec_ref, sp_ref,
                      *, n_chunk):
    sh1, sc1 = _split_mod(mod_ref[0])[:2]
    hb = _rms_mod(x_ref[...], nmix_ref[...], sh1, sc1).astype(BF16)
    zk = _dot(hb, wk_ref[...]) * (HEAD_DIM ** -0.5)
    zv = _dot(hb, wv_ref[...]).astype(BF16)
    cos2, sin2 = cos_ref[...], sin_ref[...]
    for hd in range(HEADS):
        lo = hd * HEAD_DIM
        tb = _decay_tables(dec_ref, hd)
        kh = _rope(zk[:, lo:lo + HEAD_DIM], cos2, sin2)
        kv_f, kv_b = [], []
        for c in range(n_chunk):
            kc = kh[c * CHUNK:(c + 1) * CHUNK]
            vc = zv[c * CHUNK:(c + 1) * CHUNK, lo:lo + HEAD_DIM]
            kv_f.append(_dot((kc * tb["kd_f"]).T.astype(BF16), vc))
            kv_b.append(_dot((kc * tb["kd_b"]).T.astype(BF16), vc))
        st = s0_ref[0, 0, 0, hd]
        for c in range(n_chunk):
            sp_ref[c // UNIT_CHUNKS, c % UNIT_CHUNKS, 0, hd] = st.astype(BF16)
            st = tb["cd_f"] * st + kv_f[c]
        st = s0_ref[0, 0, 1, hd]
        for c in reversed(range(n_chunk)):
            sp_ref[c // UNIT_CHUNKS, c % UNIT_CHUNKS, 1, hd] = st.astype(BF16)
            st = tb["cd_b"] * st + kv_b[c]


def _ret_state_call(x2, mod, s0, rope, nmix, w_in, dec, *, n_seq, seq_len):
    n_chunk = seq_len // CHUNK
    units = seq_len // UNIT
    return pl.pallas_call(
        functools.partial(_ret_state_kernel, n_chunk=n_chunk),
        grid=(n_seq,),
        in_specs=[
            pl.BlockSpec((seq_len, D_MODEL), lambda i: (i, 0)),
            pl.BlockSpec((1, 1, N_MOD), lambda i: (i, 0, 0)),
            pl.BlockSpec((1, 1, 2, HEADS, HEAD_DIM, HEAD_DIM), lambda i: (i, 0, 0, 0, 0, 0)),
            _resident((seq_len, HEAD_DIM)), _resident((seq_len, HEAD_DIM)),
            _resident((1, D_MODEL)),
            pl.BlockSpec((D_MODEL, QK_W), lambda i: (0, K_COL // QK_W), pipeline_mode=pl.Buffered(1)),
            pl.BlockSpec((D_MODEL, V_W), lambda i: (0, V_COL // V_W), pipeline_mode=pl.Buffered(1)),
            _resident(dec.shape),
        ],
        out_specs=pl.BlockSpec((units, UNIT_CHUNKS, 2, HEADS, HEAD_DIM, HEAD_DIM), lambda i: (i, 0, 0, 0, 0, 0)),
        out_shape=jax.ShapeDtypeStruct((n_seq * units, UNIT_CHUNKS, 2, HEADS, HEAD_DIM, HEAD_DIM), BF16),
        compiler_params=pltpu.CompilerParams(
            dimension_semantics=("arbitrary",),
            vmem_limit_bytes=V7X_VMEM_LIMIT_BYTES),
        name="ret_state",
    )(x2, mod, s0, rope[0], rope[1], nmix, w_in, w_in, dec)


def _even_kernel(*refs, n_units, use_rope, scanned_states):
    it = iter(refs)
    xa_ref, xc_ref, moda_ref, modc_ref = (next(it) for _ in range(4))
    sp_ref = next(it) if scanned_states else None
    cos_ref = next(it) if use_rope else None
    sin_ref = next(it) if use_rope else None
    nmix_ref, nmlp_ref = next(it), next(it)
    w_in_ref, w_s_ref, b_s_ref, vgain_ref, dec_ref, w_out_ref, w1_ref, w2_ref = (next(it) for _ in range(8))
    y_ref = next(it)
    st_ref = None if scanned_states else next(it)
    ug_s, vg_s, q_s, k_s, v_s, gf_s, gb_s, cat_s, tab_s = (next(it) for _ in range(9))

    step = pl.program_id(0)
    slot_a = lax.rem(step, 2)
    slot_b = 1 - slot_a

    @pl.when(step == 0)
    def _():
        for hd in range(HEADS):
            tb = _decay_tables(dec_ref, hd)
            for n, name in enumerate(TABLE_NAMES):
                tab_s[hd, n] = tb[name]

    def stage_in():
        sh1, sc1 = _split_mod(moda_ref[0])[:2]
        hb = _rms_mod(xa_ref[...], nmix_ref[...], sh1, sc1).astype(BF16)
        z = _dot(hb, w_in_ref[...])
        yield
        ug_s[slot_a] = _gelu_tanh(z[:, 0:A_WIDTH])
        yield
        zv = _gelu_tanh(z[:, A_WIDTH:2 * A_WIDTH])
        vgn = [_layer_norm(zv[:, g * A_GROUP_DIM:(g + 1) * A_GROUP_DIM]) for g in range(A_GROUPS)]
        vg_s[slot_a] = (jnp.concatenate(vgn, axis=-1) * vgain_ref[...]).astype(BF16)
        yield
        zq = z[:, 2 * A_WIDTH:K_COL]
        zk = z[:, K_COL:V_COL] * (HEAD_DIM ** -0.5)
        if use_rope:
            cos2, sin2 = cos_ref[...], sin_ref[...]
            zq = jnp.concatenate([_rope(zq[:, h * HEAD_DIM:(h + 1) * HEAD_DIM], cos2, sin2) for h in range(HEADS)], -1)
            zk = jnp.concatenate([_rope(zk[:, h * HEAD_DIM:(h + 1) * HEAD_DIM], cos2, sin2) for h in range(HEADS)], -1)
        q_s[slot_a] = zq.astype(BF16)
        yield
        k_s[slot_a] = zk
        v_s[slot_a] = z[:, V_COL:V_COL + V_W].astype(BF16)
        yield
        gf_s[slot_a] = _silu(z[:, V_COL + V_W:V_COL + 2 * V_W])
        yield
        gb_s[slot_a] = _silu(z[:, V_COL + 2 * V_W:V_COL + 3 * V_W])

    IN_LIGHT = 6

    def stage_mix():
        s = slot_b
        rows = [slice(c * CHUNK, (c + 1) * CHUNK) for c in range(UNIT_CHUNKS)]
        head_cols = [slice(hd * HEAD_DIM, (hd + 1) * HEAD_DIM) for hd in range(HEADS)]
        table = lambda hd, name: tab_s[hd, TABLE_NAMES.index(name)]

        sp2 = []
        for g in range(A_GROUPS):
            lo = g * A_GROUP_DIM
            vg2 = jnp.concatenate([vg_s[s, r, lo:lo + A_GROUP_DIM] for r in rows], axis=-1)
            sp2.append(_dot(w_s_ref[g], vg2))
        yield
        scores, kv_f, kv_b = [], [], []
        for hd in range(HEADS):
            kc = [k_s[s, r, head_cols[hd]] for r in rows]
            scores.append([_dot_nt(q_s[s, r, head_cols[hd]], kc[c].astype(BF16)) for c, r in enumerate(rows)])
            if not scanned_states:
                vc = [v_s[s, r, head_cols[hd]] for r in rows]
                kv_f.append([_dot((kc[c] * table(hd, "kd_f")).T.astype(BF16), vc[c]) for c in range(UNIT_CHUNKS)])
                kv_b.append([_dot((kc[c] * table(hd, "kd_b")).T.astype(BF16), vc[c]) for c in range(UNIT_CHUNKS)])
            yield
        for g in range(A_GROUPS):
            lo = g * A_GROUP_DIM
            for c, r in enumerate(rows):
                gate = sp2[g][:, c * CHUNK:(c + 1) * CHUNK] + b_s_ref[g]
                cat_s[s, r, lo:lo + A_GROUP_DIM] = (ug_s[s, r, lo:lo + A_GROUP_DIM] * gate).astype(BF16)
            if g % 2 == 1:
                yield
        y_raw = []
        for hd in range(HEADS):
            if scanned_states:
                enter_f = [sp_ref[0, c, 0, hd] for c in range(UNIT_CHUNKS)]
                enter_b = [sp_ref[0, c, 1, hd] for c in range(UNIT_CHUNKS)]
            else:
                enter_f, enter_b = [None], [None] * UNIT_CHUNKS
                st = kv_f[hd][0]
                for c in range(1, UNIT_CHUNKS):
                    enter_f.append(st.astype(BF16))
                    st = table(hd, "cd_f") * st + kv_f[hd][c]
                st_ref[0, 0, 0, hd] = st
                st = kv_b[hd][UNIT_CHUNKS - 1]
                for c in reversed(range(UNIT_CHUNKS - 1)):
                    enter_b[c] = st.astype(BF16)
                    st = table(hd, "cd_b") * st + kv_b[hd][c]
                st_ref[0, 0, 1, hd] = st
            per_chunk = []
            for c, r in enumerate(rows):
                vc = v_s[s, r, head_cols[hd]]
                qf = q_s[s, r, head_cols[hd]].astype(F32)

                def direction(decay, qd, enter):
                    p = (scores[hd][c] * table(hd, decay)).astype(BF16)
                    if enter is None:
                        return _dot(p, vc)
                    lhs = jnp.concatenate([p, (qf * table(hd, qd)).astype(BF16)], axis=-1)
                    return _dot(lhs, jnp.concatenate([vc, enter], axis=0))

                per_chunk.append((direction("decay_f", "qd_f", enter_f[c]), direction("decay_b", "qd_b", enter_b[c])))
            y_raw.append(per_chunk)
            yield
        for hd in range(HEADS):
            for c, r in enumerate(rows):
                y_f, y_b = y_raw[hd][c]
                cols = slice(A_WIDTH + hd * HEAD_DIM, A_WIDTH + (hd + 1) * HEAD_DIM)
                cat_s[s, r, cols] = (gf_s[s, r, head_cols[hd]] * _layer_norm(y_f)
                                     + gb_s[s, r, head_cols[hd]] * _layer_norm(y_b)).astype(BF16)
            yield

    MIX_LIGHT = 1 + HEADS + A_GROUPS // 2 + 2 * HEADS

    def stage_mlp():
        _, _, g1, sh2, sc2, g2 = _split_mod(modc_ref[0])
        x = xc_ref[...] + g1 * _dot(cat_s[slot_a], w_out_ref[...])
        hb = _rms_mod(x, nmlp_ref[...], sh2, sc2).astype(BF16)
        yield
        acc = yield from _mlp_pieces(hb, w1_ref, w2_ref)
        y_ref[...] = x + g2 * acc

    mlp_pieces = 2 + MLP_DOT_PIECES

    @pl.when(step == 0)
    def _():
        _run(stage_in())

    @pl.when(step == 1)
    def _():
        a = stage_in()
        _weave(_pieces(a, 1), [a, stage_mix()], IN_LIGHT + MIX_LIGHT)

    @pl.when((step >= 2) & (step < n_units))
    def _():
        a, c = stage_in(), stage_mlp()
        heavy = _pieces(c, 1) + _pieces(a, 1) + _pieces(c, mlp_pieces - 1)
        _weave(heavy, [a, stage_mix()], IN_LIGHT + MIX_LIGHT)

    @pl.when(step == n_units)
    def _():
        _weave(_pieces(stage_mlp(), mlp_pieces), [stage_mix()], MIX_LIGHT)

    @pl.when(step == n_units + 1)
    def _():
        _run(stage_mlp())


def _even_call(x2, mod, sp, rope, params, *, seq_len):
    n_units = x2.shape[0] // UNIT
    units_per_seq = seq_len // UNIT
    scanned = sp is not None
    use_rope = rope is not None
    assert scanned or units_per_seq == 1
    last = n_units - 1
    unit_a = lambda i: jnp.minimum(i, last)
    unit_b = lambda i: jnp.clip(i - 1, 0, last)
    unit_c = lambda i: jnp.clip(i - 2, 0, last)
    mod_row = _mod_row_map(mod, units_per_seq)
    state_block = (1, UNIT_CHUNKS if scanned else 1, 2, HEADS, HEAD_DIM, HEAD_DIM)

    in_specs = [pl.BlockSpec((UNIT, D_MODEL), lambda i: (unit_a(i), 0)),
                pl.BlockSpec((UNIT, D_MODEL), lambda i: (unit_c(i), 0)),
                pl.BlockSpec((1, 1, N_MOD), lambda i: (mod_row(unit_a(i)), 0, 0)),
                pl.BlockSpec((1, 1, N_MOD), lambda i: (mod_row(unit_c(i)), 0, 0))]
    args = [x2, x2, mod, mod]
    if scanned:
        in_specs.append(pl.BlockSpec(state_block, lambda i: (unit_b(i), 0, 0, 0, 0, 0)))
        args.append(sp)
    if use_rope:
        in_specs += [pl.BlockSpec((UNIT, HEAD_DIM), lambda i: (unit_a(i) % units_per_seq, 0))] * 2
        args += list(rope)
    in_specs += [_resident(p.shape) for p in params]
    args += list(params)

    out_shape = [jax.ShapeDtypeStruct(x2.shape, F32)]
    out_specs = [pl.BlockSpec((UNIT, D_MODEL), lambda i: (unit_c(i), 0))]
    if not scanned:
        out_shape.append(jax.ShapeDtypeStruct((n_units, 1, 2, HEADS, HEAD_DIM, HEAD_DIM), F32))
        out_specs.append(pl.BlockSpec(state_block, lambda i: (unit_b(i), 0, 0, 0, 0, 0)))

    scratch = [
        pltpu.VMEM((2, UNIT, A_WIDTH), F32),
        pltpu.VMEM((2, UNIT, A_WIDTH), BF16),
        pltpu.VMEM((2, UNIT, QK_W), BF16),
        pltpu.VMEM((2, UNIT, QK_W), F32),
        pltpu.VMEM((2, UNIT, V_W), BF16),
        pltpu.VMEM((2, UNIT, V_W), F32),
        pltpu.VMEM((2, UNIT, V_W), F32),
        pltpu.VMEM((2, UNIT, OUT_WIDTH), BF16),
        pltpu.VMEM((HEADS, len(TABLE_NAMES), CHUNK, CHUNK), F32),
    ]
    outs = pl.pallas_call(
        functools.partial(_even_kernel, n_units=n_units, use_rope=use_rope, scanned_states=scanned),
        grid=(n_units + 2,),
        in_specs=in_specs,
        out_specs=out_specs,
        out_shape=out_shape,
        scratch_shapes=scratch,
        compiler_params=pltpu.CompilerParams(
            dimension_semantics=("arbitrary",),
            vmem_limit_bytes=V7X_VMEM_LIMIT_BYTES),
        name="even_layer_rope" if use_rope else "even_layer",
    )(*args)
    return outs


def _odd_kernel(xp_ref, x_ref, xn_ref, moda_ref, modc_ref, nmix_ref, nmlp_ref, fin_ref, w_pool_ref, pscale_ref,
                w1_ref, w2_ref, y_ref, x1_s, hb_s, *, n_units, seq_len):
    step = pl.program_id(0)
    slot_a = lax.rem(step, 2)
    slot_b = 1 - slot_a
    rows_w = UNIT + 2 * POOL_HALO

    def stage_pool():
        sh1, sc1, g1, sh2, sc2, _ = _split_mod(moda_ref[0])
        xw = jnp.concatenate([xp_ref[...], x_ref[...], xn_ref[...]], axis=0)
        h = _rms_mod(xw, nmix_ref[...], sh1, sc1)
        first = jnp.minimum(step, n_units - 1) * UNIT - POOL_HALO
        t = (lax.broadcasted_iota(jnp.int32, (rows_w, 1), 0) + first) & (seq_len - 1)
        yield

        def shift_down(a, k):
            return jnp.where(t >= k, pltpu.roll(a, k, axis=0), 0.0)

        def shift_up(a, k):
            return jnp.where(t < seq_len - k, pltpu.roll(a, rows_w - k, axis=0), 0.0)

        outs = []
        for g, w in enumerate(POOL_WINDOWS):
            hg = h[:, g * POOL_DIM:(g + 1) * POOL_DIM]
            half = w // 2
            left = shift_down(hg, 1)
            right = hg
            span = 1
            while span < half:
                left = left + shift_down(left, span)
                right = right + shift_up(right, span)
                span *= 2
            cnt = (jnp.minimum(t + half, seq_len) - jnp.maximum(t - half, 0)).astype(F32)
            pooled = ((left + right) / cnt - hg)[POOL_HALO:POOL_HALO + UNIT].astype(BF16)
            outs.append(_dot(pooled, w_pool_ref[g]))
            yield
        x1 = x_ref[...] + g1 * (jnp.concatenate(outs, axis=-1) * pscale_ref[...])
        x1_s[slot_a] = x1
        hb_s[slot_a] = _rms_mod(x1, nmlp_ref[...], sh2, sc2).astype(BF16)

    def stage_mlp():
        g2 = _split_mod(modc_ref[0])[5]
        acc = yield from _mlp_pieces(hb_s[slot_b], w1_ref, w2_ref)
        y_ref[...] = _rms(x1_s[slot_b] + g2 * acc) * fin_ref[...]

    pool_light = 2 + len(POOL_WINDOWS)

    @pl.when(step == 0)
    def _():
        _run(stage_pool())

    @pl.when((step >= 1) & (step < n_units))
    def _():
        _weave(_pieces(stage_mlp(), 1 + MLP_DOT_PIECES), [stage_pool()], pool_light)

    @pl.when(step == n_units)
    def _():
        _run(stage_mlp())


def _odd_call(x2, mod, params, *, seq_len):
    n_units = x2.shape[0] // UNIT
    units_per_seq = seq_len // UNIT
    halo_per_unit = UNIT // POOL_HALO
    last = n_units - 1
    last_halo = x2.shape[0] // POOL_HALO - 1
    unit_a = lambda i: jnp.minimum(i, last)
    unit_c = lambda i: jnp.clip(i - 1, 0, last)
    mod_row = _mod_row_map(mod, units_per_seq)
    in_specs =[pl.BlockSpec((POOL_HALO, D_MODEL), lambda i: (jnp.maximum(unit_a(i) * halo_per_unit - 1, 0), 0)),
                pl.BlockSpec((UNIT, D_MODEL), lambda i: (unit_a(i), 0)),
                pl.BlockSpec((POOL_HALO, D_MODEL),
                             lambda i: (jnp.minimum((unit_a(i) + 1) * halo_per_unit, last_halo), 0)),
                pl.BlockSpec((1, 1, N_MOD), lambda i: (mod_row(unit_a(i)), 0, 0)),
                pl.BlockSpec((1, 1, N_MOD), lambda i: (mod_row(unit_c(i)), 0, 0))]
    in_specs += [_resident(p.shape) for p in params]
    return pl.pallas_call(
        functools.partial(_odd_kernel, n_units=n_units, seq_len=seq_len),
        grid=(n_units + 1,),
        in_specs=in_specs,
        out_specs=pl.BlockSpec((UNIT, D_MODEL), lambda i: (unit_c(i), 0)),
        out_shape=jax.ShapeDtypeStruct(x2.shape, F32),
        scratch_shapes=[pltpu.VMEM((2, UNIT, D_MODEL), F32),
                        pltpu.VMEM((2, UNIT, D_MODEL), BF16)],
        compiler_params=pltpu.CompilerParams(
            dimension_semantics=("arbitrary",),
            vmem_limit_bytes=V7X_VMEM_LIMIT_BYTES),
        name="odd_layer",
    )(x2, x2, x2, mod, mod, *params)


def _rope_tables(n_tokens):
    rows = n_tokens // GRID_W
    pos_r = jnp.repeat(jnp.arange(rows, dtype=F32), GRID_W)
    pos_c = jnp.tile(jnp.arange(GRID_W, dtype=F32), rows)
    inv = ROPE_BASE ** (-jnp.arange(ROPE_PAIRS_AXIS, dtype=F32) / ROPE_PAIRS_AXIS)
    ang = jnp.concatenate([pos_r[:, None] * inv, pos_c[:, None] * inv], axis=-1)
    cos, sin = jnp.cos(ang), jnp.sin(ang)
    return jnp.concatenate([cos, cos], axis=-1), jnp.concatenate([-sin, sin], axis=-1)


def kernel(x_prompt, x_sample, state_ret, c, c_ctx, ada_w, ada_b, norm_mix, norm_mlp, mlp_w1, mlp_w2,
           ev_w_in, ev_w_s, ev_b_s, ev_vnorm, ev_decay, ev_w_out, od_w_pool, od_pool_scale, final_norm):
    n_prompt, prompt_len, _ = x_prompt.shape
    n_sample, sample_len, _ = x_sample.shape
    assert DEPTH == 2 and n_sample + 1 <= COND_ROWS
    assert prompt_len == UNIT and sample_len % UNIT == 0

    cond = jnp.zeros((COND_ROWS, D_MODEL), F32).at[0].set(c_ctx).at[1:1 + n_sample].set(c)
    mod = _ada_call(cond, ada_w, ada_b)
    mod_prompt = [mod[i, 0:1].reshape(1, 1, N_MOD) for i in range(DEPTH)]
    mod_sample = [mod[i, 1:1 + n_sample].reshape(n_sample, 1, N_MOD) for i in range(DEPTH)]

    row = lambda v: v.reshape(1, -1)
    w_in = ev_w_in[0].astype(BF16)
    dec = jnp.broadcast_to(ev_decay[0].reshape(2 * HEADS, 1), (2 * HEADS, CHUNK))
    even_params = (
        row(norm_mix[0]), row(norm_mlp[0]),
        w_in, ev_w_s[0].astype(BF16),
        jnp.broadcast_to(ev_b_s[0][:, :, None], (A_GROUPS, CHUNK, A_GROUP_DIM)),
        row(ev_vnorm[0]), dec,
        ev_w_out[0].astype(BF16),
        mlp_w1[0].astype(BF16), mlp_w2[0].astype(BF16),
    )
    odd_params = (
        row(norm_mix[1]), row(norm_mlp[1]), row(final_norm),
        od_w_pool[0].astype(BF16), row(od_pool_scale[0]),
        mlp_w1[1].astype(BF16), mlp_w2[1].astype(BF16),
    )

    xp2 = x_prompt.reshape(n_prompt * prompt_len, D_MODEL)
    xp2, new_state = _even_call(xp2, mod_prompt[0], None, None, even_params, seq_len=prompt_len)
    y_prompt = _odd_call(xp2, mod_prompt[1], odd_params, seq_len=prompt_len)

    xs2 = x_sample.reshape(n_sample * sample_len, D_MODEL)
    rope = _rope_tables(sample_len)
    sp = _ret_state_call(xs2, mod_sample[0], state_ret, rope, row(norm_mix[0]), w_in, dec,
                         n_seq=n_sample, seq_len=sample_len)
    (xs2,) = _even_call(xs2, mod_sample[0], sp, rope, even_params, seq_len=sample_len)
    y_sample = _odd_call(xs2, mod_sample[1], odd_params, seq_len=sample_len)

    return (y_prompt.reshape(n_prompt, prompt_len, D_MODEL),
            y_sample.reshape(n_sample, sample_len, D_MODEL),
            new_state)
```

```python
import functools
from typing import NamedTuple

import jax
import jax.numpy as jnp
from jax import lax
from jax.experimental import pallas as pl
from jax.experimental.pallas import tpu as pltpu

D_MODEL = 1024
DEPTH = 2
GRID_W = 64
A_WIDTH = D_MODEL // 2
A_GROUPS = 4
A_GROUP_DIM = A_WIDTH // A_GROUPS
CHUNK = 128
HEADS = 4
HEAD_DIM = (D_MODEL // 2) // HEADS
ROPE_BASE = 10000.0
ROPE_PAIRS_AXIS = HEAD_DIM // 4
POOL_WINDOWS = (2, 4, 8, 16)
POOL_DIM = D_MODEL // len(POOL_WINDOWS)
POOL_HALO = max(POOL_WINDOWS) // 2
D_FF = 4 * D_MODEL
EPS = 1e-6
QK_W = HEADS * HEAD_DIM
V_W = HEADS * HEAD_DIM
IN_WIDTH = 2 * A_WIDTH + 2 * QK_W + 3 * V_W
OUT_WIDTH = A_WIDTH + V_W
N_MOD = 6 * D_MODEL
K_COL = 2 * A_WIDTH + QK_W
V_COL = K_COL + QK_W

UNIT = 2 * CHUNK
UNIT_CHUNKS = UNIT // CHUNK
FF_CHUNK = 1024
ADA_BLOCK_N = 1536
COND_ROWS = 8
SUBLANES = 8
V7X_VMEM_LIMIT_BYTES = 60 * 1024 * 1024

F32 = jnp.float32
BF16 = jnp.bfloat16


def _dot(a, b):
    return jnp.dot(a, b, preferred_element_type=F32)


def _dot_nt(a, b):
    return lax.dot_general(a, b, (((1,), (1,)), ((), ())), preferred_element_type=F32)


def _silu(x):
    return x * jax.nn.sigmoid(x)


def _gelu_tanh(x):
    return 0.5 * x * (1.0 + jnp.tanh(0.7978845608028654 * (x + 0.044715 * (x * x * x))))


def _rms(x):
    return x * lax.rsqrt(jnp.mean(x * x, axis=-1, keepdims=True) + EPS)


def _rms_mod(x, gain, shift, scale):
    return (_rms(x) * gain) * (1.0 + scale) + shift


def _layer_norm(x):
    mu = jnp.mean(x, axis=-1, keepdims=True)
    d = x - mu
    return d * lax.rsqrt(jnp.mean(d * d, axis=-1, keepdims=True) + EPS)


def _split_mod(mod):
    return [mod[:, i * D_MODEL:(i + 1) * D_MODEL] for i in range(6)]


def _rope(x, cos2, sin2):
    return x * cos2 + pltpu.roll(x, HEAD_DIM // 2, axis=1) * sin2


def _run(gen, n=None):
    done = 0
    while n is None or done < n:
        try:
            next(gen)
        except StopIteration:
            break
        done += 1
    return done


def _weave(heavy, lights, n_light):
    heavy = list(heavy)
    pending = _roundrobin(lights)
    gaps = max(len(heavy) - 1, 1)
    ran = 0
    for k, piece in enumerate(heavy):
        piece()
        want = ((k + 1) * n_light + gaps - 1) // gaps
        ran += _run(pending, want - ran)
    _run(pending)


def _pieces(gen, n):
    return [functools.partial(next, gen, None) for _ in range(n)]


def _roundrobin(gens):
    live = list(gens)
    while live:
        for g in list(live):
            try:
                next(g)
            except StopIteration:
                live.remove(g)
                continue
            yield


MLP_DOT_PIECES = 2 * (D_FF // FF_CHUNK)


def _mlp_pieces(hb, w1_ref, w2_ref):
    n_ff = D_FF // FF_CHUNK
    up = lambda j: _dot(hb, w1_ref[:, j * FF_CHUNK:(j + 1) * FF_CHUNK])
    a = up(0)
    yield
    acc = None
    for j in range(n_ff):
        a_next = None
        if j + 1 < n_ff:
            a_next = up(j + 1)
            yield
        p = _dot(jnp.square(jnp.maximum(a, 0.0)).astype(BF16), w2_ref[j * FF_CHUNK:(j + 1) * FF_CHUNK, :])
        acc = p if acc is None else acc + p
        a = a_next
        yield
    return acc


def _mod_row_map(mod, units_per_seq):
    if mod.shape[0] == 1:
        return lambda unit: 0
    return lambda unit: unit // units_per_seq


def _resident(shape):
    return pl.BlockSpec(shape, lambda i: (0,) * len(shape), pipeline_mode=pl.Buffered(1))


class _LayerOf(NamedTuple):
    stacked: jax.Array
    layer: int


def _param_spec(p):
    if isinstance(p, _LayerOf):
        rest = p.stacked.shape[1:]
        return pl.BlockSpec((None,) + rest, lambda i: (p.layer,) + (0,) * len(rest), pipeline_mode=pl.Buffered(1))
    return _resident(p.shape)


def _param_arg(p):
    return p.stacked if isinstance(p, _LayerOf) else p


def _ada_kernel(cond_ref, w_ref, b_ref, out_ref):
    s = _silu(cond_ref[...]).astype(BF16)
    out_ref[0] = _dot(s, w_ref[0].astype(BF16)) + b_ref[0]


def _ada_call(cond, ada_w, ada_b):
    return pl.pallas_call(
        _ada_kernel,
        grid=(DEPTH, N_MOD // ADA_BLOCK_N),
        in_specs=[
            pl.BlockSpec((COND_ROWS, D_MODEL), lambda i, j: (0, 0)),
            pl.BlockSpec((1, D_MODEL, ADA_BLOCK_N), lambda i, j: (i, 0, j)),
            pl.BlockSpec((1, 1, ADA_BLOCK_N), lambda i, j: (i, 0, j)),
        ],
        out_specs=pl.BlockSpec((1, COND_ROWS, ADA_BLOCK_N), lambda i, j: (i, 0, j)),
        out_shape=jax.ShapeDtypeStruct((DEPTH, COND_ROWS, N_MOD), F32),
        compiler_params=pltpu.CompilerParams(
            dimension_semantics=("arbitrary", "arbitrary"),
            vmem_limit_bytes=V7X_VMEM_LIMIT_BYTES),
        name="ada_mod",
    )(cond, ada_w, ada_b.reshape(DEPTH, 1, N_MOD))


def _decay_tables(dec_ref, hd):
    row = lax.broadcasted_iota(jnp.int32, (CHUNK, CHUNK), 0).astype(F32)
    col = lax.broadcasted_iota(jnp.int32, (CHUNK, CHUNK), 1).astype(F32)
    lg_f = jnp.broadcast_to(-jnp.exp(dec_ref[hd:hd + 1, :]), (CHUNK, CHUNK))
    lg_b = jnp.broadcast_to(-jnp.exp(dec_ref[HEADS + hd:HEADS + hd + 1, :]), (CHUNK, CHUNK))
    d_f = row - col
    d_b = col - row
    return dict(
        decay_f=jnp.where(d_f >= 0, jnp.exp(lg_f * jnp.maximum(d_f, 0.0)), 0.0),
        decay_b=jnp.where(d_b >= 0, jnp.exp(lg_b * jnp.maximum(d_b, 0.0)), 0.0),
        qd_f=jnp.exp(lg_f * (row + 1.0)),
        qd_b=jnp.exp(lg_b * (CHUNK - row)),
        kd_f=jnp.exp(lg_f * (CHUNK - 1.0 - row)),
        kd_b=jnp.exp(lg_b * row),
        cd_f=jnp.exp(lg_f * CHUNK),
        cd_b=jnp.exp(lg_b * CHUNK),
    )


TABLE_NAMES = ("decay_f", "decay_b", "qd_f", "qd_b", "kd_f", "kd_b", "cd_f", "cd_b")


def _ret_state_kernel(x_ref, mod_ref, s0_ref, cos_ref, sin_ref, nmix_ref, wk_ref, wv_ref, dec_ref, sp_ref,
                      *, n_chunk):
    sh1, sc1 = _split_mod(mod_ref[0])[:2]
    hb = _rms_mod(x_ref[...], nmix_ref[...], sh1, sc1).astype(BF16)
    zk = _dot(hb, wk_ref[...]) * (HEAD_DIM ** -0.5)
    zv = _dot(hb, wv_ref[...]).astype(BF16)
    cos2, sin2 = cos_ref[...], sin_ref[...]
    for hd in range(HEADS):
        lo = hd * HEAD_DIM
        tb = _decay_tables(dec_ref, hd)
        kh = _rope(zk[:, lo:lo + HEAD_DIM], cos2, sin2)
        kv_f, kv_b = [], []
        for c in range(n_chunk):
            kc = kh[c * CHUNK:(c + 1) * CHUNK]
            vc = zv[c * CHUNK:(c + 1) * CHUNK, lo:lo + HEAD_DIM]
            kv_f.append(_dot((kc * tb["kd_f"]).T.astype(BF16), vc))
            kv_b.append(_dot((kc * tb["kd_b"]).T.astype(BF16), vc))
        st = s0_ref[0, 0, 0, hd]
        for c in range(n_chunk):
            sp_ref[c // UNIT_CHUNKS, c % UNIT_CHUNKS, 0, hd] = st.astype(BF16)
            st = tb["cd_f"] * st + kv_f[c]
        st = s0_ref[0, 0, 1, hd]
        for c in reversed(range(n_chunk)):
            sp_ref[c // UNIT_CHUNKS, c % UNIT_CHUNKS, 1, hd] = st.astype(BF16)
            st = tb["cd_b"] * st + kv_b[c]


def _ret_state_call(x2, mod, s0, rope, nmix, w_in, dec, *, n_seq, seq_len):
    n_chunk = seq_len // CHUNK
    units = seq_len // UNIT
    return pl.pallas_call(
        functools.partial(_ret_state_kernel, n_chunk=n_chunk),
        grid=(n_seq,),
        in_specs=[
            pl.BlockSpec((seq_len, D_MODEL), lambda i: (i, 0)),
            pl.BlockSpec((1, 1, N_MOD), lambda i: (i, 0, 0)),
            pl.BlockSpec((1, 1, 2, HEADS, HEAD_DIM, HEAD_DIM), lambda i: (i, 0, 0, 0, 0, 0)),
            _resident((seq_len, HEAD_DIM)), _resident((seq_len, HEAD_DIM)),
            _resident((1, D_MODEL)),
            pl.BlockSpec((D_MODEL, QK_W), lambda i: (0, K_COL // QK_W), pipeline_mode=pl.Buffered(1)),
            pl.BlockSpec((D_MODEL, V_W), lambda i: (0, V_COL // V_W), pipeline_mode=pl.Buffered(1)),
            _resident(dec.shape),
        ],
        out_specs=pl.BlockSpec((units, UNIT_CHUNKS, 2, HEADS, HEAD_DIM, HEAD_DIM), lambda i: (i, 0, 0, 0, 0, 0)),
        out_shape=jax.ShapeDtypeStruct((n_seq * units, UNIT_CHUNKS, 2, HEADS, HEAD_DIM, HEAD_DIM), BF16),
        compiler_params=pltpu.CompilerParams(
            dimension_semantics=("arbitrary",),
            vmem_limit_bytes=V7X_VMEM_LIMIT_BYTES),
        name="ret_state",
    )(x2, mod, s0, rope[0], rope[1], nmix, w_in, w_in, dec)


def _even_kernel(*refs, n_units, use_rope, scanned_states):
    it = iter(refs)
    xa_ref, xc_ref, moda_ref, modc_ref = (next(it) for _ in range(4))
    sp_ref = next(it) if scanned_states else None
    cos_ref = next(it) if use_rope else None
    sin_ref = next(it) if use_rope else None
    nmix_ref, nmlp_ref = next(it), next(it)
    w_in_ref, w_s_ref, b_s_ref, vgain_ref, dec_ref, w_out_ref, w1_ref, w2_ref = (next(it) for _ in range(8))
    y_ref = next(it)
    st_ref = None if scanned_states else next(it)
    ug_s, vg_s, q_s, k_s, v_s, gf_s, gb_s, cat_s, tab_s = (next(it) for _ in range(9))

    step = pl.program_id(0)
    slot_a = lax.rem(step, 2)
    slot_b = 1 - slot_a

    @pl.when(step == 0)
    def _():
        for hd in range(HEADS):
            tb = _decay_tables(dec_ref, hd)
            for n, name in enumerate(TABLE_NAMES):
                tab_s[hd, n] = tb[name]

    def stage_in():
        sh1, sc1 = _split_mod(moda_ref[0])[:2]
        hb = _rms_mod(xa_ref[...], nmix_ref[...], sh1, sc1).astype(BF16)
        z = _dot(hb, w_in_ref[...])
        yield
        ug_s[slot_a] = _gelu_tanh(z[:, 0:A_WIDTH])
        yield
        zv = _gelu_tanh(z[:, A_WIDTH:2 * A_WIDTH])
        vgn = [_layer_norm(zv[:, g * A_GROUP_DIM:(g + 1) * A_GROUP_DIM]) for g in range(A_GROUPS)]
        vg_s[slot_a] = (jnp.concatenate(vgn, axis=-1) * vgain_ref[...]).astype(BF16)
        yield
        zq = z[:, 2 * A_WIDTH:K_COL]
        zk = z[:, K_COL:V_COL] * (HEAD_DIM ** -0.5)
        if use_rope:
            cos2, sin2 = cos_ref[...], sin_ref[...]
            zq = jnp.concatenate([_rope(zq[:, h * HEAD_DIM:(h + 1) * HEAD_DIM], cos2, sin2) for h in range(HEADS)], -1)
            zk = jnp.concatenate([_rope(zk[:, h * HEAD_DIM:(h + 1) * HEAD_DIM], cos2, sin2) for h in range(HEADS)], -1)
        q_s[slot_a] = zq.astype(BF16)
        yield
        k_s[slot_a] = zk
        v_s[slot_a] = z[:, V_COL:V_COL + V_W].astype(BF16)
        yield
        gf_s[slot_a] = _silu(z[:, V_COL + V_W:V_COL + 2 * V_W])
        yield
        gb_s[slot_a] = _silu(z[:, V_COL + 2 * V_W:V_COL + 3 * V_W])

    IN_LIGHT = 6

    def stage_mix():
        s = slot_b
        rows = [slice(c * CHUNK, (c + 1) * CHUNK) for c in range(UNIT_CHUNKS)]
        head_cols = [slice(hd * HEAD_DIM, (hd + 1) * HEAD_DIM) for hd in range(HEADS)]
        table = lambda hd, name: tab_s[hd, TABLE_NAMES.index(name)]

        sp2 = []
        for g in range(A_GROUPS):
            lo = g * A_GROUP_DIM
            vg2 = jnp.concatenate([vg_s[s, r, lo:lo + A_GROUP_DIM] for r in rows], axis=-1)
            sp2.append(_dot(w_s_ref[g], vg2))
        yield
        scores, kv_f, kv_b = [], [], []
        for hd in range(HEADS):
            kc = [k_s[s, r, head_cols[hd]] for r in rows]
            scores.append([_dot_nt(q_s[s, r, head_cols[hd]], kc[c].astype(BF16)) for c, r in enumerate(rows)])
            if not scanned_states:
                vc = [v_s[s, r, head_cols[hd]] for r in rows]
                kv_f.append([_dot((kc[c] * table(hd, "kd_f")).T.astype(BF16), vc[c]) for c in range(UNIT_CHUNKS)])
                kv_b.append([_dot((kc[c] * table(hd, "kd_b")).T.astype(BF16), vc[c]) for c in range(UNIT_CHUNKS)])
            yield
        for g in range(A_GROUPS):
            lo = g * A_GROUP_DIM
            for c, r in enumerate(rows):
                gate = sp2[g][:, c * CHUNK:(c + 1) * CHUNK] + b_s_ref[g]
                cat_s[s, r, lo:lo + A_GROUP_DIM] = (ug_s[s, r, lo:lo + A_GROUP_DIM] * gate).astype(BF16)
            if g % 2 == 1:
                yield
        y_raw = []
        for hd in range(HEADS):
            if scanned_states:
                enter_f = [sp_ref[0, c, 0, hd] for c in range(UNIT_CHUNKS)]
                enter_b = [sp_ref[0, c, 1, hd] for c in range(UNIT_CHUNKS)]
            else:
                enter_f, enter_b = [None], [None] * UNIT_CHUNKS
                st = kv_f[hd][0]
                for c in range(1, UNIT_CHUNKS):
                    enter_f.append(st.astype(BF16))
                    st = table(hd, "cd_f") * st + kv_f[hd][c]
                st_ref[0, 0, 0, hd] = st
                st = kv_b[hd][UNIT_CHUNKS - 1]
                for c in reversed(range(UNIT_CHUNKS - 1)):
                    enter_b[c] = st.astype(BF16)
                    st = table(hd, "cd_b") * st + kv_b[hd][c]
                st_ref[0, 0, 1, hd] = st
            per_chunk = []
            for c, r in enumerate(rows):
                vc = v_s[s, r, head_cols[hd]]
                qf = q_s[s, r, head_cols[hd]].astype(F32)

                def direction(decay, qd, enter):
                    p = (scores[hd][c] * table(hd, decay)).astype(BF16)
                    if enter is None:
                        return _dot(p, vc)
                    lhs = jnp.concatenate([p, (qf * table(hd, qd)).astype(BF16)], axis=-1)
                    return _dot(lhs, jnp.concatenate([vc, enter], axis=0))

                per_chunk.append((direction("decay_f", "qd_f", enter_f[c]), direction("decay_b", "qd_b", enter_b[c])))
            y_raw.append(per_chunk)
            yield
        for hd in range(HEADS):
            for c, r in enumerate(rows):
                y_f, y_b = y_raw[hd][c]
                cols = slice(A_WIDTH + hd * HEAD_DIM, A_WIDTH + (hd + 1) * HEAD_DIM)
                cat_s[s, r, cols] = (gf_s[s, r, head_cols[hd]] * _layer_norm(y_f)
                                     + gb_s[s, r, head_cols[hd]] * _layer_norm(y_b)).astype(BF16)
            yield

    MIX_LIGHT = 1 + HEADS + A_GROUPS // 2 + 2 * HEADS

    def stage_mlp():
        _, _, g1, sh2, sc2, g2 = _split_mod(modc_ref[0])
        x = xc_ref[...] + g1 * _dot(cat_s[slot_a], w_out_ref[...])
        hb = _rms_mod(x, nmlp_ref[...], sh2, sc2).astype(BF16)
        yield
        acc = yield from _mlp_pieces(hb, w1_ref, w2_ref)
        y_ref[...] = x + g2 * acc

    @pl.when(step == 0)
    def _():
        _run(stage_in())

    @pl.when(step == 1)
    def _():
        a = stage_in()
        _weave(_pieces(a, 1), [a, stage_mix()], IN_LIGHT + MIX_LIGHT)

    @pl.when((step >= 2) & (step < n_units))
    def _():
        a, c = stage_in(), stage_mlp()
        heavy = _pieces(c, 1) + _pieces(a, 1) + _pieces(c, MLP_DOT_PIECES)
        _weave(heavy, [a, stage_mix()], IN_LIGHT + MIX_LIGHT)
        _run(c)

    @pl.when(step == n_units)
    def _():
        c = stage_mlp()
        _weave(_pieces(c, 1 + MLP_DOT_PIECES), [stage_mix()], MIX_LIGHT)
        _run(c)

    @pl.when(step == n_units + 1)
    def _():
        _run(stage_mlp())


def _even_call(x2, mod, sp, rope, params, *, seq_len):
    n_units = x2.shape[0] // UNIT
    units_per_seq = seq_len // UNIT
    scanned = sp is not None
    use_rope = rope is not None
    assert scanned or units_per_seq == 1
    last = n_units - 1
    unit_a = lambda i: jnp.minimum(i, last)
    unit_b = lambda i: jnp.clip(i - 1, 0, last)
    unit_c = lambda i: jnp.clip(i - 2, 0, last)
    mod_row = _mod_row_map(mod, units_per_seq)
    state_block = (1, UNIT_CHUNKS if scanned else 1, 2, HEADS, HEAD_DIM, HEAD_DIM)

    in_specs = [pl.BlockSpec((UNIT, D_MODEL), lambda i: (unit_a(i), 0)),
                pl.BlockSpec((UNIT, D_MODEL), lambda i: (unit_c(i), 0)),
                pl.BlockSpec((1, 1, N_MOD), lambda i: (mod_row(unit_a(i)), 0, 0)),
                pl.BlockSpec((1, 1, N_MOD), lambda i: (mod_row(unit_c(i)), 0, 0))]
    args = [x2, x2, mod, mod]
    if scanned:
        in_specs.append(pl.BlockSpec(state_block, lambda i: (unit_b(i), 0, 0, 0, 0, 0)))
        args.append(sp)
    if use_rope:
        in_specs += [pl.BlockSpec((UNIT, HEAD_DIM), lambda i: (unit_a(i) % units_per_seq, 0))] * 2
        args += list(rope)
    in_specs += [_param_spec(p) for p in params]
    args += [_param_arg(p) for p in params]

    out_shape = [jax.ShapeDtypeStruct(x2.shape, F32)]
    out_specs = [pl.BlockSpec((UNIT, D_MODEL), lambda i: (unit_c(i), 0))]
    if not scanned:
        out_shape.append(jax.ShapeDtypeStruct((n_units, 1, 2, HEADS, HEAD_DIM, HEAD_DIM), F32))
        out_specs.append(pl.BlockSpec(state_block, lambda i: (unit_b(i), 0, 0, 0, 0, 0)))

    scratch = [
        pltpu.VMEM((2, UNIT, A_WIDTH), F32),
        pltpu.VMEM((2, UNIT, A_WIDTH), BF16),
        pltpu.VMEM((2, UNIT, QK_W), BF16),
        pltpu.VMEM((2, UNIT, QK_W), F32),
        pltpu.VMEM((2, UNIT, V_W), BF16),
        pltpu.VMEM((2, UNIT, V_W), F32),
        pltpu.VMEM((2, UNIT, V_W), F32),
        pltpu.VMEM((2, UNIT, OUT_WIDTH), BF16),
        pltpu.VMEM((HEADS, len(TABLE_NAMES), CHUNK, CHUNK), F32),
    ]
    outs = pl.pallas_call(
        functools.partial(_even_kernel, n_units=n_units, use_rope=use_rope, scanned_states=scanned),
        grid=(n_units + 2,),
        in_specs=in_specs,
        out_specs=out_specs,
        out_shape=out_shape,
        scratch_shapes=scratch,
        compiler_params=pltpu.CompilerParams(
            dimension_semantics=("arbitrary",),
            vmem_limit_bytes=V7X_VMEM_LIMIT_BYTES),
        name="even_layer_rope" if use_rope else "even_layer",
    )(*args)
    return outs


def _odd_kernel(xp_ref, x_ref, xn_ref, moda_ref, modc_ref, nmix_ref, nmlp_ref, fin_ref, w_pool_ref, pscale_ref,
                w1_ref, w2_ref, y_ref, x1_s, hb_s, *, n_units, seq_len):
    step = pl.program_id(0)
    slot_a = lax.rem(step, 2)
    slot_b = 1 - slot_a
    rows_w = UNIT + 2 * POOL_HALO

    def stage_pool():
        sh1, sc1, g1, sh2, sc2, _ = _split_mod(moda_ref[0])
        xw = jnp.concatenate([xp_ref[...], x_ref[...], xn_ref[...]], axis=0)
        h = _rms_mod(xw, nmix_ref[...], sh1, sc1)
        first = jnp.minimum(step, n_units - 1) * UNIT - POOL_HALO
        t = (lax.broadcasted_iota(jnp.int32, (rows_w, 1), 0) + first) & (seq_len - 1)
        yield

        def shift_down(a, k):
            return jnp.where(t >= k, pltpu.roll(a, k, axis=0), 0.0)

        def shift_up(a, k):
            return jnp.where(t < seq_len - k, pltpu.roll(a, rows_w - k, axis=0), 0.0)

        outs = []
        for g, w in enumerate(POOL_WINDOWS):
            hg = h[:, g * POOL_DIM:(g + 1) * POOL_DIM]
            half = w // 2
            left = shift_down(hg, 1)
            right = hg
            span = 1
            while span < half:
                left = left + shift_down(left, span)
                right = right + shift_up(right, span)
                span *= 2
            cnt = (jnp.minimum(t + half, seq_len) - jnp.maximum(t - half, 0)).astype(F32)
            pooled = ((left + right) / cnt - hg)[POOL_HALO:POOL_HALO + UNIT].astype(BF16)
            outs.append(_dot(pooled, w_pool_ref[g]))
            yield
        x1 = x_ref[...] + g1 * (jnp.concatenate(outs, axis=-1) * pscale_ref[...])
        x1_s[slot_a] = x1
        hb_s[slot_a] = _rms_mod(x1, nmlp_ref[...], sh2, sc2).astype(BF16)

    def stage_mlp():
        g2 = _split_mod(modc_ref[0])[5]
        acc = yield from _mlp_pieces(hb_s[slot_b], w1_ref, w2_ref)
        y_ref[...] = _rms(x1_s[slot_b] + g2 * acc) * fin_ref[...]

    pool_light = 2 + len(POOL_WINDOWS)

    @pl.when(step == 0)
    def _():
        _run(stage_pool())

    @pl.when((step >= 1) & (step < n_units))
    def _():
        m = stage_mlp()
        _weave(_pieces(m, MLP_DOT_PIECES), [stage_pool()], pool_light)
        _run(m)

    @pl.when(step == n_units)
    def _():
        _run(stage_mlp())


def _odd_call(x2, mod, params, *, seq_len):
    n_units = x2.shape[0] // UNIT
    units_per_seq = seq_len // UNIT
    halo_per_unit = UNIT // POOL_HALO
    last = n_units - 1
    last_halo = x2.shape[0] // POOL_HALO - 1
    unit_a = lambda i: jnp.minimum(i, last)
    unit_c = lambda i: jnp.clip(i - 1, 0, last)
    mod_row = _mod_row_map(mod, units_per_seq)
    in_specs =[pl.BlockSpec((POOL_HALO, D_MODEL), lambda i: (jnp.maximum(unit_a(i) * halo_per_unit - 1, 0), 0)),
                pl.BlockSpec((UNIT, D_MODEL), lambda i: (unit_a(i), 0)),
                pl.BlockSpec((POOL_HALO, D_MODEL),
                             lambda i: (jnp.minimum((unit_a(i) + 1) * halo_per_unit, last_halo), 0)),
                pl.BlockSpec((1, 1, N_MOD), lambda i: (mod_row(unit_a(i)), 0, 0)),
                pl.BlockSpec((1, 1, N_MOD), lambda i: (mod_row(unit_c(i)), 0, 0))]
    in_specs += [_param_spec(p) for p in params]
    return pl.pallas_call(
        functools.partial(_odd_kernel, n_units=n_units, seq_len=seq_len),
        grid=(n_units + 1,),
        in_specs=in_specs,
        out_specs=pl.BlockSpec((UNIT, D_MODEL), lambda i: (unit_c(i), 0)),
        out_shape=jax.ShapeDtypeStruct(x2.shape, F32),
        scratch_shapes=[pltpu.VMEM((2, UNIT, D_MODEL), F32),
                        pltpu.VMEM((2, UNIT, D_MODEL), BF16)],
        compiler_params=pltpu.CompilerParams(
            dimension_semantics=("arbitrary",),
            vmem_limit_bytes=V7X_VMEM_LIMIT_BYTES),
        name="odd_layer",
    )(x2, x2, x2, mod, mod, *[_param_arg(p) for p in params])


def _rope_tables(n_tokens):
    rows = n_tokens // GRID_W
    pos_r = jnp.repeat(jnp.arange(rows, dtype=F32), GRID_W)
    pos_c = jnp.tile(jnp.arange(GRID_W, dtype=F32), rows)
    inv = ROPE_BASE ** (-jnp.arange(ROPE_PAIRS_AXIS, dtype=F32) / ROPE_PAIRS_AXIS)
    ang = jnp.concatenate([pos_r[:, None] * inv, pos_c[:, None] * inv], axis=-1)
    cos, sin = jnp.cos(ang), jnp.sin(ang)
    return jnp.concatenate([cos, cos], axis=-1), jnp.concatenate([-sin, sin], axis=-1)


def kernel(x_prompt, x_sample, state_ret, c, c_ctx, ada_w, ada_b, norm_mix, norm_mlp, mlp_w1, mlp_w2,
           ev_w_in, ev_w_s, ev_b_s, ev_vnorm, ev_decay, ev_w_out, od_w_pool, od_pool_scale, final_norm):
    n_prompt, prompt_len, _ = x_prompt.shape
    n_sample, sample_len, _ = x_sample.shape
    assert DEPTH == 2 and n_sample + 1 <= COND_ROWS
    assert prompt_len == UNIT and sample_len % UNIT == 0

    cond = jnp.zeros((COND_ROWS, D_MODEL), F32).at[0].set(c_ctx).at[1:1 + n_sample].set(c)
    mod = _ada_call(cond, ada_w, ada_b)
    mod_prompt = [mod[i, 0:1].reshape(1, 1, N_MOD) for i in range(DEPTH)]
    mod_sample = [mod[i, 1:1 + n_sample].reshape(n_sample, 1, N_MOD) for i in range(DEPTH)]

    row = lambda v: v.reshape(1, -1)
    w_in = ev_w_in[0].astype(BF16)
    w1, w2 = mlp_w1.astype(BF16), mlp_w2.astype(BF16)
    dec = jnp.broadcast_to(ev_decay[0].reshape(2 * HEADS, 1), (2 * HEADS, CHUNK))
    even_params = (
        row(norm_mix[0]), row(norm_mlp[0]),
        w_in, ev_w_s[0].astype(BF16),
        jnp.broadcast_to(ev_b_s[0][:, :, None], (A_GROUPS, CHUNK, A_GROUP_DIM)),
        row(ev_vnorm[0]), dec,
        ev_w_out[0].astype(BF16),
        _LayerOf(w1, 0), _LayerOf(w2, 0),
    )
    odd_params = (
        row(norm_mix[1]), row(norm_mlp[1]), row(final_norm),
        od_w_pool[0].astype(BF16), row(od_pool_scale[0]),
        _LayerOf(w1, 1), _LayerOf(w2, 1),
    )

    xp2 = x_prompt.reshape(n_prompt * prompt_len, D_MODEL)
    xp2, new_state = _even_call(xp2, mod_prompt[0], None, None, even_params, seq_len=prompt_len)
    y_prompt = _odd_call(xp2, mod_prompt[1], odd_params, seq_len=prompt_len)

    xs2 = x_sample.reshape(n_sample * sample_len, D_MODEL)
    rope = _rope_tables(sample_len)
    sp = _ret_state_call(xs2, mod_sample[0], state_ret, rope, row(norm_mix[0]), w_in, dec,
                         n_seq=n_sample, seq_len=sample_len)
    (xs2,) = _even_call(xs2, mod_sample[0], sp, rope, even_params, seq_len=sample_len)
    y_sample = _odd_call(xs2, mod_sample[1], odd_params, seq_len=sample_len)

    return (y_prompt.reshape(n_prompt, prompt_len, D_MODEL),
            y_sample.reshape(n_sample, sample_len, D_MODEL),
            new_state)
```

```python
import functools
from typing import NamedTuple

import jax
import jax.numpy as jnp
from jax import lax
from jax.experimental import pallas as pl
from jax.experimental.pallas import tpu as pltpu

D_MODEL = 1024
DEPTH = 2
GRID_W = 64
A_WIDTH = D_MODEL // 2
A_GROUPS = 4
A_GROUP_DIM = A_WIDTH // A_GROUPS
CHUNK = 128
HEADS = 4
HEAD_DIM = (D_MODEL // 2) // HEADS
ROPE_BASE = 10000.0
ROPE_PAIRS_AXIS = HEAD_DIM // 4
POOL_WINDOWS = (2, 4, 8, 16)
POOL_DIM = D_MODEL // len(POOL_WINDOWS)
POOL_HALO = max(POOL_WINDOWS) // 2
D_FF = 4 * D_MODEL
EPS = 1e-6
QK_W = HEADS * HEAD_DIM
V_W = HEADS * HEAD_DIM
IN_WIDTH = 2 * A_WIDTH + 2 * QK_W + 3 * V_W
OUT_WIDTH = A_WIDTH + V_W
N_MOD = 6 * D_MODEL
K_COL = 2 * A_WIDTH + QK_W
V_COL = K_COL + QK_W

UNIT = 2 * CHUNK
UNIT_CHUNKS = UNIT // CHUNK
FF_CHUNK = 1024
ADA_BLOCK_N = 1536
ADA_PREP_BLOCK_N = 256
BF16_SUBLANES = 16
COND_ROWS = 8
SUBLANES = 8
V7X_VMEM_LIMIT_BYTES = 60 * 1024 * 1024

F32 = jnp.float32
BF16 = jnp.bfloat16


def _dot(a, b):
    return jnp.dot(a, b, preferred_element_type=F32)


def _dot_nt(a, b):
    return lax.dot_general(a, b, (((1,), (1,)), ((), ())), preferred_element_type=F32)


def _silu(x):
    return x * jax.nn.sigmoid(x)


def _gelu_tanh(x):
    return 0.5 * x * (1.0 + jnp.tanh(0.7978845608028654 * (x + 0.044715 * (x * x * x))))


def _rms(x):
    return x * lax.rsqrt(jnp.mean(x * x, axis=-1, keepdims=True) + EPS)


def _rms_mod(x, gain, shift, scale):
    return (_rms(x) * gain) * (1.0 + scale) + shift


def _layer_norm(x):
    mu = jnp.mean(x, axis=-1, keepdims=True)
    d = x - mu
    return d * lax.rsqrt(jnp.mean(d * d, axis=-1, keepdims=True) + EPS)


def _split_mod(mod):
    return [mod[:, i * D_MODEL:(i + 1) * D_MODEL] for i in range(6)]


def _rope(x, cos2, sin2):
    return x * cos2 + pltpu.roll(x, HEAD_DIM // 2, axis=1) * sin2


def _run(gen, n=None):
    done = 0
    while n is None or done < n:
        try:
            next(gen)
        except StopIteration:
            break
        done += 1
    return done


def _weave(heavy, lights, n_light):
    heavy = list(heavy)
    pending = _roundrobin(lights)
    ran = 0
    for k, piece in enumerate(heavy):
        piece()
        want = ((k + 1) * n_light + len(heavy) - 1) // len(heavy)
        ran += _run(pending, want - ran)
    _run(pending)


def _pieces(gen, n):
    return [functools.partial(next, gen, None) for _ in range(n)]


def _roundrobin(gens):
    live = list(gens)
    while live:
        for g in list(live):
            try:
                next(g)
            except StopIteration:
                live.remove(g)
                continue
            yield


MLP_DOT_PIECES = 2 * (D_FF // FF_CHUNK)


def _mlp_pieces(hb, w1_ref, w2_ref):
    n_ff = D_FF // FF_CHUNK
    up = lambda j: _dot(hb, w1_ref[:, j * FF_CHUNK:(j + 1) * FF_CHUNK])
    a = up(0)
    yield
    acc = None
    for j in range(n_ff):
        a_next = None
        if j + 1 < n_ff:
            a_next = up(j + 1)
            yield
        p = _dot(jnp.square(jnp.maximum(a, 0.0)).astype(BF16), w2_ref[j * FF_CHUNK:(j + 1) * FF_CHUNK, :])
        acc = p if acc is None else acc + p
        a = a_next
        yield
    return acc


def _mod_row_map(mod, units_per_seq):
    if mod.shape[0] == 1:
        return lambda unit: 0
    return lambda unit: unit // units_per_seq


def _resident(shape):
    return pl.BlockSpec(shape, lambda i: (0,) * len(shape), pipeline_mode=pl.Buffered(1))


def _ada_kernel(cond_ref, w_ref, b_ref, out_ref):
    s = _silu(cond_ref[...]).astype(BF16)
    out_ref[0] = _dot(s, w_ref[0].astype(BF16)) + b_ref[0]


def _ada_call(cond, ada_w, ada_b, *, n_layers):
    return pl.pallas_call(
        _ada_kernel,
        grid=(n_layers, N_MOD // ADA_BLOCK_N),
        in_specs=[
            pl.BlockSpec((COND_ROWS, D_MODEL), lambda i, j: (0, 0)),
            pl.BlockSpec((1, D_MODEL, ADA_BLOCK_N), lambda i, j: (i, 0, j)),
            pl.BlockSpec((1, 1, ADA_BLOCK_N), lambda i, j: (i, 0, j)),
        ],
        out_specs=pl.BlockSpec((1, COND_ROWS, ADA_BLOCK_N), lambda i, j: (i, 0, j)),
        out_shape=jax.ShapeDtypeStruct((n_layers, COND_ROWS, N_MOD), F32),
        compiler_params=pltpu.CompilerParams(
            dimension_semantics=("arbitrary", "arbitrary"),
            vmem_limit_bytes=V7X_VMEM_LIMIT_BYTES),
        name="ada_mod",
    )(cond, ada_w, ada_b)


def _decay_tables(dec_ref, hd):
    row = lax.broadcasted_iota(jnp.int32, (CHUNK, CHUNK), 0).astype(F32)
    col = lax.broadcasted_iota(jnp.int32, (CHUNK, CHUNK), 1).astype(F32)
    lg_f = jnp.broadcast_to(-jnp.exp(dec_ref[hd:hd + 1, :]), (CHUNK, CHUNK))
    lg_b = jnp.broadcast_to(-jnp.exp(dec_ref[HEADS + hd:HEADS + hd + 1, :]), (CHUNK, CHUNK))
    d_f = row - col
    d_b = col - row
    return dict(
        decay_f=jnp.where(d_f >= 0, jnp.exp(lg_f * jnp.maximum(d_f, 0.0)), 0.0),
        decay_b=jnp.where(d_b >= 0, jnp.exp(lg_b * jnp.maximum(d_b, 0.0)), 0.0),
        qd_f=jnp.exp(lg_f * (row + 1.0)),
        qd_b=jnp.exp(lg_b * (CHUNK - row)),
        kd_f=jnp.exp(lg_f * (CHUNK - 1.0 - row)),
        kd_b=jnp.exp(lg_b * row),
        cd_f=jnp.exp(lg_f * CHUNK),
        cd_b=jnp.exp(lg_b * CHUNK),
    )


TABLE_NAMES = ("decay_f", "decay_b", "qd_f", "qd_b", "kd_f", "kd_b", "cd_f", "cd_b")


def _ret_state_kernel(x_ref, mod_ref, s0_ref, cos_ref, sin_ref, nmix_ref, wk_ref, wv_ref, dec_ref, sp_ref,
                      *, n_chunk):
    sh1, sc1 = _split_mod(mod_ref[0])[:2]
    hb = _rms_mod(x_ref[...], nmix_ref[...], sh1, sc1).astype(BF16)
    zk = _dot(hb, wk_ref[...]) * (HEAD_DIM ** -0.5)
    zv = _dot(hb, wv_ref[...]).astype(BF16)
    cos2, sin2 = cos_ref[...], sin_ref[...]
    for hd in range(HEADS):
        lo = hd * HEAD_DIM
        tb = _decay_tables(dec_ref, hd)
        kh = _rope(zk[:, lo:lo + HEAD_DIM], cos2, sin2)
        kv_f, kv_b = [], []
        for c in range(n_chunk):
            kc = kh[c * CHUNK:(c + 1) * CHUNK]
            vc = zv[c * CHUNK:(c + 1) * CHUNK, lo:lo + HEAD_DIM]
            kv_f.append(_dot((kc * tb["kd_f"]).T.astype(BF16), vc))
            kv_b.append(_dot((kc * tb["kd_b"]).T.astype(BF16), vc))
        st = s0_ref[0, 0, 0, hd]
        for c in range(n_chunk):
            sp_ref[c // UNIT_CHUNKS, c % UNIT_CHUNKS, 0, hd] = st.astype(BF16)
            st = tb["cd_f"] * st + kv_f[c]
        st = s0_ref[0, 0, 1, hd]
        for c in reversed(range(n_chunk)):
            sp_ref[c // UNIT_CHUNKS, c % UNIT_CHUNKS, 1, hd] = st.astype(BF16)
            st = tb["cd_b"] * st + kv_b[c]


def _ret_state_call(x2, mod, s0, rope, nmix, w_in, dec, *, n_seq, seq_len):
    n_chunk = seq_len // CHUNK
    units = seq_len // UNIT
    return pl.pallas_call(
        functools.partial(_ret_state_kernel, n_chunk=n_chunk),
        grid=(n_seq,),
        in_specs=[
            pl.BlockSpec((seq_len, D_MODEL), lambda i: (i, 0)),
            pl.BlockSpec((1, 1, N_MOD), lambda i: (i, 0, 0)),
            pl.BlockSpec((1, 1, 2, HEADS, HEAD_DIM, HEAD_DIM), lambda i: (i, 0, 0, 0, 0, 0)),
            _resident((seq_len, HEAD_DIM)), _resident((seq_len, HEAD_DIM)),
            _resident((1, D_MODEL)),
            pl.BlockSpec((D_MODEL, QK_W), lambda i: (0, K_COL // QK_W), pipeline_mode=pl.Buffered(1)),
            pl.BlockSpec((D_MODEL, V_W), lambda i: (0, V_COL // V_W), pipeline_mode=pl.Buffered(1)),
            _resident(dec.shape),
        ],
        out_specs=pl.BlockSpec((units, UNIT_CHUNKS, 2, HEADS, HEAD_DIM, HEAD_DIM), lambda i: (i, 0, 0, 0, 0, 0)),
        out_shape=jax.ShapeDtypeStruct((n_seq * units, UNIT_CHUNKS, 2, HEADS, HEAD_DIM, HEAD_DIM), BF16),
        compiler_params=pltpu.CompilerParams(
            dimension_semantics=("arbitrary",),
            vmem_limit_bytes=V7X_VMEM_LIMIT_BYTES),
        name="ret_state",
    )(x2, mod, s0, rope[0], rope[1], nmix, w_in, w_in, dec)


def _even_kernel(*refs, n_units, use_rope, scanned_states, next_layer_prep):
    it = iter(refs)
    xa_ref, xc_ref, moda_ref, modc_ref = (next(it) for _ in range(4))
    sp_ref = next(it) if scanned_states else None
    cos_ref = next(it) if use_rope else None
    sin_ref = next(it) if use_rope else None
    nmix_ref, nmlp_ref = next(it), next(it)
    w_in_ref, w_s_ref, b_s_ref, vgain_ref, dec_ref, w_out_ref, w1_ref, w2_ref = (next(it) for _ in range(8))
    if next_layer_prep:
        w1n_ref, w2n_ref, cond_ref, adaw_ref, adab_ref = (next(it) for _ in range(5))
    y_ref = next(it)
    st_ref = None if scanned_states else next(it)
    if next_layer_prep:
        w1n_out, w2n_out, modn_out = (next(it) for _ in range(3))
    ug_s, vg_s, q_s, k_s, v_s, gf_s, gb_s, cat_s, tab_s = (next(it) for _ in range(9))

    step = pl.program_id(0)
    slot_a = lax.rem(step, 2)
    slot_b = 1 - slot_a

    @pl.when(step == 0)
    def _():
        for hd in range(HEADS):
            tb = _decay_tables(dec_ref, hd)
            for n, name in enumerate(TABLE_NAMES):
                tab_s[hd, n] = tb[name]

    def stage_in():
        sh1, sc1 = _split_mod(moda_ref[0])[:2]
        hb = _rms_mod(xa_ref[...], nmix_ref[...], sh1, sc1).astype(BF16)
        z = _dot(hb, w_in_ref[...])
        yield
        ug_s[slot_a] = _gelu_tanh(z[:, 0:A_WIDTH])
        yield
        zv = _gelu_tanh(z[:, A_WIDTH:2 * A_WIDTH])
        vgn = [_layer_norm(zv[:, g * A_GROUP_DIM:(g + 1) * A_GROUP_DIM]) for g in range(A_GROUPS)]
        vg_s[slot_a] = (jnp.concatenate(vgn, axis=-1) * vgain_ref[...]).astype(BF16)
        yield
        zq = z[:, 2 * A_WIDTH:K_COL]
        zk = z[:, K_COL:V_COL] * (HEAD_DIM ** -0.5)
        if use_rope:
            cos2, sin2 = cos_ref[...], sin_ref[...]
            zq = jnp.concatenate([_rope(zq[:, h * HEAD_DIM:(h + 1) * HEAD_DIM], cos2, sin2) for h in range(HEADS)], -1)
            zk = jnp.concatenate([_rope(zk[:, h * HEAD_DIM:(h + 1) * HEAD_DIM], cos2, sin2) for h in range(HEADS)], -1)
        q_s[slot_a] = zq.astype(BF16)
        yield
        k_s[slot_a] = zk
        v_s[slot_a] = z[:, V_COL:V_COL + V_W].astype(BF16)
        yield
        gf_s[slot_a] = _silu(z[:, V_COL + V_W:V_COL + 2 * V_W])
        yield
        gb_s[slot_a] = _silu(z[:, V_COL + 2 * V_W:V_COL + 3 * V_W])
        if next_layer_prep:
            yield
            w1n_out[...] = w1n_ref[...].astype(BF16)
            w2n_out[...] = w2n_ref[...].astype(BF16)
            modn_out[...] = _dot(_silu(cond_ref[...]).astype(BF16), adaw_ref[...].astype(BF16)) + adab_ref[...]

    in_light = 7 if next_layer_prep else 6

    def stage_mix():
        s = slot_b
        rows = [slice(c * CHUNK, (c + 1) * CHUNK) for c in range(UNIT_CHUNKS)]
        head_cols = [slice(hd * HEAD_DIM, (hd + 1) * HEAD_DIM) for hd in range(HEADS)]
        table = lambda hd, name: tab_s[hd, TABLE_NAMES.index(name)]

        sp2 = []
        for g in range(A_GROUPS):
            lo = g * A_GROUP_DIM
            vg2 = jnp.concatenate([vg_s[s, r, lo:lo + A_GROUP_DIM] for r in rows], axis=-1)
            sp2.append(_dot(w_s_ref[g], vg2))
        yield
        scores, kv_f, kv_b = [], [], []
        for hd in range(HEADS):
            kc = [k_s[s, r, head_cols[hd]] for r in rows]
            scores.append([_dot_nt(q_s[s, r, head_cols[hd]], kc[c].astype(BF16)) for c, r in enumerate(rows)])
            if not scanned_states:
                vc = [v_s[s, r, head_cols[hd]] for r in rows]
                kv_f.append([_dot((kc[c] * table(hd, "kd_f")).T.astype(BF16), vc[c]) for c in range(UNIT_CHUNKS)])
                kv_b.append([_dot((kc[c] * table(hd, "kd_b")).T.astype(BF16), vc[c]) for c in range(UNIT_CHUNKS)])
            yield
        for g in range(A_GROUPS):
            lo = g * A_GROUP_DIM
            for c, r in enumerate(rows):
                gate = sp2[g][:, c * CHUNK:(c + 1) * CHUNK] + b_s_ref[g]
                cat_s[s, r, lo:lo + A_GROUP_DIM] = (ug_s[s, r, lo:lo + A_GROUP_DIM] * gate).astype(BF16)
            if g % 2 == 1:
                yield
        y_raw = []
        for hd in range(HEADS):
            if scanned_states:
                enter_f = [sp_ref[0, c, 0, hd] for c in range(UNIT_CHUNKS)]
                enter_b = [sp_ref[0, c, 1, hd] for c in range(UNIT_CHUNKS)]
            else:
                enter_f, enter_b = [None], [None] * UNIT_CHUNKS
                st = kv_f[hd][0]
                for c in range(1, UNIT_CHUNKS):
                    enter_f.append(st.astype(BF16))
                    st = table(hd, "cd_f") * st + kv_f[hd][c]
                st_ref[0, 0, 0, hd] = st
                st = kv_b[hd][UNIT_CHUNKS - 1]
                for c in reversed(range(UNIT_CHUNKS - 1)):
                    enter_b[c] = st.astype(BF16)
                    st = table(hd, "cd_b") * st + kv_b[hd][c]
                st_ref[0, 0, 1, hd] = st
            per_chunk = []
            for c, r in enumerate(rows):
                vc = v_s[s, r, head_cols[hd]]
                qf = q_s[s, r, head_cols[hd]].astype(F32)

                def direction(decay, qd, enter):
                    p = (scores[hd][c] * table(hd, decay)).astype(BF16)
                    if enter is None:
                        return _dot(p, vc)
                    lhs = jnp.concatenate([p, (qf * table(hd, qd)).astype(BF16)], axis=-1)
                    return _dot(lhs, jnp.concatenate([vc, enter], axis=0))

                per_chunk.append((direction("decay_f", "qd_f", enter_f[c]), direction("decay_b", "qd_b", enter_b[c])))
            y_raw.append(per_chunk)
            yield
        for hd in range(HEADS):
            for c, r in enumerate(rows):
                y_f, y_b = y_raw[hd][c]
                cols = slice(A_WIDTH + hd * HEAD_DIM, A_WIDTH + (hd + 1) * HEAD_DIM)
                cat_s[s, r, cols] = (gf_s[s, r, head_cols[hd]] * _layer_norm(y_f)
                                     + gb_s[s, r, head_cols[hd]] * _layer_norm(y_b)).astype(BF16)
            yield

    MIX_LIGHT = 1 + HEADS + A_GROUPS // 2 + 2 * HEADS

    def stage_mlp():
        _, _, g1, sh2, sc2, g2 = _split_mod(modc_ref[0])
        x = xc_ref[...] + g1 * _dot(cat_s[slot_a], w_out_ref[...])
        hb = _rms_mod(x, nmlp_ref[...], sh2, sc2).astype(BF16)
        yield
        acc = yield from _mlp_pieces(hb, w1_ref, w2_ref)
        y_ref[...] = x + g2 * acc

    @pl.when(step == 0)
    def _():
        _run(stage_in())

    @pl.when(step == 1)
    def _():
        a = stage_in()
        _weave(_pieces(a, 1), [a, stage_mix()], in_light + MIX_LIGHT)

    @pl.when((step >= 2) & (step < n_units))
    def _():
        a, c = stage_in(), stage_mlp()
        heavy = _pieces(c, 1) + _pieces(a, 1) + _pieces(c, MLP_DOT_PIECES + 1)
        _weave(heavy, [a, stage_mix()], in_light + MIX_LIGHT)

    @pl.when(step == n_units)
    def _():
        _weave(_pieces(stage_mlp(), MLP_DOT_PIECES + 2), [stage_mix()], MIX_LIGHT)

    @pl.when(step == n_units + 1)
    def _():
        _run(stage_mlp())


class _NextLayerPrep(NamedTuple):
    layer: int
    mlp_w1: jax.Array
    mlp_w2: jax.Array
    cond: jax.Array
    ada_w: jax.Array
    ada_b: jax.Array


def _even_call(x2, mod, sp, rope, params, *, seq_len, prep=None):
    n_units = x2.shape[0] // UNIT
    units_per_seq = seq_len // UNIT
    scanned = sp is not None
    use_rope = rope is not None
    assert scanned or units_per_seq == 1
    last = n_units - 1
    unit_a = lambda i: jnp.minimum(i, last)
    unit_b = lambda i: jnp.clip(i - 1, 0, last)
    unit_c = lambda i: jnp.clip(i - 2, 0, last)
    mod_row = _mod_row_map(mod, units_per_seq)
    state_block = (1, UNIT_CHUNKS if scanned else 1, 2, HEADS, HEAD_DIM, HEAD_DIM)

    in_specs = [pl.BlockSpec((UNIT, D_MODEL), lambda i: (unit_a(i), 0)),
                pl.BlockSpec((UNIT, D_MODEL), lambda i: (unit_c(i), 0)),
                pl.BlockSpec((1, 1, N_MOD), lambda i: (mod_row(unit_a(i)), 0, 0)),
                pl.BlockSpec((1, 1, N_MOD), lambda i: (mod_row(unit_c(i)), 0, 0))]
    args = [x2, x2, mod, mod]
    if scanned:
        in_specs.append(pl.BlockSpec(state_block, lambda i: (unit_b(i), 0, 0, 0, 0, 0)))
        args.append(sp)
    if use_rope:
        in_specs += [pl.BlockSpec((UNIT, HEAD_DIM), lambda i: (unit_a(i) % units_per_seq, 0))] * 2
        args += list(rope)
    in_specs += [_resident(p.shape) for p in params]
    args += list(params)

    out_shape = [jax.ShapeDtypeStruct(x2.shape, F32)]
    out_specs = [pl.BlockSpec((UNIT, D_MODEL), lambda i: (unit_c(i), 0))]
    if not scanned:
        out_shape.append(jax.ShapeDtypeStruct((n_units, 1, 2, HEADS, HEAD_DIM, HEAD_DIM), F32))
        out_specs.append(pl.BlockSpec(state_block, lambda i: (unit_b(i), 0, 0, 0, 0, 0)))
    if prep is not None:
        w1_rows, w2_rows = D_MODEL // n_units, D_FF // n_units
        assert w1_rows * n_units == D_MODEL and w1_rows % BF16_SUBLANES == 0
        mod_block = lambda i: jnp.minimum(unit_a(i), N_MOD // ADA_PREP_BLOCK_N - 1)
        in_specs += [
            pl.BlockSpec((None, w1_rows, D_FF), lambda i: (prep.layer, unit_a(i), 0)),
            pl.BlockSpec((None, w2_rows, D_MODEL), lambda i: (prep.layer, unit_a(i), 0)),
            _resident((COND_ROWS, D_MODEL)),
            pl.BlockSpec((None, D_MODEL, ADA_PREP_BLOCK_N), lambda i: (prep.layer, 0, mod_block(i))),
            pl.BlockSpec((None, 1, ADA_PREP_BLOCK_N), lambda i: (prep.layer, 0, mod_block(i))),
        ]
        args += [prep.mlp_w1, prep.mlp_w2, prep.cond, prep.ada_w, prep.ada_b]
        out_shape += [jax.ShapeDtypeStruct((D_MODEL, D_FF), BF16), jax.ShapeDtypeStruct((D_FF, D_MODEL), BF16),
                      jax.ShapeDtypeStruct((COND_ROWS, N_MOD), F32)]
        out_specs += [pl.BlockSpec((w1_rows, D_FF), lambda i: (unit_a(i), 0)),
                      pl.BlockSpec((w2_rows, D_MODEL), lambda i: (unit_a(i), 0)),
                      pl.BlockSpec((COND_ROWS, ADA_PREP_BLOCK_N), lambda i: (0, mod_block(i)))]

    scratch = [
        pltpu.VMEM((2, UNIT, A_WIDTH), F32),
        pltpu.VMEM((2, UNIT, A_WIDTH), BF16),
        pltpu.VMEM((2, UNIT, QK_W), BF16),
        pltpu.VMEM((2, UNIT, QK_W), F32),
        pltpu.VMEM((2, UNIT, V_W), BF16),
        pltpu.VMEM((2, UNIT, V_W), F32),
        pltpu.VMEM((2, UNIT, V_W), F32),
        pltpu.VMEM((2, UNIT, OUT_WIDTH), BF16),
        pltpu.VMEM((HEADS, len(TABLE_NAMES), CHUNK, CHUNK), F32),
    ]
    outs = pl.pallas_call(
        functools.partial(_even_kernel, n_units=n_units, use_rope=use_rope, scanned_states=scanned,
                          next_layer_prep=prep is not None),
        grid=(n_units + 2,),
        in_specs=in_specs,
        out_specs=out_specs,
        out_shape=out_shape,
        scratch_shapes=scratch,
        compiler_params=pltpu.CompilerParams(
            dimension_semantics=("arbitrary",),
            vmem_limit_bytes=V7X_VMEM_LIMIT_BYTES),
        name="even_layer_rope" if use_rope else "even_layer",
    )(*args)
    return outs


def _odd_kernel(xp_ref, x_ref, xn_ref, moda_ref, modc_ref, nmix_ref, nmlp_ref, fin_ref, w_pool_ref, pscale_ref,
                w1_ref, w2_ref, y_ref, x1_s, hb_s, *, n_units, seq_len):
    step = pl.program_id(0)
    slot_a = lax.rem(step, 2)
    slot_b = 1 - slot_a
    rows_w = UNIT + 2 * POOL_HALO

    def stage_pool():
        sh1, sc1, g1, sh2, sc2, _ = _split_mod(moda_ref[0])
        xw = jnp.concatenate([xp_ref[...], x_ref[...], xn_ref[...]], axis=0)
        h = _rms_mod(xw, nmix_ref[...], sh1, sc1)
        first = jnp.minimum(step, n_units - 1) * UNIT - POOL_HALO
        t = (lax.broadcasted_iota(jnp.int32, (rows_w, 1), 0) + first) & (seq_len - 1)
        yield

        def shift_down(a, k):
            return jnp.where(t >= k, pltpu.roll(a, k, axis=0), 0.0)

        def shift_up(a, k):
            return jnp.where(t < seq_len - k, pltpu.roll(a, rows_w - k, axis=0), 0.0)

        outs = []
        for g, w in enumerate(POOL_WINDOWS):
            hg = h[:, g * POOL_DIM:(g + 1) * POOL_DIM]
            half = w // 2
            left = shift_down(hg, 1)
            right = hg
            span = 1
            while span < half:
                left = left + shift_down(left, span)
                right = right + shift_up(right, span)
                span *= 2
            cnt = (jnp.minimum(t + half, seq_len) - jnp.maximum(t - half, 0)).astype(F32)
            pooled = ((left + right) / cnt - hg)[POOL_HALO:POOL_HALO + UNIT].astype(BF16)
            outs.append(_dot(pooled, w_pool_ref[g]))
            yield
        x1 = x_ref[...] + g1 * (jnp.concatenate(outs, axis=-1) * pscale_ref[...])
        x1_s[slot_a] = x1
        hb_s[slot_a] = _rms_mod(x1, nmlp_ref[...], sh2, sc2).astype(BF16)

    def stage_mlp():
        g2 = _split_mod(modc_ref[0])[5]
        acc = yield from _mlp_pieces(hb_s[slot_b], w1_ref, w2_ref)
        y_ref[...] = _rms(x1_s[slot_b] + g2 * acc) * fin_ref[...]

    pool_light = 2 + len(POOL_WINDOWS)

    @pl.when(step == 0)
    def _():
        _run(stage_pool())

    @pl.when((step >= 1) & (step < n_units))
    def _():
        _weave(_pieces(stage_mlp(), MLP_DOT_PIECES + 1), [stage_pool()], pool_light)

    @pl.when(step == n_units)
    def _():
        _run(stage_mlp())


def _odd_call(x2, mod, params, *, seq_len):
    n_units = x2.shape[0] // UNIT
    units_per_seq = seq_len // UNIT
    halo_per_unit = UNIT // POOL_HALO
    last = n_units - 1
    last_halo = x2.shape[0] // POOL_HALO - 1
    unit_a = lambda i: jnp.minimum(i, last)
    unit_c = lambda i: jnp.clip(i - 1, 0, last)
    mod_row = _mod_row_map(mod, units_per_seq)
    in_specs =[pl.BlockSpec((POOL_HALO, D_MODEL), lambda i: (jnp.maximum(unit_a(i) * halo_per_unit - 1, 0), 0)),
                pl.BlockSpec((UNIT, D_MODEL), lambda i: (unit_a(i), 0)),
                pl.BlockSpec((POOL_HALO, D_MODEL),
                             lambda i: (jnp.minimum((unit_a(i) + 1) * halo_per_unit, last_halo), 0)),
                pl.BlockSpec((1, 1, N_MOD), lambda i: (mod_row(unit_a(i)), 0, 0)),
                pl.BlockSpec((1, 1, N_MOD), lambda i: (mod_row(unit_c(i)), 0, 0))]
    in_specs += [_resident(p.shape) for p in params]
    return pl.pallas_call(
        functools.partial(_odd_kernel, n_units=n_units, seq_len=seq_len),
        grid=(n_units + 1,),
        in_specs=in_specs,
        out_specs=pl.BlockSpec((UNIT, D_MODEL), lambda i: (unit_c(i), 0)),
        out_shape=jax.ShapeDtypeStruct(x2.shape, F32),
        scratch_shapes=[pltpu.VMEM((2, UNIT, D_MODEL), F32),
                        pltpu.VMEM((2, UNIT, D_MODEL), BF16)],
        compiler_params=pltpu.CompilerParams(
            dimension_semantics=("arbitrary",),
            vmem_limit_bytes=V7X_VMEM_LIMIT_BYTES),
        name="odd_layer",
    )(x2, x2, x2, mod, mod, *params)


def _rope_tables(n_tokens):
    rows = n_tokens // GRID_W
    pos_r = jnp.repeat(jnp.arange(rows, dtype=F32), GRID_W)
    pos_c = jnp.tile(jnp.arange(GRID_W, dtype=F32), rows)
    inv = ROPE_BASE ** (-jnp.arange(ROPE_PAIRS_AXIS, dtype=F32) / ROPE_PAIRS_AXIS)
    ang = jnp.concatenate([pos_r[:, None] * inv, pos_c[:, None] * inv], axis=-1)
    cos, sin = jnp.cos(ang), jnp.sin(ang)
    return jnp.concatenate([cos, cos], axis=-1), jnp.concatenate([-sin, sin], axis=-1)


def kernel(x_prompt, x_sample, state_ret, c, c_ctx, ada_w, ada_b, norm_mix, norm_mlp, mlp_w1, mlp_w2,
           ev_w_in, ev_w_s, ev_b_s, ev_vnorm, ev_decay, ev_w_out, od_w_pool, od_pool_scale, final_norm):
    n_prompt, prompt_len, _ = x_prompt.shape
    n_sample, sample_len, _ = x_sample.shape
    assert DEPTH == 2 and n_sample + 1 <= COND_ROWS
    assert prompt_len == UNIT and sample_len % UNIT == 0

    cond = jnp.zeros((COND_ROWS, D_MODEL), F32).at[0].set(c_ctx).at[1:1 + n_sample].set(c)
    ada_b3 = ada_b.reshape(DEPTH, 1, N_MOD)
    mod0 = _ada_call(cond, ada_w, ada_b3, n_layers=1)[0]
    mod_rows = lambda m: (m[0:1].reshape(1, 1, N_MOD), m[1:1 + n_sample].reshape(n_sample, 1, N_MOD))

    row = lambda v: v.reshape(1, -1)
    w_in = ev_w_in[0].astype(BF16)
    dec = jnp.broadcast_to(ev_decay[0].reshape(2 * HEADS, 1), (2 * HEADS, CHUNK))
    even_params = (
        row(norm_mix[0]), row(norm_mlp[0]),
        w_in, ev_w_s[0].astype(BF16),
        jnp.broadcast_to(ev_b_s[0][:, :, None], (A_GROUPS, CHUNK, A_GROUP_DIM)),
        row(ev_vnorm[0]), dec,
        ev_w_out[0].astype(BF16),
        mlp_w1[0].astype(BF16), mlp_w2[0].astype(BF16),
    )
    mod0_prompt, mod0_sample = mod_rows(mod0)

    xp2 = x_prompt.reshape(n_prompt * prompt_len, D_MODEL)
    prep = _NextLayerPrep(1, mlp_w1, mlp_w2, cond, ada_w, ada_b3)
    xp2, new_state, w1_odd, w2_odd, mod1 = _even_call(xp2, mod0_prompt, None, None, even_params,
                                                      seq_len=prompt_len, prep=prep)
    mod1_prompt, mod1_sample = mod_rows(mod1)
    odd_params = (
        row(norm_mix[1]), row(norm_mlp[1]), row(final_norm),
        od_w_pool[0].astype(BF16), row(od_pool_scale[0]),
        w1_odd, w2_odd,
    )
    y_prompt = _odd_call(xp2, mod1_prompt, odd_params, seq_len=prompt_len)

    xs2 = x_sample.reshape(n_sample * sample_len, D_MODEL)
    rope = _rope_tables(sample_len)
    sp = _ret_state_call(xs2, mod0_sample, state_ret, rope, row(norm_mix[0]), w_in, dec,
                         n_seq=n_sample, seq_len=sample_len)
    (xs2,) = _even_call(xs2, mod0_sample, sp, rope, even_params, seq_len=sample_len)
    y_sample = _odd_call(xs2, mod1_sample, odd_params, seq_len=sample_len)

    return (y_prompt.reshape(n_prompt, prompt_len, D_MODEL),
            y_sample.reshape(n_sample, sample_len, D_MODEL),
            new_state)
```

```python
import functools
from typing import NamedTuple

import jax
import jax.numpy as jnp
from jax import lax
from jax.experimental import pallas as pl
from jax.experimental.pallas import tpu as pltpu

D_MODEL = 1024
DEPTH = 2
GRID_W = 64
A_WIDTH = D_MODEL // 2
A_GROUPS = 4
A_GROUP_DIM = A_WIDTH // A_GROUPS
CHUNK = 128
HEADS = 4
HEAD_DIM = (D_MODEL // 2) // HEADS
ROPE_BASE = 10000.0
ROPE_PAIRS_AXIS = HEAD_DIM // 4
POOL_WINDOWS = (2, 4, 8, 16)
POOL_DIM = D_MODEL // len(POOL_WINDOWS)
POOL_HALO = max(POOL_WINDOWS) // 2
D_FF = 4 * D_MODEL
EPS = 1e-6
QK_W = HEADS * HEAD_DIM
V_W = HEADS * HEAD_DIM
IN_WIDTH = 2 * A_WIDTH + 2 * QK_W + 3 * V_W
OUT_WIDTH = A_WIDTH + V_W
N_MOD = 6 * D_MODEL
K_COL = 2 * A_WIDTH + QK_W
V_COL = K_COL + QK_W

UNIT = 2 * CHUNK
UNIT_CHUNKS = UNIT // CHUNK
FF_CHUNK = 1024
ADA_BLOCK_N = 1536
ADA_PREP_BLOCK_N = 256
BF16_SUBLANES = 16
COND_ROWS = 8
SUBLANES = 8
LANES = 128
V7X_VMEM_LIMIT_BYTES = 60 * 1024 * 1024

F32 = jnp.float32
BF16 = jnp.bfloat16


def _dot(a, b):
    return jnp.dot(a, b, preferred_element_type=F32)


def _dot_nt(a, b):
    return lax.dot_general(a, b, (((1,), (1,)), ((), ())), preferred_element_type=F32)


def _silu(x):
    return x * jax.nn.sigmoid(x)


def _gelu_tanh(x):
    return 0.5 * x * (1.0 + jnp.tanh(0.7978845608028654 * (x + 0.044715 * (x * x * x))))


def _rms(x):
    return x * lax.rsqrt(jnp.mean(x * x, axis=-1, keepdims=True) + EPS)


def _rms_mod(x, gain, shift, scale):
    return (_rms(x) * gain) * (1.0 + scale) + shift


def _layer_norm(x):
    mu = jnp.mean(x, axis=-1, keepdims=True)
    d = x - mu
    return d * lax.rsqrt(jnp.mean(d * d, axis=-1, keepdims=True) + EPS)


def _split_mod(mod):
    return [mod[:, i * D_MODEL:(i + 1) * D_MODEL] for i in range(6)]


def _rope(x, cos2, sin2):
    return x * cos2 + pltpu.roll(x, HEAD_DIM // 2, axis=1) * sin2


def _run(gen, n=None):
    done = 0
    while n is None or done < n:
        try:
            next(gen)
        except StopIteration:
            break
        done += 1
    return done


def _weave(heavy, lights, n_light):
    heavy = list(heavy)
    pending = _roundrobin(lights)
    ran = 0
    for k, piece in enumerate(heavy):
        piece()
        want = ((k + 1) * n_light + len(heavy) - 1) // len(heavy)
        ran += _run(pending, want - ran)
    _run(pending)


def _pieces(gen, n):
    return [functools.partial(next, gen, None) for _ in range(n)]


def _roundrobin(gens):
    live = list(gens)
    while live:
        for g in list(live):
            try:
                next(g)
            except StopIteration:
                live.remove(g)
                continue
            yield


MLP_DOT_PIECES = 2 * (D_FF // FF_CHUNK)


def _mlp_pieces(hb, w1_ref, w2_ref):
    n_ff = D_FF // FF_CHUNK
    up = lambda j: _dot(hb, w1_ref[:, j * FF_CHUNK:(j + 1) * FF_CHUNK])
    a = up(0)
    yield
    acc = None
    for j in range(n_ff):
        a_next = None
        if j + 1 < n_ff:
            a_next = up(j + 1)
            yield
        p = _dot(jnp.square(jnp.maximum(a, 0.0)).astype(BF16), w2_ref[j * FF_CHUNK:(j + 1) * FF_CHUNK, :])
        acc = p if acc is None else acc + p
        a = a_next
        yield
    return acc


def _mod_row_map(mod, units_per_seq):
    if mod.shape[0] == 1:
        return lambda unit: 0
    return lambda unit: unit // units_per_seq


def _resident(shape):
    return pl.BlockSpec(shape, lambda i: (0,) * len(shape), pipeline_mode=pl.Buffered(1))


def _ada_kernel(cond_ref, w_ref, b_ref, out_ref):
    s = _silu(cond_ref[...]).astype(BF16)
    out_ref[0] = _dot(s, w_ref[0].astype(BF16)) + b_ref[0]


def _ada_call(cond, ada_w, ada_b, *, n_layers):
    return pl.pallas_call(
        _ada_kernel,
        grid=(n_layers, N_MOD // ADA_BLOCK_N),
        in_specs=[
            pl.BlockSpec((COND_ROWS, D_MODEL), lambda i, j: (0, 0)),
            pl.BlockSpec((1, D_MODEL, ADA_BLOCK_N), lambda i, j: (i, 0, j)),
            pl.BlockSpec((1, 1, ADA_BLOCK_N), lambda i, j: (i, 0, j)),
        ],
        out_specs=pl.BlockSpec((1, COND_ROWS, ADA_BLOCK_N), lambda i, j: (i, 0, j)),
        out_shape=jax.ShapeDtypeStruct((n_layers, COND_ROWS, N_MOD), F32),
        compiler_params=pltpu.CompilerParams(
            dimension_semantics=("arbitrary", "arbitrary"),
            vmem_limit_bytes=V7X_VMEM_LIMIT_BYTES),
        name="ada_mod",
    )(cond, ada_w, ada_b)


def _decay_tables(dec_ref, hd):
    row = lax.broadcasted_iota(jnp.int32, (CHUNK, CHUNK), 0).astype(F32)
    col = lax.broadcasted_iota(jnp.int32, (CHUNK, CHUNK), 1).astype(F32)
    lg_f = jnp.broadcast_to(-jnp.exp(dec_ref[hd:hd + 1, :]), (CHUNK, CHUNK))
    lg_b = jnp.broadcast_to(-jnp.exp(dec_ref[HEADS + hd:HEADS + hd + 1, :]), (CHUNK, CHUNK))
    d_f = row - col
    d_b = col - row
    return dict(
        decay_f=jnp.where(d_f >= 0, jnp.exp(lg_f * jnp.maximum(d_f, 0.0)), 0.0),
        decay_b=jnp.where(d_b >= 0, jnp.exp(lg_b * jnp.maximum(d_b, 0.0)), 0.0),
        qd_f=jnp.exp(lg_f * (row + 1.0)),
        qd_b=jnp.exp(lg_b * (CHUNK - row)),
        kd_f=jnp.exp(lg_f * (CHUNK - 1.0 - row)),
        kd_b=jnp.exp(lg_b * row),
        cd_f=jnp.exp(lg_f * CHUNK),
        cd_b=jnp.exp(lg_b * CHUNK),
    )


TABLE_NAMES = ("decay_f", "decay_b", "qd_f", "qd_b", "kd_f", "kd_b", "cd_f", "cd_b")


def _ret_state_kernel(x_ref, mod_ref, s0_ref, cos_ref, sin_ref, nmix_ref, wk_ref, wv_ref, dec_ref, sp_ref,
                      *, n_chunk):
    sh1, sc1 = _split_mod(mod_ref[0])[:2]
    hb = _rms_mod(x_ref[...], nmix_ref[...], sh1, sc1).astype(BF16)
    zk = _dot(hb, wk_ref[...]) * (HEAD_DIM ** -0.5)
    zv = _dot(hb, wv_ref[...]).astype(BF16)
    cos2, sin2 = cos_ref[...], sin_ref[...]
    for hd in range(HEADS):
        lo = hd * HEAD_DIM
        tb = _decay_tables(dec_ref, hd)
        kh = _rope(zk[:, lo:lo + HEAD_DIM], cos2, sin2)
        kv_f, kv_b = [], []
        for c in range(n_chunk):
            kc = kh[c * CHUNK:(c + 1) * CHUNK]
            vc = zv[c * CHUNK:(c + 1) * CHUNK, lo:lo + HEAD_DIM]
            kv_f.append(_dot((kc * tb["kd_f"]).T.astype(BF16), vc))
            kv_b.append(_dot((kc * tb["kd_b"]).T.astype(BF16), vc))
        st = s0_ref[0, 0, 0, hd]
        for c in range(n_chunk):
            sp_ref[c // UNIT_CHUNKS, c % UNIT_CHUNKS, 0, hd] = st.astype(BF16)
            st = tb["cd_f"] * st + kv_f[c]
        st = s0_ref[0, 0, 1, hd]
        for c in reversed(range(n_chunk)):
            sp_ref[c // UNIT_CHUNKS, c % UNIT_CHUNKS, 1, hd] = st.astype(BF16)
            st = tb["cd_b"] * st + kv_b[c]


def _ret_state_call(x2, mod, s0, rope, nmix, w_in, dec, *, n_seq, seq_len):
    n_chunk = seq_len // CHUNK
    units = seq_len // UNIT
    return pl.pallas_call(
        functools.partial(_ret_state_kernel, n_chunk=n_chunk),
        grid=(n_seq,),
        in_specs=[
            pl.BlockSpec((seq_len, D_MODEL), lambda i: (i, 0)),
            pl.BlockSpec((1, 1, N_MOD), lambda i: (i, 0, 0)),
            pl.BlockSpec((1, 1, 2, HEADS, HEAD_DIM, HEAD_DIM), lambda i: (i, 0, 0, 0, 0, 0)),
            _resident((seq_len, HEAD_DIM)), _resident((seq_len, HEAD_DIM)),
            _resident((1, D_MODEL)),
            pl.BlockSpec((D_MODEL, QK_W), lambda i: (0, K_COL // QK_W), pipeline_mode=pl.Buffered(1)),
            pl.BlockSpec((D_MODEL, V_W), lambda i: (0, V_COL // V_W), pipeline_mode=pl.Buffered(1)),
            _resident(dec.shape),
        ],
        out_specs=pl.BlockSpec((units, UNIT_CHUNKS, 2, HEADS, HEAD_DIM, HEAD_DIM), lambda i: (i, 0, 0, 0, 0, 0)),
        out_shape=jax.ShapeDtypeStruct((n_seq * units, UNIT_CHUNKS, 2, HEADS, HEAD_DIM, HEAD_DIM), BF16),
        compiler_params=pltpu.CompilerParams(
            dimension_semantics=("arbitrary",),
            vmem_limit_bytes=V7X_VMEM_LIMIT_BYTES),
        name="ret_state",
    )(x2, mod, s0, rope[0], rope[1], nmix, w_in, w_in, dec)


def _even_kernel(*refs, n_units, use_rope, scanned_states, next_layer_prep):
    it = iter(refs)
    xa_ref, xc_ref, moda_ref, modc_ref = (next(it) for _ in range(4))
    sp_ref = next(it) if scanned_states else None
    cos_ref = next(it) if use_rope else None
    sin_ref = next(it) if use_rope else None
    nmix_ref, nmlp_ref = next(it), next(it)
    w_in_ref, w_s_ref, b_s_ref, vgain_ref, dec_ref, w_out_ref, w1_ref, w2_ref = (next(it) for _ in range(8))
    if next_layer_prep:
        w1n_ref, w2n_ref, cond_ref, adaw_ref, adab_ref = (next(it) for _ in range(5))
    y_ref = next(it)
    st_ref = None if scanned_states else next(it)
    if next_layer_prep:
        w1n_out, w2n_out, modn_out = (next(it) for _ in range(3))
    ug_s, vg_s, q_s, k_s, v_s, gf_s, gb_s, cat_s, tab_s = (next(it) for _ in range(9))
    scond_s = next(it) if next_layer_prep else None

    step = pl.program_id(0)
    slot_a = lax.rem(step, 2)
    slot_b = 1 - slot_a

    @pl.when(step == 0)
    def _():
        for hd in range(HEADS):
            tb = _decay_tables(dec_ref, hd)
            for n, name in enumerate(TABLE_NAMES):
                tab_s[hd, n] = tb[name]
        if next_layer_prep:
            scond_s[...] = _silu(cond_ref[...])

    def stage_in():
        sh1, sc1 = _split_mod(moda_ref[0])[:2]
        hb = _rms_mod(xa_ref[...], nmix_ref[...], sh1, sc1).astype(BF16)
        z = _dot(hb, w_in_ref[...])
        yield
        ug_s[slot_a] = _gelu_tanh(z[:, 0:A_WIDTH])
        yield
        zv = _gelu_tanh(z[:, A_WIDTH:2 * A_WIDTH])
        vgn = [_layer_norm(zv[:, g * A_GROUP_DIM:(g + 1) * A_GROUP_DIM]) for g in range(A_GROUPS)]
        vg_s[slot_a] = (jnp.concatenate(vgn, axis=-1) * vgain_ref[...]).astype(BF16)
        yield
        zq = z[:, 2 * A_WIDTH:K_COL]
        zk = z[:, K_COL:V_COL] * (HEAD_DIM ** -0.5)
        if use_rope:
            cos2, sin2 = cos_ref[...], sin_ref[...]
            zq = jnp.concatenate([_rope(zq[:, h * HEAD_DIM:(h + 1) * HEAD_DIM], cos2, sin2) for h in range(HEADS)], -1)
            zk = jnp.concatenate([_rope(zk[:, h * HEAD_DIM:(h + 1) * HEAD_DIM], cos2, sin2) for h in range(HEADS)], -1)
        q_s[slot_a] = zq.astype(BF16)
        yield
        k_s[slot_a] = zk
        v_s[slot_a] = z[:, V_COL:V_COL + V_W].astype(BF16)
        yield
        gf_s[slot_a] = _silu(z[:, V_COL + V_W:V_COL + 2 * V_W])
        yield
        gb_s[slot_a] = _silu(z[:, V_COL + 2 * V_W:V_COL + 3 * V_W])
        if next_layer_prep:
            yield
            w1n_out[...] = w1n_ref[...].astype(BF16)
            w2n_out[...] = w2n_ref[...].astype(BF16)
            w_blk = adaw_ref[...]
            bias = adab_ref[...]
            rows = []
            for r in range(scond_s.shape[0]):
                halves = [jnp.sum(w_blk[:, h * LANES:(h + 1) * LANES] * scond_s[r], axis=0, keepdims=True)
                          for h in range(ADA_PREP_BLOCK_N // LANES)]
                rows.append(jnp.concatenate(halves, axis=1) + bias)
            rows += [bias] * (COND_ROWS - len(rows))
            modn_out[...] = jnp.concatenate(rows, axis=0)

    in_light = 7 if next_layer_prep else 6

    def stage_mix():
        s = slot_b
        rows = [slice(c * CHUNK, (c + 1) * CHUNK) for c in range(UNIT_CHUNKS)]
        head_cols = [slice(hd * HEAD_DIM, (hd + 1) * HEAD_DIM) for hd in range(HEADS)]
        table = lambda hd, name: tab_s[hd, TABLE_NAMES.index(name)]

        sp2 = []
        for g in range(A_GROUPS):
            lo = g * A_GROUP_DIM
            vg2 = jnp.concatenate([vg_s[s, r, lo:lo + A_GROUP_DIM] for r in rows], axis=-1)
            sp2.append(_dot(w_s_ref[g], vg2))
        yield
        scores, kv_f, kv_b = [], [], []
        for hd in range(HEADS):
            kc = [k_s[s, r, head_cols[hd]] for r in rows]
            scores.append([_dot_nt(q_s[s, r, head_cols[hd]], kc[c].astype(BF16)) for c, r in enumerate(rows)])
            if not scanned_states:
                vc = [v_s[s, r, head_cols[hd]] for r in rows]
                kv_f.append([_dot((kc[c] * table(hd, "kd_f")).T.astype(BF16), vc[c]) for c in range(UNIT_CHUNKS)])
                kv_b.append([_dot((kc[c] * table(hd, "kd_b")).T.astype(BF16), vc[c]) for c in range(UNIT_CHUNKS)])
            yield
        for g in range(A_GROUPS):
            lo = g * A_GROUP_DIM
            for c, r in enumerate(rows):
                gate = sp2[g][:, c * CHUNK:(c + 1) * CHUNK] + b_s_ref[g]
                cat_s[s, r, lo:lo + A_GROUP_DIM] = (ug_s[s, r, lo:lo + A_GROUP_DIM] * gate).astype(BF16)
            if g % 2 == 1:
                yield
        y_raw = []
        for hd in range(HEADS):
            if scanned_states:
                enter_f = [sp_ref[0, c, 0, hd] for c in range(UNIT_CHUNKS)]
                enter_b = [sp_ref[0, c, 1, hd] for c in range(UNIT_CHUNKS)]
            else:
                enter_f, enter_b = [None], [None] * UNIT_CHUNKS
                st = kv_f[hd][0]
                for c in range(1, UNIT_CHUNKS):
                    enter_f.append(st.astype(BF16))
                    st = table(hd, "cd_f") * st + kv_f[hd][c]
                st_ref[0, 0, 0, hd] = st
                st = kv_b[hd][UNIT_CHUNKS - 1]
                for c in reversed(range(UNIT_CHUNKS - 1)):
                    enter_b[c] = st.astype(BF16)
                    st = table(hd, "cd_b") * st + kv_b[hd][c]
                st_ref[0, 0, 1, hd] = st
            per_chunk = []
            for c, r in enumerate(rows):
                vc = v_s[s, r, head_cols[hd]]
                qf = q_s[s, r, head_cols[hd]].astype(F32)

                def direction(decay, qd, enter):
                    p = (scores[hd][c] * table(hd, decay)).astype(BF16)
                    if enter is None:
                        return _dot(p, vc)
                    lhs = jnp.concatenate([p, (qf * table(hd, qd)).astype(BF16)], axis=-1)
                    return _dot(lhs, jnp.concatenate([vc, enter], axis=0))

                per_chunk.append((direction("decay_f", "qd_f", enter_f[c]), direction("decay_b", "qd_b", enter_b[c])))
            y_raw.append(per_chunk)
            yield
        for hd in range(HEADS):
            for c, r in enumerate(rows):
                y_f, y_b = y_raw[hd][c]
                cols = slice(A_WIDTH + hd * HEAD_DIM, A_WIDTH + (hd + 1) * HEAD_DIM)
                cat_s[s, r, cols] = (gf_s[s, r, head_cols[hd]] * _layer_norm(y_f)
                                     + gb_s[s, r, head_cols[hd]] * _layer_norm(y_b)).astype(BF16)
            yield

    MIX_LIGHT = 1 + HEADS + A_GROUPS // 2 + 2 * HEADS

    def stage_mlp():
        _, _, g1, sh2, sc2, g2 = _split_mod(modc_ref[0])
        x = xc_ref[...] + g1 * _dot(cat_s[slot_a], w_out_ref[...])
        hb = _rms_mod(x, nmlp_ref[...], sh2, sc2).astype(BF16)
        yield
        acc = yield from _mlp_pieces(hb, w1_ref, w2_ref)
        y_ref[...] = x + g2 * acc

    @pl.when(step == 0)
    def _():
        _run(stage_in())

    @pl.when(step == 1)
    def _():
        a = stage_in()
        _weave(_pieces(a, 1), [a, stage_mix()], in_light + MIX_LIGHT)

    @pl.when((step >= 2) & (step < n_units))
    def _():
        a, c = stage_in(), stage_mlp()
        heavy = _pieces(c, 1) + _pieces(a, 1) + _pieces(c, MLP_DOT_PIECES + 1)
        _weave(heavy, [a, stage_mix()], in_light + MIX_LIGHT)

    @pl.when(step == n_units)
    def _():
        _weave(_pieces(stage_mlp(), MLP_DOT_PIECES + 2), [stage_mix()], MIX_LIGHT)

    @pl.when(step == n_units + 1)
    def _():
        _run(stage_mlp())


class _NextLayerPrep(NamedTuple):
    layer: int
    mlp_w1: jax.Array
    mlp_w2: jax.Array
    cond: jax.Array
    ada_w: jax.Array
    ada_b: jax.Array


def _even_call(x2, mod, sp, rope, params, *, seq_len, prep=None):
    n_units = x2.shape[0] // UNIT
    units_per_seq = seq_len // UNIT
    scanned = sp is not None
    use_rope = rope is not None
    assert scanned or units_per_seq == 1
    last = n_units - 1
    unit_a = lambda i: jnp.minimum(i, last)
    unit_b = lambda i: jnp.clip(i - 1, 0, last)
    unit_c = lambda i: jnp.clip(i - 2, 0, last)
    mod_row = _mod_row_map(mod, units_per_seq)
    state_block = (1, UNIT_CHUNKS if scanned else 1, 2, HEADS, HEAD_DIM, HEAD_DIM)

    in_specs = [pl.BlockSpec((UNIT, D_MODEL), lambda i: (unit_a(i), 0)),
                pl.BlockSpec((UNIT, D_MODEL), lambda i: (unit_c(i), 0)),
                pl.BlockSpec((1, 1, N_MOD), lambda i: (mod_row(unit_a(i)), 0, 0)),
                pl.BlockSpec((1, 1, N_MOD), lambda i: (mod_row(unit_c(i)), 0, 0))]
    args = [x2, x2, mod, mod]
    if scanned:
        in_specs.append(pl.BlockSpec(state_block, lambda i: (unit_b(i), 0, 0, 0, 0, 0)))
        args.append(sp)
    if use_rope:
        in_specs += [pl.BlockSpec((UNIT, HEAD_DIM), lambda i: (unit_a(i) % units_per_seq, 0))] * 2
        args += list(rope)
    in_specs += [_resident(p.shape) for p in params]
    args += list(params)

    out_shape = [jax.ShapeDtypeStruct(x2.shape, F32)]
    out_specs = [pl.BlockSpec((UNIT, D_MODEL), lambda i: (unit_c(i), 0))]
    if not scanned:
        out_shape.append(jax.ShapeDtypeStruct((n_units, 1, 2, HEADS, HEAD_DIM, HEAD_DIM), F32))
        out_specs.append(pl.BlockSpec(state_block, lambda i: (unit_b(i), 0, 0, 0, 0, 0)))
    if prep is not None:
        w1_rows, w2_rows = D_MODEL // n_units, D_FF // n_units
        assert w1_rows * n_units == D_MODEL and w1_rows % BF16_SUBLANES == 0
        mod_block = lambda i: jnp.minimum(unit_a(i), N_MOD // ADA_PREP_BLOCK_N - 1)
        in_specs += [
            pl.BlockSpec((None, w1_rows, D_FF), lambda i: (prep.layer, unit_a(i), 0)),
            pl.BlockSpec((None, w2_rows, D_MODEL), lambda i: (prep.layer, unit_a(i), 0)),
            _resident(prep.cond.shape),
            pl.BlockSpec((None, D_MODEL, ADA_PREP_BLOCK_N), lambda i: (prep.layer, 0, mod_block(i))),
            pl.BlockSpec((None, 1, ADA_PREP_BLOCK_N), lambda i: (prep.layer, 0, mod_block(i))),
        ]
        args += [prep.mlp_w1, prep.mlp_w2, prep.cond, prep.ada_w, prep.ada_b]
        out_shape += [jax.ShapeDtypeStruct((D_MODEL, D_FF), BF16), jax.ShapeDtypeStruct((D_FF, D_MODEL), BF16),
                      jax.ShapeDtypeStruct((COND_ROWS, N_MOD), F32)]
        out_specs += [pl.BlockSpec((w1_rows, D_FF), lambda i: (unit_a(i), 0)),
                      pl.BlockSpec((w2_rows, D_MODEL), lambda i: (unit_a(i), 0)),
                      pl.BlockSpec((COND_ROWS, ADA_PREP_BLOCK_N), lambda i: (0, mod_block(i)))]

    scratch = [
        pltpu.VMEM((2, UNIT, A_WIDTH), F32),
        pltpu.VMEM((2, UNIT, A_WIDTH), BF16),
        pltpu.VMEM((2, UNIT, QK_W), BF16),
        pltpu.VMEM((2, UNIT, QK_W), F32),
        pltpu.VMEM((2, UNIT, V_W), BF16),
        pltpu.VMEM((2, UNIT, V_W), F32),
        pltpu.VMEM((2, UNIT, V_W), F32),
        pltpu.VMEM((2, UNIT, OUT_WIDTH), BF16),
        pltpu.VMEM((HEADS, len(TABLE_NAMES), CHUNK, CHUNK), F32),
    ]
    if prep is not None:
        scratch.append(pltpu.VMEM(prep.cond.shape, F32))
    outs = pl.pallas_call(
        functools.partial(_even_kernel, n_units=n_units, use_rope=use_rope, scanned_states=scanned,
                          next_layer_prep=prep is not None),
        grid=(n_units + 2,),
        in_specs=in_specs,
        out_specs=out_specs,
        out_shape=out_shape,
        scratch_shapes=scratch,
        compiler_params=pltpu.CompilerParams(
            dimension_semantics=("arbitrary",),
            vmem_limit_bytes=V7X_VMEM_LIMIT_BYTES),
        name="even_layer_rope" if use_rope else "even_layer",
    )(*args)
    return outs


def _odd_kernel(xp_ref, x_ref, xn_ref, moda_ref, modc_ref, nmix_ref, nmlp_ref, fin_ref, w_pool_ref, pscale_ref,
                w1_ref, w2_ref, y_ref, x1_s, hb_s, *, n_units, seq_len):
    step = pl.program_id(0)
    slot_a = lax.rem(step, 2)
    slot_b = 1 - slot_a
    rows_w = UNIT + 2 * POOL_HALO

    def stage_pool():
        sh1, sc1, g1, sh2, sc2, _ = _split_mod(moda_ref[0])
        xw = jnp.concatenate([xp_ref[...], x_ref[...], xn_ref[...]], axis=0)
        h = _rms_mod(xw, nmix_ref[...], sh1, sc1)
        first = jnp.minimum(step, n_units - 1) * UNIT - POOL_HALO
        t = (lax.broadcasted_iota(jnp.int32, (rows_w, 1), 0) + first) & (seq_len - 1)
        yield

        def shift_down(a, k):
            return jnp.where(t >= k, pltpu.roll(a, k, axis=0), 0.0)

        def shift_up(a, k):
            return jnp.where(t < seq_len - k, pltpu.roll(a, rows_w - k, axis=0), 0.0)

        outs = []
        for g, w in enumerate(POOL_WINDOWS):
            hg = h[:, g * POOL_DIM:(g + 1) * POOL_DIM]
            half = w // 2
            left = shift_down(hg, 1)
            right = hg
            span = 1
            while span < half:
                left = left + shift_down(left, span)
                right = right + shift_up(right, span)
                span *= 2
            cnt = (jnp.minimum(t + half, seq_len) - jnp.maximum(t - half, 0)).astype(F32)
            pooled = ((left + right) / cnt - hg)[POOL_HALO:POOL_HALO + UNIT].astype(BF16)
            outs.append(_dot(pooled, w_pool_ref[g]))
            yield
        x1 = x_ref[...] + g1 * (jnp.concatenate(outs, axis=-1) * pscale_ref[...])
        x1_s[slot_a] = x1
        hb_s[slot_a] = _rms_mod(x1, nmlp_ref[...], sh2, sc2).astype(BF16)

    def stage_mlp():
        g2 = _split_mod(modc_ref[0])[5]
        acc = yield from _mlp_pieces(hb_s[slot_b], w1_ref, w2_ref)
        y_ref[...] = _rms(x1_s[slot_b] + g2 * acc) * fin_ref[...]

    pool_light = 2 + len(POOL_WINDOWS)

    @pl.when(step == 0)
    def _():
        _run(stage_pool())

    @pl.when((step >= 1) & (step < n_units))
    def _():
        _weave(_pieces(stage_mlp(), MLP_DOT_PIECES + 1), [stage_pool()], pool_light)

    @pl.when(step == n_units)
    def _():
        _run(stage_mlp())


def _odd_call(x2, mod, params, *, seq_len):
    n_units = x2.shape[0] // UNIT
    units_per_seq = seq_len // UNIT
    halo_per_unit = UNIT // POOL_HALO
    last = n_units - 1
    last_halo = x2.shape[0] // POOL_HALO - 1
    unit_a = lambda i: jnp.minimum(i, last)
    unit_c = lambda i: jnp.clip(i - 1, 0, last)
    mod_row = _mod_row_map(mod, units_per_seq)
    in_specs =[pl.BlockSpec((POOL_HALO, D_MODEL), lambda i: (jnp.maximum(unit_a(i) * halo_per_unit - 1, 0), 0)),
                pl.BlockSpec((UNIT, D_MODEL), lambda i: (unit_a(i), 0)),
                pl.BlockSpec((POOL_HALO, D_MODEL),
                             lambda i: (jnp.minimum((unit_a(i) + 1) * halo_per_unit, last_halo), 0)),
                pl.BlockSpec((1, 1, N_MOD), lambda i: (mod_row(unit_a(i)), 0, 0)),
                pl.BlockSpec((1, 1, N_MOD), lambda i: (mod_row(unit_c(i)), 0, 0))]
    in_specs += [_resident(p.shape) for p in params]
    return pl.pallas_call(
        functools.partial(_odd_kernel, n_units=n_units, seq_len=seq_len),
        grid=(n_units + 1,),
        in_specs=in_specs,
        out_specs=pl.BlockSpec((UNIT, D_MODEL), lambda i: (unit_c(i), 0)),
        out_shape=jax.ShapeDtypeStruct(x2.shape, F32),
        scratch_shapes=[pltpu.VMEM((2, UNIT, D_MODEL), F32),
                        pltpu.VMEM((2, UNIT, D_MODEL), BF16)],
        compiler_params=pltpu.CompilerParams(
            dimension_semantics=("arbitrary",),
            vmem_limit_bytes=V7X_VMEM_LIMIT_BYTES),
        name="odd_layer",
    )(x2, x2, x2, mod, mod, *params)


def _rope_tables(n_tokens):
    rows = n_tokens // GRID_W
    pos_r = jnp.repeat(jnp.arange(rows, dtype=F32), GRID_W)
    pos_c = jnp.tile(jnp.arange(GRID_W, dtype=F32), rows)
    inv = ROPE_BASE ** (-jnp.arange(ROPE_PAIRS_AXIS, dtype=F32) / ROPE_PAIRS_AXIS)
    ang = jnp.concatenate([pos_r[:, None] * inv, pos_c[:, None] * inv], axis=-1)
    cos, sin = jnp.cos(ang), jnp.sin(ang)
    return jnp.concatenate([cos, cos], axis=-1), jnp.concatenate([-sin, sin], axis=-1)


def kernel(x_prompt, x_sample, state_ret, c, c_ctx, ada_w, ada_b, norm_mix, norm_mlp, mlp_w1, mlp_w2,
           ev_w_in, ev_w_s, ev_b_s, ev_vnorm, ev_decay, ev_w_out, od_w_pool, od_pool_scale, final_norm):
    n_prompt, prompt_len, _ = x_prompt.shape
    n_sample, sample_len, _ = x_sample.shape
    assert DEPTH == 2 and n_sample + 1 <= COND_ROWS
    assert prompt_len == UNIT and sample_len % UNIT == 0

    cond = jnp.zeros((COND_ROWS, D_MODEL), F32).at[0].set(c_ctx).at[1:1 + n_sample].set(c)
    ada_b3 = ada_b.reshape(DEPTH, 1, N_MOD)
    mod0 = _ada_call(cond, ada_w, ada_b3, n_layers=1)[0]
    mod_rows = lambda m: (m[0:1].reshape(1, 1, N_MOD), m[1:1 + n_sample].reshape(n_sample, 1, N_MOD))

    row = lambda v: v.reshape(1, -1)
    w_in = ev_w_in[0].astype(BF16)
    dec = jnp.broadcast_to(ev_decay[0].reshape(2 * HEADS, 1), (2 * HEADS, CHUNK))
    even_params = (
        row(norm_mix[0]), row(norm_mlp[0]),
        w_in, ev_w_s[0].astype(BF16),
        jnp.broadcast_to(ev_b_s[0][:, :, None], (A_GROUPS, CHUNK, A_GROUP_DIM)),
        row(ev_vnorm[0]), dec,
        ev_w_out[0].astype(BF16),
        mlp_w1[0].astype(BF16), mlp_w2[0].astype(BF16),
    )
    mod0_prompt, mod0_sample = mod_rows(mod0)

    xp2 = x_prompt.reshape(n_prompt * prompt_len, D_MODEL)
    cond_cols = jnp.broadcast_to(cond[:1 + n_sample, :, None], (1 + n_sample, D_MODEL, LANES))
    prep = _NextLayerPrep(1, mlp_w1, mlp_w2, cond_cols, ada_w, ada_b3)
    xp2, new_state, w1_odd, w2_odd, mod1 = _even_call(xp2, mod0_prompt, None, None, even_params,
                                                      seq_len=prompt_len, prep=prep)
    mod1_prompt, mod1_sample = mod_rows(mod1)
    odd_params = (
        row(norm_mix[1]), row(norm_mlp[1]), row(final_norm),
        od_w_pool[0].astype(BF16), row(od_pool_scale[0]),
        w1_odd, w2_odd,
    )
    y_prompt = _odd_call(xp2, mod1_prompt, odd_params, seq_len=prompt_len)

    xs2 = x_sample.reshape(n_sample * sample_len, D_MODEL)
    rope = _rope_tables(sample_len)
    sp = _ret_state_call(xs2, mod0_sample, state_ret, rope, row(norm_mix[0]), w_in, dec,
                         n_seq=n_sample, seq_len=sample_len)
    (xs2,) = _even_call(xs2, mod0_sample, sp, rope, even_params, seq_len=sample_len)
    y_sample = _odd_call(xs2, mod1_sample, odd_params, seq_len=sample_len)

    return (y_prompt.reshape(n_prompt, prompt_len, D_MODEL),
            y_sample.reshape(n_sample, sample_len, D_MODEL),
            new_state)
```

```python
import functools
from typing import NamedTuple

import jax
import jax.numpy as jnp
from jax import lax
from jax.experimental import pallas as pl
from jax.experimental.pallas import tpu as pltpu

D_MODEL = 1024
DEPTH = 2
GRID_W = 64
A_WIDTH = D_MODEL // 2
A_GROUPS = 4
A_GROUP_DIM = A_WIDTH // A_GROUPS
CHUNK = 128
HEADS = 4
HEAD_DIM = (D_MODEL // 2) // HEADS
ROPE_BASE = 10000.0
ROPE_PAIRS_AXIS = HEAD_DIM // 4
POOL_WINDOWS = (2, 4, 8, 16)
POOL_DIM = D_MODEL // len(POOL_WINDOWS)
POOL_HALO = max(POOL_WINDOWS) // 2
D_FF = 4 * D_MODEL
EPS = 1e-6
QK_W = HEADS * HEAD_DIM
V_W = HEADS * HEAD_DIM
IN_WIDTH = 2 * A_WIDTH + 2 * QK_W + 3 * V_W
OUT_WIDTH = A_WIDTH + V_W
N_MOD = 6 * D_MODEL
K_COL = 2 * A_WIDTH + QK_W
V_COL = K_COL + QK_W

UNIT = 2 * CHUNK
UNIT_CHUNKS = UNIT // CHUNK
FF_CHUNK = 1024
ADA_BLOCK_N = 1536
PREP_GROUP = 4
X_RING = 3
BF16_SUBLANES = 16
COND_ROWS = 8
SUBLANES = 8
V7X_VMEM_LIMIT_BYTES = 60 * 1024 * 1024

F32 = jnp.float32
BF16 = jnp.bfloat16


def _dot(a, b):
    return jnp.dot(a, b, preferred_element_type=F32)


def _dot_nt(a, b):
    return lax.dot_general(a, b, (((1,), (1,)), ((), ())), preferred_element_type=F32)


def _silu(x):
    return x * jax.nn.sigmoid(x)


def _gelu_tanh(x):
    return 0.5 * x * (1.0 + jnp.tanh(0.7978845608028654 * (x + 0.044715 * (x * x * x))))


def _rms(x):
    return x * lax.rsqrt(jnp.mean(x * x, axis=-1, keepdims=True) + EPS)


def _rms_mod(x, gain, shift, scale):
    return (_rms(x) * gain) * (1.0 + scale) + shift


def _layer_norm(x):
    mu = jnp.mean(x, axis=-1, keepdims=True)
    d = x - mu
    return d * lax.rsqrt(jnp.mean(d * d, axis=-1, keepdims=True) + EPS)


def _split_mod(mod):
    return [mod[:, i * D_MODEL:(i + 1) * D_MODEL] for i in range(6)]


def _rope(x, cos2, sin2):
    return x * cos2 + pltpu.roll(x, HEAD_DIM // 2, axis=1) * sin2


def _run(gen, n=None):
    done = 0
    while n is None or done < n:
        try:
            next(gen)
        except StopIteration:
            break
        done += 1
    return done


def _weave(heavy, lights, n_light):
    heavy = list(heavy)
    pending = _roundrobin(lights)
    ran = 0
    for k, piece in enumerate(heavy):
        piece()
        want = ((k + 1) * n_light + len(heavy) - 1) // len(heavy)
        ran += _run(pending, want - ran)
    _run(pending)


def _pieces(gen, n):
    return [functools.partial(next, gen, None) for _ in range(n)]


def _roundrobin(gens):
    live = list(gens)
    while live:
        for g in list(live):
            try:
                next(g)
            except StopIteration:
                live.remove(g)
                continue
            yield


MLP_DOT_PIECES = 2 * (D_FF // FF_CHUNK)


def _mlp_pieces(hb, w1_ref, w2_ref):
    n_ff = D_FF // FF_CHUNK
    up = lambda j: _dot(hb, w1_ref[:, j * FF_CHUNK:(j + 1) * FF_CHUNK])
    a = up(0)
    yield
    acc = None
    for j in range(n_ff):
        a_next = None
        if j + 1 < n_ff:
            a_next = up(j + 1)
            yield
        p = _dot(jnp.square(jnp.maximum(a, 0.0)).astype(BF16), w2_ref[j * FF_CHUNK:(j + 1) * FF_CHUNK, :])
        acc = p if acc is None else acc + p
        a = a_next
        yield
    return acc


def _mod_row_map(mod, units_per_seq):
    if mod.shape[0] == 1:
        return lambda unit: 0
    return lambda unit: unit // units_per_seq


def _resident(shape):
    return pl.BlockSpec(shape, lambda i: (0,) * len(shape), pipeline_mode=pl.Buffered(1))


def _ada_kernel(cond_ref, w_ref, b_ref, out_ref):
    s = _silu(cond_ref[...]).astype(BF16)
    out_ref[0] = _dot(s, w_ref[0].astype(BF16)) + b_ref[0]


def _ada_call(cond, ada_w, ada_b, *, n_layers):
    return pl.pallas_call(
        _ada_kernel,
        grid=(n_layers, N_MOD // ADA_BLOCK_N),
        in_specs=[
            pl.BlockSpec((COND_ROWS, D_MODEL), lambda i, j: (0, 0)),
            pl.BlockSpec((1, D_MODEL, ADA_BLOCK_N), lambda i, j: (i, 0, j)),
            pl.BlockSpec((1, 1, ADA_BLOCK_N), lambda i, j: (i, 0, j)),
        ],
        out_specs=pl.BlockSpec((1, COND_ROWS, ADA_BLOCK_N), lambda i, j: (i, 0, j)),
        out_shape=jax.ShapeDtypeStruct((n_layers, COND_ROWS, N_MOD), F32),
        compiler_params=pltpu.CompilerParams(
            dimension_semantics=("arbitrary", "arbitrary"),
            vmem_limit_bytes=V7X_VMEM_LIMIT_BYTES),
        name="ada_mod",
    )(cond, ada_w, ada_b)


def _decay_tables(dec_ref, hd):
    row = lax.broadcasted_iota(jnp.int32, (CHUNK, CHUNK), 0).astype(F32)
    col = lax.broadcasted_iota(jnp.int32, (CHUNK, CHUNK), 1).astype(F32)
    lg_f = jnp.broadcast_to(-jnp.exp(dec_ref[hd:hd + 1, :]), (CHUNK, CHUNK))
    lg_b = jnp.broadcast_to(-jnp.exp(dec_ref[HEADS + hd:HEADS + hd + 1, :]), (CHUNK, CHUNK))
    d_f = row - col
    d_b = col - row
    return dict(
        decay_f=jnp.where(d_f >= 0, jnp.exp(lg_f * jnp.maximum(d_f, 0.0)), 0.0),
        decay_b=jnp.where(d_b >= 0, jnp.exp(lg_b * jnp.maximum(d_b, 0.0)), 0.0),
        qd_f=jnp.exp(lg_f * (row + 1.0)),
        qd_b=jnp.exp(lg_b * (CHUNK - row)),
        kd_f=jnp.exp(lg_f * (CHUNK - 1.0 - row)),
        kd_b=jnp.exp(lg_b * row),
        cd_f=jnp.exp(lg_f * CHUNK),
        cd_b=jnp.exp(lg_b * CHUNK),
    )


TABLE_NAMES = ("decay_f", "decay_b", "qd_f", "qd_b", "kd_f", "kd_b", "cd_f", "cd_b")


def _ret_state_kernel(x_ref, mod_ref, s0_ref, cos_ref, sin_ref, nmix_ref, wk_ref, wv_ref, dec_ref, sp_ref,
                      *, n_chunk):
    sh1, sc1 = _split_mod(mod_ref[0])[:2]
    hb = _rms_mod(x_ref[...], nmix_ref[...], sh1, sc1).astype(BF16)
    zk = _dot(hb, wk_ref[...]) * (HEAD_DIM ** -0.5)
    zv = _dot(hb, wv_ref[...]).astype(BF16)
    cos2, sin2 = cos_ref[...], sin_ref[...]
    for hd in range(HEADS):
        lo = hd * HEAD_DIM
        tb = _decay_tables(dec_ref, hd)
        kh = _rope(zk[:, lo:lo + HEAD_DIM], cos2, sin2)
        kv_f, kv_b = [], []
        for c in range(n_chunk):
            kc = kh[c * CHUNK:(c + 1) * CHUNK]
            vc = zv[c * CHUNK:(c + 1) * CHUNK, lo:lo + HEAD_DIM]
            kv_f.append(_dot((kc * tb["kd_f"]).T.astype(BF16), vc))
            kv_b.append(_dot((kc * tb["kd_b"]).T.astype(BF16), vc))
        st = s0_ref[0, 0, 0, hd]
        for c in range(n_chunk):
            sp_ref[c // UNIT_CHUNKS, c % UNIT_CHUNKS, 0, hd] = st.astype(BF16)
            st = tb["cd_f"] * st + kv_f[c]
        st = s0_ref[0, 0, 1, hd]
        for c in reversed(range(n_chunk)):
            sp_ref[c // UNIT_CHUNKS, c % UNIT_CHUNKS, 1, hd] = st.astype(BF16)
            st = tb["cd_b"] * st + kv_b[c]


def _ret_state_call(x2, mod, s0, rope, nmix, w_in, dec, *, n_seq, seq_len):
    n_chunk = seq_len // CHUNK
    units = seq_len // UNIT
    return pl.pallas_call(
        functools.partial(_ret_state_kernel, n_chunk=n_chunk),
        grid=(n_seq,),
        in_specs=[
            pl.BlockSpec((seq_len, D_MODEL), lambda i: (i, 0)),
            pl.BlockSpec((1, 1, N_MOD), lambda i: (i, 0, 0)),
            pl.BlockSpec((1, 1, 2, HEADS, HEAD_DIM, HEAD_DIM), lambda i: (i, 0, 0, 0, 0, 0)),
            _resident((seq_len, HEAD_DIM)), _resident((seq_len, HEAD_DIM)),
            _resident((1, D_MODEL)),
            pl.BlockSpec((D_MODEL, QK_W), lambda i: (0, K_COL // QK_W), pipeline_mode=pl.Buffered(1)),
            pl.BlockSpec((D_MODEL, V_W), lambda i: (0, V_COL // V_W), pipeline_mode=pl.Buffered(1)),
            _resident(dec.shape),
        ],
        out_specs=pl.BlockSpec((units, UNIT_CHUNKS, 2, HEADS, HEAD_DIM, HEAD_DIM), lambda i: (i, 0, 0, 0, 0, 0)),
        out_shape=jax.ShapeDtypeStruct((n_seq * units, UNIT_CHUNKS, 2, HEADS, HEAD_DIM, HEAD_DIM), BF16),
        compiler_params=pltpu.CompilerParams(
            dimension_semantics=("arbitrary",),
            vmem_limit_bytes=V7X_VMEM_LIMIT_BYTES),
        name="ret_state",
    )(x2, mod, s0, rope[0], rope[1], nmix, w_in, w_in, dec)


def _even_kernel(*refs, n_units, use_rope, scanned_states, next_layer_prep):
    it = iter(refs)
    xa_ref, moda_ref, modc_ref = (next(it) for _ in range(3))
    sp_ref = next(it) if scanned_states else None
    cos_ref = next(it) if use_rope else None
    sin_ref = next(it) if use_rope else None
    nmix_ref, nmlp_ref = next(it), next(it)
    w_in_ref, w_s_ref, b_s_ref, vgain_ref, dec_ref, w_out_ref, w1_ref, w2_ref = (next(it) for _ in range(8))
    if next_layer_prep:
        w1n_ref, w2n_ref = next(it), next(it)
    y_ref = next(it)
    st_ref = None if scanned_states else next(it)
    if next_layer_prep:
        w1n_out, w2n_out = next(it), next(it)
    ug_s, vg_s, q_s, k_s, v_s, gf_s, gb_s, cat_s, tab_s, x_s = (next(it) for _ in range(10))

    step = pl.program_id(0)
    slot_a = lax.rem(step, 2)
    slot_b = 1 - slot_a

    @pl.when(step == 0)
    def _():
        for hd in range(HEADS):
            tb = _decay_tables(dec_ref, hd)
            for n, name in enumerate(TABLE_NAMES):
                tab_s[hd, n] = tb[name]

    def stage_in():
        sh1, sc1 = _split_mod(moda_ref[0])[:2]
        x = xa_ref[...]
        hb = _rms_mod(x, nmix_ref[...], sh1, sc1).astype(BF16)
        z = _dot(hb, w_in_ref[...])
        yield
        x_s[lax.rem(step, X_RING)] = x
        ug_s[slot_a] = _gelu_tanh(z[:, 0:A_WIDTH])
        yield
        zv = _gelu_tanh(z[:, A_WIDTH:2 * A_WIDTH])
        vgn = [_layer_norm(zv[:, g * A_GROUP_DIM:(g + 1) * A_GROUP_DIM]) for g in range(A_GROUPS)]
        vg_s[slot_a] = (jnp.concatenate(vgn, axis=-1) * vgain_ref[...]).astype(BF16)
        yield
        zq = z[:, 2 * A_WIDTH:K_COL]
        zk = z[:, K_COL:V_COL] * (HEAD_DIM ** -0.5)
        if use_rope:
            units_per_seq = cos_ref.shape[0] // UNIT
            pos = pl.ds(pl.multiple_of(lax.rem(step, units_per_seq) * UNIT, UNIT), UNIT)
            cos2, sin2 = cos_ref[pos, :], sin_ref[pos, :]
            zq = jnp.concatenate([_rope(zq[:, h * HEAD_DIM:(h + 1) * HEAD_DIM], cos2, sin2) for h in range(HEADS)], -1)
            zk = jnp.concatenate([_rope(zk[:, h * HEAD_DIM:(h + 1) * HEAD_DIM], cos2, sin2) for h in range(HEADS)], -1)
        q_s[slot_a] = zq.astype(BF16)
        yield
        k_s[slot_a] = zk
        v_s[slot_a] = z[:, V_COL:V_COL + V_W].astype(BF16)
        yield
        gf_s[slot_a] = _silu(z[:, V_COL + V_W:V_COL + 2 * V_W])
        yield
        gb_s[slot_a] = _silu(z[:, V_COL + 2 * V_W:V_COL + 3 * V_W])
        if next_layer_prep:
            yield
            part = lax.rem(step, PREP_GROUP)
            for src, dst in ((w1n_ref, w1n_out), (w2n_ref, w2n_out)):
                n = src.shape[0] // PREP_GROUP
                rows = pl.ds(pl.multiple_of(part * n, n), n)
                dst[rows, :] = src[rows, :].astype(BF16)

    in_light = 7 if next_layer_prep else 6

    def stage_mix():
        s = slot_b
        rows = [slice(c * CHUNK, (c + 1) * CHUNK) for c in range(UNIT_CHUNKS)]
        head_cols = [slice(hd * HEAD_DIM, (hd + 1) * HEAD_DIM) for hd in range(HEADS)]
        table = lambda hd, name: tab_s[hd, TABLE_NAMES.index(name)]

        sp2 = []
        for g in range(A_GROUPS):
            lo = g * A_GROUP_DIM
            vg2 = jnp.concatenate([vg_s[s, r, lo:lo + A_GROUP_DIM] for r in rows], axis=-1)
            sp2.append(_dot(w_s_ref[g], vg2))
        yield
        scores, kv_f, kv_b = [], [], []
        for hd in range(HEADS):
            kc = [k_s[s, r, head_cols[hd]] for r in rows]
            scores.append([_dot_nt(q_s[s, r, head_cols[hd]], kc[c].astype(BF16)) for c, r in enumerate(rows)])
            if not scanned_states:
                vc = [v_s[s, r, head_cols[hd]] for r in rows]
                kv_f.append([_dot((kc[c] * table(hd, "kd_f")).T.astype(BF16), vc[c]) for c in range(UNIT_CHUNKS)])
                kv_b.append([_dot((kc[c] * table(hd, "kd_b")).T.astype(BF16), vc[c]) for c in range(UNIT_CHUNKS)])
            yield
        for g in range(A_GROUPS):
            lo = g * A_GROUP_DIM
            for c, r in enumerate(rows):
                gate = sp2[g][:, c * CHUNK:(c + 1) * CHUNK] + b_s_ref[g]
                cat_s[s, r, lo:lo + A_GROUP_DIM] = (ug_s[s, r, lo:lo + A_GROUP_DIM] * gate).astype(BF16)
            if g % 2 == 1:
                yield
        y_raw = []
        for hd in range(HEADS):
            if scanned_states:
                enter_f = [sp_ref[0, c, 0, hd] for c in range(UNIT_CHUNKS)]
                enter_b = [sp_ref[0, c, 1, hd] for c in range(UNIT_CHUNKS)]
            else:
                enter_f, enter_b = [None], [None] * UNIT_CHUNKS
                st = kv_f[hd][0]
                for c in range(1, UNIT_CHUNKS):
                    enter_f.append(st.astype(BF16))
                    st = table(hd, "cd_f") * st + kv_f[hd][c]
                st_ref[0, 0, 0, hd] = st
                st = kv_b[hd][UNIT_CHUNKS - 1]
                for c in reversed(range(UNIT_CHUNKS - 1)):
                    enter_b[c] = st.astype(BF16)
                    st = table(hd, "cd_b") * st + kv_b[hd][c]
                st_ref[0, 0, 1, hd] = st
            per_chunk = []
            for c, r in enumerate(rows):
                vc = v_s[s, r, head_cols[hd]]
                qf = q_s[s, r, head_cols[hd]].astype(F32)

                def direction(decay, qd, enter):
                    p = (scores[hd][c] * table(hd, decay)).astype(BF16)
                    if enter is None:
                        return _dot(p, vc)
                    lhs = jnp.concatenate([p, (qf * table(hd, qd)).astype(BF16)], axis=-1)
                    return _dot(lhs, jnp.concatenate([vc, enter], axis=0))

                per_chunk.append((direction("decay_f", "qd_f", enter_f[c]), direction("decay_b", "qd_b", enter_b[c])))
            y_raw.append(per_chunk)
            yield
        for hd in range(HEADS):
            for c, r in enumerate(rows):
                y_f, y_b = y_raw[hd][c]
                cols = slice(A_WIDTH + hd * HEAD_DIM, A_WIDTH + (hd + 1) * HEAD_DIM)
                cat_s[s, r, cols] = (gf_s[s, r, head_cols[hd]] * _layer_norm(y_f)
                                     + gb_s[s, r, head_cols[hd]] * _layer_norm(y_b)).astype(BF16)
            yield

    MIX_LIGHT = 1 + HEADS + A_GROUPS // 2 + 2 * HEADS

    def stage_mlp():
        _, _, g1, sh2, sc2, g2 = _split_mod(modc_ref[0])
        x = x_s[lax.rem(step + 1, X_RING)] + g1 * _dot(cat_s[slot_a], w_out_ref[...])
        hb = _rms_mod(x, nmlp_ref[...], sh2, sc2).astype(BF16)
        yield
        acc = yield from _mlp_pieces(hb, w1_ref, w2_ref)
        y_ref[...] = x + g2 * acc

    @pl.when(step == 0)
    def _():
        _run(stage_in())

    @pl.when(step == 1)
    def _():
        a = stage_in()
        _weave(_pieces(a, 1), [a, stage_mix()], in_light + MIX_LIGHT)

    @pl.when((step >= 2) & (step < n_units))
    def _():
        a, c = stage_in(), stage_mlp()
        heavy = _pieces(c, 1) + _pieces(a, 1) + _pieces(c, MLP_DOT_PIECES + 1)
        _weave(heavy, [a, stage_mix()], in_light + MIX_LIGHT)

    @pl.when(step == n_units)
    def _():
        _weave(_pieces(stage_mlp(), MLP_DOT_PIECES + 2), [stage_mix()], MIX_LIGHT)

    @pl.when(step == n_units + 1)
    def _():
        _run(stage_mlp())


class _NextLayerPrep(NamedTuple):
    layer: int
    mlp_w1: jax.Array
    mlp_w2: jax.Array


def _even_call(x2, mod, sp, rope, params, *, seq_len, prep=None):
    n_units = x2.shape[0] // UNIT
    units_per_seq = seq_len // UNIT
    scanned = sp is not None
    use_rope = rope is not None
    assert scanned or units_per_seq == 1
    last = n_units - 1
    unit_a = lambda i: jnp.minimum(i, last)
    unit_b = lambda i: jnp.clip(i - 1, 0, last)
    unit_c = lambda i: jnp.clip(i - 2, 0, last)
    mod_row = _mod_row_map(mod, units_per_seq)
    state_block = (1, UNIT_CHUNKS if scanned else 1, 2, HEADS, HEAD_DIM, HEAD_DIM)

    in_specs = [pl.BlockSpec((UNIT, D_MODEL), lambda i: (unit_a(i), 0)),
                pl.BlockSpec((1, 1, N_MOD), lambda i: (mod_row(unit_a(i)), 0, 0)),
                pl.BlockSpec((1, 1, N_MOD), lambda i: (mod_row(unit_c(i)), 0, 0))]
    args = [x2, mod, mod]
    if scanned:
        in_specs.append(pl.BlockSpec(state_block, lambda i: (unit_b(i), 0, 0, 0, 0, 0)))
        args.append(sp)
    if use_rope:
        in_specs += [_resident((seq_len, HEAD_DIM))] * 2
        args += list(rope)
    in_specs += [_resident(p.shape) for p in params]
    args += list(params)

    out_shape = [jax.ShapeDtypeStruct(x2.shape, F32)]
    out_specs = [pl.BlockSpec((UNIT, D_MODEL), lambda i: (unit_c(i), 0))]
    if not scanned:
        out_shape.append(jax.ShapeDtypeStruct((n_units, 1, 2, HEADS, HEAD_DIM, HEAD_DIM), F32))
        out_specs.append(pl.BlockSpec(state_block, lambda i: (unit_b(i), 0, 0, 0, 0, 0)))
    if prep is not None:
        n_groups = n_units // PREP_GROUP
        w1_rows, w2_rows = D_MODEL // n_groups, D_FF // n_groups
        assert n_groups * PREP_GROUP == n_units and w1_rows * n_groups == D_MODEL
        assert w1_rows % (PREP_GROUP * BF16_SUBLANES) == 0
        group = lambda i: unit_a(i) // PREP_GROUP
        in_specs += [pl.BlockSpec((None, w1_rows, D_FF), lambda i: (prep.layer, group(i), 0)),
                     pl.BlockSpec((None, w2_rows, D_MODEL), lambda i: (prep.layer, group(i), 0))]
        args += [prep.mlp_w1, prep.mlp_w2]
        out_shape += [jax.ShapeDtypeStruct((D_MODEL, D_FF), BF16), jax.ShapeDtypeStruct((D_FF, D_MODEL), BF16)]
        out_specs += [pl.BlockSpec((w1_rows, D_FF), lambda i: (group(i), 0)),
                      pl.BlockSpec((w2_rows, D_MODEL), lambda i: (group(i), 0))]

    scratch = [
        pltpu.VMEM((2, UNIT, A_WIDTH), F32),
        pltpu.VMEM((2, UNIT, A_WIDTH), BF16),
        pltpu.VMEM((2, UNIT, QK_W), BF16),
        pltpu.VMEM((2, UNIT, QK_W), F32),
        pltpu.VMEM((2, UNIT, V_W), BF16),
        pltpu.VMEM((2, UNIT, V_W), F32),
        pltpu.VMEM((2, UNIT, V_W), F32),
        pltpu.VMEM((2, UNIT, OUT_WIDTH), BF16),
        pltpu.VMEM((HEADS, len(TABLE_NAMES), CHUNK, CHUNK), F32),
        pltpu.VMEM((X_RING, UNIT, D_MODEL), F32),
    ]
    outs = pl.pallas_call(
        functools.partial(_even_kernel, n_units=n_units, use_rope=use_rope, scanned_states=scanned,
                          next_layer_prep=prep is not None),
        grid=(n_units + 2,),
        in_specs=in_specs,
        out_specs=out_specs,
        out_shape=out_shape,
        scratch_shapes=scratch,
        compiler_params=pltpu.CompilerParams(
            dimension_semantics=("arbitrary",),
            vmem_limit_bytes=V7X_VMEM_LIMIT_BYTES),
        name="even_layer_rope" if use_rope else "even_layer",
    )(*args)
    return outs


def _odd_kernel(*refs, n_units, seq_len, halo):
    it = iter(refs)
    xp_ref = next(it) if halo else None
    x_ref = next(it)
    xn_ref = next(it) if halo else None
    moda_ref, modc_ref, nmix_ref, nmlp_ref, fin_ref, w_pool_ref, pscale_ref, w1_ref, w2_ref = (
        next(it) for _ in range(9))
    y_ref, x1_s, hb_s = next(it), next(it), next(it)
    step = pl.program_id(0)
    slot_a = lax.rem(step, 2)
    slot_b = 1 - slot_a
    rows_w = UNIT + 2 * halo

    def stage_pool():
        sh1, sc1, g1, sh2, sc2, _ = _split_mod(moda_ref[0])
        xw = jnp.concatenate([xp_ref[...], x_ref[...], xn_ref[...]], axis=0) if halo else x_ref[...]
        h = _rms_mod(xw, nmix_ref[...], sh1, sc1)
        first = jnp.minimum(step, n_units - 1) * UNIT - halo
        t = (lax.broadcasted_iota(jnp.int32, (rows_w, 1), 0) + first) & (seq_len - 1)
        yield

        def shift_down(a, k):
            return jnp.where(t >= k, pltpu.roll(a, k, axis=0), 0.0)

        def shift_up(a, k):
            return jnp.where(t < seq_len - k, pltpu.roll(a, rows_w - k, axis=0), 0.0)

        outs = []
        for g, w in enumerate(POOL_WINDOWS):
            hg = h[:, g * POOL_DIM:(g + 1) * POOL_DIM]
            half = w // 2
            left = shift_down(hg, 1)
            right = hg
            span = 1
            while span < half:
                left = left + shift_down(left, span)
                right = right + shift_up(right, span)
                span *= 2
            cnt = (jnp.minimum(t + half, seq_len) - jnp.maximum(t - half, 0)).astype(F32)
            pooled = ((left + right) / cnt - hg)[halo:halo + UNIT].astype(BF16)
            outs.append(_dot(pooled, w_pool_ref[g]))
            yield
        x1 = x_ref[...] + g1 * (jnp.concatenate(outs, axis=-1) * pscale_ref[...])
        x1_s[slot_a] = x1
        hb_s[slot_a] = _rms_mod(x1, nmlp_ref[...], sh2, sc2).astype(BF16)

    def stage_mlp():
        g2 = _split_mod(modc_ref[0])[5]
        acc = yield from _mlp_pieces(hb_s[slot_b], w1_ref, w2_ref)
        y_ref[...] = _rms(x1_s[slot_b] + g2 * acc) * fin_ref[...]

    pool_light = 2 + len(POOL_WINDOWS)

    @pl.when(step == 0)
    def _():
        _run(stage_pool())

    @pl.when((step >= 1) & (step < n_units))
    def _():
        _weave(_pieces(stage_mlp(), MLP_DOT_PIECES + 1), [stage_pool()], pool_light)

    @pl.when(step == n_units)
    def _():
        _run(stage_mlp())


def _odd_call(x2, mod, params, *, seq_len):
    n_units = x2.shape[0] // UNIT
    units_per_seq = seq_len // UNIT
    halo_per_unit = UNIT // POOL_HALO
    last = n_units - 1
    last_halo = x2.shape[0] // POOL_HALO - 1
    unit_a = lambda i: jnp.minimum(i, last)
    unit_c = lambda i: jnp.clip(i - 1, 0, last)
    mod_row = _mod_row_map(mod, units_per_seq)
    halo = POOL_HALO if units_per_seq > 1 else 0
    x_spec = pl.BlockSpec((UNIT, D_MODEL), lambda i: (unit_a(i), 0))
    if halo:
        x_specs = [pl.BlockSpec((halo, D_MODEL), lambda i: (jnp.maximum(unit_a(i) * halo_per_unit - 1, 0), 0)),
                   x_spec,
                   pl.BlockSpec((halo, D_MODEL), lambda i: (jnp.minimum((unit_a(i) + 1) * halo_per_unit, last_halo), 0))]
    else:
        x_specs = [x_spec]
    in_specs = x_specs + [pl.BlockSpec((1, 1, N_MOD), lambda i: (mod_row(unit_a(i)), 0, 0)),
                          pl.BlockSpec((1, 1, N_MOD), lambda i: (mod_row(unit_c(i)), 0, 0))]
    in_specs += [_resident(p.shape) for p in params]
    return pl.pallas_call(
        functools.partial(_odd_kernel, n_units=n_units, seq_len=seq_len, halo=halo),
        grid=(n_units + 1,),
        in_specs=in_specs,
        out_specs=pl.BlockSpec((UNIT, D_MODEL), lambda i: (unit_c(i), 0)),
        out_shape=jax.ShapeDtypeStruct(x2.shape, F32),
        scratch_shapes=[pltpu.VMEM((2, UNIT, D_MODEL), F32),
                        pltpu.VMEM((2, UNIT, D_MODEL), BF16)],
        compiler_params=pltpu.CompilerParams(
            dimension_semantics=("arbitrary",),
            vmem_limit_bytes=V7X_VMEM_LIMIT_BYTES),
        name="odd_layer",
    )(*[x2] * len(x_specs), mod, mod, *params)


def _rope_tables(n_tokens):
    rows = n_tokens // GRID_W
    pos_r = jnp.repeat(jnp.arange(rows, dtype=F32), GRID_W)
    pos_c = jnp.tile(jnp.arange(GRID_W, dtype=F32), rows)
    inv = ROPE_BASE ** (-jnp.arange(ROPE_PAIRS_AXIS, dtype=F32) / ROPE_PAIRS_AXIS)
    ang = jnp.concatenate([pos_r[:, None] * inv, pos_c[:, None] * inv], axis=-1)
    cos, sin = jnp.cos(ang), jnp.sin(ang)
    return jnp.concatenate([cos, cos], axis=-1), jnp.concatenate([-sin, sin], axis=-1)


def kernel(x_prompt, x_sample, state_ret, c, c_ctx, ada_w, ada_b, norm_mix, norm_mlp, mlp_w1, mlp_w2,
           ev_w_in, ev_w_s, ev_b_s, ev_vnorm, ev_decay, ev_w_out, od_w_pool, od_pool_scale, final_norm):
    n_prompt, prompt_len, _ = x_prompt.shape
    n_sample, sample_len, _ = x_sample.shape
    assert DEPTH == 2 and n_sample + 1 <= COND_ROWS
    assert prompt_len == UNIT and sample_len % UNIT == 0

    cond = jnp.zeros((COND_ROWS, D_MODEL), F32).at[0].set(c_ctx).at[1:1 + n_sample].set(c)
    mod = _ada_call(cond, ada_w, ada_b.reshape(DEPTH, 1, N_MOD), n_layers=DEPTH)
    mod_rows = lambda m: (m[0:1].reshape(1, 1, N_MOD), m[1:1 + n_sample].reshape(n_sample, 1, N_MOD))

    row = lambda v: v.reshape(1, -1)
    w_in = ev_w_in[0].astype(BF16)
    dec = jnp.broadcast_to(ev_decay[0].reshape(2 * HEADS, 1), (2 * HEADS, CHUNK))
    even_params = (
        row(norm_mix[0]), row(norm_mlp[0]),
        w_in, ev_w_s[0].astype(BF16),
        jnp.broadcast_to(ev_b_s[0][:, :, None], (A_GROUPS, CHUNK, A_GROUP_DIM)),
        row(ev_vnorm[0]), dec,
        ev_w_out[0].astype(BF16),
        mlp_w1[0].astype(BF16), mlp_w2[0].astype(BF16),
    )
    mod0_prompt, mod0_sample = mod_rows(mod[0])
    mod1_prompt, mod1_sample = mod_rows(mod[1])

    xp2 = x_prompt.reshape(n_prompt * prompt_len, D_MODEL)
    xp2, new_state, w1_odd, w2_odd = _even_call(xp2, mod0_prompt, None, None, even_params, seq_len=prompt_len,
                                                prep=_NextLayerPrep(1, mlp_w1, mlp_w2))
    odd_params = (
        row(norm_mix[1]), row(norm_mlp[1]), row(final_norm),
        od_w_pool[0].astype(BF16), row(od_pool_scale[0]),
        w1_odd, w2_odd,
    )
    y_prompt = _odd_call(xp2, mod1_prompt, odd_params, seq_len=prompt_len)

    xs2 = x_sample.reshape(n_sample * sample_len, D_MODEL)
    rope = _rope_tables(sample_len)
    sp = _ret_state_call(xs2, mod0_sample, state_ret, rope, row(norm_mix[0]), w_in, dec,
                         n_seq=n_sample, seq_len=sample_len)
    (xs2,) = _even_call(xs2, mod0_sample, sp, rope, even_params, seq_len=sample_len)
    y_sample = _odd_call(xs2, mod1_sample, odd_params, seq_len=sample_len)

    return (y_prompt.reshape(n_prompt, prompt_len, D_MODEL),
            y_sample.reshape(n_sample, sample_len, D_MODEL),
            new_state)
```

```python
import functools

import jax
import jax.numpy as jnp
from jax import lax
from jax.experimental import pallas as pl
from jax.experimental.pallas import tpu as pltpu

D_MODEL = 1024
DEPTH = 2
GRID_W = 64
A_WIDTH = D_MODEL // 2
A_GROUPS = 4
A_GROUP_DIM = A_WIDTH // A_GROUPS
CHUNK = 128
HEADS = 4
HEAD_DIM = (D_MODEL // 2) // HEADS
ROPE_BASE = 10000.0
ROPE_PAIRS_AXIS = HEAD_DIM // 4
POOL_WINDOWS = (2, 4, 8, 16)
POOL_DIM = D_MODEL // len(POOL_WINDOWS)
POOL_HALO = max(POOL_WINDOWS) // 2
D_FF = 4 * D_MODEL
EPS = 1e-6
QK_W = HEADS * HEAD_DIM
V_W = HEADS * HEAD_DIM
IN_WIDTH = 2 * A_WIDTH + 2 * QK_W + 3 * V_W
OUT_WIDTH = A_WIDTH + V_W
N_MOD = 6 * D_MODEL
K_COL = 2 * A_WIDTH + QK_W
V_COL = K_COL + QK_W

UNIT = 2 * CHUNK
UNIT_CHUNKS = UNIT // CHUNK
FF_CHUNK = 1024
ADA_BLOCK_N = 1536
COND_ROWS = 8
V7X_VMEM_LIMIT_BYTES = 60 * 1024 * 1024

F32 = jnp.float32
BF16 = jnp.bfloat16


def _dot(a, b):
    return jnp.dot(a, b, preferred_element_type=F32)


def _dot_nt(a, b):
    return lax.dot_general(a, b, (((1,), (1,)), ((), ())), preferred_element_type=F32)


def _silu(x):
    return x * jax.nn.sigmoid(x)


def _gelu_tanh(x):
    return 0.5 * x * (1.0 + jnp.tanh(0.7978845608028654 * (x + 0.044715 * (x * x * x))))


def _rms(x):
    return x * lax.rsqrt(jnp.mean(x * x, axis=-1, keepdims=True) + EPS)


def _rms_mod(x, gain, shift, scale):
    return (_rms(x) * gain) * (1.0 + scale) + shift


def _layer_norm(x):
    mu = jnp.mean(x, axis=-1, keepdims=True)
    d = x - mu
    return d * lax.rsqrt(jnp.mean(d * d, axis=-1, keepdims=True) + EPS)


def _split_mod(mod):
    return [mod[:, i * D_MODEL:(i + 1) * D_MODEL] for i in range(6)]


def _rope(x, cos2, sin2):
    return x * cos2 + pltpu.roll(x, HEAD_DIM // 2, axis=1) * sin2


def _pick_block(take_first, a_ref, b_ref):
    rows = lax.broadcasted_iota(jnp.int32, (a_ref.shape[0], 1), 0)
    return jnp.where(rows < jnp.where(take_first, a_ref.shape[0], 0), a_ref[...], b_ref[...])


def _run(gen, n=None):
    done = 0
    while n is None or done < n:
        try:
            next(gen)
        except StopIteration:
            break
        done += 1
    return done


def _weave(heavy, lights, n_light):
    heavy = list(heavy)
    pending = _roundrobin(lights)
    ran = 0
    for k, piece in enumerate(heavy):
        piece()
        want = ((k + 1) * n_light + len(heavy) - 1) // len(heavy)
        ran += _run(pending, want - ran)
    _run(pending)


def _pieces(gen, n):
    return [functools.partial(next, gen, None) for _ in range(n)]


def _roundrobin(gens):
    live = list(gens)
    while live:
        for g in list(live):
            try:
                next(g)
            except StopIteration:
                live.remove(g)
                continue
            yield


MLP_DOT_PIECES = 2 * (D_FF // FF_CHUNK)


def _mlp_pieces(hb, w1_ref, w2_ref):
    n_ff = D_FF // FF_CHUNK
    up = lambda j: _dot(hb, w1_ref[:, j * FF_CHUNK:(j + 1) * FF_CHUNK])
    a = up(0)
    yield
    acc = None
    for j in range(n_ff):
        a_next = None
        if j + 1 < n_ff:
            a_next = up(j + 1)
            yield
        p = _dot(jnp.square(jnp.maximum(a, 0.0)).astype(BF16), w2_ref[j * FF_CHUNK:(j + 1) * FF_CHUNK, :])
        acc = p if acc is None else acc + p
        a = a_next
        yield
    return acc


def _resident(shape):
    return pl.BlockSpec(shape, lambda i: (0,) * len(shape), pipeline_mode=pl.Buffered(1))


class _Units:
    def __init__(self, n_short, n_long, units_per_long):
        self.n_short, self.n_long, self.units_per_long = n_short, n_long * units_per_long, units_per_long
        self.total = self.n_short + self.n_long

    def clamp(self, unit):
        return jnp.clip(unit, 0, self.total - 1)

    def short_block(self, unit):
        return jnp.minimum(unit, self.n_short - 1)

    def long_block(self, unit):
        return jnp.clip(unit - self.n_short, 0, self.n_long - 1)

    def mod_row(self, unit):
        return jnp.where(unit < self.n_short, 0, 1 + (unit - self.n_short) // self.units_per_long)

    def rope_block(self, unit):
        return jnp.where(unit < self.n_short, 0, 1 + (unit - self.n_short) % self.units_per_long)

    def state_block(self, unit):
        return jnp.where(unit < self.n_short, 0, self.units_per_long + unit - self.n_short)


def _ada_kernel(cond_ref, w_ref, b_ref, out_ref):
    s = _silu(cond_ref[...]).astype(BF16)
    out_ref[0] = _dot(s, w_ref[0].astype(BF16)) + b_ref[0]


def _ada_call(cond, ada_w, ada_b):
    return pl.pallas_call(
        _ada_kernel,
        grid=(DEPTH, N_MOD // ADA_BLOCK_N),
        in_specs=[
            pl.BlockSpec((COND_ROWS, D_MODEL), lambda i, j: (0, 0)),
            pl.BlockSpec((1, D_MODEL, ADA_BLOCK_N), lambda i, j: (i, 0, j)),
            pl.BlockSpec((1, 1, ADA_BLOCK_N), lambda i, j: (i, 0, j)),
        ],
        out_specs=pl.BlockSpec((1, COND_ROWS, ADA_BLOCK_N), lambda i, j: (i, 0, j)),
        out_shape=jax.ShapeDtypeStruct((DEPTH, COND_ROWS, N_MOD), F32),
        compiler_params=pltpu.CompilerParams(
            dimension_semantics=("arbitrary", "arbitrary"),
            vmem_limit_bytes=V7X_VMEM_LIMIT_BYTES),
        name="ada_mod",
    )(cond, ada_w, ada_b.reshape(DEPTH, 1, N_MOD))


def _decay_tables(dec_ref, hd):
    row = lax.broadcasted_iota(jnp.int32, (CHUNK, CHUNK), 0).astype(F32)
    col = lax.broadcasted_iota(jnp.int32, (CHUNK, CHUNK), 1).astype(F32)
    lg_f = jnp.broadcast_to(-jnp.exp(dec_ref[hd:hd + 1, :]), (CHUNK, CHUNK))
    lg_b = jnp.broadcast_to(-jnp.exp(dec_ref[HEADS + hd:HEADS + hd + 1, :]), (CHUNK, CHUNK))
    d_f = row - col
    d_b = col - row
    return dict(
        decay_f=jnp.where(d_f >= 0, jnp.exp(lg_f * jnp.maximum(d_f, 0.0)), 0.0),
        decay_b=jnp.where(d_b >= 0, jnp.exp(lg_b * jnp.maximum(d_b, 0.0)), 0.0),
        qd_f=jnp.exp(lg_f * (row + 1.0)),
        qd_b=jnp.exp(lg_b * (CHUNK - row)),
        kd_f=jnp.exp(lg_f * (CHUNK - 1.0 - row)),
        kd_b=jnp.exp(lg_b * row),
        cd_f=jnp.exp(lg_f * CHUNK),
        cd_b=jnp.exp(lg_b * CHUNK),
    )


TABLE_NAMES = ("decay_f", "decay_b", "qd_f", "qd_b", "kd_f", "kd_b", "cd_f", "cd_b")


def _ret_state_kernel(x_ref, mod_ref, s0_ref, cos_ref, sin_ref, nmix_ref, wk_ref, wv_ref, dec_ref, sp_ref,
                      *, n_chunk):
    step = pl.program_id(0)

    @pl.when(step == 0)
    def _():
        sp_ref[...] = jnp.zeros_like(sp_ref)

    @pl.when(step > 0)
    def _():
        sh1, sc1 = _split_mod(mod_ref[0])[:2]
        hb = _rms_mod(x_ref[...], nmix_ref[...], sh1, sc1).astype(BF16)
        zk = _dot(hb, wk_ref[...]) * (HEAD_DIM ** -0.5)
        zv = _dot(hb, wv_ref[...]).astype(BF16)
        cos2, sin2 = cos_ref[...], sin_ref[...]
        for hd in range(HEADS):
            lo = hd * HEAD_DIM
            tb = _decay_tables(dec_ref, hd)
            kh = _rope(zk[:, lo:lo + HEAD_DIM], cos2, sin2)
            kv_f, kv_b = [], []
            for c in range(n_chunk):
                kc = kh[c * CHUNK:(c + 1) * CHUNK]
                vc = zv[c * CHUNK:(c + 1) * CHUNK, lo:lo + HEAD_DIM]
                kv_f.append(_dot((kc * tb["kd_f"]).T.astype(BF16), vc))
                kv_b.append(_dot((kc * tb["kd_b"]).T.astype(BF16), vc))
            st = s0_ref[0, 0, 0, hd]
            for c in range(n_chunk):
                if c % UNIT_CHUNKS == 0:
                    sp_ref[c // UNIT_CHUNKS, 0, hd] = st
                st = tb["cd_f"] * st + kv_f[c]
            st = s0_ref[0, 0, 1, hd]
            for c in reversed(range(n_chunk)):
                if c % UNIT_CHUNKS == UNIT_CHUNKS - 1:
                    sp_ref[c // UNIT_CHUNKS, 1, hd] = st
                st = tb["cd_b"] * st + kv_b[c]


def _ret_state_call(x2, mod, s0, rope, nmix, w_in, dec, *, n_seq, seq_len):
    n_chunk = seq_len // CHUNK
    units = seq_len // UNIT
    seq = lambda i: jnp.maximum(i - 1, 0)
    return pl.pallas_call(
        functools.partial(_ret_state_kernel, n_chunk=n_chunk),
        grid=(n_seq + 1,),
        in_specs=[
            pl.BlockSpec((seq_len, D_MODEL), lambda i: (seq(i), 0)),
            pl.BlockSpec((1, 1, N_MOD), lambda i: (seq(i), 0, 0)),
            pl.BlockSpec((1, 1, 2, HEADS, HEAD_DIM, HEAD_DIM), lambda i: (seq(i), 0, 0, 0, 0, 0)),
            _resident((seq_len, HEAD_DIM)), _resident((seq_len, HEAD_DIM)),
            _resident((1, D_MODEL)),
            pl.BlockSpec((D_MODEL, QK_W), lambda i: (0, K_COL // QK_W), pipeline_mode=pl.Buffered(1)),
            pl.BlockSpec((D_MODEL, V_W), lambda i: (0, V_COL // V_W), pipeline_mode=pl.Buffered(1)),
            _resident(dec.shape),
        ],
        out_specs=pl.BlockSpec((units, 2, HEADS, HEAD_DIM, HEAD_DIM), lambda i: (i, 0, 0, 0, 0)),
        out_shape=jax.ShapeDtypeStruct(((n_seq + 1) * units, 2, HEADS, HEAD_DIM, HEAD_DIM), F32),
        compiler_params=pltpu.CompilerParams(
            dimension_semantics=("arbitrary",),
            vmem_limit_bytes=V7X_VMEM_LIMIT_BYTES),
        name="ret_state",
    )(x2, mod, s0, rope[0], rope[1], nmix, w_in, w_in, dec)


def _even_kernel(xa_short, xa_long, xc_short, xc_long, moda_ref, modc_ref, sp_ref, cos_ref, sin_ref,
                 nmix_ref, nmlp_ref, w_in_ref, w_s_ref, b_s_ref, vgain_ref, dec_ref, w_out_ref, w1_ref, w2_ref,
                 y_ref, st_ref, ug_s, vg_s, q_s, k_s, v_s, gf_s, gb_s, cat_s, tab_s, *, n_units, n_short):
    step = pl.program_id(0)
    slot_a = lax.rem(step, 2)
    slot_b = 1 - slot_a

    @pl.when(step == 0)
    def _():
        for hd in range(HEADS):
            tb = _decay_tables(dec_ref, hd)
            for n, name in enumerate(TABLE_NAMES):
                tab_s[hd, n] = tb[name]

    def stage_in():
        sh1, sc1 = _split_mod(moda_ref[0])[:2]
        x = _pick_block(step < n_short, xa_short, xa_long)
        hb = _rms_mod(x, nmix_ref[...], sh1, sc1).astype(BF16)
        z = _dot(hb, w_in_ref[...])
        yield
        ug_s[slot_a] = _gelu_tanh(z[:, 0:A_WIDTH])
        yield
        zv = _gelu_tanh(z[:, A_WIDTH:2 * A_WIDTH])
        vgn = [_layer_norm(zv[:, g * A_GROUP_DIM:(g + 1) * A_GROUP_DIM]) for g in range(A_GROUPS)]
        vg_s[slot_a] = (jnp.concatenate(vgn, axis=-1) * vgain_ref[...]).astype(BF16)
        yield
        cos2, sin2 = cos_ref[...], sin_ref[...]
        heads = lambda zz: jnp.concatenate(
            [_rope(zz[:, h * HEAD_DIM:(h + 1) * HEAD_DIM], cos2, sin2) for h in range(HEADS)], -1)
        q_s[slot_a] = heads(z[:, 2 * A_WIDTH:K_COL]).astype(BF16)
        yield
        k_s[slot_a] = heads(z[:, K_COL:V_COL] * (HEAD_DIM ** -0.5))
        v_s[slot_a] = z[:, V_COL:V_COL + V_W].astype(BF16)
        yield
        gf_s[slot_a] = _silu(z[:, V_COL + V_W:V_COL + 2 * V_W])
        yield
        gb_s[slot_a] = _silu(z[:, V_COL + 2 * V_W:V_COL + 3 * V_W])

    IN_LIGHT = 6

    def stage_mix(final_states):
        s = slot_b
        rows = [slice(c * CHUNK, (c + 1) * CHUNK) for c in range(UNIT_CHUNKS)]
        head_cols = [slice(hd * HEAD_DIM, (hd + 1) * HEAD_DIM) for hd in range(HEADS)]
        table = lambda hd, name: tab_s[hd, TABLE_NAMES.index(name)]

        sp2 = []
        for g in range(A_GROUPS):
            lo = g * A_GROUP_DIM
            vg2 = jnp.concatenate([vg_s[s, r, lo:lo + A_GROUP_DIM] for r in rows], axis=-1)
            sp2.append(_dot(w_s_ref[g], vg2))
        yield
        scores, kv_f, kv_b = [], [], []
        for hd in range(HEADS):
            kc = [k_s[s, r, head_cols[hd]] for r in rows]
            vc = [v_s[s, r, head_cols[hd]] for r in rows]
            scores.append([_dot_nt(q_s[s, r, head_cols[hd]], kc[c].astype(BF16)) for c, r in enumerate(rows)])
            kv_f.append([_dot((kc[c] * table(hd, "kd_f")).T.astype(BF16), vc[c]) for c in range(UNIT_CHUNKS)])
            kv_b.append([_dot((kc[c] * table(hd, "kd_b")).T.astype(BF16), vc[c]) for c in range(UNIT_CHUNKS)])
            yield
        for g in range(A_GROUPS):
            lo = g * A_GROUP_DIM
            for c, r in enumerate(rows):
                gate = sp2[g][:, c * CHUNK:(c + 1) * CHUNK] + b_s_ref[g]
                cat_s[s, r, lo:lo + A_GROUP_DIM] = (ug_s[s, r, lo:lo + A_GROUP_DIM] * gate).astype(BF16)
            if g % 2 == 1:
                yield
        y_raw = []
        for hd in range(HEADS):
            enter_f, st = [], sp_ref[0, 0, hd]
            for c in range(UNIT_CHUNKS):
                enter_f.append(st)
                st = table(hd, "cd_f") * st + kv_f[hd][c]
            final_f = st
            enter_b, st = [None] * UNIT_CHUNKS, sp_ref[0, 1, hd]
            for c in reversed(range(UNIT_CHUNKS)):
                enter_b[c] = st
                st = table(hd, "cd_b") * st + kv_b[hd][c]
            final_states.append((final_f, st))
            per_chunk = []
            for c, r in enumerate(rows):
                vc = v_s[s, r, head_cols[hd]]
                qf = q_s[s, r, head_cols[hd]].astype(F32)

                def direction(decay, qd, enter):
                    p = (scores[hd][c] * table(hd, decay)).astype(BF16)
                    lhs = jnp.concatenate([p, (qf * table(hd, qd)).astype(BF16)], axis=-1)
                    return _dot(lhs, jnp.concatenate([vc, enter.astype(BF16)], axis=0))

                per_chunk.append((direction("decay_f", "qd_f", enter_f[c]), direction("decay_b", "qd_b", enter_b[c])))
            y_raw.append(per_chunk)
            yield
        for hd in range(HEADS):
            for c, r in enumerate(rows):
                y_f, y_b = y_raw[hd][c]
                cols = slice(A_WIDTH + hd * HEAD_DIM, A_WIDTH + (hd + 1) * HEAD_DIM)
                cat_s[s, r, cols] = (gf_s[s, r, head_cols[hd]] * _layer_norm(y_f)
                                     + gb_s[s, r, head_cols[hd]] * _layer_norm(y_b)).astype(BF16)
            yield

    MIX_LIGHT = 1 + HEADS + A_GROUPS // 2 + 2 * HEADS

    def emit_states(final_states):
        @pl.when(step - 1 < n_short)
        def _():
            for hd, (final_f, final_b) in enumerate(final_states):
                st_ref[0, 0, 0, hd] = final_f
                st_ref[0, 0, 1, hd] = final_b

    def stage_mlp():
        _, _, g1, sh2, sc2, g2 = _split_mod(modc_ref[0])
        x = _pick_block(step - 2 < n_short, xc_short, xc_long) + g1 * _dot(cat_s[slot_a], w_out_ref[...])
        hb = _rms_mod(x, nmlp_ref[...], sh2, sc2).astype(BF16)
        yield
        acc = yield from _mlp_pieces(hb, w1_ref, w2_ref)
        y_ref[...] = x + g2 * acc

    @pl.when(step == 0)
    def _():
        _run(stage_in())

    @pl.when(step == 1)
    def _():
        a, finals = stage_in(), []
        _weave(_pieces(a, 1), [a, stage_mix(finals)], IN_LIGHT + MIX_LIGHT)
        emit_states(finals)

    @pl.when((step >= 2) & (step < n_units))
    def _():
        a, c, finals = stage_in(), stage_mlp(), []
        heavy = _pieces(c, 1) + _pieces(a, 1) + _pieces(c, MLP_DOT_PIECES + 1)
        _weave(heavy, [a, stage_mix(finals)], IN_LIGHT + MIX_LIGHT)
        emit_states(finals)

    @pl.when(step == n_units)
    def _():
        finals = []
        _weave(_pieces(stage_mlp(), MLP_DOT_PIECES + 2), [stage_mix(finals)], MIX_LIGHT)
        emit_states(finals)

    @pl.when(step == n_units + 1)
    def _():
        _run(stage_mlp())


def _even_call(x_short, x_long, mod, sp, rope, params, units):
    n_units = units.total
    unit_a = lambda i: units.clamp(i)
    unit_b = lambda i: units.clamp(i - 1)
    unit_c = lambda i: units.clamp(i - 2)
    state_block = (1, 1, 2, HEADS, HEAD_DIM, HEAD_DIM)

    in_specs = [pl.BlockSpec((UNIT, D_MODEL), lambda i: (units.short_block(unit_a(i)), 0)),
                pl.BlockSpec((UNIT, D_MODEL), lambda i: (units.long_block(unit_a(i)), 0)),
                pl.BlockSpec((UNIT, D_MODEL), lambda i: (units.short_block(unit_c(i)), 0)),
                pl.BlockSpec((UNIT, D_MODEL), lambda i: (units.long_block(unit_c(i)), 0)),
                pl.BlockSpec((1, 1, N_MOD), lambda i: (units.mod_row(unit_a(i)), 0, 0)),
                pl.BlockSpec((1, 1, N_MOD), lambda i: (units.mod_row(unit_c(i)), 0, 0)),
                pl.BlockSpec((1, 2, HEADS, HEAD_DIM, HEAD_DIM), lambda i: (units.state_block(unit_b(i)), 0, 0, 0, 0)),
                pl.BlockSpec((UNIT, HEAD_DIM), lambda i: (units.rope_block(unit_a(i)), 0)),
                pl.BlockSpec((UNIT, HEAD_DIM), lambda i: (units.rope_block(unit_a(i)), 0))]
    in_specs += [_resident(p.shape) for p in params]
    scratch = [
        pltpu.VMEM((2, UNIT, A_WIDTH), F32),
        pltpu.VMEM((2, UNIT, A_WIDTH), BF16),
        pltpu.VMEM((2, UNIT, QK_W), BF16),
        pltpu.VMEM((2, UNIT, QK_W), F32),
        pltpu.VMEM((2, UNIT, V_W), BF16),
        pltpu.VMEM((2, UNIT, V_W), F32),
        pltpu.VMEM((2, UNIT, V_W), F32),
        pltpu.VMEM((2, UNIT, OUT_WIDTH), BF16),
        pltpu.VMEM((HEADS, len(TABLE_NAMES), CHUNK, CHUNK), F32),
    ]
    return pl.pallas_call(
        functools.partial(_even_kernel, n_units=n_units, n_short=units.n_short),
        grid=(n_units + 2,),
        in_specs=in_specs,
        out_specs=[pl.BlockSpec((UNIT, D_MODEL), lambda i: (unit_c(i), 0)),
                   pl.BlockSpec(state_block, lambda i: (units.short_block(unit_b(i)), 0, 0, 0, 0, 0))],
        out_shape=[jax.ShapeDtypeStruct((n_units * UNIT, D_MODEL), F32),
                   jax.ShapeDtypeStruct((units.n_short,) + state_block[1:], F32)],
        scratch_shapes=scratch,
        compiler_params=pltpu.CompilerParams(
            dimension_semantics=("arbitrary",),
            vmem_limit_bytes=V7X_VMEM_LIMIT_BYTES),
        name="even_layer",
    )(x_short, x_long, x_short, x_long, mod, mod, sp, rope[0], rope[1], *params)


def _odd_kernel(xp_ref, x_ref, xn_ref, moda_ref, modc_ref, nmix_ref, nmlp_ref, fin_ref, w_pool_ref, pscale_ref,
                w1_ref, w2_ref, y_short, y_long, x1_s, hb_s, *, n_units, n_short, short_len, long_len):
    step = pl.program_id(0)
    slot_a = lax.rem(step, 2)
    slot_b = 1 - slot_a
    rows_w = UNIT + 2 * POOL_HALO

    def stage_pool():
        sh1, sc1, g1, sh2, sc2, _ = _split_mod(moda_ref[0])
        xw = jnp.concatenate([xp_ref[...], x_ref[...], xn_ref[...]], axis=0)
        h = _rms_mod(xw, nmix_ref[...], sh1, sc1)
        seq_len = jnp.where(step < n_short, short_len, long_len)
        first = jnp.minimum(step, n_units - 1) * UNIT - POOL_HALO
        t = (lax.broadcasted_iota(jnp.int32, (rows_w, 1), 0) + first) & (seq_len - 1)
        yield

        def shift_down(a, k):
            return jnp.where(t >= k, pltpu.roll(a, k, axis=0), 0.0)

        def shift_up(a, k):
            return jnp.where(t < seq_len - k, pltpu.roll(a, rows_w - k, axis=0), 0.0)

        outs = []
        for g, w in enumerate(POOL_WINDOWS):
            hg = h[:, g * POOL_DIM:(g + 1) * POOL_DIM]
            half = w // 2
            left = shift_down(hg, 1)
            right = hg
            span = 1
            while span < half:
                left = left + shift_down(left, span)
                right = right + shift_up(right, span)
                span *= 2
            cnt = (jnp.minimum(t + half, seq_len) - jnp.maximum(t - half, 0)).astype(F32)
            pooled = ((left + right) / cnt - hg)[POOL_HALO:POOL_HALO + UNIT].astype(BF16)
            outs.append(_dot(pooled, w_pool_ref[g]))
            yield
        x1 = x_ref[...] + g1 * (jnp.concatenate(outs, axis=-1) * pscale_ref[...])
        x1_s[slot_a] = x1
        hb_s[slot_a] = _rms_mod(x1, nmlp_ref[...], sh2, sc2).astype(BF16)

    def stage_mlp(result):
        g2 = _split_mod(modc_ref[0])[5]
        acc = yield from _mlp_pieces(hb_s[slot_b], w1_ref, w2_ref)
        result.append(_rms(x1_s[slot_b] + g2 * acc) * fin_ref[...])

    def store(result):
        @pl.when(step - 1 < n_short)
        def _():
            y_short[...] = result[0]

        @pl.when(step - 1 >= n_short)
        def _():
            y_long[...] = result[0]

    pool_light = 2 + len(POOL_WINDOWS)

    @pl.when(step == 0)
    def _():
        _run(stage_pool())

    @pl.when((step >= 1) & (step < n_units))
    def _():
        result = []
        _weave(_pieces(stage_mlp(result), MLP_DOT_PIECES + 1), [stage_pool()], pool_light)
        store(result)

    @pl.when(step == n_units)
    def _():
        result = []
        _run(stage_mlp(result))
        store(result)


def _odd_call(x2, mod, params, units, *, short_len, long_len):
    n_units = units.total
    halo_per_unit = UNIT // POOL_HALO
    last_halo = x2.shape[0] // POOL_HALO - 1
    unit_a = lambda i: units.clamp(i)
    unit_c = lambda i: units.clamp(i - 1)
    assert (units.n_short * UNIT) % long_len == 0
    in_specs = [pl.BlockSpec((POOL_HALO, D_MODEL), lambda i: (jnp.maximum(unit_a(i) * halo_per_unit - 1, 0), 0)),
                pl.BlockSpec((UNIT, D_MODEL), lambda i: (unit_a(i), 0)),
                pl.BlockSpec((POOL_HALO, D_MODEL),
                             lambda i: (jnp.minimum((unit_a(i) + 1) * halo_per_unit, last_halo), 0)),
                pl.BlockSpec((1, 1, N_MOD), lambda i: (units.mod_row(unit_a(i)), 0, 0)),
                pl.BlockSpec((1, 1, N_MOD), lambda i: (units.mod_row(unit_c(i)), 0, 0))]
    in_specs += [_resident(p.shape) for p in params]
    return pl.pallas_call(
        functools.partial(_odd_kernel, n_units=n_units, n_short=units.n_short,
                          short_len=short_len, long_len=long_len),
        grid=(n_units + 1,),
        in_specs=in_specs,
        out_specs=[pl.BlockSpec((UNIT, D_MODEL), lambda i: (units.short_block(unit_c(i)), 0)),
                   pl.BlockSpec((UNIT, D_MODEL), lambda i: (units.long_block(unit_c(i)), 0))],
        out_shape=[jax.ShapeDtypeStruct((units.n_short * UNIT, D_MODEL), F32),
                   jax.ShapeDtypeStruct((units.n_long * UNIT, D_MODEL), F32)],
        scratch_shapes=[pltpu.VMEM((2, UNIT, D_MODEL), F32),
                        pltpu.VMEM((2, UNIT, D_MODEL), BF16)],
        compiler_params=pltpu.CompilerParams(
            dimension_semantics=("arbitrary",),
            vmem_limit_bytes=V7X_VMEM_LIMIT_BYTES),
        name="odd_layer",
    )(x2, x2, x2, mod, mod, *params)


def _rope_tables(n_tokens):
    rows = n_tokens // GRID_W
    pos_r = jnp.repeat(jnp.arange(rows, dtype=F32), GRID_W)
    pos_c = jnp.tile(jnp.arange(GRID_W, dtype=F32), rows)
    inv = ROPE_BASE ** (-jnp.arange(ROPE_PAIRS_AXIS, dtype=F32) / ROPE_PAIRS_AXIS)
    ang = jnp.concatenate([pos_r[:, None] * inv, pos_c[:, None] * inv], axis=-1)
    cos, sin = jnp.cos(ang), jnp.sin(ang)
    return jnp.concatenate([cos, cos], axis=-1), jnp.concatenate([-sin, sin], axis=-1)


def kernel(x_prompt, x_sample, state_ret, c, c_ctx, ada_w, ada_b, norm_mix, norm_mlp, mlp_w1, mlp_w2,
           ev_w_in, ev_w_s, ev_b_s, ev_vnorm, ev_decay, ev_w_out, od_w_pool, od_pool_scale, final_norm):
    n_prompt, prompt_len, _ = x_prompt.shape
    n_sample, sample_len, _ = x_sample.shape
    assert DEPTH == 2 and n_sample + 1 <= COND_ROWS
    assert prompt_len == UNIT and sample_len % UNIT == 0
    units = _Units(n_prompt, n_sample, sample_len // UNIT)

    cond = jnp.zeros((COND_ROWS, D_MODEL), F32).at[0].set(c_ctx).at[1:1 + n_sample].set(c)
    mod = _ada_call(cond, ada_w, ada_b)[:, :1 + n_sample].reshape(DEPTH, 1 + n_sample, 1, N_MOD)

    row = lambda v: v.reshape(1, -1)
    w_in = ev_w_in[0].astype(BF16)
    dec = jnp.broadcast_to(ev_decay[0].reshape(2 * HEADS, 1), (2 * HEADS, CHUNK))
    even_params = (
        row(norm_mix[0]), row(norm_mlp[0]),
        w_in, ev_w_s[0].astype(BF16),
        jnp.broadcast_to(ev_b_s[0][:, :, None], (A_GROUPS, CHUNK, A_GROUP_DIM)),
        row(ev_vnorm[0]), dec,
        ev_w_out[0].astype(BF16),
        mlp_w1[0].astype(BF16), mlp_w2[0].astype(BF16),
    )
    odd_params = (
        row(norm_mix[1]), row(norm_mlp[1]), row(final_norm),
        od_w_pool[0].astype(BF16), row(od_pool_scale[0]),
        mlp_w1[1].astype(BF16), mlp_w2[1].astype(BF16),
    )

    xp2 = x_prompt.reshape(n_prompt * prompt_len, D_MODEL)
    xs2 = x_sample.reshape(n_sample * sample_len, D_MODEL)
    cos2, sin2 = _rope_tables(sample_len)
    sp = _ret_state_call(xs2, mod[0, 1:], state_ret, (cos2, sin2), row(norm_mix[0]), w_in, dec,
                         n_seq=n_sample, seq_len=sample_len)
    rope = (jnp.concatenate([jnp.ones((UNIT, HEAD_DIM), F32), cos2]),
            jnp.concatenate([jnp.zeros((UNIT, HEAD_DIM), F32), sin2]))
    x_all, new_state = _even_call(xp2, xs2, mod[0], sp, rope, even_params, units)
    y_prompt, y_sample = _odd_call(x_all, mod[1], odd_params, units, short_len=prompt_len, long_len=sample_len)

    return (y_prompt.reshape(n_prompt, prompt_len, D_MODEL),
            y_sample.reshape(n_sample, sample_len, D_MODEL),
            new_state)
```

```python
import functools

import jax
import jax.numpy as jnp
from jax import lax
from jax.experimental import pallas as pl
from jax.experimental.pallas import tpu as pltpu

D_MODEL = 1024
DEPTH = 2
GRID_W = 64
A_WIDTH = D_MODEL // 2
A_GROUPS = 4
A_GROUP_DIM = A_WIDTH // A_GROUPS
CHUNK = 128
HEADS = 4
HEAD_DIM = (D_MODEL // 2) // HEADS
ROPE_BASE = 10000.0
ROPE_PAIRS_AXIS = HEAD_DIM // 4
POOL_WINDOWS = (2, 4, 8, 16)
POOL_DIM = D_MODEL // len(POOL_WINDOWS)
POOL_HALO = max(POOL_WINDOWS) // 2
D_FF = 4 * D_MODEL
EPS = 1e-6
QK_W = HEADS * HEAD_DIM
V_W = HEADS * HEAD_DIM
IN_WIDTH = 2 * A_WIDTH + 2 * QK_W + 3 * V_W
OUT_WIDTH = A_WIDTH + V_W
N_MOD = 6 * D_MODEL
K_COL = 2 * A_WIDTH + QK_W
V_COL = K_COL + QK_W

UNIT = 2 * CHUNK
UNIT_CHUNKS = UNIT // CHUNK
FF_CHUNK = 1024
ADA_BLOCK_N = 1536
COND_ROWS = 8
V7X_VMEM_LIMIT_BYTES = 60 * 1024 * 1024

F32 = jnp.float32
BF16 = jnp.bfloat16


def _dot(a, b):
    return jnp.dot(a, b, preferred_element_type=F32)


def _dot_nt(a, b):
    return lax.dot_general(a, b, (((1,), (1,)), ((), ())), preferred_element_type=F32)


def _silu(x):
    return x * jax.nn.sigmoid(x)


def _gelu_tanh(x):
    return 0.5 * x * (1.0 + jnp.tanh(0.7978845608028654 * (x + 0.044715 * (x * x * x))))


def _rms(x):
    return x * lax.rsqrt(jnp.mean(x * x, axis=-1, keepdims=True) + EPS)


def _rms_mod(x, gain, shift, scale):
    return (_rms(x) * gain) * (1.0 + scale) + shift


def _layer_norm(x):
    mu = jnp.mean(x, axis=-1, keepdims=True)
    d = x - mu
    return d * lax.rsqrt(jnp.mean(d * d, axis=-1, keepdims=True) + EPS)


def _split_mod(mod):
    return [mod[:, i * D_MODEL:(i + 1) * D_MODEL] for i in range(6)]


def _rope(x, cos2, sin2):
    return x * cos2 + pltpu.roll(x, HEAD_DIM // 2, axis=1) * sin2


def _pick_block(take_first, a_ref, b_ref):
    rows = lax.broadcasted_iota(jnp.int32, (a_ref.shape[0], 1), 0)
    return jnp.where(rows < jnp.where(take_first, a_ref.shape[0], 0), a_ref[...], b_ref[...])


def _run(gen, n=None):
    done = 0
    while n is None or done < n:
        try:
            next(gen)
        except StopIteration:
            break
        done += 1
    return done


def _weave(heavy, lights, n_light):
    heavy = list(heavy)
    pending = _roundrobin(lights)
    ran = 0
    for k, piece in enumerate(heavy):
        piece()
        want = ((k + 1) * n_light + len(heavy) - 1) // len(heavy)
        ran += _run(pending, want - ran)
    _run(pending)


def _pieces(gen, n):
    return [functools.partial(next, gen, None) for _ in range(n)]


def _roundrobin(gens):
    live = list(gens)
    while live:
        for g in list(live):
            try:
                next(g)
            except StopIteration:
                live.remove(g)
                continue
            yield


MLP_DOT_PIECES = 2 * (D_FF // FF_CHUNK)


def _mlp_pieces(hb, w1_ref, w2_ref):
    n_ff = D_FF // FF_CHUNK
    up = lambda j: _dot(hb, w1_ref[:, j * FF_CHUNK:(j + 1) * FF_CHUNK])
    a = up(0)
    yield
    acc = None
    for j in range(n_ff):
        a_next = None
        if j + 1 < n_ff:
            a_next = up(j + 1)
            yield
        p = _dot(jnp.square(jnp.maximum(a, 0.0)).astype(BF16), w2_ref[j * FF_CHUNK:(j + 1) * FF_CHUNK, :])
        acc = p if acc is None else acc + p
        a = a_next
        yield
    return acc


def _resident(shape):
    return pl.BlockSpec(shape, lambda i: (0,) * len(shape), pipeline_mode=pl.Buffered(1))


class _Units:
    def __init__(self, n_short, n_long, units_per_long):
        self.n_short, self.n_long, self.units_per_long = n_short, n_long * units_per_long, units_per_long
        self.total = self.n_short + self.n_long

    def clamp(self, unit):
        return jnp.clip(unit, 0, self.total - 1)

    def short_block(self, unit):
        return jnp.minimum(unit, self.n_short - 1)

    def long_block(self, unit):
        return jnp.clip(unit - self.n_short, 0, self.n_long - 1)

    def mod_row(self, unit):
        return jnp.where(unit < self.n_short, 0, 1 + (unit - self.n_short) // self.units_per_long)

    def rope_block(self, unit):
        return jnp.where(unit < self.n_short, 0, 1 + (unit - self.n_short) % self.units_per_long)

    def state_block(self, unit):
        return jnp.where(unit < self.n_short, 0, self.units_per_long + unit - self.n_short)


def _ada_kernel(cond_ref, w_ref, b_ref, out_ref):
    s = _silu(cond_ref[...]).astype(BF16)
    out_ref[0] = _dot(s, w_ref[0].astype(BF16)) + b_ref[0]


def _ada_call(cond, ada_w, ada_b):
    return pl.pallas_call(
        _ada_kernel,
        grid=(DEPTH, N_MOD // ADA_BLOCK_N),
        in_specs=[
            pl.BlockSpec((COND_ROWS, D_MODEL), lambda i, j: (0, 0)),
            pl.BlockSpec((1, D_MODEL, ADA_BLOCK_N), lambda i, j: (i, 0, j)),
            pl.BlockSpec((1, 1, ADA_BLOCK_N), lambda i, j: (i, 0, j)),
        ],
        out_specs=pl.BlockSpec((1, COND_ROWS, ADA_BLOCK_N), lambda i, j: (i, 0, j)),
        out_shape=jax.ShapeDtypeStruct((DEPTH, COND_ROWS, N_MOD), F32),
        compiler_params=pltpu.CompilerParams(
            dimension_semantics=("arbitrary", "arbitrary"),
            vmem_limit_bytes=V7X_VMEM_LIMIT_BYTES),
        name="ada_mod",
    )(cond, ada_w, ada_b.reshape(DEPTH, 1, N_MOD))


def _decay_tables(dec_ref, hd):
    row = lax.broadcasted_iota(jnp.int32, (CHUNK, CHUNK), 0).astype(F32)
    col = lax.broadcasted_iota(jnp.int32, (CHUNK, CHUNK), 1).astype(F32)
    lg_f = jnp.broadcast_to(-jnp.exp(dec_ref[hd:hd + 1, :]), (CHUNK, CHUNK))
    lg_b = jnp.broadcast_to(-jnp.exp(dec_ref[HEADS + hd:HEADS + hd + 1, :]), (CHUNK, CHUNK))
    d_f = row - col
    d_b = col - row
    return dict(
        decay_f=jnp.where(d_f >= 0, jnp.exp(lg_f * jnp.maximum(d_f, 0.0)), 0.0),
        decay_b=jnp.where(d_b >= 0, jnp.exp(lg_b * jnp.maximum(d_b, 0.0)), 0.0),
        qd_f=jnp.exp(lg_f * (row + 1.0)),
        qd_b=jnp.exp(lg_b * (CHUNK - row)),
        kd_f=jnp.exp(lg_f * (CHUNK - 1.0 - row)),
        kd_b=jnp.exp(lg_b * row),
        cd_f=jnp.exp(lg_f * CHUNK),
        cd_b=jnp.exp(lg_b * CHUNK),
    )


TABLE_NAMES = ("decay_f", "decay_b", "qd_f", "qd_b", "kd_f", "kd_b", "cd_f", "cd_b")


def _ret_state_kernel(x_ref, mod_ref, s0_ref, cos_ref, sin_ref, nmix_ref, wk_ref, wv_ref, dec_ref, sp_ref,
                      *, n_chunk):
    step = pl.program_id(0)

    @pl.when(step == 0)
    def _():
        sp_ref[...] = jnp.zeros_like(sp_ref)

    @pl.when(step > 0)
    def _():
        sh1, sc1 = _split_mod(mod_ref[0])[:2]
        hb = _rms_mod(x_ref[...], nmix_ref[...], sh1, sc1).astype(BF16)
        zk = _dot(hb, wk_ref[...]) * (HEAD_DIM ** -0.5)
        zv = _dot(hb, wv_ref[...]).astype(BF16)
        cos2, sin2 = cos_ref[...], sin_ref[...]
        for hd in range(HEADS):
            lo = hd * HEAD_DIM
            tb = _decay_tables(dec_ref, hd)
            kh = _rope(zk[:, lo:lo + HEAD_DIM], cos2, sin2)
            kv_f, kv_b = [], []
            for c in range(n_chunk):
                kc = kh[c * CHUNK:(c + 1) * CHUNK]
                vc = zv[c * CHUNK:(c + 1) * CHUNK, lo:lo + HEAD_DIM]
                kv_f.append(_dot((kc * tb["kd_f"]).T.astype(BF16), vc))
                kv_b.append(_dot((kc * tb["kd_b"]).T.astype(BF16), vc))
            st = s0_ref[0, 0, 0, hd]
            for c in range(n_chunk):
                if c % UNIT_CHUNKS == 0:
                    sp_ref[c // UNIT_CHUNKS, 0, hd] = st
                st = tb["cd_f"] * st + kv_f[c]
            st = s0_ref[0, 0, 1, hd]
            for c in reversed(range(n_chunk)):
                if c % UNIT_CHUNKS == UNIT_CHUNKS - 1:
                    sp_ref[c // UNIT_CHUNKS, 1, hd] = st
                st = tb["cd_b"] * st + kv_b[c]


def _ret_state_call(x2, mod, s0, rope, nmix, w_in, dec, *, n_seq, seq_len):
    n_chunk = seq_len // CHUNK
    units = seq_len // UNIT
    seq = lambda i: jnp.maximum(i - 1, 0)
    return pl.pallas_call(
        functools.partial(_ret_state_kernel, n_chunk=n_chunk),
        grid=(n_seq + 1,),
        in_specs=[
            pl.BlockSpec((seq_len, D_MODEL), lambda i: (seq(i), 0)),
            pl.BlockSpec((1, 1, N_MOD), lambda i: (seq(i), 0, 0)),
            pl.BlockSpec((1, 1, 2, HEADS, HEAD_DIM, HEAD_DIM), lambda i: (seq(i), 0, 0, 0, 0, 0)),
            _resident((seq_len, HEAD_DIM)), _resident((seq_len, HEAD_DIM)),
            _resident((1, D_MODEL)),
            pl.BlockSpec((D_MODEL, QK_W), lambda i: (0, K_COL // QK_W), pipeline_mode=pl.Buffered(1)),
            pl.BlockSpec((D_MODEL, V_W), lambda i: (0, V_COL // V_W), pipeline_mode=pl.Buffered(1)),
            _resident(dec.shape),
        ],
        out_specs=pl.BlockSpec((units, 2, HEADS, HEAD_DIM, HEAD_DIM), lambda i: (i, 0, 0, 0, 0)),
        out_shape=jax.ShapeDtypeStruct(((n_seq + 1) * units, 2, HEADS, HEAD_DIM, HEAD_DIM), F32),
        compiler_params=pltpu.CompilerParams(
            dimension_semantics=("arbitrary",),
            vmem_limit_bytes=V7X_VMEM_LIMIT_BYTES),
        name="ret_state",
    )(x2, mod, s0, rope[0], rope[1], nmix, w_in, w_in, dec)


def _even_kernel(xa_short, xa_long, xc_short, xc_long, moda_ref, modc_ref, sp_ref, cos_ref, sin_ref,
                 nmix_ref, nmlp_ref, w_in_ref, w_s_ref, b_s_ref, vgain_ref, dec_ref, w_out_ref, w1_ref, w2_ref,
                 y_ref, st_ref, ug_s, vg_s, q_s, k_s, v_s, gf_s, gb_s, cat_s, tab_s, *, units):
    n_units, n_short = units.total, units.n_short
    step = pl.program_id(0)
    slot_a = lax.rem(step, 2)
    slot_b = 1 - slot_a

    @pl.when(step == 0)
    def _():
        for hd in range(HEADS):
            tb = _decay_tables(dec_ref, hd)
            for n, name in enumerate(TABLE_NAMES):
                tab_s[hd, n] = tb[name]

    def stage_in():
        sh1, sc1 = _split_mod(moda_ref[0])[:2]
        x = _pick_block(step < n_short, xa_short, xa_long)
        hb = _rms_mod(x, nmix_ref[...], sh1, sc1).astype(BF16)
        z = _dot(hb, w_in_ref[...])
        yield
        ug_s[slot_a] = _gelu_tanh(z[:, 0:A_WIDTH])
        yield
        zv = _gelu_tanh(z[:, A_WIDTH:2 * A_WIDTH])
        vgn = [_layer_norm(zv[:, g * A_GROUP_DIM:(g + 1) * A_GROUP_DIM]) for g in range(A_GROUPS)]
        vg_s[slot_a] = (jnp.concatenate(vgn, axis=-1) * vgain_ref[...]).astype(BF16)
        yield
        pos = pl.ds(pl.multiple_of(units.rope_block(step) * UNIT, UNIT), UNIT)
        cos2, sin2 = cos_ref[pos, :], sin_ref[pos, :]
        heads = lambda zz: jnp.concatenate(
            [_rope(zz[:, h * HEAD_DIM:(h + 1) * HEAD_DIM], cos2, sin2) for h in range(HEADS)], -1)
        q_s[slot_a] = heads(z[:, 2 * A_WIDTH:K_COL]).astype(BF16)
        yield
        k_s[slot_a] = heads(z[:, K_COL:V_COL] * (HEAD_DIM ** -0.5))
        v_s[slot_a] = z[:, V_COL:V_COL + V_W].astype(BF16)
        yield
        gf_s[slot_a] = _silu(z[:, V_COL + V_W:V_COL + 2 * V_W])
        yield
        gb_s[slot_a] = _silu(z[:, V_COL + 2 * V_W:V_COL + 3 * V_W])

    IN_LIGHT = 6

    def stage_mix(final_states):
        s = slot_b
        rows = [slice(c * CHUNK, (c + 1) * CHUNK) for c in range(UNIT_CHUNKS)]
        head_cols = [slice(hd * HEAD_DIM, (hd + 1) * HEAD_DIM) for hd in range(HEADS)]
        table = lambda hd, name: tab_s[hd, TABLE_NAMES.index(name)]

        sp2 = []
        for g in range(A_GROUPS):
            lo = g * A_GROUP_DIM
            vg2 = jnp.concatenate([vg_s[s, r, lo:lo + A_GROUP_DIM] for r in rows], axis=-1)
            sp2.append(_dot(w_s_ref[g], vg2))
        yield
        scores, kv_f, kv_b = [], [], []
        for hd in range(HEADS):
            kc = [k_s[s, r, head_cols[hd]] for r in rows]
            vc = [v_s[s, r, head_cols[hd]] for r in rows]
            scores.append([_dot_nt(q_s[s, r, head_cols[hd]], kc[c].astype(BF16)) for c, r in enumerate(rows)])
            kv_f.append([_dot((kc[c] * table(hd, "kd_f")).T.astype(BF16), vc[c]) for c in range(UNIT_CHUNKS)])
            kv_b.append([_dot((kc[c] * table(hd, "kd_b")).T.astype(BF16), vc[c]) for c in range(UNIT_CHUNKS)])
            yield
        for g in range(A_GROUPS):
            lo = g * A_GROUP_DIM
            for c, r in enumerate(rows):
                gate = sp2[g][:, c * CHUNK:(c + 1) * CHUNK] + b_s_ref[g]
                cat_s[s, r, lo:lo + A_GROUP_DIM] = (ug_s[s, r, lo:lo + A_GROUP_DIM] * gate).astype(BF16)
            if g % 2 == 1:
                yield
        y_raw = []
        for hd in range(HEADS):
            enter_f, st = [], sp_ref[units.state_block(step - 1), 0, hd]
            for c in range(UNIT_CHUNKS):
                enter_f.append(st)
                st = table(hd, "cd_f") * st + kv_f[hd][c]
            final_f = st
            enter_b, st = [None] * UNIT_CHUNKS, sp_ref[units.state_block(step - 1), 1, hd]
            for c in reversed(range(UNIT_CHUNKS)):
                enter_b[c] = st
                st = table(hd, "cd_b") * st + kv_b[hd][c]
            final_states.append((final_f, st))
            per_chunk = []
            for c, r in enumerate(rows):
                vc = v_s[s, r, head_cols[hd]]
                qf = q_s[s, r, head_cols[hd]].astype(F32)

                def direction(decay, qd, enter):
                    p = (scores[hd][c] * table(hd, decay)).astype(BF16)
                    lhs = jnp.concatenate([p, (qf * table(hd, qd)).astype(BF16)], axis=-1)
                    return _dot(lhs, jnp.concatenate([vc, enter.astype(BF16)], axis=0))

                per_chunk.append((direction("decay_f", "qd_f", enter_f[c]), direction("decay_b", "qd_b", enter_b[c])))
            y_raw.append(per_chunk)
            yield
        for hd in range(HEADS):
            for c, r in enumerate(rows):
                y_f, y_b = y_raw[hd][c]
                cols = slice(A_WIDTH + hd * HEAD_DIM, A_WIDTH + (hd + 1) * HEAD_DIM)
                cat_s[s, r, cols] = (gf_s[s, r, head_cols[hd]] * _layer_norm(y_f)
                                     + gb_s[s, r, head_cols[hd]] * _layer_norm(y_b)).astype(BF16)
            yield

    MIX_LIGHT = 1 + HEADS + A_GROUPS // 2 + 2 * HEADS

    def emit_states(final_states):
        @pl.when(step - 1 < n_short)
        def _():
            for hd, (final_f, final_b) in enumerate(final_states):
                st_ref[0, 0, 0, hd] = final_f
                st_ref[0, 0, 1, hd] = final_b

    def stage_mlp():
        _, _, g1, sh2, sc2, g2 = _split_mod(modc_ref[0])
        x = _pick_block(step - 2 < n_short, xc_short, xc_long) + g1 * _dot(cat_s[slot_a], w_out_ref[...])
        hb = _rms_mod(x, nmlp_ref[...], sh2, sc2).astype(BF16)
        yield
        acc = yield from _mlp_pieces(hb, w1_ref, w2_ref)
        y_ref[...] = x + g2 * acc

    @pl.when(step == 0)
    def _():
        _run(stage_in())

    @pl.when(step == 1)
    def _():
        a, finals = stage_in(), []
        _weave(_pieces(a, 1), [a, stage_mix(finals)], IN_LIGHT + MIX_LIGHT)
        emit_states(finals)

    @pl.when((step >= 2) & (step < n_units))
    def _():
        a, c, finals = stage_in(), stage_mlp(), []
        heavy = _pieces(c, 1) + _pieces(a, 1) + _pieces(c, MLP_DOT_PIECES + 1)
        _weave(heavy, [a, stage_mix(finals)], IN_LIGHT + MIX_LIGHT)
        emit_states(finals)

    @pl.when(step == n_units)
    def _():
        finals = []
        _weave(_pieces(stage_mlp(), MLP_DOT_PIECES + 2), [stage_mix(finals)], MIX_LIGHT)
        emit_states(finals)

    @pl.when(step == n_units + 1)
    def _():
        _run(stage_mlp())


def _even_call(x_short, x_long, mod, sp, rope, params, units):
    n_units = units.total
    unit_a = lambda i: units.clamp(i)
    unit_b = lambda i: units.clamp(i - 1)
    unit_c = lambda i: units.clamp(i - 2)
    state_block = (1, 1, 2, HEADS, HEAD_DIM, HEAD_DIM)

    in_specs = [pl.BlockSpec((UNIT, D_MODEL), lambda i: (units.short_block(unit_a(i)), 0)),
                pl.BlockSpec((UNIT, D_MODEL), lambda i: (units.long_block(unit_a(i)), 0)),
                pl.BlockSpec((UNIT, D_MODEL), lambda i: (units.short_block(unit_c(i)), 0)),
                pl.BlockSpec((UNIT, D_MODEL), lambda i: (units.long_block(unit_c(i)), 0)),
                pl.BlockSpec((1, 1, N_MOD), lambda i: (units.mod_row(unit_a(i)), 0, 0)),
                pl.BlockSpec((1, 1, N_MOD), lambda i: (units.mod_row(unit_c(i)), 0, 0)),
                _resident(sp.shape), _resident(rope[0].shape), _resident(rope[1].shape)]
    in_specs += [_resident(p.shape) for p in params]
    scratch = [
        pltpu.VMEM((2, UNIT, A_WIDTH), F32),
        pltpu.VMEM((2, UNIT, A_WIDTH), BF16),
        pltpu.VMEM((2, UNIT, QK_W), BF16),
        pltpu.VMEM((2, UNIT, QK_W), F32),
        pltpu.VMEM((2, UNIT, V_W), BF16),
        pltpu.VMEM((2, UNIT, V_W), F32),
        pltpu.VMEM((2, UNIT, V_W), F32),
        pltpu.VMEM((2, UNIT, OUT_WIDTH), BF16),
        pltpu.VMEM((HEADS, len(TABLE_NAMES), CHUNK, CHUNK), F32),
    ]
    return pl.pallas_call(
        functools.partial(_even_kernel, units=units),
        grid=(n_units + 2,),
        in_specs=in_specs,
        out_specs=[pl.BlockSpec((UNIT, D_MODEL), lambda i: (unit_c(i), 0)),
                   pl.BlockSpec(state_block, lambda i: (units.short_block(unit_b(i)), 0, 0, 0, 0, 0))],
        out_shape=[jax.ShapeDtypeStruct((n_units * UNIT, D_MODEL), F32),
                   jax.ShapeDtypeStruct((units.n_short,) + state_block[1:], F32)],
        scratch_shapes=scratch,
        compiler_params=pltpu.CompilerParams(
            dimension_semantics=("arbitrary",),
            vmem_limit_bytes=V7X_VMEM_LIMIT_BYTES),
        name="even_layer",
    )(x_short, x_long, x_short, x_long, mod, mod, sp, rope[0], rope[1], *params)


def _odd_kernel(xp_ref, x_ref, xn_ref, moda_ref, modc_ref, nmix_ref, nmlp_ref, fin_ref, w_pool_ref, pscale_ref,
                w1_ref, w2_ref, y_short, y_long, x1_s, hb_s, *, n_units, n_short, short_len, long_len):
    step = pl.program_id(0)
    slot_a = lax.rem(step, 2)
    slot_b = 1 - slot_a
    rows_w = UNIT + 2 * POOL_HALO

    def stage_pool():
        sh1, sc1, g1, sh2, sc2, _ = _split_mod(moda_ref[0])
        xw = jnp.concatenate([xp_ref[...], x_ref[...], xn_ref[...]], axis=0)
        h = _rms_mod(xw, nmix_ref[...], sh1, sc1)
        seq_len = jnp.where(step < n_short, short_len, long_len)
        first = jnp.minimum(step, n_units - 1) * UNIT - POOL_HALO
        t = (lax.broadcasted_iota(jnp.int32, (rows_w, 1), 0) + first) & (seq_len - 1)
        yield

        def shift_down(a, k):
            return jnp.where(t >= k, pltpu.roll(a, k, axis=0), 0.0)

        def shift_up(a, k):
            return jnp.where(t < seq_len - k, pltpu.roll(a, rows_w - k, axis=0), 0.0)

        outs = []
        for g, w in enumerate(POOL_WINDOWS):
            hg = h[:, g * POOL_DIM:(g + 1) * POOL_DIM]
            half = w // 2
            left = shift_down(hg, 1)
            right = hg
            span = 1
            while span < half:
                left = left + shift_down(left, span)
                right = right + shift_up(right, span)
                span *= 2
            cnt = (jnp.minimum(t + half, seq_len) - jnp.maximum(t - half, 0)).astype(F32)
            pooled = ((left + right) / cnt - hg)[POOL_HALO:POOL_HALO + UNIT].astype(BF16)
            outs.append(_dot(pooled, w_pool_ref[g]))
            yield
        x1 = x_ref[...] + g1 * (jnp.concatenate(outs, axis=-1) * pscale_ref[...])
        x1_s[slot_a] = x1
        hb_s[slot_a] = _rms_mod(x1, nmlp_ref[...], sh2, sc2).astype(BF16)

    def stage_mlp(result):
        g2 = _split_mod(modc_ref[0])[5]
        acc = yield from _mlp_pieces(hb_s[slot_b], w1_ref, w2_ref)
        result.append(_rms(x1_s[slot_b] + g2 * acc) * fin_ref[...])

    def store(result):
        @pl.when(step - 1 < n_short)
        def _():
            y_short[...] = result[0]

        @pl.when(step - 1 >= n_short)
        def _():
            y_long[...] = result[0]

    pool_light = 2 + len(POOL_WINDOWS)

    @pl.when(step == 0)
    def _():
        _run(stage_pool())

    @pl.when((step >= 1) & (step < n_units))
    def _():
        result = []
        _weave(_pieces(stage_mlp(result), MLP_DOT_PIECES + 1), [stage_pool()], pool_light)
        store(result)

    @pl.when(step == n_units)
    def _():
        result = []
        _run(stage_mlp(result))
        store(result)


def _odd_call(x2, mod, params, units, *, short_len, long_len):
    n_units = units.total
    halo_per_unit = UNIT // POOL_HALO
    last_halo = x2.shape[0] // POOL_HALO - 1
    unit_a = lambda i: units.clamp(i)
    unit_c = lambda i: units.clamp(i - 1)
    assert (units.n_short * UNIT) % long_len == 0
    in_specs = [pl.BlockSpec((POOL_HALO, D_MODEL), lambda i: (jnp.maximum(unit_a(i) * halo_per_unit - 1, 0), 0)),
                pl.BlockSpec((UNIT, D_MODEL), lambda i: (unit_a(i), 0)),
                pl.BlockSpec((POOL_HALO, D_MODEL),
                             lambda i: (jnp.minimum((unit_a(i) + 1) * halo_per_unit, last_halo), 0)),
                pl.BlockSpec((1, 1, N_MOD), lambda i: (units.mod_row(unit_a(i)), 0, 0)),
                pl.BlockSpec((1, 1, N_MOD), lambda i: (units.mod_row(unit_c(i)), 0, 0))]
    in_specs += [_resident(p.shape) for p in params]
    return pl.pallas_call(
        functools.partial(_odd_kernel, n_units=n_units, n_short=units.n_short,
                          short_len=short_len, long_len=long_len),
        grid=(n_units + 1,),
        in_specs=in_specs,
        out_specs=[pl.BlockSpec((UNIT, D_MODEL), lambda i: (units.short_block(unit_c(i)), 0)),
                   pl.BlockSpec((UNIT, D_MODEL), lambda i: (units.long_block(unit_c(i)), 0))],
        out_shape=[jax.ShapeDtypeStruct((units.n_short * UNIT, D_MODEL), F32),
                   jax.ShapeDtypeStruct((units.n_long * UNIT, D_MODEL), F32)],
        scratch_shapes=[pltpu.VMEM((2, UNIT, D_MODEL), F32),
                        pltpu.VMEM((2, UNIT, D_MODEL), BF16)],
        compiler_params=pltpu.CompilerParams(
            dimension_semantics=("arbitrary",),
            vmem_limit_bytes=V7X_VMEM_LIMIT_BYTES),
        name="odd_layer",
    )(x2, x2, x2, mod, mod, *params)


def _rope_tables(n_tokens):
    rows = n_tokens // GRID_W
    pos_r = jnp.repeat(jnp.arange(rows, dtype=F32), GRID_W)
    pos_c = jnp.tile(jnp.arange(GRID_W, dtype=F32), rows)
    inv = ROPE_BASE ** (-jnp.arange(ROPE_PAIRS_AXIS, dtype=F32) / ROPE_PAIRS_AXIS)
    ang = jnp.concatenate([pos_r[:, None] * inv, pos_c[:, None] * inv], axis=-1)
    cos, sin = jnp.cos(ang), jnp.sin(ang)
    return jnp.concatenate([cos, cos], axis=-1), jnp.concatenate([-sin, sin], axis=-1)


def kernel(x_prompt, x_sample, state_ret, c, c_ctx, ada_w, ada_b, norm_mix, norm_mlp, mlp_w1, mlp_w2,
           ev_w_in, ev_w_s, ev_b_s, ev_vnorm, ev_decay, ev_w_out, od_w_pool, od_pool_scale, final_norm):
    n_prompt, prompt_len, _ = x_prompt.shape
    n_sample, sample_len, _ = x_sample.shape
    assert DEPTH == 2 and n_sample + 1 <= COND_ROWS
    assert prompt_len == UNIT and sample_len % UNIT == 0
    units = _Units(n_prompt, n_sample, sample_len // UNIT)

    cond = jnp.zeros((COND_ROWS, D_MODEL), F32).at[0].set(c_ctx).at[1:1 + n_sample].set(c)
    mod = _ada_call(cond, ada_w, ada_b)[:, :1 + n_sample].reshape(DEPTH, 1 + n_sample, 1, N_MOD)

    row = lambda v: v.reshape(1, -1)
    w_in = ev_w_in[0].astype(BF16)
    dec = jnp.broadcast_to(ev_decay[0].reshape(2 * HEADS, 1), (2 * HEADS, CHUNK))
    even_params = (
        row(norm_mix[0]), row(norm_mlp[0]),
        w_in, ev_w_s[0].astype(BF16),
        jnp.broadcast_to(ev_b_s[0][:, :, None], (A_GROUPS, CHUNK, A_GROUP_DIM)),
        row(ev_vnorm[0]), dec,
        ev_w_out[0].astype(BF16),
        mlp_w1[0].astype(BF16), mlp_w2[0].astype(BF16),
    )
    odd_params = (
        row(norm_mix[1]), row(norm_mlp[1]), row(final_norm),
        od_w_pool[0].astype(BF16), row(od_pool_scale[0]),
        mlp_w1[1].astype(BF16), mlp_w2[1].astype(BF16),
    )

    xp2 = x_prompt.reshape(n_prompt * prompt_len, D_MODEL)
    xs2 = x_sample.reshape(n_sample * sample_len, D_MODEL)
    cos2, sin2 = _rope_tables(sample_len)
    sp = _ret_state_call(xs2, mod[0, 1:], state_ret, (cos2, sin2), row(norm_mix[0]), w_in, dec,
                         n_seq=n_sample, seq_len=sample_len)
    rope = (jnp.concatenate([jnp.ones((UNIT, HEAD_DIM), F32), cos2]),
            jnp.concatenate([jnp.zeros((UNIT, HEAD_DIM), F32), sin2]))
    x_all, new_state = _even_call(xp2, xs2, mod[0], sp, rope, even_params, units)
    y_prompt, y_sample = _odd_call(x_all, mod[1], odd_params, units, short_len=prompt_len, long_len=sample_len)

    return (y_prompt.reshape(n_prompt, prompt_len, D_MODEL),
            y_sample.reshape(n_sample, sample_len, D_MODEL),
            new_state)
```

```python
import functools

import jax
import jax.numpy as jnp
from jax import lax
from jax.experimental import pallas as pl
from jax.experimental.pallas import tpu as pltpu

D_MODEL = 1024
DEPTH = 2
GRID_W = 64
A_WIDTH = D_MODEL // 2
A_GROUPS = 4
A_GROUP_DIM = A_WIDTH // A_GROUPS
CHUNK = 128
HEADS = 4
HEAD_DIM = (D_MODEL // 2) // HEADS
ROPE_BASE = 10000.0
ROPE_PAIRS_AXIS = HEAD_DIM // 4
POOL_WINDOWS = (2, 4, 8, 16)
POOL_DIM = D_MODEL // len(POOL_WINDOWS)
POOL_HALO = max(POOL_WINDOWS) // 2
D_FF = 4 * D_MODEL
EPS = 1e-6
QK_W = HEADS * HEAD_DIM
V_W = HEADS * HEAD_DIM
IN_WIDTH = 2 * A_WIDTH + 2 * QK_W + 3 * V_W
OUT_WIDTH = A_WIDTH + V_W
N_MOD = 6 * D_MODEL
K_COL = 2 * A_WIDTH + QK_W
V_COL = K_COL + QK_W

UNIT = 2 * CHUNK
UNIT_CHUNKS = UNIT // CHUNK
FF_CHUNK = 1024
ADA_BLOCK_N = 1536
COND_ROWS = 8
STAGE_ROWS = 1024
STAGE_COLS = 1024
V7X_VMEM_LIMIT_BYTES = 60 * 1024 * 1024

F32 = jnp.float32
BF16 = jnp.bfloat16


def _dot(a, b):
    return jnp.dot(a, b, preferred_element_type=F32)


def _dot_nt(a, b):
    return lax.dot_general(a, b, (((1,), (1,)), ((), ())), preferred_element_type=F32)


def _silu(x):
    return x * jax.nn.sigmoid(x)


def _gelu_tanh(x):
    return 0.5 * x * (1.0 + jnp.tanh(0.7978845608028654 * (x + 0.044715 * (x * x * x))))


def _rms(x):
    return x * lax.rsqrt(jnp.mean(x * x, axis=-1, keepdims=True) + EPS)


def _rms_mod(x, gain, shift, scale):
    return (_rms(x) * gain) * (1.0 + scale) + shift


def _layer_norm(x):
    mu = jnp.mean(x, axis=-1, keepdims=True)
    d = x - mu
    return d * lax.rsqrt(jnp.mean(d * d, axis=-1, keepdims=True) + EPS)


def _split_mod(mod):
    return [mod[:, i * D_MODEL:(i + 1) * D_MODEL] for i in range(6)]


def _rope(x, cos2, sin2):
    return x * cos2 + pltpu.roll(x, HEAD_DIM // 2, axis=1) * sin2


def _pick_block(take_first, a_ref, b_ref):
    rows = lax.broadcasted_iota(jnp.int32, (a_ref.shape[0], 1), 0)
    return jnp.where(rows < jnp.where(take_first, a_ref.shape[0], 0), a_ref[...], b_ref[...])


def _run(gen, n=None):
    done = 0
    while n is None or done < n:
        try:
            next(gen)
        except StopIteration:
            break
        done += 1
    return done


def _weave(heavy, lights, n_light):
    heavy = list(heavy)
    pending = _roundrobin(lights)
    ran = 0
    for k, piece in enumerate(heavy):
        piece()
        want = ((k + 1) * n_light + len(heavy) - 1) // len(heavy)
        ran += _run(pending, want - ran)
    _run(pending)


def _pieces(gen, n):
    return [functools.partial(next, gen, None) for _ in range(n)]


def _roundrobin(gens):
    live = list(gens)
    while live:
        for g in list(live):
            try:
                next(g)
            except StopIteration:
                live.remove(g)
                continue
            yield


MLP_DOT_PIECES = 2 * (D_FF // FF_CHUNK)


def _mlp_pieces(hb, w1_ref, w2_ref):
    n_ff = D_FF // FF_CHUNK
    up = lambda j: _dot(hb, w1_ref[:, j * FF_CHUNK:(j + 1) * FF_CHUNK])
    a = up(0)
    yield
    acc = None
    for j in range(n_ff):
        a_next = None
        if j + 1 < n_ff:
            a_next = up(j + 1)
            yield
        p = _dot(jnp.square(jnp.maximum(a, 0.0)).astype(BF16), w2_ref[j * FF_CHUNK:(j + 1) * FF_CHUNK, :])
        acc = p if acc is None else acc + p
        a = a_next
        yield
    return acc


def _resident(shape):
    return pl.BlockSpec(shape, lambda i: (0,) * len(shape), pipeline_mode=pl.Buffered(1))


def _weight_chunks(w_hbm, w_vmem):
    rows, cols = w_hbm.shape
    jobs = []
    for r0 in range(0, rows, STAGE_ROWS):
        for c0 in range(0, cols, STAGE_COLS):
            window = (pl.ds(r0, min(STAGE_ROWS, rows - r0)), pl.ds(c0, min(STAGE_COLS, cols - c0)))
            jobs.append((w_hbm.at[window], w_vmem.at[window]))
    return jobs


def _load_weights_bf16(jobs, stage, sem):
    def copy(k):
        r, c = jobs[k][0].shape
        return pltpu.make_async_copy(jobs[k][0], stage.at[k % 2, pl.ds(0, r), pl.ds(0, c)], sem.at[k % 2])

    copy(0).start()
    for k, (src, dst) in enumerate(jobs):
        if k + 1 < len(jobs):
            copy(k + 1).start()
        copy(k).wait()
        r, c = src.shape
        dst[...] = stage[k % 2, 0:r, 0:c].astype(BF16)


class _Units:
    def __init__(self, n_short, n_long, units_per_long):
        self.n_short, self.n_long, self.units_per_long = n_short, n_long * units_per_long, units_per_long
        self.total = self.n_short + self.n_long

    def clamp(self, unit):
        return jnp.clip(unit, 0, self.total - 1)

    def short_block(self, unit):
        return jnp.minimum(unit, self.n_short - 1)

    def long_block(self, unit):
        return jnp.clip(unit - self.n_short, 0, self.n_long - 1)

    def mod_row(self, unit):
        return jnp.where(unit < self.n_short, 0, 1 + (unit - self.n_short) // self.units_per_long)

    def rope_block(self, unit):
        return jnp.where(unit < self.n_short, 0, 1 + (unit - self.n_short) % self.units_per_long)

    def state_block(self, unit):
        return jnp.where(unit < self.n_short, 0, self.units_per_long + unit - self.n_short)


def _ada_kernel(cond_ref, w_ref, b_ref, out_ref):
    s = _silu(cond_ref[...]).astype(BF16)
    out_ref[0] = _dot(s, w_ref[0].astype(BF16)) + b_ref[0]


def _ada_call(cond, ada_w, ada_b):
    return pl.pallas_call(
        _ada_kernel,
        grid=(DEPTH, N_MOD // ADA_BLOCK_N),
        in_specs=[
            pl.BlockSpec((COND_ROWS, D_MODEL), lambda i, j: (0, 0)),
            pl.BlockSpec((1, D_MODEL, ADA_BLOCK_N), lambda i, j: (i, 0, j)),
            pl.BlockSpec((1, 1, ADA_BLOCK_N), lambda i, j: (i, 0, j)),
        ],
        out_specs=pl.BlockSpec((1, COND_ROWS, ADA_BLOCK_N), lambda i, j: (i, 0, j)),
        out_shape=jax.ShapeDtypeStruct((DEPTH, COND_ROWS, N_MOD), F32),
        compiler_params=pltpu.CompilerParams(
            dimension_semantics=("arbitrary", "arbitrary"),
            vmem_limit_bytes=V7X_VMEM_LIMIT_BYTES),
        name="ada_mod",
    )(cond, ada_w, ada_b.reshape(DEPTH, 1, N_MOD))


def _decay_tables(dec_ref, hd):
    row = lax.broadcasted_iota(jnp.int32, (CHUNK, CHUNK), 0).astype(F32)
    col = lax.broadcasted_iota(jnp.int32, (CHUNK, CHUNK), 1).astype(F32)
    lg_f = jnp.broadcast_to(-jnp.exp(dec_ref[hd:hd + 1, :]), (CHUNK, CHUNK))
    lg_b = jnp.broadcast_to(-jnp.exp(dec_ref[HEADS + hd:HEADS + hd + 1, :]), (CHUNK, CHUNK))
    d_f = row - col
    d_b = col - row
    return dict(
        decay_f=jnp.where(d_f >= 0, jnp.exp(lg_f * jnp.maximum(d_f, 0.0)), 0.0),
        decay_b=jnp.where(d_b >= 0, jnp.exp(lg_b * jnp.maximum(d_b, 0.0)), 0.0),
        qd_f=jnp.exp(lg_f * (row + 1.0)),
        qd_b=jnp.exp(lg_b * (CHUNK - row)),
        kd_f=jnp.exp(lg_f * (CHUNK - 1.0 - row)),
        kd_b=jnp.exp(lg_b * row),
        cd_f=jnp.exp(lg_f * CHUNK),
        cd_b=jnp.exp(lg_b * CHUNK),
    )


TABLE_NAMES = ("decay_f", "decay_b", "qd_f", "qd_b", "kd_f", "kd_b", "cd_f", "cd_b")


def _ret_state_kernel(x_ref, mod_ref, s0_ref, cos_ref, sin_ref, nmix_ref, wk_ref, wv_ref, dec_ref, sp_ref,
                      *, n_chunk):
    step = pl.program_id(0)

    @pl.when(step == 0)
    def _():
        sp_ref[...] = jnp.zeros_like(sp_ref)

    @pl.when(step > 0)
    def _():
        sh1, sc1 = _split_mod(mod_ref[0])[:2]
        hb = _rms_mod(x_ref[...], nmix_ref[...], sh1, sc1).astype(BF16)
        zk = _dot(hb, wk_ref[...].astype(BF16)) * (HEAD_DIM ** -0.5)
        zv = _dot(hb, wv_ref[...].astype(BF16)).astype(BF16)
        cos2, sin2 = cos_ref[...], sin_ref[...]
        for hd in range(HEADS):
            lo = hd * HEAD_DIM
            tb = _decay_tables(dec_ref, hd)
            kh = _rope(zk[:, lo:lo + HEAD_DIM], cos2, sin2)
            kv_f, kv_b = [], []
            for c in range(n_chunk):
                kc = kh[c * CHUNK:(c + 1) * CHUNK]
                vc = zv[c * CHUNK:(c + 1) * CHUNK, lo:lo + HEAD_DIM]
                kv_f.append(_dot((kc * tb["kd_f"]).T.astype(BF16), vc))
                kv_b.append(_dot((kc * tb["kd_b"]).T.astype(BF16), vc))
            st = s0_ref[0, 0, 0, hd]
            for c in range(n_chunk):
                if c % UNIT_CHUNKS == 0:
                    sp_ref[c // UNIT_CHUNKS, 0, hd] = st
                st = tb["cd_f"] * st + kv_f[c]
            st = s0_ref[0, 0, 1, hd]
            for c in reversed(range(n_chunk)):
                if c % UNIT_CHUNKS == UNIT_CHUNKS - 1:
                    sp_ref[c // UNIT_CHUNKS, 1, hd] = st
                st = tb["cd_b"] * st + kv_b[c]


def _ret_state_call(x2, mod, s0, rope, nmix, w_in, dec, *, n_seq, seq_len, even_index):
    n_chunk = seq_len // CHUNK
    units = seq_len // UNIT
    seq = lambda i: jnp.maximum(i - 1, 0)
    return pl.pallas_call(
        functools.partial(_ret_state_kernel, n_chunk=n_chunk),
        grid=(n_seq + 1,),
        in_specs=[
            pl.BlockSpec((seq_len, D_MODEL), lambda i: (seq(i), 0)),
            pl.BlockSpec((1, 1, N_MOD), lambda i: (seq(i), 0, 0)),
            pl.BlockSpec((1, 1, 2, HEADS, HEAD_DIM, HEAD_DIM), lambda i: (seq(i), 0, 0, 0, 0, 0)),
            _resident((seq_len, HEAD_DIM)), _resident((seq_len, HEAD_DIM)),
            _resident((1, D_MODEL)),
            pl.BlockSpec((None, D_MODEL, QK_W), lambda i: (even_index, 0, K_COL // QK_W),
                         pipeline_mode=pl.Buffered(1)),
            pl.BlockSpec((None, D_MODEL, V_W), lambda i: (even_index, 0, V_COL // V_W),
                         pipeline_mode=pl.Buffered(1)),
            _resident(dec.shape),
        ],
        out_specs=pl.BlockSpec((units, 2, HEADS, HEAD_DIM, HEAD_DIM), lambda i: (i, 0, 0, 0, 0)),
        out_shape=jax.ShapeDtypeStruct(((n_seq + 1) * units, 2, HEADS, HEAD_DIM, HEAD_DIM), F32),
        compiler_params=pltpu.CompilerParams(
            dimension_semantics=("arbitrary",),
            vmem_limit_bytes=V7X_VMEM_LIMIT_BYTES),
        name="ret_state",
    )(x2, mod, s0, rope[0], rope[1], nmix, w_in, w_in, dec)


def _even_kernel(xa_short, xa_long, xc_short, xc_long, moda_ref, modc_ref, sp_ref, cos_ref, sin_ref,
                 nmix_ref, nmlp_ref, w_s_ref, b_s_ref, vgain_ref, dec_ref, w_in_hbm, w_out_hbm, w1_hbm, w2_hbm,
                 y_ref, st_ref, ug_s, vg_s, q_s, k_s, v_s, gf_s, gb_s, cat_s, tab_s,
                 w_in_ref, w_out_ref, w1_ref, w2_ref, stage, sem, *, units, layer, even_index):
    n_units, n_short = units.total, units.n_short
    step = pl.program_id(0)
    slot_a = lax.rem(step, 2)
    slot_b = 1 - slot_a

    @pl.when(step == 0)
    def _():
        _load_weights_bf16(_weight_chunks(w_in_hbm.at[even_index], w_in_ref)
                           + _weight_chunks(w_out_hbm.at[even_index], w_out_ref)
                           + _weight_chunks(w1_hbm.at[layer], w1_ref)
                           + _weight_chunks(w2_hbm.at[layer], w2_ref), stage, sem)
        for hd in range(HEADS):
            tb = _decay_tables(dec_ref, hd)
            for n, name in enumerate(TABLE_NAMES):
                tab_s[hd, n] = tb[name]

    def stage_in():
        sh1, sc1 = _split_mod(moda_ref[0])[:2]
        x = _pick_block(step < n_short, xa_short, xa_long)
        hb = _rms_mod(x, nmix_ref[...], sh1, sc1).astype(BF16)
        z = _dot(hb, w_in_ref[...])
        yield
        ug_s[slot_a] = _gelu_tanh(z[:, 0:A_WIDTH])
        yield
        zv = _gelu_tanh(z[:, A_WIDTH:2 * A_WIDTH])
        vgn = [_layer_norm(zv[:, g * A_GROUP_DIM:(g + 1) * A_GROUP_DIM]) for g in range(A_GROUPS)]
        vg_s[slot_a] = (jnp.concatenate(vgn, axis=-1) * vgain_ref[...]).astype(BF16)
        yield
        cos2, sin2 = cos_ref[...], sin_ref[...]
        heads = lambda zz: jnp.concatenate(
            [_rope(zz[:, h * HEAD_DIM:(h + 1) * HEAD_DIM], cos2, sin2) for h in range(HEADS)], -1)
        q_s[slot_a] = heads(z[:, 2 * A_WIDTH:K_COL]).astype(BF16)
        yield
        k_s[slot_a] = heads(z[:, K_COL:V_COL] * (HEAD_DIM ** -0.5))
        v_s[slot_a] = z[:, V_COL:V_COL + V_W].astype(BF16)
        yield
        gf_s[slot_a] = _silu(z[:, V_COL + V_W:V_COL + 2 * V_W])
        yield
        gb_s[slot_a] = _silu(z[:, V_COL + 2 * V_W:V_COL + 3 * V_W])

    IN_LIGHT = 6

    def stage_mix(final_states):
        s = slot_b
        rows = [slice(c * CHUNK, (c + 1) * CHUNK) for c in range(UNIT_CHUNKS)]
        head_cols = [slice(hd * HEAD_DIM, (hd + 1) * HEAD_DIM) for hd in range(HEADS)]
        table = lambda hd, name: tab_s[hd, TABLE_NAMES.index(name)]

        sp2 = []
        for g in range(A_GROUPS):
            lo = g * A_GROUP_DIM
            vg2 = jnp.concatenate([vg_s[s, r, lo:lo + A_GROUP_DIM] for r in rows], axis=-1)
            sp2.append(_dot(w_s_ref[g], vg2))
        yield
        scores, kv_f, kv_b = [], [], []
        for hd in range(HEADS):
            kc = [k_s[s, r, head_cols[hd]] for r in rows]
            vc = [v_s[s, r, head_cols[hd]] for r in rows]
            scores.append([_dot_nt(q_s[s, r, head_cols[hd]], kc[c].astype(BF16)) for c, r in enumerate(rows)])
            kv_f.append([_dot((kc[c] * table(hd, "kd_f")).T.astype(BF16), vc[c]) for c in range(UNIT_CHUNKS)])
            kv_b.append([_dot((kc[c] * table(hd, "kd_b")).T.astype(BF16), vc[c]) for c in range(UNIT_CHUNKS)])
            yield
        for g in range(A_GROUPS):
            lo = g * A_GROUP_DIM
            for c, r in enumerate(rows):
                gate = sp2[g][:, c * CHUNK:(c + 1) * CHUNK] + b_s_ref[g]
                cat_s[s, r, lo:lo + A_GROUP_DIM] = (ug_s[s, r, lo:lo + A_GROUP_DIM] * gate).astype(BF16)
            if g % 2 == 1:
                yield
        y_raw = []
        for hd in range(HEADS):
            enter_f, st = [], sp_ref[0, 0, hd]
            for c in range(UNIT_CHUNKS):
                enter_f.append(st)
                st = table(hd, "cd_f") * st + kv_f[hd][c]
            final_f = st
            enter_b, st = [None] * UNIT_CHUNKS, sp_ref[0, 1, hd]
            for c in reversed(range(UNIT_CHUNKS)):
                enter_b[c] = st
                st = table(hd, "cd_b") * st + kv_b[hd][c]
            final_states.append((final_f, st))
            per_chunk = []
            for c, r in enumerate(rows):
                vc = v_s[s, r, head_cols[hd]]
                qf = q_s[s, r, head_cols[hd]].astype(F32)

                def direction(decay, qd, enter):
                    p = (scores[hd][c] * table(hd, decay)).astype(BF16)
                    lhs = jnp.concatenate([p, (qf * table(hd, qd)).astype(BF16)], axis=-1)
                    return _dot(lhs, jnp.concatenate([vc, enter.astype(BF16)], axis=0))

                per_chunk.append((direction("decay_f", "qd_f", enter_f[c]), direction("decay_b", "qd_b", enter_b[c])))
            y_raw.append(per_chunk)
            yield
        for hd in range(HEADS):
            for c, r in enumerate(rows):
                y_f, y_b = y_raw[hd][c]
                cols = slice(A_WIDTH + hd * HEAD_DIM, A_WIDTH + (hd + 1) * HEAD_DIM)
                cat_s[s, r, cols] = (gf_s[s, r, head_cols[hd]] * _layer_norm(y_f)
                                     + gb_s[s, r, head_cols[hd]] * _layer_norm(y_b)).astype(BF16)
            yield

    MIX_LIGHT = 1 + HEADS + A_GROUPS // 2 + 2 * HEADS

    def emit_states(final_states):
        @pl.when(step - 1 < n_short)
        def _():
            for hd, (final_f, final_b) in enumerate(final_states):
                st_ref[0, 0, 0, hd] = final_f
                st_ref[0, 0, 1, hd] = final_b

    def stage_mlp():
        _, _, g1, sh2, sc2, g2 = _split_mod(modc_ref[0])
        x = _pick_block(step - 2 < n_short, xc_short, xc_long) + g1 * _dot(cat_s[slot_a], w_out_ref[...])
        hb = _rms_mod(x, nmlp_ref[...], sh2, sc2).astype(BF16)
        yield
        acc = yield from _mlp_pieces(hb, w1_ref, w2_ref)
        y_ref[...] = x + g2 * acc

    @pl.when(step == 0)
    def _():
        _run(stage_in())

    @pl.when(step == 1)
    def _():
        a, finals = stage_in(), []
        _weave(_pieces(a, 1), [a, stage_mix(finals)], IN_LIGHT + MIX_LIGHT)
        emit_states(finals)

    @pl.when((step >= 2) & (step < n_units))
    def _():
        a, c, finals = stage_in(), stage_mlp(), []
        heavy = _pieces(c, 1) + _pieces(a, 1) + _pieces(c, MLP_DOT_PIECES + 1)
        _weave(heavy, [a, stage_mix(finals)], IN_LIGHT + MIX_LIGHT)
        emit_states(finals)

    @pl.when(step == n_units)
    def _():
        finals = []
        _weave(_pieces(stage_mlp(), MLP_DOT_PIECES + 2), [stage_mix(finals)], MIX_LIGHT)
        emit_states(finals)

    @pl.when(step == n_units + 1)
    def _():
        _run(stage_mlp())


_HBM = pl.BlockSpec(memory_space=pl.ANY)
_WEIGHT_STAGING = [pltpu.VMEM((2, STAGE_ROWS, STAGE_COLS), F32), pltpu.SemaphoreType.DMA((2,))]


def _even_call(x_short, x_long, mod, sp, rope, params, weights, units, *, layer, even_index):
    n_units = units.total
    unit_a = lambda i: units.clamp(i)
    unit_b = lambda i: units.clamp(i - 1)
    unit_c = lambda i: units.clamp(i - 2)
    state_block = (1, 1, 2, HEADS, HEAD_DIM, HEAD_DIM)

    in_specs = [pl.BlockSpec((UNIT, D_MODEL), lambda i: (units.short_block(unit_a(i)), 0)),
                pl.BlockSpec((UNIT, D_MODEL), lambda i: (units.long_block(unit_a(i)), 0)),
                pl.BlockSpec((UNIT, D_MODEL), lambda i: (units.short_block(unit_c(i)), 0)),
                pl.BlockSpec((UNIT, D_MODEL), lambda i: (units.long_block(unit_c(i)), 0)),
                pl.BlockSpec((1, 1, N_MOD), lambda i: (units.mod_row(unit_a(i)), 0, 0)),
                pl.BlockSpec((1, 1, N_MOD), lambda i: (units.mod_row(unit_c(i)), 0, 0)),
                pl.BlockSpec((1, 2, HEADS, HEAD_DIM, HEAD_DIM), lambda i: (units.state_block(unit_b(i)), 0, 0, 0, 0)),
                pl.BlockSpec((UNIT, HEAD_DIM), lambda i: (units.rope_block(unit_a(i)), 0)),
                pl.BlockSpec((UNIT, HEAD_DIM), lambda i: (units.rope_block(unit_a(i)), 0))]
    in_specs += [_resident(p.shape) for p in params] + [_HBM] * len(weights)
    scratch = [
        pltpu.VMEM((2, UNIT, A_WIDTH), F32),
        pltpu.VMEM((2, UNIT, A_WIDTH), BF16),
        pltpu.VMEM((2, UNIT, QK_W), BF16),
        pltpu.VMEM((2, UNIT, QK_W), F32),
        pltpu.VMEM((2, UNIT, V_W), BF16),
        pltpu.VMEM((2, UNIT, V_W), F32),
        pltpu.VMEM((2, UNIT, V_W), F32),
        pltpu.VMEM((2, UNIT, OUT_WIDTH), BF16),
        pltpu.VMEM((HEADS, len(TABLE_NAMES), CHUNK, CHUNK), F32),
    ]
    scratch += [pltpu.VMEM(w.shape[1:], BF16) for w in weights] + _WEIGHT_STAGING
    return pl.pallas_call(
        functools.partial(_even_kernel, units=units, layer=layer, even_index=even_index),
        grid=(n_units + 2,),
        in_specs=in_specs,
        out_specs=[pl.BlockSpec((UNIT, D_MODEL), lambda i: (unit_c(i), 0)),
                   pl.BlockSpec(state_block, lambda i: (units.short_block(unit_b(i)), 0, 0, 0, 0, 0))],
        out_shape=[jax.ShapeDtypeStruct((n_units * UNIT, D_MODEL), F32),
                   jax.ShapeDtypeStruct((units.n_short,) + state_block[1:], F32)],
        scratch_shapes=scratch,
        compiler_params=pltpu.CompilerParams(
            dimension_semantics=("arbitrary",),
            vmem_limit_bytes=V7X_VMEM_LIMIT_BYTES),
        name="even_layer",
    )(x_short, x_long, x_short, x_long, mod, mod, sp, rope[0], rope[1], *params, *weights)


def _odd_kernel(xp_ref, x_ref, xn_ref, moda_ref, modc_ref, nmix_ref, nmlp_ref, fin_ref, pscale_ref,
                w_pool_hbm, w1_hbm, w2_hbm, y_short, y_long, x1_s, hb_s, w_pool_ref, w1_ref, w2_ref, stage, sem,
                *, n_units, n_short, short_len, long_len, layer, odd_index):
    step = pl.program_id(0)
    slot_a = lax.rem(step, 2)
    slot_b = 1 - slot_a
    rows_w = UNIT + 2 * POOL_HALO

    @pl.when(step == 0)
    def _():
        _load_weights_bf16(_weight_chunks(w_pool_hbm.at[odd_index], w_pool_ref)
                           + _weight_chunks(w1_hbm.at[layer], w1_ref)
                           + _weight_chunks(w2_hbm.at[layer], w2_ref), stage, sem)

    def stage_pool():
        sh1, sc1, g1, sh2, sc2, _ = _split_mod(moda_ref[0])
        xw = jnp.concatenate([xp_ref[...], x_ref[...], xn_ref[...]], axis=0)
        h = _rms_mod(xw, nmix_ref[...], sh1, sc1)
        seq_len = jnp.where(step < n_short, short_len, long_len)
        first = jnp.minimum(step, n_units - 1) * UNIT - POOL_HALO
        t = (lax.broadcasted_iota(jnp.int32, (rows_w, 1), 0) + first) & (seq_len - 1)
        yield

        def shift_down(a, k):
            return jnp.where(t >= k, pltpu.roll(a, k, axis=0), 0.0)

        def shift_up(a, k):
            return jnp.where(t < seq_len - k, pltpu.roll(a, rows_w - k, axis=0), 0.0)

        outs = []
        for g, w in enumerate(POOL_WINDOWS):
            hg = h[:, g * POOL_DIM:(g + 1) * POOL_DIM]
            half = w // 2
            left = shift_down(hg, 1)
            right = hg
            span = 1
            while span < half:
                left = left + shift_down(left, span)
                right = right + shift_up(right, span)
                span *= 2
            cnt = (jnp.minimum(t + half, seq_len) - jnp.maximum(t - half, 0)).astype(F32)
            pooled = ((left + right) / cnt - hg)[POOL_HALO:POOL_HALO + UNIT].astype(BF16)
            outs.append(_dot(pooled, w_pool_ref[g * POOL_DIM:(g + 1) * POOL_DIM, :]))
            yield
        x1 = x_ref[...] + g1 * (jnp.concatenate(outs, axis=-1) * pscale_ref[...])
        x1_s[slot_a] = x1
        hb_s[slot_a] = _rms_mod(x1, nmlp_ref[...], sh2, sc2).astype(BF16)

    def stage_mlp(result):
        g2 = _split_mod(modc_ref[0])[5]
        acc = yield from _mlp_pieces(hb_s[slot_b], w1_ref, w2_ref)
        result.append(_rms(x1_s[slot_b] + g2 * acc) * fin_ref[...])

    def store(result):
        @pl.when(step - 1 < n_short)
        def _():
            y_short[...] = result[0]

        @pl.when(step - 1 >= n_short)
        def _():
            y_long[...] = result[0]

    pool_light = 2 + len(POOL_WINDOWS)

    @pl.when(step == 0)
    def _():
        _run(stage_pool())

    @pl.when((step >= 1) & (step < n_units))
    def _():
        result = []
        _weave(_pieces(stage_mlp(result), MLP_DOT_PIECES + 1), [stage_pool()], pool_light)
        store(result)

    @pl.when(step == n_units)
    def _():
        result = []
        _run(stage_mlp(result))
        store(result)


def _odd_call(x2, mod, params, weights, units, *, short_len, long_len, layer, odd_index):
    n_units = units.total
    halo_per_unit = UNIT // POOL_HALO
    last_halo = x2.shape[0] // POOL_HALO - 1
    unit_a = lambda i: units.clamp(i)
    unit_c = lambda i: units.clamp(i - 1)
    assert (units.n_short * UNIT) % long_len == 0
    in_specs = [pl.BlockSpec((POOL_HALO, D_MODEL), lambda i: (jnp.maximum(unit_a(i) * halo_per_unit - 1, 0), 0)),
                pl.BlockSpec((UNIT, D_MODEL), lambda i: (unit_a(i), 0)),
                pl.BlockSpec((POOL_HALO, D_MODEL),
                             lambda i: (jnp.minimum((unit_a(i) + 1) * halo_per_unit, last_halo), 0)),
                pl.BlockSpec((1, 1, N_MOD), lambda i: (units.mod_row(unit_a(i)), 0, 0)),
                pl.BlockSpec((1, 1, N_MOD), lambda i: (units.mod_row(unit_c(i)), 0, 0))]
    in_specs += [_resident(p.shape) for p in params] + [_HBM] * len(weights)
    scratch = [pltpu.VMEM((2, UNIT, D_MODEL), F32),
               pltpu.VMEM((2, UNIT, D_MODEL), BF16)]
    scratch += [pltpu.VMEM(w.shape[1:], BF16) for w in weights] + _WEIGHT_STAGING
    return pl.pallas_call(
        functools.partial(_odd_kernel, n_units=n_units, n_short=units.n_short,
                          short_len=short_len, long_len=long_len, layer=layer, odd_index=odd_index),
        grid=(n_units + 1,),
        in_specs=in_specs,
        out_specs=[pl.BlockSpec((UNIT, D_MODEL), lambda i: (units.short_block(unit_c(i)), 0)),
                   pl.BlockSpec((UNIT, D_MODEL), lambda i: (units.long_block(unit_c(i)), 0))],
        out_shape=[jax.ShapeDtypeStruct((units.n_short * UNIT, D_MODEL), F32),
                   jax.ShapeDtypeStruct((units.n_long * UNIT, D_MODEL), F32)],
        scratch_shapes=scratch,
        compiler_params=pltpu.CompilerParams(
            dimension_semantics=("arbitrary",),
            vmem_limit_bytes=V7X_VMEM_LIMIT_BYTES),
        name="odd_layer",
    )(x2, x2, x2, mod, mod, *params, *weights)


def _rope_tables(n_tokens):
    rows = n_tokens // GRID_W
    pos_r = jnp.repeat(jnp.arange(rows, dtype=F32), GRID_W)
    pos_c = jnp.tile(jnp.arange(GRID_W, dtype=F32), rows)
    inv = ROPE_BASE ** (-jnp.arange(ROPE_PAIRS_AXIS, dtype=F32) / ROPE_PAIRS_AXIS)
    ang = jnp.concatenate([pos_r[:, None] * inv, pos_c[:, None] * inv], axis=-1)
    cos, sin = jnp.cos(ang), jnp.sin(ang)
    return jnp.concatenate([cos, cos], axis=-1), jnp.concatenate([-sin, sin], axis=-1)


def kernel(x_prompt, x_sample, state_ret, c, c_ctx, ada_w, ada_b, norm_mix, norm_mlp, mlp_w1, mlp_w2,
           ev_w_in, ev_w_s, ev_b_s, ev_vnorm, ev_decay, ev_w_out, od_w_pool, od_pool_scale, final_norm):
    n_prompt, prompt_len, _ = x_prompt.shape
    n_sample, sample_len, _ = x_sample.shape
    assert DEPTH == 2 and n_sample + 1 <= COND_ROWS
    assert prompt_len == UNIT and sample_len % UNIT == 0
    units = _Units(n_prompt, n_sample, sample_len // UNIT)

    cond = jnp.zeros((COND_ROWS, D_MODEL), F32).at[0].set(c_ctx).at[1:1 + n_sample].set(c)
    mod = _ada_call(cond, ada_w, ada_b)[:, :1 + n_sample].reshape(DEPTH, 1 + n_sample, 1, N_MOD)

    row = lambda v: v.reshape(1, -1)
    dec = jnp.broadcast_to(ev_decay[0].reshape(2 * HEADS, 1), (2 * HEADS, CHUNK))
    even_params = (
        row(norm_mix[0]), row(norm_mlp[0]),
        ev_w_s[0].astype(BF16),
        jnp.broadcast_to(ev_b_s[0][:, :, None], (A_GROUPS, CHUNK, A_GROUP_DIM)),
        row(ev_vnorm[0]), dec,
    )
    odd_params = (row(norm_mix[1]), row(norm_mlp[1]), row(final_norm), row(od_pool_scale[0]))
    even_weights = (ev_w_in, ev_w_out, mlp_w1, mlp_w2)
    odd_weights = (od_w_pool.reshape(od_w_pool.shape[0], D_MODEL, POOL_DIM), mlp_w1, mlp_w2)

    xp2 = x_prompt.reshape(n_prompt * prompt_len, D_MODEL)
    xs2 = x_sample.reshape(n_sample * sample_len, D_MODEL)
    cos2, sin2 = _rope_tables(sample_len)
    sp = _ret_state_call(xs2, mod[0, 1:], state_ret, (cos2, sin2), row(norm_mix[0]), ev_w_in, dec,
                         n_seq=n_sample, seq_len=sample_len, even_index=0)
    rope = (jnp.concatenate([jnp.ones((UNIT, HEAD_DIM), F32), cos2]),
            jnp.concatenate([jnp.zeros((UNIT, HEAD_DIM), F32), sin2]))
    x_all, new_state = _even_call(xp2, xs2, mod[0], sp, rope, even_params, even_weights, units,
                                  layer=0, even_index=0)
    y_prompt, y_sample = _odd_call(x_all, mod[1], odd_params, odd_weights, units,
                                   short_len=prompt_len, long_len=sample_len, layer=1, odd_index=0)

    return (y_prompt.reshape(n_prompt, prompt_len, D_MODEL),
            y_sample.reshape(n_sample, sample_len, D_MODEL),
            new_state)
```

```python
import functools

import jax
import jax.numpy as jnp
from jax import lax
from jax.experimental import pallas as pl
from jax.experimental.pallas import tpu as pltpu

D_MODEL = 1024
DEPTH = 2
GRID_W = 64
A_WIDTH = D_MODEL // 2
A_GROUPS = 4
A_GROUP_DIM = A_WIDTH // A_GROUPS
CHUNK = 128
HEADS = 4
HEAD_DIM = (D_MODEL // 2) // HEADS
ROPE_BASE = 10000.0
ROPE_PAIRS_AXIS = HEAD_DIM // 4
POOL_WINDOWS = (2, 4, 8, 16)
POOL_DIM = D_MODEL // len(POOL_WINDOWS)
POOL_HALO = max(POOL_WINDOWS) // 2
D_FF = 4 * D_MODEL
EPS = 1e-6
QK_W = HEADS * HEAD_DIM
V_W = HEADS * HEAD_DIM
IN_WIDTH = 2 * A_WIDTH + 2 * QK_W + 3 * V_W
OUT_WIDTH = A_WIDTH + V_W
N_MOD = 6 * D_MODEL
K_COL = 2 * A_WIDTH + QK_W
V_COL = K_COL + QK_W

UNIT = 2 * CHUNK
UNIT_CHUNKS = UNIT // CHUNK
FF_CHUNK = 1024
ADA_BLOCK_N = 1536
COND_ROWS = 8
STAGE_ROWS = 512
STAGE_COLS = 1024
STAGE_SLOTS = 4
V7X_VMEM_LIMIT_BYTES = 60 * 1024 * 1024

F32 = jnp.float32
BF16 = jnp.bfloat16


def _dot(a, b):
    return jnp.dot(a, b, preferred_element_type=F32)


def _dot_nt(a, b):
    return lax.dot_general(a, b, (((1,), (1,)), ((), ())), preferred_element_type=F32)


def _silu(x):
    return x * jax.nn.sigmoid(x)


def _gelu_tanh(x):
    return 0.5 * x * (1.0 + jnp.tanh(0.7978845608028654 * (x + 0.044715 * (x * x * x))))


def _rms(x):
    return x * lax.rsqrt(jnp.mean(x * x, axis=-1, keepdims=True) + EPS)


def _rms_mod(x, gain, shift, scale):
    return (_rms(x) * gain) * (1.0 + scale) + shift


def _layer_norm(x):
    mu = jnp.mean(x, axis=-1, keepdims=True)
    d = x - mu
    return d * lax.rsqrt(jnp.mean(d * d, axis=-1, keepdims=True) + EPS)


def _split_mod(mod):
    return [mod[:, i * D_MODEL:(i + 1) * D_MODEL] for i in range(6)]


def _rope(x, cos2, sin2):
    return x * cos2 + pltpu.roll(x, HEAD_DIM // 2, axis=1) * sin2


def _pick_block(take_first, a_ref, b_ref):
    rows = lax.broadcasted_iota(jnp.int32, (a_ref.shape[0], 1), 0)
    return jnp.where(rows < jnp.where(take_first, a_ref.shape[0], 0), a_ref[...], b_ref[...])


def _run(gen, n=None):
    done = 0
    while n is None or done < n:
        try:
            next(gen)
        except StopIteration:
            break
        done += 1
    return done


def _weave(heavy, lights, n_light):
    heavy = list(heavy)
    pending = _roundrobin(lights)
    ran = 0
    for k, piece in enumerate(heavy):
        piece()
        want = ((k + 1) * n_light + len(heavy) - 1) // len(heavy)
        ran += _run(pending, want - ran)
    _run(pending)


def _pieces(gen, n):
    return [functools.partial(next, gen, None) for _ in range(n)]


def _roundrobin(gens):
    live = list(gens)
    while live:
        for g in list(live):
            try:
                next(g)
            except StopIteration:
                live.remove(g)
                continue
            yield


MLP_DOT_PIECES = 2 * (D_FF // FF_CHUNK)


def _mlp_pieces(hb, w1_ref, w2_ref):
    n_ff = D_FF // FF_CHUNK
    up = lambda j: _dot(hb, w1_ref[:, j * FF_CHUNK:(j + 1) * FF_CHUNK])
    a = up(0)
    yield
    acc = None
    for j in range(n_ff):
        a_next = None
        if j + 1 < n_ff:
            a_next = up(j + 1)
            yield
        p = _dot(jnp.square(jnp.maximum(a, 0.0)).astype(BF16), w2_ref[j * FF_CHUNK:(j + 1) * FF_CHUNK, :])
        acc = p if acc is None else acc + p
        a = a_next
        yield
    return acc


def _resident(shape):
    return pl.BlockSpec(shape, lambda i: (0,) * len(shape), pipeline_mode=pl.Buffered(1))


def _weight_chunks(w_hbm, w_vmem):
    rows, cols = w_hbm.shape
    jobs = []
    for r0 in range(0, rows, STAGE_ROWS):
        for c0 in range(0, cols, STAGE_COLS):
            window = (pl.ds(r0, min(STAGE_ROWS, rows - r0)), pl.ds(c0, min(STAGE_COLS, cols - c0)))
            jobs.append((w_hbm.at[window], w_vmem.at[window]))
    return jobs


def _load_weights_bf16(jobs, stage, sem):
    n_slots = stage.shape[0]

    def copy(k):
        r, c = jobs[k][0].shape
        slot = k % n_slots
        return pltpu.make_async_copy(jobs[k][0], stage.at[slot, pl.ds(0, r), pl.ds(0, c)], sem.at[slot])

    for k in range(min(n_slots - 1, len(jobs))):
        copy(k).start()
    for k, (src, dst) in enumerate(jobs):
        if k + n_slots - 1 < len(jobs):
            copy(k + n_slots - 1).start()
        copy(k).wait()
        r, c = src.shape
        dst[...] = stage[k % n_slots, 0:r, 0:c].astype(BF16)


class _Units:
    def __init__(self, n_short, n_long, units_per_long):
        self.n_short, self.n_long, self.units_per_long = n_short, n_long * units_per_long, units_per_long
        self.total = self.n_short + self.n_long

    def clamp(self, unit):
        return jnp.clip(unit, 0, self.total - 1)

    def short_block(self, unit):
        return jnp.minimum(unit, self.n_short - 1)

    def long_block(self, unit):
        return jnp.clip(unit - self.n_short, 0, self.n_long - 1)

    def mod_row(self, unit):
        return jnp.where(unit < self.n_short, 0, 1 + (unit - self.n_short) // self.units_per_long)

    def rope_block(self, unit):
        return jnp.where(unit < self.n_short, 0, 1 + (unit - self.n_short) % self.units_per_long)

    def state_block(self, unit):
        return jnp.where(unit < self.n_short, 0, self.units_per_long + unit - self.n_short)


def _ada_kernel(cond_ref, w_ref, b_ref, out_ref):
    s = _silu(cond_ref[...]).astype(BF16)
    out_ref[0] = _dot(s, w_ref[0].astype(BF16)) + b_ref[0]


def _ada_call(cond, ada_w, ada_b):
    return pl.pallas_call(
        _ada_kernel,
        grid=(DEPTH, N_MOD // ADA_BLOCK_N),
        in_specs=[
            pl.BlockSpec((COND_ROWS, D_MODEL), lambda i, j: (0, 0)),
            pl.BlockSpec((1, D_MODEL, ADA_BLOCK_N), lambda i, j: (i, 0, j)),
            pl.BlockSpec((1, 1, ADA_BLOCK_N), lambda i, j: (i, 0, j)),
        ],
        out_specs=pl.BlockSpec((1, COND_ROWS, ADA_BLOCK_N), lambda i, j: (i, 0, j)),
        out_shape=jax.ShapeDtypeStruct((DEPTH, COND_ROWS, N_MOD), F32),
        compiler_params=pltpu.CompilerParams(
            dimension_semantics=("arbitrary", "arbitrary"),
            vmem_limit_bytes=V7X_VMEM_LIMIT_BYTES),
        name="ada_mod",
    )(cond, ada_w, ada_b.reshape(DEPTH, 1, N_MOD))


def _decay_tables(dec_ref, hd):
    row = lax.broadcasted_iota(jnp.int32, (CHUNK, CHUNK), 0).astype(F32)
    col = lax.broadcasted_iota(jnp.int32, (CHUNK, CHUNK), 1).astype(F32)
    lg_f = jnp.broadcast_to(-jnp.exp(dec_ref[hd:hd + 1, :]), (CHUNK, CHUNK))
    lg_b = jnp.broadcast_to(-jnp.exp(dec_ref[HEADS + hd:HEADS + hd + 1, :]), (CHUNK, CHUNK))
    d_f = row - col
    d_b = col - row
    return dict(
        decay_f=jnp.where(d_f >= 0, jnp.exp(lg_f * jnp.maximum(d_f, 0.0)), 0.0),
        decay_b=jnp.where(d_b >= 0, jnp.exp(lg_b * jnp.maximum(d_b, 0.0)), 0.0),
        qd_f=jnp.exp(lg_f * (row + 1.0)),
        qd_b=jnp.exp(lg_b * (CHUNK - row)),
        kd_f=jnp.exp(lg_f * (CHUNK - 1.0 - row)),
        kd_b=jnp.exp(lg_b * row),
        cd_f=jnp.exp(lg_f * CHUNK),
        cd_b=jnp.exp(lg_b * CHUNK),
    )


TABLE_NAMES = ("decay_f", "decay_b", "qd_f", "qd_b", "kd_f", "kd_b", "cd_f", "cd_b")


def _ret_state_kernel(x_ref, mod_ref, s0_ref, cos_ref, sin_ref, nmix_ref, wk_ref, wv_ref, dec_ref, sp_ref,
                      *, n_chunk):
    step = pl.program_id(0)

    @pl.when(step == 0)
    def _():
        sp_ref[...] = jnp.zeros_like(sp_ref)

    @pl.when(step > 0)
    def _():
        sh1, sc1 = _split_mod(mod_ref[0])[:2]
        hb = _rms_mod(x_ref[...], nmix_ref[...], sh1, sc1).astype(BF16)
        zk = _dot(hb, wk_ref[...].astype(BF16)) * (HEAD_DIM ** -0.5)
        zv = _dot(hb, wv_ref[...].astype(BF16)).astype(BF16)
        cos2, sin2 = cos_ref[...], sin_ref[...]
        for hd in range(HEADS):
            lo = hd * HEAD_DIM
            tb = _decay_tables(dec_ref, hd)
            kh = _rope(zk[:, lo:lo + HEAD_DIM], cos2, sin2)
            kv_f, kv_b = [], []
            for c in range(n_chunk):
                kc = kh[c * CHUNK:(c + 1) * CHUNK]
                vc = zv[c * CHUNK:(c + 1) * CHUNK, lo:lo + HEAD_DIM]
                kv_f.append(_dot((kc * tb["kd_f"]).T.astype(BF16), vc))
                kv_b.append(_dot((kc * tb["kd_b"]).T.astype(BF16), vc))
            st = s0_ref[0, 0, 0, hd]
            for c in range(n_chunk):
                if c % UNIT_CHUNKS == 0:
                    sp_ref[c // UNIT_CHUNKS, 0, hd] = st
                st = tb["cd_f"] * st + kv_f[c]
            st = s0_ref[0, 0, 1, hd]
            for c in reversed(range(n_chunk)):
                if c % UNIT_CHUNKS == UNIT_CHUNKS - 1:
                    sp_ref[c // UNIT_CHUNKS, 1, hd] = st
                st = tb["cd_b"] * st + kv_b[c]


def _ret_state_call(x2, mod, s0, rope, nmix, w_in, dec, *, n_seq, seq_len, even_index):
    n_chunk = seq_len // CHUNK
    units = seq_len // UNIT
    seq = lambda i: jnp.maximum(i - 1, 0)
    return pl.pallas_call(
        functools.partial(_ret_state_kernel, n_chunk=n_chunk),
        grid=(n_seq + 1,),
        in_specs=[
            pl.BlockSpec((seq_len, D_MODEL), lambda i: (seq(i), 0)),
            pl.BlockSpec((1, 1, N_MOD), lambda i: (seq(i), 0, 0)),
            pl.BlockSpec((1, 1, 2, HEADS, HEAD_DIM, HEAD_DIM), lambda i: (seq(i), 0, 0, 0, 0, 0)),
            _resident((seq_len, HEAD_DIM)), _resident((seq_len, HEAD_DIM)),
            _resident((1, D_MODEL)),
            pl.BlockSpec((None, D_MODEL, QK_W), lambda i: (even_index, 0, K_COL // QK_W),
                         pipeline_mode=pl.Buffered(1)),
            pl.BlockSpec((None, D_MODEL, V_W), lambda i: (even_index, 0, V_COL // V_W),
                         pipeline_mode=pl.Buffered(1)),
            _resident(dec.shape),
        ],
        out_specs=pl.BlockSpec((units, 2, HEADS, HEAD_DIM, HEAD_DIM), lambda i: (i, 0, 0, 0, 0)),
        out_shape=jax.ShapeDtypeStruct(((n_seq + 1) * units, 2, HEADS, HEAD_DIM, HEAD_DIM), F32),
        compiler_params=pltpu.CompilerParams(
            dimension_semantics=("arbitrary",),
            vmem_limit_bytes=V7X_VMEM_LIMIT_BYTES),
        name="ret_state",
    )(x2, mod, s0, rope[0], rope[1], nmix, w_in, w_in, dec)


def _even_kernel(xa_short, xa_long, xc_short, xc_long, moda_ref, modc_ref, sp_ref, cos_ref, sin_ref,
                 nmix_ref, nmlp_ref, w_s_ref, b_s_ref, vgain_ref, dec_ref, w_in_hbm, w_out_hbm, w1_hbm, w2_hbm,
                 y_ref, st_ref, ug_s, vg_s, q_s, k_s, v_s, gf_s, gb_s, cat_s, tab_s,
                 w_in_ref, w_out_ref, w1_ref, w2_ref, stage, sem, *, units, layer, even_index):
    n_units, n_short = units.total, units.n_short
    step = pl.program_id(0)
    slot_a = lax.rem(step, 2)
    slot_b = 1 - slot_a

    @pl.when(step == 0)
    def _():
        _load_weights_bf16(_weight_chunks(w_in_hbm.at[even_index], w_in_ref)
                           + _weight_chunks(w_out_hbm.at[even_index], w_out_ref)
                           + _weight_chunks(w1_hbm.at[layer], w1_ref)
                           + _weight_chunks(w2_hbm.at[layer], w2_ref), stage, sem)
        for hd in range(HEADS):
            tb = _decay_tables(dec_ref, hd)
            for n, name in enumerate(TABLE_NAMES):
                tab_s[hd, n] = tb[name]

    def stage_in():
        sh1, sc1 = _split_mod(moda_ref[0])[:2]
        x = _pick_block(step < n_short, xa_short, xa_long)
        hb = _rms_mod(x, nmix_ref[...], sh1, sc1).astype(BF16)
        z = _dot(hb, w_in_ref[...])
        yield
        ug_s[slot_a] = _gelu_tanh(z[:, 0:A_WIDTH])
        yield
        zv = _gelu_tanh(z[:, A_WIDTH:2 * A_WIDTH])
        vgn = [_layer_norm(zv[:, g * A_GROUP_DIM:(g + 1) * A_GROUP_DIM]) for g in range(A_GROUPS)]
        vg_s[slot_a] = (jnp.concatenate(vgn, axis=-1) * vgain_ref[...]).astype(BF16)
        yield
        cos2, sin2 = cos_ref[...], sin_ref[...]
        heads = lambda zz: jnp.concatenate(
            [_rope(zz[:, h * HEAD_DIM:(h + 1) * HEAD_DIM], cos2, sin2) for h in range(HEADS)], -1)
        q_s[slot_a] = heads(z[:, 2 * A_WIDTH:K_COL]).astype(BF16)
        yield
        k_s[slot_a] = heads(z[:, K_COL:V_COL] * (HEAD_DIM ** -0.5))
        v_s[slot_a] = z[:, V_COL:V_COL + V_W].astype(BF16)
        yield
        gf_s[slot_a] = _silu(z[:, V_COL + V_W:V_COL + 2 * V_W])
        yield
        gb_s[slot_a] = _silu(z[:, V_COL + 2 * V_W:V_COL + 3 * V_W])

    IN_LIGHT = 6

    def stage_mix(final_states):
        s = slot_b
        rows = [slice(c * CHUNK, (c + 1) * CHUNK) for c in range(UNIT_CHUNKS)]
        head_cols = [slice(hd * HEAD_DIM, (hd + 1) * HEAD_DIM) for hd in range(HEADS)]
        table = lambda hd, name: tab_s[hd, TABLE_NAMES.index(name)]

        sp2 = []
        for g in range(A_GROUPS):
            lo = g * A_GROUP_DIM
            vg2 = jnp.concatenate([vg_s[s, r, lo:lo + A_GROUP_DIM] for r in rows], axis=-1)
            sp2.append(_dot(w_s_ref[g], vg2))
        yield
        scores, kv_f, kv_b = [], [], []
        for hd in range(HEADS):
            kc = [k_s[s, r, head_cols[hd]] for r in rows]
            vc = [v_s[s, r, head_cols[hd]] for r in rows]
            scores.append([_dot_nt(q_s[s, r, head_cols[hd]], kc[c].astype(BF16)) for c, r in enumerate(rows)])
            kv_f.append([_dot((kc[c] * table(hd, "kd_f")).T.astype(BF16), vc[c]) for c in range(UNIT_CHUNKS)])
            kv_b.append([_dot((kc[c] * table(hd, "kd_b")).T.astype(BF16), vc[c]) for c in range(UNIT_CHUNKS)])
            yield
        for g in range(A_GROUPS):
            lo = g * A_GROUP_DIM
            for c, r in enumerate(rows):
                gate = sp2[g][:, c * CHUNK:(c + 1) * CHUNK] + b_s_ref[g]
                cat_s[s, r, lo:lo + A_GROUP_DIM] = (ug_s[s, r, lo:lo + A_GROUP_DIM] * gate).astype(BF16)
            if g % 2 == 1:
                yield
        y_raw = []
        for hd in range(HEADS):
            enter_f, st = [], sp_ref[0, 0, hd]
            for c in range(UNIT_CHUNKS):
                enter_f.append(st)
                st = table(hd, "cd_f") * st + kv_f[hd][c]
            final_f = st
            enter_b, st = [None] * UNIT_CHUNKS, sp_ref[0, 1, hd]
            for c in reversed(range(UNIT_CHUNKS)):
                enter_b[c] = st
                st = table(hd, "cd_b") * st + kv_b[hd][c]
            final_states.append((final_f, st))
            per_chunk = []
            for c, r in enumerate(rows):
                vc = v_s[s, r, head_cols[hd]]
                qf = q_s[s, r, head_cols[hd]].astype(F32)

                def direction(decay, qd, enter):
                    p = (scores[hd][c] * table(hd, decay)).astype(BF16)
                    lhs = jnp.concatenate([p, (qf * table(hd, qd)).astype(BF16)], axis=-1)
                    return _dot(lhs, jnp.concatenate([vc, enter.astype(BF16)], axis=0))

                per_chunk.append((direction("decay_f", "qd_f", enter_f[c]), direction("decay_b", "qd_b", enter_b[c])))
            y_raw.append(per_chunk)
            yield
        for hd in range(HEADS):
            for c, r in enumerate(rows):
                y_f, y_b = y_raw[hd][c]
                cols = slice(A_WIDTH + hd * HEAD_DIM, A_WIDTH + (hd + 1) * HEAD_DIM)
                cat_s[s, r, cols] = (gf_s[s, r, head_cols[hd]] * _layer_norm(y_f)
                                     + gb_s[s, r, head_cols[hd]] * _layer_norm(y_b)).astype(BF16)
            yield

    MIX_LIGHT = 1 + HEADS + A_GROUPS // 2 + 2 * HEADS

    def emit_states(final_states):
        @pl.when(step - 1 < n_short)
        def _():
            for hd, (final_f, final_b) in enumerate(final_states):
                st_ref[0, 0, 0, hd] = final_f
                st_ref[0, 0, 1, hd] = final_b

    def stage_mlp():
        _, _, g1, sh2, sc2, g2 = _split_mod(modc_ref[0])
        x = _pick_block(step - 2 < n_short, xc_short, xc_long) + g1 * _dot(cat_s[slot_a], w_out_ref[...])
        hb = _rms_mod(x, nmlp_ref[...], sh2, sc2).astype(BF16)
        yield
        acc = yield from _mlp_pieces(hb, w1_ref, w2_ref)
        y_ref[...] = x + g2 * acc

    @pl.when(step == 0)
    def _():
        _run(stage_in())

    @pl.when(step == 1)
    def _():
        a, finals = stage_in(), []
        _weave(_pieces(a, 1), [a, stage_mix(finals)], IN_LIGHT + MIX_LIGHT)
        emit_states(finals)

    @pl.when((step >= 2) & (step < n_units))
    def _():
        a, c, finals = stage_in(), stage_mlp(), []
        heavy = _pieces(c, 1) + _pieces(a, 1) + _pieces(c, MLP_DOT_PIECES + 1)
        _weave(heavy, [a, stage_mix(finals)], IN_LIGHT + MIX_LIGHT)
        emit_states(finals)

    @pl.when(step == n_units)
    def _():
        finals = []
        _weave(_pieces(stage_mlp(), MLP_DOT_PIECES + 2), [stage_mix(finals)], MIX_LIGHT)
        emit_states(finals)

    @pl.when(step == n_units + 1)
    def _():
        _run(stage_mlp())


_HBM = pl.BlockSpec(memory_space=pl.ANY)
_WEIGHT_STAGING = [pltpu.VMEM((STAGE_SLOTS, STAGE_ROWS, STAGE_COLS), F32), pltpu.SemaphoreType.DMA((STAGE_SLOTS,))]


def _even_call(x_short, x_long, mod, sp, rope, params, weights, units, *, layer, even_index):
    n_units = units.total
    unit_a = lambda i: units.clamp(i)
    unit_b = lambda i: units.clamp(i - 1)
    unit_c = lambda i: units.clamp(i - 2)
    state_block = (1, 1, 2, HEADS, HEAD_DIM, HEAD_DIM)

    in_specs = [pl.BlockSpec((UNIT, D_MODEL), lambda i: (units.short_block(unit_a(i)), 0)),
                pl.BlockSpec((UNIT, D_MODEL), lambda i: (units.long_block(unit_a(i)), 0)),
                pl.BlockSpec((UNIT, D_MODEL), lambda i: (units.short_block(unit_c(i)), 0)),
                pl.BlockSpec((UNIT, D_MODEL), lambda i: (units.long_block(unit_c(i)), 0)),
                pl.BlockSpec((1, 1, N_MOD), lambda i: (units.mod_row(unit_a(i)), 0, 0)),
                pl.BlockSpec((1, 1, N_MOD), lambda i: (units.mod_row(unit_c(i)), 0, 0)),
                pl.BlockSpec((1, 2, HEADS, HEAD_DIM, HEAD_DIM), lambda i: (units.state_block(unit_b(i)), 0, 0, 0, 0)),
                pl.BlockSpec((UNIT, HEAD_DIM), lambda i: (units.rope_block(unit_a(i)), 0)),
                pl.BlockSpec((UNIT, HEAD_DIM), lambda i: (units.rope_block(unit_a(i)), 0))]
    in_specs += [_resident(p.shape) for p in params] + [_HBM] * len(weights)
    scratch = [
        pltpu.VMEM((2, UNIT, A_WIDTH), F32),
        pltpu.VMEM((2, UNIT, A_WIDTH), BF16),
        pltpu.VMEM((2, UNIT, QK_W), BF16),
        pltpu.VMEM((2, UNIT, QK_W), F32),
        pltpu.VMEM((2, UNIT, V_W), BF16),
        pltpu.VMEM((2, UNIT, V_W), F32),
        pltpu.VMEM((2, UNIT, V_W), F32),
        pltpu.VMEM((2, UNIT, OUT_WIDTH), BF16),
        pltpu.VMEM((HEADS, len(TABLE_NAMES), CHUNK, CHUNK), F32),
    ]
    scratch += [pltpu.VMEM(w.shape[1:], BF16) for w in weights] + _WEIGHT_STAGING
    return pl.pallas_call(
        functools.partial(_even_kernel, units=units, layer=layer, even_index=even_index),
        grid=(n_units + 2,),
        in_specs=in_specs,
        out_specs=[pl.BlockSpec((UNIT, D_MODEL), lambda i: (unit_c(i), 0)),
                   pl.BlockSpec(state_block, lambda i: (units.short_block(unit_b(i)), 0, 0, 0, 0, 0))],
        out_shape=[jax.ShapeDtypeStruct((n_units * UNIT, D_MODEL), F32),
                   jax.ShapeDtypeStruct((units.n_short,) + state_block[1:], F32)],
        scratch_shapes=scratch,
        compiler_params=pltpu.CompilerParams(
            dimension_semantics=("arbitrary",),
            vmem_limit_bytes=V7X_VMEM_LIMIT_BYTES),
        name="even_layer",
    )(x_short, x_long, x_short, x_long, mod, mod, sp, rope[0], rope[1], *params, *weights)


def _odd_kernel(xp_ref, x_ref, xn_ref, moda_ref, modc_ref, nmix_ref, nmlp_ref, fin_ref, pscale_ref,
                w_pool_hbm, w1_hbm, w2_hbm, y_short, y_long, x1_s, hb_s, w_pool_ref, w1_ref, w2_ref, stage, sem,
                *, n_units, n_short, short_len, long_len, layer, odd_index):
    step = pl.program_id(0)
    slot_a = lax.rem(step, 2)
    slot_b = 1 - slot_a
    rows_w = UNIT + 2 * POOL_HALO

    @pl.when(step == 0)
    def _():
        _load_weights_bf16(_weight_chunks(w_pool_hbm.at[odd_index], w_pool_ref)
                           + _weight_chunks(w1_hbm.at[layer], w1_ref)
                           + _weight_chunks(w2_hbm.at[layer], w2_ref), stage, sem)

    def stage_pool():
        sh1, sc1, g1, sh2, sc2, _ = _split_mod(moda_ref[0])
        xw = jnp.concatenate([xp_ref[...], x_ref[...], xn_ref[...]], axis=0)
        h = _rms_mod(xw, nmix_ref[...], sh1, sc1)
        seq_len = jnp.where(step < n_short, short_len, long_len)
        first = jnp.minimum(step, n_units - 1) * UNIT - POOL_HALO
        t = (lax.broadcasted_iota(jnp.int32, (rows_w, 1), 0) + first) & (seq_len - 1)
        yield

        def shift_down(a, k):
            return jnp.where(t >= k, pltpu.roll(a, k, axis=0), 0.0)

        def shift_up(a, k):
            return jnp.where(t < seq_len - k, pltpu.roll(a, rows_w - k, axis=0), 0.0)

        outs = []
        for g, w in enumerate(POOL_WINDOWS):
            hg = h[:, g * POOL_DIM:(g + 1) * POOL_DIM]
            half = w // 2
            left = shift_down(hg, 1)
            right = hg
            span = 1
            while span < half:
                left = left + shift_down(left, span)
                right = right + shift_up(right, span)
                span *= 2
            cnt = (jnp.minimum(t + half, seq_len) - jnp.maximum(t - half, 0)).astype(F32)
            pooled = ((left + right) / cnt - hg)[POOL_HALO:POOL_HALO + UNIT].astype(BF16)
            outs.append(_dot(pooled, w_pool_ref[g * POOL_DIM:(g + 1) * POOL_DIM, :]))
            yield
        x1 = x_ref[...] + g1 * (jnp.concatenate(outs, axis=-1) * pscale_ref[...])
        x1_s[slot_a] = x1
        hb_s[slot_a] = _rms_mod(x1, nmlp_ref[...], sh2, sc2).astype(BF16)

    def stage_mlp(result):
        g2 = _split_mod(modc_ref[0])[5]
        acc = yield from _mlp_pieces(hb_s[slot_b], w1_ref, w2_ref)
        result.append(_rms(x1_s[slot_b] + g2 * acc) * fin_ref[...])

    def store(result):
        @pl.when(step - 1 < n_short)
        def _():
            y_short[...] = result[0]

        @pl.when(step - 1 >= n_short)
        def _():
            y_long[...] = result[0]

    pool_light = 2 + len(POOL_WINDOWS)

    @pl.when(step == 0)
    def _():
        _run(stage_pool())

    @pl.when((step >= 1) & (step < n_units))
    def _():
        result = []
        _weave(_pieces(stage_mlp(result), MLP_DOT_PIECES + 1), [stage_pool()], pool_light)
        store(result)

    @pl.when(step == n_units)
    def _():
        result = []
        _run(stage_mlp(result))
        store(result)


def _odd_call(x2, mod, params, weights, units, *, short_len, long_len, layer, odd_index):
    n_units = units.total
    halo_per_unit = UNIT // POOL_HALO
    last_halo = x2.shape[0] // POOL_HALO - 1
    unit_a = lambda i: units.clamp(i)
    unit_c = lambda i: units.clamp(i - 1)
    assert (units.n_short * UNIT) % long_len == 0
    in_specs = [pl.BlockSpec((POOL_HALO, D_MODEL), lambda i: (jnp.maximum(unit_a(i) * halo_per_unit - 1, 0), 0)),
                pl.BlockSpec((UNIT, D_MODEL), lambda i: (unit_a(i), 0)),
                pl.BlockSpec((POOL_HALO, D_MODEL),
                             lambda i: (jnp.minimum((unit_a(i) + 1) * halo_per_unit, last_halo), 0)),
                pl.BlockSpec((1, 1, N_MOD), lambda i: (units.mod_row(unit_a(i)), 0, 0)),
                pl.BlockSpec((1, 1, N_MOD), lambda i: (units.mod_row(unit_c(i)), 0, 0))]
    in_specs += [_resident(p.shape) for p in params] + [_HBM] * len(weights)
    scratch = [pltpu.VMEM((2, UNIT, D_MODEL), F32),
               pltpu.VMEM((2, UNIT, D_MODEL), BF16)]
    scratch += [pltpu.VMEM(w.shape[1:], BF16) for w in weights] + _WEIGHT_STAGING
    return pl.pallas_call(
        functools.partial(_odd_kernel, n_units=n_units, n_short=units.n_short,
                          short_len=short_len, long_len=long_len, layer=layer, odd_index=odd_index),
        grid=(n_units + 1,),
        in_specs=in_specs,
        out_specs=[pl.BlockSpec((UNIT, D_MODEL), lambda i: (units.short_block(unit_c(i)), 0)),
                   pl.BlockSpec((UNIT, D_MODEL), lambda i: (units.long_block(unit_c(i)), 0))],
        out_shape=[jax.ShapeDtypeStruct((units.n_short * UNIT, D_MODEL), F32),
                   jax.ShapeDtypeStruct((units.n_long * UNIT, D_MODEL), F32)],
        scratch_shapes=scratch,
        compiler_params=pltpu.CompilerParams(
            dimension_semantics=("arbitrary",),
            vmem_limit_bytes=V7X_VMEM_LIMIT_BYTES),
        name="odd_layer",
    )(x2, x2, x2, mod, mod, *params, *weights)


def _rope_tables(n_tokens):
    rows = n_tokens // GRID_W
    pos_r = jnp.repeat(jnp.arange(rows, dtype=F32), GRID_W)
    pos_c = jnp.tile(jnp.arange(GRID_W, dtype=F32), rows)
    inv = ROPE_BASE ** (-jnp.arange(ROPE_PAIRS_AXIS, dtype=F32) / ROPE_PAIRS_AXIS)
    ang = jnp.concatenate([pos_r[:, None] * inv, pos_c[:, None] * inv], axis=-1)
    cos, sin = jnp.cos(ang), jnp.sin(ang)
    return jnp.concatenate([cos, cos], axis=-1), jnp.concatenate([-sin, sin], axis=-1)


def kernel(x_prompt, x_sample, state_ret, c, c_ctx, ada_w, ada_b, norm_mix, norm_mlp, mlp_w1, mlp_w2,
           ev_w_in, ev_w_s, ev_b_s, ev_vnorm, ev_decay, ev_w_out, od_w_pool, od_pool_scale, final_norm):
    n_prompt, prompt_len, _ = x_prompt.shape
    n_sample, sample_len, _ = x_sample.shape
    assert DEPTH == 2 and n_sample + 1 <= COND_ROWS
    assert prompt_len == UNIT and sample_len % UNIT == 0
    units = _Units(n_prompt, n_sample, sample_len // UNIT)

    cond = jnp.zeros((COND_ROWS, D_MODEL), F32).at[0].set(c_ctx).at[1:1 + n_sample].set(c)
    mod = _ada_call(cond, ada_w, ada_b)[:, :1 + n_sample].reshape(DEPTH, 1 + n_sample, 1, N_MOD)

    row = lambda v: v.reshape(1, -1)
    dec = jnp.broadcast_to(ev_decay[0].reshape(2 * HEADS, 1), (2 * HEADS, CHUNK))
    even_params = (
        row(norm_mix[0]), row(norm_mlp[0]),
        ev_w_s[0].astype(BF16),
        jnp.broadcast_to(ev_b_s[0][:, :, None], (A_GROUPS, CHUNK, A_GROUP_DIM)),
        row(ev_vnorm[0]), dec,
    )
    odd_params = (row(norm_mix[1]), row(norm_mlp[1]), row(final_norm), row(od_pool_scale[0]))
    even_weights = (ev_w_in, ev_w_out, mlp_w1, mlp_w2)
    odd_weights = (od_w_pool.reshape(od_w_pool.shape[0], D_MODEL, POOL_DIM), mlp_w1, mlp_w2)

    xp2 = x_prompt.reshape(n_prompt * prompt_len, D_MODEL)
    xs2 = x_sample.reshape(n_sample * sample_len, D_MODEL)
    cos2, sin2 = _rope_tables(sample_len)
    sp = _ret_state_call(xs2, mod[0, 1:], state_ret, (cos2, sin2), row(norm_mix[0]), ev_w_in, dec,
                         n_seq=n_sample, seq_len=sample_len, even_index=0)
    rope = (jnp.concatenate([jnp.ones((UNIT, HEAD_DIM), F32), cos2]),
            jnp.concatenate([jnp.zeros((UNIT, HEAD_DIM), F32), sin2]))
    x_all, new_state = _even_call(xp2, xs2, mod[0], sp, rope, even_params, even_weights, units,
                                  layer=0, even_index=0)
    y_prompt, y_sample = _odd_call(x_all, mod[1], odd_params, odd_weights, units,
                                   short_len=prompt_len, long_len=sample_len, layer=1, odd_index=0)

    return (y_prompt.reshape(n_prompt, prompt_len, D_MODEL),
            y_sample.reshape(n_sample, sample_len, D_MODEL),
            new_state)
```

```python
import functools

import jax
import jax.numpy as jnp
from jax import lax
from jax.experimental import pallas as pl
from jax.experimental.pallas import tpu as pltpu

D_MODEL = 1024
DEPTH = 2
GRID_W = 64
A_WIDTH = D_MODEL // 2
A_GROUPS = 4
A_GROUP_DIM = A_WIDTH // A_GROUPS
CHUNK = 128
HEADS = 4
HEAD_DIM = (D_MODEL // 2) // HEADS
ROPE_BASE = 10000.0
ROPE_PAIRS_AXIS = HEAD_DIM // 4
POOL_WINDOWS = (2, 4, 8, 16)
POOL_DIM = D_MODEL // len(POOL_WINDOWS)
POOL_HALO = max(POOL_WINDOWS) // 2
D_FF = 4 * D_MODEL
EPS = 1e-6
QK_W = HEADS * HEAD_DIM
V_W = HEADS * HEAD_DIM
IN_WIDTH = 2 * A_WIDTH + 2 * QK_W + 3 * V_W
OUT_WIDTH = A_WIDTH + V_W
N_MOD = 6 * D_MODEL
K_COL = 2 * A_WIDTH + QK_W
V_COL = K_COL + QK_W

UNIT = 2 * CHUNK
UNIT_CHUNKS = UNIT // CHUNK
FF_CHUNK = 1024
ADA_BLOCK_N = 1536
COND_ROWS = 8
STAGE_ROWS = 512
STAGE_COLS = 1024
STAGE_SLOTS = 3
V7X_VMEM_LIMIT_BYTES = 60 * 1024 * 1024

F32 = jnp.float32
BF16 = jnp.bfloat16


def _dot(a, b):
    return jnp.dot(a, b, preferred_element_type=F32)


def _dot_nt(a, b):
    return lax.dot_general(a, b, (((1,), (1,)), ((), ())), preferred_element_type=F32)


def _silu(x):
    return x * jax.nn.sigmoid(x)


def _gelu_tanh(x):
    return 0.5 * x * (1.0 + jnp.tanh(0.7978845608028654 * (x + 0.044715 * (x * x * x))))


def _rms(x):
    return x * lax.rsqrt(jnp.mean(x * x, axis=-1, keepdims=True) + EPS)


def _rms_mod(x, gain, shift, scale):
    return (_rms(x) * gain) * (1.0 + scale) + shift


def _layer_norm(x):
    mu = jnp.mean(x, axis=-1, keepdims=True)
    d = x - mu
    return d * lax.rsqrt(jnp.mean(d * d, axis=-1, keepdims=True) + EPS)


def _split_mod(mod):
    return [mod[:, i * D_MODEL:(i + 1) * D_MODEL] for i in range(6)]


def _rope(x, cos2, sin2):
    return x * cos2 + pltpu.roll(x, HEAD_DIM // 2, axis=1) * sin2


def _pick_block(take_first, a_ref, b_ref):
    rows = lax.broadcasted_iota(jnp.int32, (a_ref.shape[0], 1), 0)
    return jnp.where(rows < jnp.where(take_first, a_ref.shape[0], 0), a_ref[...], b_ref[...])


def _run(gen, n=None):
    done = 0
    while n is None or done < n:
        try:
            next(gen)
        except StopIteration:
            break
        done += 1
    return done


def _weave(heavy, lights, n_light):
    heavy = list(heavy)
    pending = _roundrobin(lights)
    ran = 0
    for k, piece in enumerate(heavy):
        piece()
        want = ((k + 1) * n_light + len(heavy) - 1) // len(heavy)
        ran += _run(pending, want - ran)
    _run(pending)


def _pieces(gen, n):
    return [functools.partial(next, gen, None) for _ in range(n)]


def _roundrobin(gens):
    live = list(gens)
    while live:
        for g in list(live):
            try:
                next(g)
            except StopIteration:
                live.remove(g)
                continue
            yield


MLP_DOT_PIECES = 2 * (D_FF // FF_CHUNK)


def _mlp_pieces(hb, w1_ref, w2_ref):
    n_ff = D_FF // FF_CHUNK
    up = lambda j: _dot(hb, w1_ref[:, j * FF_CHUNK:(j + 1) * FF_CHUNK])
    a = up(0)
    yield
    acc = None
    for j in range(n_ff):
        a_next = None
        if j + 1 < n_ff:
            a_next = up(j + 1)
            yield
        p = _dot(jnp.square(jnp.maximum(a, 0.0)).astype(BF16), w2_ref[j * FF_CHUNK:(j + 1) * FF_CHUNK, :])
        acc = p if acc is None else acc + p
        a = a_next
        yield
    return acc


def _resident(shape):
    return pl.BlockSpec(shape, lambda i: (0,) * len(shape), pipeline_mode=pl.Buffered(1))


def _weight_chunks(w_hbm, w_vmem):
    rows, cols = w_hbm.shape
    jobs = []
    for r0 in range(0, rows, STAGE_ROWS):
        for c0 in range(0, cols, STAGE_COLS):
            window = (pl.ds(r0, min(STAGE_ROWS, rows - r0)), pl.ds(c0, min(STAGE_COLS, cols - c0)))
            jobs.append((w_hbm.at[window], w_vmem.at[window]))
    return jobs


def _load_weights_bf16(jobs, stage, sem):
    n_slots = stage.shape[0]

    def copy(k):
        r, c = jobs[k][0].shape
        slot = k % n_slots
        return pltpu.make_async_copy(jobs[k][0], stage.at[slot, pl.ds(0, r), pl.ds(0, c)], sem.at[slot])

    for k in range(min(n_slots - 1, len(jobs))):
        copy(k).start()
    for k, (src, dst) in enumerate(jobs):
        if k + n_slots - 1 < len(jobs):
            copy(k + n_slots - 1).start()
        copy(k).wait()
        r, c = src.shape
        dst[...] = stage[k % n_slots, 0:r, 0:c].astype(BF16)


class _Units:
    def __init__(self, n_short, n_long, units_per_long):
        self.n_short, self.n_long, self.units_per_long = n_short, n_long * units_per_long, units_per_long
        self.total = self.n_short + self.n_long

    def clamp(self, unit):
        return jnp.clip(unit, 0, self.total - 1)

    def short_block(self, unit):
        return jnp.minimum(unit, self.n_short - 1)

    def long_block(self, unit):
        return jnp.clip(unit - self.n_short, 0, self.n_long - 1)

    def mod_row(self, unit):
        return jnp.where(unit < self.n_short, 0, 1 + (unit - self.n_short) // self.units_per_long)

    def rope_block(self, unit):
        return jnp.where(unit < self.n_short, 0, 1 + (unit - self.n_short) % self.units_per_long)

    def state_block(self, unit):
        return jnp.where(unit < self.n_short, 0, self.units_per_long + unit - self.n_short)


def _ada_kernel(cond_ref, w_ref, b_ref, out_ref):
    s = _silu(cond_ref[...]).astype(BF16)
    out_ref[0] = _dot(s, w_ref[0].astype(BF16)) + b_ref[0]


def _ada_call(cond, ada_w, ada_b):
    return pl.pallas_call(
        _ada_kernel,
        grid=(DEPTH, N_MOD // ADA_BLOCK_N),
        in_specs=[
            pl.BlockSpec((COND_ROWS, D_MODEL), lambda i, j: (0, 0)),
            pl.BlockSpec((1, D_MODEL, ADA_BLOCK_N), lambda i, j: (i, 0, j)),
            pl.BlockSpec((1, 1, ADA_BLOCK_N), lambda i, j: (i, 0, j)),
        ],
        out_specs=pl.BlockSpec((1, COND_ROWS, ADA_BLOCK_N), lambda i, j: (i, 0, j)),
        out_shape=jax.ShapeDtypeStruct((DEPTH, COND_ROWS, N_MOD), F32),
        compiler_params=pltpu.CompilerParams(
            dimension_semantics=("arbitrary", "arbitrary"),
            vmem_limit_bytes=V7X_VMEM_LIMIT_BYTES),
        name="ada_mod",
    )(cond, ada_w, ada_b.reshape(DEPTH, 1, N_MOD))


def _decay_tables(dec_ref, hd):
    row = lax.broadcasted_iota(jnp.int32, (CHUNK, CHUNK), 0).astype(F32)
    col = lax.broadcasted_iota(jnp.int32, (CHUNK, CHUNK), 1).astype(F32)
    lg_f = jnp.broadcast_to(-jnp.exp(dec_ref[hd:hd + 1, :]), (CHUNK, CHUNK))
    lg_b = jnp.broadcast_to(-jnp.exp(dec_ref[HEADS + hd:HEADS + hd + 1, :]), (CHUNK, CHUNK))
    d_f = row - col
    d_b = col - row
    return dict(
        decay_f=jnp.where(d_f >= 0, jnp.exp(lg_f * jnp.maximum(d_f, 0.0)), 0.0),
        decay_b=jnp.where(d_b >= 0, jnp.exp(lg_b * jnp.maximum(d_b, 0.0)), 0.0),
        qd_f=jnp.exp(lg_f * (row + 1.0)),
        qd_b=jnp.exp(lg_b * (CHUNK - row)),
        kd_f=jnp.exp(lg_f * (CHUNK - 1.0 - row)),
        kd_b=jnp.exp(lg_b * row),
        cd_f=jnp.exp(lg_f * CHUNK),
        cd_b=jnp.exp(lg_b * CHUNK),
    )


TABLE_NAMES = ("decay_f", "decay_b", "qd_f", "qd_b", "kd_f", "kd_b", "cd_f", "cd_b")


def _ret_state_kernel(x_ref, mod_ref, s0_ref, cos_ref, sin_ref, nmix_ref, wk_ref, wv_ref, dec_ref, sp_ref,
                      *, n_chunk):
    step = pl.program_id(0)

    @pl.when(step == 0)
    def _():
        sp_ref[...] = jnp.zeros_like(sp_ref)

    @pl.when(step > 0)
    def _():
        sh1, sc1 = _split_mod(mod_ref[0])[:2]
        hb = _rms_mod(x_ref[...], nmix_ref[...], sh1, sc1).astype(BF16)
        zk = _dot(hb, wk_ref[...].astype(BF16)) * (HEAD_DIM ** -0.5)
        zv = _dot(hb, wv_ref[...].astype(BF16)).astype(BF16)
        cos2, sin2 = cos_ref[...], sin_ref[...]
        for hd in range(HEADS):
            lo = hd * HEAD_DIM
            tb = _decay_tables(dec_ref, hd)
            kh = _rope(zk[:, lo:lo + HEAD_DIM], cos2, sin2)
            kv_f, kv_b = [], []
            for c in range(n_chunk):
                kc = kh[c * CHUNK:(c + 1) * CHUNK]
                vc = zv[c * CHUNK:(c + 1) * CHUNK, lo:lo + HEAD_DIM]
                kv_f.append(_dot((kc * tb["kd_f"]).T.astype(BF16), vc))
                kv_b.append(_dot((kc * tb["kd_b"]).T.astype(BF16), vc))
            st = s0_ref[0, 0, 0, hd]
            for c in range(n_chunk):
                if c % UNIT_CHUNKS == 0:
                    sp_ref[c // UNIT_CHUNKS, 0, hd] = st
                st = tb["cd_f"] * st + kv_f[c]
            st = s0_ref[0, 0, 1, hd]
            for c in reversed(range(n_chunk)):
                if c % UNIT_CHUNKS == UNIT_CHUNKS - 1:
                    sp_ref[c // UNIT_CHUNKS, 1, hd] = st
                st = tb["cd_b"] * st + kv_b[c]


def _ret_state_call(x2, mod, s0, rope, nmix, w_in, dec, *, n_seq, seq_len, even_index):
    n_chunk = seq_len // CHUNK
    units = seq_len // UNIT
    seq = lambda i: jnp.maximum(i - 1, 0)
    return pl.pallas_call(
        functools.partial(_ret_state_kernel, n_chunk=n_chunk),
        grid=(n_seq + 1,),
        in_specs=[
            pl.BlockSpec((seq_len, D_MODEL), lambda i: (seq(i), 0)),
            pl.BlockSpec((1, 1, N_MOD), lambda i: (seq(i), 0, 0)),
            pl.BlockSpec((1, 1, 2, HEADS, HEAD_DIM, HEAD_DIM), lambda i: (seq(i), 0, 0, 0, 0, 0)),
            _resident((seq_len, HEAD_DIM)), _resident((seq_len, HEAD_DIM)),
            _resident((1, D_MODEL)),
            pl.BlockSpec((None, D_MODEL, QK_W), lambda i: (even_index, 0, K_COL // QK_W),
                         pipeline_mode=pl.Buffered(1)),
            pl.BlockSpec((None, D_MODEL, V_W), lambda i: (even_index, 0, V_COL // V_W),
                         pipeline_mode=pl.Buffered(1)),
            _resident(dec.shape),
        ],
        out_specs=pl.BlockSpec((units, 2, HEADS, HEAD_DIM, HEAD_DIM), lambda i: (i, 0, 0, 0, 0)),
        out_shape=jax.ShapeDtypeStruct(((n_seq + 1) * units, 2, HEADS, HEAD_DIM, HEAD_DIM), F32),
        compiler_params=pltpu.CompilerParams(
            dimension_semantics=("arbitrary",),
            vmem_limit_bytes=V7X_VMEM_LIMIT_BYTES),
        name="ret_state",
    )(x2, mod, s0, rope[0], rope[1], nmix, w_in, w_in, dec)


def _even_kernel(xa_short, xa_long, xc_short, xc_long, moda_ref, modc_ref, sp_ref, cos_ref, sin_ref,
                 nmix_ref, nmlp_ref, w_s_ref, b_s_ref, vgain_ref, dec_ref, w_in_hbm, w_out_hbm, w1_hbm, w2_hbm,
                 y_ref, st_ref, ug_s, vg_s, q_s, k_s, v_s, gf_s, gb_s, cat_s, tab_s,
                 w_in_ref, w_out_ref, w1_ref, w2_ref, stage, sem, *, units, layer, even_index):
    n_units, n_short = units.total, units.n_short
    step = pl.program_id(0)
    slot_a = lax.rem(step, 2)
    slot_b = 1 - slot_a

    @pl.when(step == 0)
    def _():
        _load_weights_bf16(_weight_chunks(w_in_hbm.at[even_index], w_in_ref)
                           + _weight_chunks(w_out_hbm.at[even_index], w_out_ref)
                           + _weight_chunks(w1_hbm.at[layer], w1_ref)
                           + _weight_chunks(w2_hbm.at[layer], w2_ref), stage, sem)
        for hd in range(HEADS):
            tb = _decay_tables(dec_ref, hd)
            for n, name in enumerate(TABLE_NAMES):
                tab_s[hd, n] = tb[name]

    def stage_in():
        sh1, sc1 = _split_mod(moda_ref[0])[:2]
        x = _pick_block(step < n_short, xa_short, xa_long)
        hb = _rms_mod(x, nmix_ref[...], sh1, sc1).astype(BF16)
        z = _dot(hb, w_in_ref[...])
        yield
        ug_s[slot_a] = _gelu_tanh(z[:, 0:A_WIDTH])
        yield
        zv = _gelu_tanh(z[:, A_WIDTH:2 * A_WIDTH])
        vgn = [_layer_norm(zv[:, g * A_GROUP_DIM:(g + 1) * A_GROUP_DIM]) for g in range(A_GROUPS)]
        vg_s[slot_a] = (jnp.concatenate(vgn, axis=-1) * vgain_ref[...]).astype(BF16)
        yield
        cos2, sin2 = cos_ref[...], sin_ref[...]
        heads = lambda zz: jnp.concatenate(
            [_rope(zz[:, h * HEAD_DIM:(h + 1) * HEAD_DIM], cos2, sin2) for h in range(HEADS)], -1)
        q_s[slot_a] = heads(z[:, 2 * A_WIDTH:K_COL]).astype(BF16)
        yield
        k_s[slot_a] = heads(z[:, K_COL:V_COL] * (HEAD_DIM ** -0.5))
        v_s[slot_a] = z[:, V_COL:V_COL + V_W].astype(BF16)
        yield
        gf_s[slot_a] = _silu(z[:, V_COL + V_W:V_COL + 2 * V_W])
        yield
        gb_s[slot_a] = _silu(z[:, V_COL + 2 * V_W:V_COL + 3 * V_W])

    IN_LIGHT = 6

    def stage_mix(final_states):
        s = slot_b
        rows = [slice(c * CHUNK, (c + 1) * CHUNK) for c in range(UNIT_CHUNKS)]
        head_cols = [slice(hd * HEAD_DIM, (hd + 1) * HEAD_DIM) for hd in range(HEADS)]
        table = lambda hd, name: tab_s[hd, TABLE_NAMES.index(name)]

        sp2 = []
        for g in range(A_GROUPS):
            lo = g * A_GROUP_DIM
            vg2 = jnp.concatenate([vg_s[s, r, lo:lo + A_GROUP_DIM] for r in rows], axis=-1)
            sp2.append(_dot(w_s_ref[g], vg2))
        yield
        scores, kv_f, kv_b = [], [], []
        for hd in range(HEADS):
            kc = [k_s[s, r, head_cols[hd]] for r in rows]
            vc = [v_s[s, r, head_cols[hd]] for r in rows]
            scores.append([_dot_nt(q_s[s, r, head_cols[hd]], kc[c].astype(BF16)) for c, r in enumerate(rows)])
            kv_f.append([_dot((kc[c] * table(hd, "kd_f")).T.astype(BF16), vc[c]) for c in range(UNIT_CHUNKS)])
            kv_b.append([_dot((kc[c] * table(hd, "kd_b")).T.astype(BF16), vc[c]) for c in range(UNIT_CHUNKS)])
            yield
        for g in range(A_GROUPS):
            lo = g * A_GROUP_DIM
            for c, r in enumerate(rows):
                gate = sp2[g][:, c * CHUNK:(c + 1) * CHUNK] + b_s_ref[g]
                cat_s[s, r, lo:lo + A_GROUP_DIM] = (ug_s[s, r, lo:lo + A_GROUP_DIM] * gate).astype(BF16)
            if g % 2 == 1:
                yield
        y_raw = []
        for hd in range(HEADS):
            enter_f, st = [], sp_ref[0, 0, hd]
            for c in range(UNIT_CHUNKS):
                enter_f.append(st)
                st = table(hd, "cd_f") * st + kv_f[hd][c]
            final_f = st
            enter_b, st = [None] * UNIT_CHUNKS, sp_ref[0, 1, hd]
            for c in reversed(range(UNIT_CHUNKS)):
                enter_b[c] = st
                st = table(hd, "cd_b") * st + kv_b[hd][c]
            final_states.append((final_f, st))
            per_chunk = []
            for c, r in enumerate(rows):
                vc = v_s[s, r, head_cols[hd]]
                qf = q_s[s, r, head_cols[hd]].astype(F32)

                def direction(decay, qd, enter):
                    p = (scores[hd][c] * table(hd, decay)).astype(BF16)
                    lhs = jnp.concatenate([p, (qf * table(hd, qd)).astype(BF16)], axis=-1)
                    return _dot(lhs, jnp.concatenate([vc, enter.astype(BF16)], axis=0))

                per_chunk.append((direction("decay_f", "qd_f", enter_f[c]), direction("decay_b", "qd_b", enter_b[c])))
            y_raw.append(per_chunk)
            yield
        for hd in range(HEADS):
            for c, r in enumerate(rows):
                y_f, y_b = y_raw[hd][c]
                cols = slice(A_WIDTH + hd * HEAD_DIM, A_WIDTH + (hd + 1) * HEAD_DIM)
                cat_s[s, r, cols] = (gf_s[s, r, head_cols[hd]] * _layer_norm(y_f)
                                     + gb_s[s, r, head_cols[hd]] * _layer_norm(y_b)).astype(BF16)
            yield

    MIX_LIGHT = 1 + HEADS + A_GROUPS // 2 + 2 * HEADS

    def emit_states(final_states):
        @pl.when(step - 1 < n_short)
        def _():
            for hd, (final_f, final_b) in enumerate(final_states):
                st_ref[0, 0, 0, hd] = final_f
                st_ref[0, 0, 1, hd] = final_b

    def stage_mlp():
        _, _, g1, sh2, sc2, g2 = _split_mod(modc_ref[0])
        x = _pick_block(step - 2 < n_short, xc_short, xc_long) + g1 * _dot(cat_s[slot_a], w_out_ref[...])
        hb = _rms_mod(x, nmlp_ref[...], sh2, sc2).astype(BF16)
        yield
        acc = yield from _mlp_pieces(hb, w1_ref, w2_ref)
        y_ref[...] = x + g2 * acc

    @pl.when(step == 0)
    def _():
        _run(stage_in())

    @pl.when(step == 1)
    def _():
        a, finals = stage_in(), []
        _weave(_pieces(a, 1), [a, stage_mix(finals)], IN_LIGHT + MIX_LIGHT)
        emit_states(finals)

    @pl.when((step >= 2) & (step < n_units))
    def _():
        a, c, finals = stage_in(), stage_mlp(), []
        heavy = _pieces(c, 1) + _pieces(a, 1) + _pieces(c, MLP_DOT_PIECES + 1)
        _weave(heavy, [a, stage_mix(finals)], IN_LIGHT + MIX_LIGHT)
        emit_states(finals)

    @pl.when(step == n_units)
    def _():
        finals = []
        _weave(_pieces(stage_mlp(), MLP_DOT_PIECES + 2), [stage_mix(finals)], MIX_LIGHT)
        emit_states(finals)

    @pl.when(step == n_units + 1)
    def _():
        _run(stage_mlp())


_HBM = pl.BlockSpec(memory_space=pl.ANY)
_WEIGHT_STAGING = [pltpu.VMEM((STAGE_SLOTS, STAGE_ROWS, STAGE_COLS), F32), pltpu.SemaphoreType.DMA((STAGE_SLOTS,))]


def _even_call(x_short, x_long, mod, sp, rope, params, weights, units, *, layer, even_index):
    n_units = units.total
    unit_a = lambda i: units.clamp(i)
    unit_b = lambda i: units.clamp(i - 1)
    unit_c = lambda i: units.clamp(i - 2)
    state_block = (1, 1, 2, HEADS, HEAD_DIM, HEAD_DIM)

    in_specs = [pl.BlockSpec((UNIT, D_MODEL), lambda i: (units.short_block(unit_a(i)), 0)),
                pl.BlockSpec((UNIT, D_MODEL), lambda i: (units.long_block(unit_a(i)), 0)),
                pl.BlockSpec((UNIT, D_MODEL), lambda i: (units.short_block(unit_c(i)), 0)),
                pl.BlockSpec((UNIT, D_MODEL), lambda i: (units.long_block(unit_c(i)), 0)),
                pl.BlockSpec((1, 1, N_MOD), lambda i: (units.mod_row(unit_a(i)), 0, 0)),
                pl.BlockSpec((1, 1, N_MOD), lambda i: (units.mod_row(unit_c(i)), 0, 0)),
                pl.BlockSpec((1, 2, HEADS, HEAD_DIM, HEAD_DIM), lambda i: (units.state_block(unit_b(i)), 0, 0, 0, 0)),
                pl.BlockSpec((UNIT, HEAD_DIM), lambda i: (units.rope_block(unit_a(i)), 0)),
                pl.BlockSpec((UNIT, HEAD_DIM), lambda i: (units.rope_block(unit_a(i)), 0))]
    in_specs += [_resident(p.shape) for p in params] + [_HBM] * len(weights)
    scratch = [
        pltpu.VMEM((2, UNIT, A_WIDTH), F32),
        pltpu.VMEM((2, UNIT, A_WIDTH), BF16),
        pltpu.VMEM((2, UNIT, QK_W), BF16),
        pltpu.VMEM((2, UNIT, QK_W), F32),
        pltpu.VMEM((2, UNIT, V_W), BF16),
        pltpu.VMEM((2, UNIT, V_W), F32),
        pltpu.VMEM((2, UNIT, V_W), F32),
        pltpu.VMEM((2, UNIT, OUT_WIDTH), BF16),
        pltpu.VMEM((HEADS, len(TABLE_NAMES), CHUNK, CHUNK), F32),
    ]
    scratch += [pltpu.VMEM(w.shape[1:], BF16) for w in weights] + _WEIGHT_STAGING
    return pl.pallas_call(
        functools.partial(_even_kernel, units=units, layer=layer, even_index=even_index),
        grid=(n_units + 2,),
        in_specs=in_specs,
        out_specs=[pl.BlockSpec((UNIT, D_MODEL), lambda i: (unit_c(i), 0)),
                   pl.BlockSpec(state_block, lambda i: (units.short_block(unit_b(i)), 0, 0, 0, 0, 0))],
        out_shape=[jax.ShapeDtypeStruct((n_units * UNIT, D_MODEL), F32),
                   jax.ShapeDtypeStruct((units.n_short,) + state_block[1:], F32)],
        scratch_shapes=scratch,
        compiler_params=pltpu.CompilerParams(
            dimension_semantics=("arbitrary",),
            vmem_limit_bytes=V7X_VMEM_LIMIT_BYTES),
        name="even_layer",
    )(x_short, x_long, x_short, x_long, mod, mod, sp, rope[0], rope[1], *params, *weights)


def _odd_kernel(xp_ref, x_ref, xn_ref, moda_ref, modc_ref, nmix_ref, nmlp_ref, fin_ref, pscale_ref,
                w_pool_hbm, w1_hbm, w2_hbm, y_short, y_long, x1_s, hb_s, w_pool_ref, w1_ref, w2_ref, stage, sem,
                *, n_units, n_short, short_len, long_len, layer, odd_index):
    step = pl.program_id(0)
    slot_a = lax.rem(step, 2)
    slot_b = 1 - slot_a
    rows_w = UNIT + 2 * POOL_HALO

    @pl.when(step == 0)
    def _():
        _load_weights_bf16(_weight_chunks(w_pool_hbm.at[odd_index], w_pool_ref)
                           + _weight_chunks(w1_hbm.at[layer], w1_ref)
                           + _weight_chunks(w2_hbm.at[layer], w2_ref), stage, sem)

    def stage_pool():
        sh1, sc1, g1, sh2, sc2, _ = _split_mod(moda_ref[0])
        xw = jnp.concatenate([xp_ref[...], x_ref[...], xn_ref[...]], axis=0)
        h = _rms_mod(xw, nmix_ref[...], sh1, sc1)
        seq_len = jnp.where(step < n_short, short_len, long_len)
        first = jnp.minimum(step, n_units - 1) * UNIT - POOL_HALO
        t = (lax.broadcasted_iota(jnp.int32, (rows_w, 1), 0) + first) & (seq_len - 1)
        yield

        def shift_down(a, k):
            return jnp.where(t >= k, pltpu.roll(a, k, axis=0), 0.0)

        def shift_up(a, k):
            return jnp.where(t < seq_len - k, pltpu.roll(a, rows_w - k, axis=0), 0.0)

        outs = []
        for g, w in enumerate(POOL_WINDOWS):
            hg = h[:, g * POOL_DIM:(g + 1) * POOL_DIM]
            half = w // 2
            left = shift_down(hg, 1)
            right = hg
            span = 1
            while span < half:
                left = left + shift_down(left, span)
                right = right + shift_up(right, span)
                span *= 2
            cnt = (jnp.minimum(t + half, seq_len) - jnp.maximum(t - half, 0)).astype(F32)
            pooled = ((left + right) / cnt - hg)[POOL_HALO:POOL_HALO + UNIT].astype(BF16)
            outs.append(_dot(pooled, w_pool_ref[g * POOL_DIM:(g + 1) * POOL_DIM, :]))
            yield
        x1 = x_ref[...] + g1 * (jnp.concatenate(outs, axis=-1) * pscale_ref[...])
        x1_s[slot_a] = x1
        hb_s[slot_a] = _rms_mod(x1, nmlp_ref[...], sh2, sc2).astype(BF16)

    def stage_mlp(result):
        g2 = _split_mod(modc_ref[0])[5]
        acc = yield from _mlp_pieces(hb_s[slot_b], w1_ref, w2_ref)
        result.append(_rms(x1_s[slot_b] + g2 * acc) * fin_ref[...])

    def store(result):
        @pl.when(step - 1 < n_short)
        def _():
            y_short[...] = result[0]

        @pl.when(step - 1 >= n_short)
        def _():
            y_long[...] = result[0]

    pool_light = 2 + len(POOL_WINDOWS)

    @pl.when(step == 0)
    def _():
        _run(stage_pool())

    @pl.when((step >= 1) & (step < n_units))
    def _():
        result = []
        _weave(_pieces(stage_mlp(result), MLP_DOT_PIECES + 1), [stage_pool()], pool_light)
        store(result)

    @pl.when(step == n_units)
    def _():
        result = []
        _run(stage_mlp(result))
        store(result)


def _odd_call(x2, mod, params, weights, units, *, short_len, long_len, layer, odd_index):
    n_units = units.total
    halo_per_unit = UNIT // POOL_HALO
    last_halo = x2.shape[0] // POOL_HALO - 1
    unit_a = lambda i: units.clamp(i)
    unit_c = lambda i: units.clamp(i - 1)
    assert (units.n_short * UNIT) % long_len == 0
    in_specs = [pl.BlockSpec((POOL_HALO, D_MODEL), lambda i: (jnp.maximum(unit_a(i) * halo_per_unit - 1, 0), 0)),
                pl.BlockSpec((UNIT, D_MODEL), lambda i: (unit_a(i), 0)),
                pl.BlockSpec((POOL_HALO, D_MODEL),
                             lambda i: (jnp.minimum((unit_a(i) + 1) * halo_per_unit, last_halo), 0)),
                pl.BlockSpec((1, 1, N_MOD), lambda i: (units.mod_row(unit_a(i)), 0, 0)),
                pl.BlockSpec((1, 1, N_MOD), lambda i: (units.mod_row(unit_c(i)), 0, 0))]
    in_specs += [_resident(p.shape) for p in params] + [_HBM] * len(weights)
    scratch = [pltpu.VMEM((2, UNIT, D_MODEL), F32),
               pltpu.VMEM((2, UNIT, D_MODEL), BF16)]
    scratch += [pltpu.VMEM(w.shape[1:], BF16) for w in weights] + _WEIGHT_STAGING
    return pl.pallas_call(
        functools.partial(_odd_kernel, n_units=n_units, n_short=units.n_short,
                          short_len=short_len, long_len=long_len, layer=layer, odd_index=odd_index),
        grid=(n_units + 1,),
        in_specs=in_specs,
        out_specs=[pl.BlockSpec((UNIT, D_MODEL), lambda i: (units.short_block(unit_c(i)), 0)),
                   pl.BlockSpec((UNIT, D_MODEL), lambda i: (units.long_block(unit_c(i)), 0))],
        out_shape=[jax.ShapeDtypeStruct((units.n_short * UNIT, D_MODEL), F32),
                   jax.ShapeDtypeStruct((units.n_long * UNIT, D_MODEL), F32)],
        scratch_shapes=scratch,
        compiler_params=pltpu.CompilerParams(
            dimension_semantics=("arbitrary",),
            vmem_limit_bytes=V7X_VMEM_LIMIT_BYTES),
        name="odd_layer",
    )(x2, x2, x2, mod, mod, *params, *weights)


def _rope_tables(n_tokens):
    rows = n_tokens // GRID_W
    pos_r = jnp.repeat(jnp.arange(rows, dtype=F32), GRID_W)
    pos_c = jnp.tile(jnp.arange(GRID_W, dtype=F32), rows)
    inv = ROPE_BASE ** (-jnp.arange(ROPE_PAIRS_AXIS, dtype=F32) / ROPE_PAIRS_AXIS)
    ang = jnp.concatenate([pos_r[:, None] * inv, pos_c[:, None] * inv], axis=-1)
    cos, sin = jnp.cos(ang), jnp.sin(ang)
    return jnp.concatenate([cos, cos], axis=-1), jnp.concatenate([-sin, sin], axis=-1)


def kernel(x_prompt, x_sample, state_ret, c, c_ctx, ada_w, ada_b, norm_mix, norm_mlp, mlp_w1, mlp_w2,
           ev_w_in, ev_w_s, ev_b_s, ev_vnorm, ev_decay, ev_w_out, od_w_pool, od_pool_scale, final_norm):
    n_prompt, prompt_len, _ = x_prompt.shape
    n_sample, sample_len, _ = x_sample.shape
    assert DEPTH == 2 and n_sample + 1 <= COND_ROWS
    assert prompt_len == UNIT and sample_len % UNIT == 0
    units = _Units(n_prompt, n_sample, sample_len // UNIT)

    cond = jnp.zeros((COND_ROWS, D_MODEL), F32).at[0].set(c_ctx).at[1:1 + n_sample].set(c)
    mod = _ada_call(cond, ada_w, ada_b)[:, :1 + n_sample].reshape(DEPTH, 1 + n_sample, 1, N_MOD)

    row = lambda v: v.reshape(1, -1)
    dec = jnp.broadcast_to(ev_decay[0].reshape(2 * HEADS, 1), (2 * HEADS, CHUNK))
    even_params = (
        row(norm_mix[0]), row(norm_mlp[0]),
        ev_w_s[0].astype(BF16),
        jnp.broadcast_to(ev_b_s[0][:, :, None], (A_GROUPS, CHUNK, A_GROUP_DIM)),
        row(ev_vnorm[0]), dec,
    )
    odd_params = (row(norm_mix[1]), row(norm_mlp[1]), row(final_norm), row(od_pool_scale[0]))
    even_weights = (ev_w_in, ev_w_out, mlp_w1, mlp_w2)
    odd_weights = (od_w_pool.reshape(od_w_pool.shape[0], D_MODEL, POOL_DIM), mlp_w1, mlp_w2)

    xp2 = x_prompt.reshape(n_prompt * prompt_len, D_MODEL)
    xs2 = x_sample.reshape(n_sample * sample_len, D_MODEL)
    cos2, sin2 = _rope_tables(sample_len)
    sp = _ret_state_call(xs2, mod[0, 1:], state_ret, (cos2, sin2), row(norm_mix[0]), ev_w_in, dec,
                         n_seq=n_sample, seq_len=sample_len, even_index=0)
    rope = (jnp.concatenate([jnp.ones((UNIT, HEAD_DIM), F32), cos2]),
            jnp.concatenate([jnp.zeros((UNIT, HEAD_DIM), F32), sin2]))
    x_all, new_state = _even_call(xp2, xs2, mod[0], sp, rope, even_params, even_weights, units,
                                  layer=0, even_index=0)
    y_prompt, y_sample = _odd_call(x_all, mod[1], odd_params, odd_weights, units,
                                   short_len=prompt_len, long_len=sample_len, layer=1, odd_index=0)

    return (y_prompt.reshape(n_prompt, prompt_len, D_MODEL),
            y_sample.reshape(n_sample, sample_len, D_MODEL),
            new_state)
```

```python
import functools

import jax
import jax.numpy as jnp
from jax import lax
from jax.experimental import pallas as pl
from jax.experimental.pallas import tpu as pltpu

D_MODEL = 1024
DEPTH = 2
GRID_W = 64
A_WIDTH = D_MODEL // 2
A_GROUPS = 4
A_GROUP_DIM = A_WIDTH // A_GROUPS
CHUNK = 128
HEADS = 4
HEAD_DIM = (D_MODEL // 2) // HEADS
ROPE_BASE = 10000.0
ROPE_PAIRS_AXIS = HEAD_DIM // 4
POOL_WINDOWS = (2, 4, 8, 16)
POOL_DIM = D_MODEL // len(POOL_WINDOWS)
POOL_HALO = max(POOL_WINDOWS) // 2
D_FF = 4 * D_MODEL
EPS = 1e-6
QK_W = HEADS * HEAD_DIM
V_W = HEADS * HEAD_DIM
IN_WIDTH = 2 * A_WIDTH + 2 * QK_W + 3 * V_W
OUT_WIDTH = A_WIDTH + V_W
N_MOD = 6 * D_MODEL
K_COL = 2 * A_WIDTH + QK_W
V_COL = K_COL + QK_W

UNIT = 2 * CHUNK
UNIT_CHUNKS = UNIT // CHUNK
ODD_UNIT = 2 * UNIT
FF_CHUNK = 1024
ADA_BLOCK_N = 1536
COND_ROWS = 8
STAGE_ROWS = 512
STAGE_COLS = 1024
STAGE_SLOTS = 3
V7X_VMEM_LIMIT_BYTES = 60 * 1024 * 1024

F32 = jnp.float32
BF16 = jnp.bfloat16


def _dot(a, b):
    return jnp.dot(a, b, preferred_element_type=F32)


def _dot_nt(a, b):
    return lax.dot_general(a, b, (((1,), (1,)), ((), ())), preferred_element_type=F32)


def _silu(x):
    return x * jax.nn.sigmoid(x)


def _gelu_tanh(x):
    return 0.5 * x * (1.0 + jnp.tanh(0.7978845608028654 * (x + 0.044715 * (x * x * x))))


def _rms(x):
    return x * lax.rsqrt(jnp.mean(x * x, axis=-1, keepdims=True) + EPS)


def _rms_mod(x, gain, shift, scale):
    return (_rms(x) * gain) * (1.0 + scale) + shift


def _layer_norm(x):
    mu = jnp.mean(x, axis=-1, keepdims=True)
    d = x - mu
    return d * lax.rsqrt(jnp.mean(d * d, axis=-1, keepdims=True) + EPS)


def _split_mod(mod):
    return [mod[:, i * D_MODEL:(i + 1) * D_MODEL] for i in range(6)]


def _rope(x, cos2, sin2):
    return x * cos2 + pltpu.roll(x, HEAD_DIM // 2, axis=1) * sin2


def _pick_block(take_first, a_ref, b_ref):
    rows = lax.broadcasted_iota(jnp.int32, (a_ref.shape[0], 1), 0)
    return jnp.where(rows < jnp.where(take_first, a_ref.shape[0], 0), a_ref[...], b_ref[...])


def _run(gen, n=None):
    done = 0
    while n is None or done < n:
        try:
            next(gen)
        except StopIteration:
            break
        done += 1
    return done


def _weave(heavy, lights, n_light):
    heavy = list(heavy)
    pending = _roundrobin(lights)
    ran = 0
    for k, piece in enumerate(heavy):
        piece()
        want = ((k + 1) * n_light + len(heavy) - 1) // len(heavy)
        ran += _run(pending, want - ran)
    _run(pending)


def _pieces(gen, n):
    return [functools.partial(next, gen, None) for _ in range(n)]


def _roundrobin(gens):
    live = list(gens)
    while live:
        for g in list(live):
            try:
                next(g)
            except StopIteration:
                live.remove(g)
                continue
            yield


MLP_DOT_PIECES = 2 * (D_FF // FF_CHUNK)


def _mlp_pieces(hb, w1_ref, w2_ref):
    n_ff = D_FF // FF_CHUNK
    up = lambda j: _dot(hb, w1_ref[:, j * FF_CHUNK:(j + 1) * FF_CHUNK])
    a = up(0)
    yield
    acc = None
    for j in range(n_ff):
        a_next = None
        if j + 1 < n_ff:
            a_next = up(j + 1)
            yield
        p = _dot(jnp.square(jnp.maximum(a, 0.0)).astype(BF16), w2_ref[j * FF_CHUNK:(j + 1) * FF_CHUNK, :])
        acc = p if acc is None else acc + p
        a = a_next
        yield
    return acc


def _resident(shape):
    return pl.BlockSpec(shape, lambda i: (0,) * len(shape), pipeline_mode=pl.Buffered(1))


def _weight_chunks(w_hbm, w_vmem):
    rows, cols = w_hbm.shape
    jobs = []
    for r0 in range(0, rows, STAGE_ROWS):
        for c0 in range(0, cols, STAGE_COLS):
            window = (pl.ds(r0, min(STAGE_ROWS, rows - r0)), pl.ds(c0, min(STAGE_COLS, cols - c0)))
            jobs.append((w_hbm.at[window], w_vmem.at[window]))
    return jobs


def _load_weights_bf16(jobs, stage, sem):
    n_slots = stage.shape[0]

    def copy(k):
        r, c = jobs[k][0].shape
        slot = k % n_slots
        return pltpu.make_async_copy(jobs[k][0], stage.at[slot, pl.ds(0, r), pl.ds(0, c)], sem.at[slot])

    for k in range(min(n_slots - 1, len(jobs))):
        copy(k).start()
    for k, (src, dst) in enumerate(jobs):
        if k + n_slots - 1 < len(jobs):
            copy(k + n_slots - 1).start()
        copy(k).wait()
        r, c = src.shape
        dst[...] = stage[k % n_slots, 0:r, 0:c].astype(BF16)


class _Units:
    def __init__(self, n_short, n_long, units_per_long):
        self.n_short, self.n_long, self.units_per_long = n_short, n_long * units_per_long, units_per_long
        self.total = self.n_short + self.n_long

    def clamp(self, unit):
        return jnp.clip(unit, 0, self.total - 1)

    def short_block(self, unit):
        return jnp.minimum(unit, self.n_short - 1)

    def long_block(self, unit):
        return jnp.clip(unit - self.n_short, 0, self.n_long - 1)

    def mod_row(self, unit):
        return jnp.where(unit < self.n_short, 0, 1 + (unit - self.n_short) // self.units_per_long)

    def rope_block(self, unit):
        return jnp.where(unit < self.n_short, 0, 1 + (unit - self.n_short) % self.units_per_long)

    def state_block(self, unit):
        return jnp.where(unit < self.n_short, 0, self.units_per_long + unit - self.n_short)


def _ada_kernel(cond_ref, w_ref, b_ref, out_ref):
    s = _silu(cond_ref[...]).astype(BF16)
    out_ref[0] = _dot(s, w_ref[0].astype(BF16)) + b_ref[0]


def _ada_call(cond, ada_w, ada_b):
    return pl.pallas_call(
        _ada_kernel,
        grid=(DEPTH, N_MOD // ADA_BLOCK_N),
        in_specs=[
            pl.BlockSpec((COND_ROWS, D_MODEL), lambda i, j: (0, 0)),
            pl.BlockSpec((1, D_MODEL, ADA_BLOCK_N), lambda i, j: (i, 0, j)),
            pl.BlockSpec((1, 1, ADA_BLOCK_N), lambda i, j: (i, 0, j)),
        ],
        out_specs=pl.BlockSpec((1, COND_ROWS, ADA_BLOCK_N), lambda i, j: (i, 0, j)),
        out_shape=jax.ShapeDtypeStruct((DEPTH, COND_ROWS, N_MOD), F32),
        compiler_params=pltpu.CompilerParams(
            dimension_semantics=("arbitrary", "arbitrary"),
            vmem_limit_bytes=V7X_VMEM_LIMIT_BYTES),
        name="ada_mod",
    )(cond, ada_w, ada_b.reshape(DEPTH, 1, N_MOD))


def _decay_tables(dec_ref, hd):
    row = lax.broadcasted_iota(jnp.int32, (CHUNK, CHUNK), 0).astype(F32)
    col = lax.broadcasted_iota(jnp.int32, (CHUNK, CHUNK), 1).astype(F32)
    lg_f = jnp.broadcast_to(-jnp.exp(dec_ref[hd:hd + 1, :]), (CHUNK, CHUNK))
    lg_b = jnp.broadcast_to(-jnp.exp(dec_ref[HEADS + hd:HEADS + hd + 1, :]), (CHUNK, CHUNK))
    d_f = row - col
    d_b = col - row
    return dict(
        decay_f=jnp.where(d_f >= 0, jnp.exp(lg_f * jnp.maximum(d_f, 0.0)), 0.0),
        decay_b=jnp.where(d_b >= 0, jnp.exp(lg_b * jnp.maximum(d_b, 0.0)), 0.0),
        qd_f=jnp.exp(lg_f * (row + 1.0)),
        qd_b=jnp.exp(lg_b * (CHUNK - row)),
        kd_f=jnp.exp(lg_f * (CHUNK - 1.0 - row)),
        kd_b=jnp.exp(lg_b * row),
        cd_f=jnp.exp(lg_f * CHUNK),
        cd_b=jnp.exp(lg_b * CHUNK),
    )


TABLE_NAMES = ("decay_f", "decay_b", "qd_f", "qd_b", "kd_f", "kd_b", "cd_f", "cd_b")


def _ret_state_kernel(x_ref, mod_ref, s0_ref, cos_ref, sin_ref, nmix_ref, wk_ref, wv_ref, dec_ref, sp_ref,
                      *, n_chunk):
    step = pl.program_id(0)

    @pl.when(step == 0)
    def _():
        sp_ref[...] = jnp.zeros_like(sp_ref)

    @pl.when(step > 0)
    def _():
        sh1, sc1 = _split_mod(mod_ref[0])[:2]
        hb = _rms_mod(x_ref[...], nmix_ref[...], sh1, sc1).astype(BF16)
        zk = _dot(hb, wk_ref[...].astype(BF16)) * (HEAD_DIM ** -0.5)
        zv = _dot(hb, wv_ref[...].astype(BF16)).astype(BF16)
        cos2, sin2 = cos_ref[...], sin_ref[...]
        for hd in range(HEADS):
            lo = hd * HEAD_DIM
            tb = _decay_tables(dec_ref, hd)
            kh = _rope(zk[:, lo:lo + HEAD_DIM], cos2, sin2)
            kv_f, kv_b = [], []
            for c in range(n_chunk):
                kc = kh[c * CHUNK:(c + 1) * CHUNK]
                vc = zv[c * CHUNK:(c + 1) * CHUNK, lo:lo + HEAD_DIM]
                kv_f.append(_dot((kc * tb["kd_f"]).T.astype(BF16), vc))
                kv_b.append(_dot((kc * tb["kd_b"]).T.astype(BF16), vc))
            st = s0_ref[0, 0, 0, hd]
            for c in range(n_chunk):
                if c % UNIT_CHUNKS == 0:
                    sp_ref[c // UNIT_CHUNKS, 0, hd] = st
                st = tb["cd_f"] * st + kv_f[c]
            st = s0_ref[0, 0, 1, hd]
            for c in reversed(range(n_chunk)):
                if c % UNIT_CHUNKS == UNIT_CHUNKS - 1:
                    sp_ref[c // UNIT_CHUNKS, 1, hd] = st
                st = tb["cd_b"] * st + kv_b[c]


def _ret_state_call(x2, mod, s0, rope, nmix, w_in, dec, *, n_seq, seq_len, even_index):
    n_chunk = seq_len // CHUNK
    units = seq_len // UNIT
    seq = lambda i: jnp.maximum(i - 1, 0)
    return pl.pallas_call(
        functools.partial(_ret_state_kernel, n_chunk=n_chunk),
        grid=(n_seq + 1,),
        in_specs=[
            pl.BlockSpec((seq_len, D_MODEL), lambda i: (seq(i), 0)),
            pl.BlockSpec((1, 1, N_MOD), lambda i: (seq(i), 0, 0)),
            pl.BlockSpec((1, 1, 2, HEADS, HEAD_DIM, HEAD_DIM), lambda i: (seq(i), 0, 0, 0, 0, 0)),
            _resident((seq_len, HEAD_DIM)), _resident((seq_len, HEAD_DIM)),
            _resident((1, D_MODEL)),
            pl.BlockSpec((None, D_MODEL, QK_W), lambda i: (even_index, 0, K_COL // QK_W),
                         pipeline_mode=pl.Buffered(1)),
            pl.BlockSpec((None, D_MODEL, V_W), lambda i: (even_index, 0, V_COL // V_W),
                         pipeline_mode=pl.Buffered(1)),
            _resident(dec.shape),
        ],
        out_specs=pl.BlockSpec((units, 2, HEADS, HEAD_DIM, HEAD_DIM), lambda i: (i, 0, 0, 0, 0)),
        out_shape=jax.ShapeDtypeStruct(((n_seq + 1) * units, 2, HEADS, HEAD_DIM, HEAD_DIM), F32),
        compiler_params=pltpu.CompilerParams(
            dimension_semantics=("arbitrary",),
            vmem_limit_bytes=V7X_VMEM_LIMIT_BYTES),
        name="ret_state",
    )(x2, mod, s0, rope[0], rope[1], nmix, w_in, w_in, dec)


def _even_kernel(xa_short, xa_long, xc_short, xc_long, moda_ref, modc_ref, sp_ref, cos_ref, sin_ref,
                 nmix_ref, nmlp_ref, w_s_ref, b_s_ref, vgain_ref, dec_ref, w_in_hbm, w_out_hbm, w1_hbm, w2_hbm,
                 y_ref, st_ref, ug_s, vg_s, q_s, k_s, v_s, gf_s, gb_s, cat_s, tab_s,
                 w_in_ref, w_out_ref, w1_ref, w2_ref, stage, sem, *, units, layer, even_index):
    n_units, n_short = units.total, units.n_short
    step = pl.program_id(0)
    slot_a = lax.rem(step, 2)
    slot_b = 1 - slot_a

    @pl.when(step == 0)
    def _():
        _load_weights_bf16(_weight_chunks(w_in_hbm.at[even_index], w_in_ref)
                           + _weight_chunks(w_out_hbm.at[even_index], w_out_ref)
                           + _weight_chunks(w1_hbm.at[layer], w1_ref)
                           + _weight_chunks(w2_hbm.at[layer], w2_ref), stage, sem)
        for hd in range(HEADS):
            tb = _decay_tables(dec_ref, hd)
            for n, name in enumerate(TABLE_NAMES):
                tab_s[hd, n] = tb[name]

    def stage_in():
        sh1, sc1 = _split_mod(moda_ref[0])[:2]
        x = _pick_block(step < n_short, xa_short, xa_long)
        hb = _rms_mod(x, nmix_ref[...], sh1, sc1).astype(BF16)
        z = _dot(hb, w_in_ref[...])
        yield
        ug_s[slot_a] = _gelu_tanh(z[:, 0:A_WIDTH])
        yield
        zv = _gelu_tanh(z[:, A_WIDTH:2 * A_WIDTH])
        vgn = [_layer_norm(zv[:, g * A_GROUP_DIM:(g + 1) * A_GROUP_DIM]) for g in range(A_GROUPS)]
        vg_s[slot_a] = (jnp.concatenate(vgn, axis=-1) * vgain_ref[...]).astype(BF16)
        yield
        cos2, sin2 = cos_ref[...], sin_ref[...]
        heads = lambda zz: jnp.concatenate(
            [_rope(zz[:, h * HEAD_DIM:(h + 1) * HEAD_DIM], cos2, sin2) for h in range(HEADS)], -1)
        q_s[slot_a] = heads(z[:, 2 * A_WIDTH:K_COL]).astype(BF16)
        yield
        k_s[slot_a] = heads(z[:, K_COL:V_COL] * (HEAD_DIM ** -0.5))
        v_s[slot_a] = z[:, V_COL:V_COL + V_W].astype(BF16)
        yield
        gf_s[slot_a] = _silu(z[:, V_COL + V_W:V_COL + 2 * V_W])
        yield
        gb_s[slot_a] = _silu(z[:, V_COL + 2 * V_W:V_COL + 3 * V_W])

    IN_LIGHT = 6

    def stage_mix(final_states):
        s = slot_b
        rows = [slice(c * CHUNK, (c + 1) * CHUNK) for c in range(UNIT_CHUNKS)]
        head_cols = [slice(hd * HEAD_DIM, (hd + 1) * HEAD_DIM) for hd in range(HEADS)]
        table = lambda hd, name: tab_s[hd, TABLE_NAMES.index(name)]

        sp2 = []
        for g in range(A_GROUPS):
            lo = g * A_GROUP_DIM
            vg2 = jnp.concatenate([vg_s[s, r, lo:lo + A_GROUP_DIM] for r in rows], axis=-1)
            sp2.append(_dot(w_s_ref[g], vg2))
        yield
        scores, kv_f, kv_b = [], [], []
        for hd in range(HEADS):
            kc = [k_s[s, r, head_cols[hd]] for r in rows]
            vc = [v_s[s, r, head_cols[hd]] for r in rows]
            scores.append([_dot_nt(q_s[s, r, head_cols[hd]], kc[c].astype(BF16)) for c, r in enumerate(rows)])
            kv_f.append([_dot((kc[c] * table(hd, "kd_f")).T.astype(BF16), vc[c]) for c in range(UNIT_CHUNKS)])
            kv_b.append([_dot((kc[c] * table(hd, "kd_b")).T.astype(BF16), vc[c]) for c in range(UNIT_CHUNKS)])
            yield
        for g in range(A_GROUPS):
            lo = g * A_GROUP_DIM
            for c, r in enumerate(rows):
                gate = sp2[g][:, c * CHUNK:(c + 1) * CHUNK] + b_s_ref[g]
                cat_s[s, r, lo:lo + A_GROUP_DIM] = (ug_s[s, r, lo:lo + A_GROUP_DIM] * gate).astype(BF16)
            if g % 2 == 1:
                yield
        y_raw = []
        for hd in range(HEADS):
            enter_f, st = [], sp_ref[0, 0, hd]
            for c in range(UNIT_CHUNKS):
                enter_f.append(st)
                st = table(hd, "cd_f") * st + kv_f[hd][c]
            final_f = st
            enter_b, st = [None] * UNIT_CHUNKS, sp_ref[0, 1, hd]
            for c in reversed(range(UNIT_CHUNKS)):
                enter_b[c] = st
                st = table(hd, "cd_b") * st + kv_b[hd][c]
            final_states.append((final_f, st))
            per_chunk = []
            for c, r in enumerate(rows):
                vc = v_s[s, r, head_cols[hd]]
                qf = q_s[s, r, head_cols[hd]].astype(F32)

                def direction(decay, qd, enter):
                    p = (scores[hd][c] * table(hd, decay)).astype(BF16)
                    lhs = jnp.concatenate([p, (qf * table(hd, qd)).astype(BF16)], axis=-1)
                    return _dot(lhs, jnp.concatenate([vc, enter.astype(BF16)], axis=0))

                per_chunk.append((direction("decay_f", "qd_f", enter_f[c]), direction("decay_b", "qd_b", enter_b[c])))
            y_raw.append(per_chunk)
            yield
        for hd in range(HEADS):
            for c, r in enumerate(rows):
                y_f, y_b = y_raw[hd][c]
                cols = slice(A_WIDTH + hd * HEAD_DIM, A_WIDTH + (hd + 1) * HEAD_DIM)
                cat_s[s, r, cols] = (gf_s[s, r, head_cols[hd]] * _layer_norm(y_f)
                                     + gb_s[s, r, head_cols[hd]] * _layer_norm(y_b)).astype(BF16)
            yield

    MIX_LIGHT = 1 + HEADS + A_GROUPS // 2 + 2 * HEADS

    def emit_states(final_states):
        @pl.when(step - 1 < n_short)
        def _():
            for hd, (final_f, final_b) in enumerate(final_states):
                st_ref[0, 0, 0, hd] = final_f
                st_ref[0, 0, 1, hd] = final_b

    def stage_mlp():
        _, _, g1, sh2, sc2, g2 = _split_mod(modc_ref[0])
        x = _pick_block(step - 2 < n_short, xc_short, xc_long) + g1 * _dot(cat_s[slot_a], w_out_ref[...])
        hb = _rms_mod(x, nmlp_ref[...], sh2, sc2).astype(BF16)
        yield
        acc = yield from _mlp_pieces(hb, w1_ref, w2_ref)
        y_ref[...] = x + g2 * acc

    @pl.when(step == 0)
    def _():
        _run(stage_in())

    @pl.when(step == 1)
    def _():
        a, finals = stage_in(), []
        _weave(_pieces(a, 1), [a, stage_mix(finals)], IN_LIGHT + MIX_LIGHT)
        emit_states(finals)

    @pl.when((step >= 2) & (step < n_units))
    def _():
        a, c, finals = stage_in(), stage_mlp(), []
        heavy = _pieces(c, 1) + _pieces(a, 1) + _pieces(c, MLP_DOT_PIECES + 1)
        _weave(heavy, [a, stage_mix(finals)], IN_LIGHT + MIX_LIGHT)
        emit_states(finals)

    @pl.when(step == n_units)
    def _():
        finals = []
        _weave(_pieces(stage_mlp(), MLP_DOT_PIECES + 2), [stage_mix(finals)], MIX_LIGHT)
        emit_states(finals)

    @pl.when(step == n_units + 1)
    def _():
        _run(stage_mlp())


_HBM = pl.BlockSpec(memory_space=pl.ANY)
_WEIGHT_STAGING = [pltpu.VMEM((STAGE_SLOTS, STAGE_ROWS, STAGE_COLS), F32), pltpu.SemaphoreType.DMA((STAGE_SLOTS,))]


def _even_call(x_short, x_long, mod, sp, rope, params, weights, units, *, layer, even_index):
    n_units = units.total
    unit_a = lambda i: units.clamp(i)
    unit_b = lambda i: units.clamp(i - 1)
    unit_c = lambda i: units.clamp(i - 2)
    state_block = (1, 1, 2, HEADS, HEAD_DIM, HEAD_DIM)

    in_specs = [pl.BlockSpec((UNIT, D_MODEL), lambda i: (units.short_block(unit_a(i)), 0)),
                pl.BlockSpec((UNIT, D_MODEL), lambda i: (units.long_block(unit_a(i)), 0)),
                pl.BlockSpec((UNIT, D_MODEL), lambda i: (units.short_block(unit_c(i)), 0)),
                pl.BlockSpec((UNIT, D_MODEL), lambda i: (units.long_block(unit_c(i)), 0)),
                pl.BlockSpec((1, 1, N_MOD), lambda i: (units.mod_row(unit_a(i)), 0, 0)),
                pl.BlockSpec((1, 1, N_MOD), lambda i: (units.mod_row(unit_c(i)), 0, 0)),
                pl.BlockSpec((1, 2, HEADS, HEAD_DIM, HEAD_DIM), lambda i: (units.state_block(unit_b(i)), 0, 0, 0, 0)),
                pl.BlockSpec((UNIT, HEAD_DIM), lambda i: (units.rope_block(unit_a(i)), 0)),
                pl.BlockSpec((UNIT, HEAD_DIM), lambda i: (units.rope_block(unit_a(i)), 0))]
    in_specs += [_resident(p.shape) for p in params] + [_HBM] * len(weights)
    scratch = [
        pltpu.VMEM((2, UNIT, A_WIDTH), F32),
        pltpu.VMEM((2, UNIT, A_WIDTH), BF16),
        pltpu.VMEM((2, UNIT, QK_W), BF16),
        pltpu.VMEM((2, UNIT, QK_W), F32),
        pltpu.VMEM((2, UNIT, V_W), BF16),
        pltpu.VMEM((2, UNIT, V_W), F32),
        pltpu.VMEM((2, UNIT, V_W), F32),
        pltpu.VMEM((2, UNIT, OUT_WIDTH), BF16),
        pltpu.VMEM((HEADS, len(TABLE_NAMES), CHUNK, CHUNK), F32),
    ]
    scratch += [pltpu.VMEM(w.shape[1:], BF16) for w in weights] + _WEIGHT_STAGING
    return pl.pallas_call(
        functools.partial(_even_kernel, units=units, layer=layer, even_index=even_index),
        grid=(n_units + 2,),
        in_specs=in_specs,
        out_specs=[pl.BlockSpec((UNIT, D_MODEL), lambda i: (unit_c(i), 0)),
                   pl.BlockSpec(state_block, lambda i: (units.short_block(unit_b(i)), 0, 0, 0, 0, 0))],
        out_shape=[jax.ShapeDtypeStruct((n_units * UNIT, D_MODEL), F32),
                   jax.ShapeDtypeStruct((units.n_short,) + state_block[1:], F32)],
        scratch_shapes=scratch,
        compiler_params=pltpu.CompilerParams(
            dimension_semantics=("arbitrary",),
            vmem_limit_bytes=V7X_VMEM_LIMIT_BYTES),
        name="even_layer",
    )(x_short, x_long, x_short, x_long, mod, mod, sp, rope[0], rope[1], *params, *weights)


def _odd_kernel(xp_ref, x_ref, xn_ref, moda_ref, modc_ref, nmix_ref, nmlp_ref, fin_ref, pscale_ref,
                w_pool_hbm, w1_hbm, w2_hbm, y_short, y_long, x1_s, hb_s, w_pool_ref, w1_ref, w2_ref, stage, sem,
                *, n_units, n_short, short_len, long_len, layer, odd_index):
    step = pl.program_id(0)
    slot_a = lax.rem(step, 2)
    slot_b = 1 - slot_a
    unit = x_ref.shape[0]
    rows_w = unit + 2 * POOL_HALO

    @pl.when(step == 0)
    def _():
        _load_weights_bf16(_weight_chunks(w_pool_hbm.at[odd_index], w_pool_ref)
                           + _weight_chunks(w1_hbm.at[layer], w1_ref)
                           + _weight_chunks(w2_hbm.at[layer], w2_ref), stage, sem)

    def stage_pool():
        sh1, sc1, g1, sh2, sc2, _ = _split_mod(moda_ref[0])
        xw = jnp.concatenate([xp_ref[...], x_ref[...], xn_ref[...]], axis=0)
        h = _rms_mod(xw, nmix_ref[...], sh1, sc1)
        seq_len = jnp.where(step < n_short, short_len, long_len)
        first = jnp.minimum(step, n_units - 1) * unit - POOL_HALO
        t = (lax.broadcasted_iota(jnp.int32, (rows_w, 1), 0) + first) & (seq_len - 1)
        yield

        def shift_down(a, k):
            return jnp.where(t >= k, pltpu.roll(a, k, axis=0), 0.0)

        def shift_up(a, k):
            return jnp.where(t < seq_len - k, pltpu.roll(a, rows_w - k, axis=0), 0.0)

        outs = []
        for g, w in enumerate(POOL_WINDOWS):
            hg = h[:, g * POOL_DIM:(g + 1) * POOL_DIM]
            half = w // 2
            left = shift_down(hg, 1)
            right = hg
            span = 1
            while span < half:
                left = left + shift_down(left, span)
                right = right + shift_up(right, span)
                span *= 2
            cnt = (jnp.minimum(t + half, seq_len) - jnp.maximum(t - half, 0)).astype(F32)
            pooled = ((left + right) / cnt - hg)[POOL_HALO:POOL_HALO + unit].astype(BF16)
            outs.append(_dot(pooled, w_pool_ref[g * POOL_DIM:(g + 1) * POOL_DIM, :]))
            yield
        x1 = x_ref[...] + g1 * (jnp.concatenate(outs, axis=-1) * pscale_ref[...])
        x1_s[slot_a] = x1
        hb_s[slot_a] = _rms_mod(x1, nmlp_ref[...], sh2, sc2).astype(BF16)

    def stage_mlp(result):
        g2 = _split_mod(modc_ref[0])[5]
        acc = yield from _mlp_pieces(hb_s[slot_b], w1_ref, w2_ref)
        result.append(_rms(x1_s[slot_b] + g2 * acc) * fin_ref[...])

    def store(result):
        @pl.when(step - 1 < n_short)
        def _():
            y_short[...] = result[0]

        @pl.when(step - 1 >= n_short)
        def _():
            y_long[...] = result[0]

    pool_light = 2 + len(POOL_WINDOWS)

    @pl.when(step == 0)
    def _():
        _run(stage_pool())

    @pl.when((step >= 1) & (step < n_units))
    def _():
        result = []
        _weave(_pieces(stage_mlp(result), MLP_DOT_PIECES + 1), [stage_pool()], pool_light)
        store(result)

    @pl.when(step == n_units)
    def _():
        result = []
        _run(stage_mlp(result))
        store(result)


def _odd_call(x2, mod, params, weights, units, *, unit, short_len, long_len, layer, odd_index):
    n_units = units.total
    halo_per_unit = unit // POOL_HALO
    last_halo = x2.shape[0] // POOL_HALO - 1
    unit_a = lambda i: units.clamp(i)
    unit_c = lambda i: units.clamp(i - 1)
    assert (units.n_short * unit) % long_len == 0
    in_specs = [pl.BlockSpec((POOL_HALO, D_MODEL), lambda i: (jnp.maximum(unit_a(i) * halo_per_unit - 1, 0), 0)),
                pl.BlockSpec((unit, D_MODEL), lambda i: (unit_a(i), 0)),
                pl.BlockSpec((POOL_HALO, D_MODEL),
                             lambda i: (jnp.minimum((unit_a(i) + 1) * halo_per_unit, last_halo), 0)),
                pl.BlockSpec((1, 1, N_MOD), lambda i: (units.mod_row(unit_a(i)), 0, 0)),
                pl.BlockSpec((1, 1, N_MOD), lambda i: (units.mod_row(unit_c(i)), 0, 0))]
    in_specs += [_resident(p.shape) for p in params] + [_HBM] * len(weights)
    scratch = [pltpu.VMEM((2, unit, D_MODEL), F32),
               pltpu.VMEM((2, unit, D_MODEL), BF16)]
    scratch += [pltpu.VMEM(w.shape[1:], BF16) for w in weights] + _WEIGHT_STAGING
    return pl.pallas_call(
        functools.partial(_odd_kernel, n_units=n_units, n_short=units.n_short,
                          short_len=short_len, long_len=long_len, layer=layer, odd_index=odd_index),
        grid=(n_units + 1,),
        in_specs=in_specs,
        out_specs=[pl.BlockSpec((unit, D_MODEL), lambda i: (units.short_block(unit_c(i)), 0)),
                   pl.BlockSpec((unit, D_MODEL), lambda i: (units.long_block(unit_c(i)), 0))],
        out_shape=[jax.ShapeDtypeStruct((units.n_short * unit, D_MODEL), F32),
                   jax.ShapeDtypeStruct((units.n_long * unit, D_MODEL), F32)],
        scratch_shapes=scratch,
        compiler_params=pltpu.CompilerParams(
            dimension_semantics=("arbitrary",),
            vmem_limit_bytes=V7X_VMEM_LIMIT_BYTES),
        name="odd_layer",
    )(x2, x2, x2, mod, mod, *params, *weights)


def _rope_tables(n_tokens):
    rows = n_tokens // GRID_W
    pos_r = jnp.repeat(jnp.arange(rows, dtype=F32), GRID_W)
    pos_c = jnp.tile(jnp.arange(GRID_W, dtype=F32), rows)
    inv = ROPE_BASE ** (-jnp.arange(ROPE_PAIRS_AXIS, dtype=F32) / ROPE_PAIRS_AXIS)
    ang = jnp.concatenate([pos_r[:, None] * inv, pos_c[:, None] * inv], axis=-1)
    cos, sin = jnp.cos(ang), jnp.sin(ang)
    return jnp.concatenate([cos, cos], axis=-1), jnp.concatenate([-sin, sin], axis=-1)


def kernel(x_prompt, x_sample, state_ret, c, c_ctx, ada_w, ada_b, norm_mix, norm_mlp, mlp_w1, mlp_w2,
           ev_w_in, ev_w_s, ev_b_s, ev_vnorm, ev_decay, ev_w_out, od_w_pool, od_pool_scale, final_norm):
    n_prompt, prompt_len, _ = x_prompt.shape
    n_sample, sample_len, _ = x_sample.shape
    assert DEPTH == 2 and n_sample + 1 <= COND_ROWS
    assert prompt_len == UNIT and sample_len % UNIT == 0
    units = _Units(n_prompt, n_sample, sample_len // UNIT)

    cond = jnp.zeros((COND_ROWS, D_MODEL), F32).at[0].set(c_ctx).at[1:1 + n_sample].set(c)
    mod = _ada_call(cond, ada_w, ada_b)[:, :1 + n_sample].reshape(DEPTH, 1 + n_sample, 1, N_MOD)

    row = lambda v: v.reshape(1, -1)
    dec = jnp.broadcast_to(ev_decay[0].reshape(2 * HEADS, 1), (2 * HEADS, CHUNK))
    even_params = (
        row(norm_mix[0]), row(norm_mlp[0]),
        ev_w_s[0].astype(BF16),
        jnp.broadcast_to(ev_b_s[0][:, :, None], (A_GROUPS, CHUNK, A_GROUP_DIM)),
        row(ev_vnorm[0]), dec,
    )
    odd_params = (row(norm_mix[1]), row(norm_mlp[1]), row(final_norm), row(od_pool_scale[0]))
    even_weights = (ev_w_in, ev_w_out, mlp_w1, mlp_w2)
    odd_weights = (od_w_pool.reshape(od_w_pool.shape[0], D_MODEL, POOL_DIM), mlp_w1, mlp_w2)

    xp2 = x_prompt.reshape(n_prompt * prompt_len, D_MODEL)
    xs2 = x_sample.reshape(n_sample * sample_len, D_MODEL)
    cos2, sin2 = _rope_tables(sample_len)
    sp = _ret_state_call(xs2, mod[0, 1:], state_ret, (cos2, sin2), row(norm_mix[0]), ev_w_in, dec,
                         n_seq=n_sample, seq_len=sample_len, even_index=0)
    rope = (jnp.concatenate([jnp.ones((UNIT, HEAD_DIM), F32), cos2]),
            jnp.concatenate([jnp.zeros((UNIT, HEAD_DIM), F32), sin2]))
    x_all, new_state = _even_call(xp2, xs2, mod[0], sp, rope, even_params, even_weights, units,
                                  layer=0, even_index=0)
    odd_units = _Units(n_prompt * prompt_len // ODD_UNIT, n_sample, sample_len // ODD_UNIT)
    y_prompt, y_sample = _odd_call(x_all, mod[1], odd_params, odd_weights, odd_units, unit=ODD_UNIT,
                                   short_len=prompt_len, long_len=sample_len, layer=1, odd_index=0)

    return (y_prompt.reshape(n_prompt, prompt_len, D_MODEL),
            y_sample.reshape(n_sample, sample_len, D_MODEL),
            new_state)
```

```python
import functools

import jax
import jax.numpy as jnp
from jax import lax
from jax.experimental import pallas as pl
from jax.experimental.pallas import tpu as pltpu

D_MODEL = 1024
DEPTH = 2
GRID_W = 64
A_WIDTH = D_MODEL // 2
A_GROUPS = 4
A_GROUP_DIM = A_WIDTH // A_GROUPS
CHUNK = 128
HEADS = 4
HEAD_DIM = (D_MODEL // 2) // HEADS
ROPE_BASE = 10000.0
ROPE_PAIRS_AXIS = HEAD_DIM // 4
POOL_WINDOWS = (2, 4, 8, 16)
POOL_DIM = D_MODEL // len(POOL_WINDOWS)
POOL_HALO = max(POOL_WINDOWS) // 2
D_FF = 4 * D_MODEL
EPS = 1e-6
QK_W = HEADS * HEAD_DIM
V_W = HEADS * HEAD_DIM
IN_WIDTH = 2 * A_WIDTH + 2 * QK_W + 3 * V_W
OUT_WIDTH = A_WIDTH + V_W
N_MOD = 6 * D_MODEL
K_COL = 2 * A_WIDTH + QK_W
V_COL = K_COL + QK_W

UNIT = 2 * CHUNK
UNIT_CHUNKS = UNIT // CHUNK
FF_CHUNK = 1024
ADA_BLOCK_N = 1536
COND_ROWS = 8
STAGE_ROWS = 512
STAGE_COLS = 1024
STAGE_SLOTS = 3
V7X_VMEM_LIMIT_BYTES = 60 * 1024 * 1024

F32 = jnp.float32
BF16 = jnp.bfloat16


def _dot(a, b):
    return jnp.dot(a, b, preferred_element_type=F32)


def _dot_nt(a, b):
    return lax.dot_general(a, b, (((1,), (1,)), ((), ())), preferred_element_type=F32)


def _silu(x):
    return x * jax.nn.sigmoid(x)


def _gelu_tanh(x):
    return 0.5 * x * (1.0 + jnp.tanh(0.7978845608028654 * (x + 0.044715 * (x * x * x))))


def _rms(x):
    return x * lax.rsqrt(jnp.mean(x * x, axis=-1, keepdims=True) + EPS)


def _rms_mod(x, gain, shift, scale):
    return (_rms(x) * gain) * (1.0 + scale) + shift


def _layer_norm(x):
    mu = jnp.mean(x, axis=-1, keepdims=True)
    d = x - mu
    return d * lax.rsqrt(jnp.mean(d * d, axis=-1, keepdims=True) + EPS)


def _split_mod(mod):
    return [mod[:, i * D_MODEL:(i + 1) * D_MODEL] for i in range(6)]


def _rope(x, cos2, sin2):
    return x * cos2 + pltpu.roll(x, HEAD_DIM // 2, axis=1) * sin2


def _pick_block(take_first, a_ref, b_ref):
    rows = lax.broadcasted_iota(jnp.int32, (a_ref.shape[0], 1), 0)
    return jnp.where(rows < jnp.where(take_first, a_ref.shape[0], 0), a_ref[...], b_ref[...])


def _run(gen, n=None):
    done = 0
    while n is None or done < n:
        try:
            next(gen)
        except StopIteration:
            break
        done += 1
    return done


def _weave(heavy, lights, n_light):
    heavy = list(heavy)
    pending = _roundrobin(lights)
    ran = 0
    for k, piece in enumerate(heavy):
        piece()
        want = ((k + 1) * n_light + len(heavy) - 1) // len(heavy)
        ran += _run(pending, want - ran)
    _run(pending)


def _pieces(gen, n):
    return [functools.partial(next, gen, None) for _ in range(n)]


def _roundrobin(gens):
    live = list(gens)
    while live:
        for g in list(live):
            try:
                next(g)
            except StopIteration:
                live.remove(g)
                continue
            yield


MLP_DOT_PIECES = 2 * (D_FF // FF_CHUNK)


def _mlp_pieces(hb, w1_ref, w2_ref):
    n_ff = D_FF // FF_CHUNK
    up = lambda j: _dot(hb, w1_ref[:, j * FF_CHUNK:(j + 1) * FF_CHUNK])
    a = up(0)
    yield
    acc = None
    for j in range(n_ff):
        a_next = None
        if j + 1 < n_ff:
            a_next = up(j + 1)
            yield
        p = _dot(jnp.square(jnp.maximum(a, 0.0)).astype(BF16), w2_ref[j * FF_CHUNK:(j + 1) * FF_CHUNK, :])
        acc = p if acc is None else acc + p
        a = a_next
        yield
    return acc


def _resident(shape):
    return pl.BlockSpec(shape, lambda i: (0,) * len(shape), pipeline_mode=pl.Buffered(1))


def _weight_chunks(w_hbm, w_vmem):
    rows, cols = w_hbm.shape
    jobs = []
    for r0 in range(0, rows, STAGE_ROWS):
        for c0 in range(0, cols, STAGE_COLS):
            window = (pl.ds(r0, min(STAGE_ROWS, rows - r0)), pl.ds(c0, min(STAGE_COLS, cols - c0)))
            jobs.append((w_hbm.at[window], w_vmem.at[window]))
    return jobs


def _load_weights_bf16(jobs, stage, sem):
    n_slots = stage.shape[0]

    def copy(k):
        r, c = jobs[k][0].shape
        slot = k % n_slots
        return pltpu.make_async_copy(jobs[k][0], stage.at[slot, pl.ds(0, r), pl.ds(0, c)], sem.at[slot])

    for k in range(min(n_slots - 1, len(jobs))):
        copy(k).start()
    for k, (src, dst) in enumerate(jobs):
        if k + n_slots - 1 < len(jobs):
            copy(k + n_slots - 1).start()
        copy(k).wait()
        r, c = src.shape
        dst[...] = stage[k % n_slots, 0:r, 0:c].astype(BF16)


class _Units:
    def __init__(self, n_short, n_long, units_per_long):
        self.n_short, self.n_long, self.units_per_long = n_short, n_long * units_per_long, units_per_long
        self.total = self.n_short + self.n_long

    def clamp(self, unit):
        return jnp.clip(unit, 0, self.total - 1)

    def short_block(self, unit):
        return jnp.minimum(unit, self.n_short - 1)

    def long_block(self, unit):
        return jnp.clip(unit - self.n_short, 0, self.n_long - 1)

    def mod_row(self, unit):
        return jnp.where(unit < self.n_short, 0, 1 + (unit - self.n_short) // self.units_per_long)

    def rope_block(self, unit):
        return jnp.where(unit < self.n_short, 0, 1 + (unit - self.n_short) % self.units_per_long)

    def state_block(self, unit):
        return jnp.where(unit < self.n_short, 0, self.units_per_long + unit - self.n_short)


def _ada_kernel(cond_ref, w_ref, b_ref, out_ref):
    s = _silu(cond_ref[...]).astype(BF16)
    out_ref[0] = _dot(s, w_ref[0].astype(BF16)) + b_ref[0]


def _ada_call(cond, ada_w, ada_b):
    return pl.pallas_call(
        _ada_kernel,
        grid=(DEPTH, N_MOD // ADA_BLOCK_N),
        in_specs=[
            pl.BlockSpec((COND_ROWS, D_MODEL), lambda i, j: (0, 0)),
            pl.BlockSpec((1, D_MODEL, ADA_BLOCK_N), lambda i, j: (i, 0, j)),
            pl.BlockSpec((1, 1, ADA_BLOCK_N), lambda i, j: (i, 0, j)),
        ],
        out_specs=pl.BlockSpec((1, COND_ROWS, ADA_BLOCK_N), lambda i, j: (i, 0, j)),
        out_shape=jax.ShapeDtypeStruct((DEPTH, COND_ROWS, N_MOD), F32),
        compiler_params=pltpu.CompilerParams(
            dimension_semantics=("arbitrary", "arbitrary"),
            vmem_limit_bytes=V7X_VMEM_LIMIT_BYTES),
        name="ada_mod",
    )(cond, ada_w, ada_b.reshape(DEPTH, 1, N_MOD))


def _decay_tables(dec_ref, hd):
    row = lax.broadcasted_iota(jnp.int32, (CHUNK, CHUNK), 0).astype(F32)
    col = lax.broadcasted_iota(jnp.int32, (CHUNK, CHUNK), 1).astype(F32)
    lg_f = jnp.broadcast_to(-jnp.exp(dec_ref[hd:hd + 1, :]), (CHUNK, CHUNK))
    lg_b = jnp.broadcast_to(-jnp.exp(dec_ref[HEADS + hd:HEADS + hd + 1, :]), (CHUNK, CHUNK))
    d_f = row - col
    d_b = col - row
    return dict(
        decay_f=jnp.where(d_f >= 0, jnp.exp(lg_f * jnp.maximum(d_f, 0.0)), 0.0),
        decay_b=jnp.where(d_b >= 0, jnp.exp(lg_b * jnp.maximum(d_b, 0.0)), 0.0),
        qd_f=jnp.exp(lg_f * (row + 1.0)),
        qd_b=jnp.exp(lg_b * (CHUNK - row)),
        kd_f=jnp.exp(lg_f * (CHUNK - 1.0 - row)),
        kd_b=jnp.exp(lg_b * row),
        cd_f=jnp.exp(lg_f * CHUNK),
        cd_b=jnp.exp(lg_b * CHUNK),
    )


TABLE_NAMES = ("decay_f", "decay_b", "qd_f", "qd_b", "kd_f", "kd_b", "cd_f", "cd_b")


def _ret_state_kernel(x_ref, mod_ref, s0_ref, cos_ref, sin_ref, nmix_ref, wk_ref, wv_ref, dec_ref, sp_ref,
                      *, n_chunk):
    step = pl.program_id(0)

    @pl.when(step == 0)
    def _():
        sp_ref[...] = jnp.zeros_like(sp_ref)

    @pl.when(step > 0)
    def _():
        sh1, sc1 = _split_mod(mod_ref[0])[:2]
        hb = _rms_mod(x_ref[...], nmix_ref[...], sh1, sc1).astype(BF16)
        zk = _dot(hb, wk_ref[...].astype(BF16)) * (HEAD_DIM ** -0.5)
        zv = _dot(hb, wv_ref[...].astype(BF16)).astype(BF16)
        cos2, sin2 = cos_ref[...], sin_ref[...]
        for hd in range(HEADS):
            lo = hd * HEAD_DIM
            tb = _decay_tables(dec_ref, hd)
            kh = _rope(zk[:, lo:lo + HEAD_DIM], cos2, sin2)
            kv_f, kv_b = [], []
            for c in range(n_chunk):
                kc = kh[c * CHUNK:(c + 1) * CHUNK]
                vc = zv[c * CHUNK:(c + 1) * CHUNK, lo:lo + HEAD_DIM]
                kv_f.append(_dot((kc * tb["kd_f"]).T.astype(BF16), vc))
                kv_b.append(_dot((kc * tb["kd_b"]).T.astype(BF16), vc))
            st = s0_ref[0, 0, 0, hd]
            for c in range(n_chunk):
                if c % UNIT_CHUNKS == 0:
                    sp_ref[c // UNIT_CHUNKS, 0, hd] = st
                st = tb["cd_f"] * st + kv_f[c]
            st = s0_ref[0, 0, 1, hd]
            for c in reversed(range(n_chunk)):
                if c % UNIT_CHUNKS == UNIT_CHUNKS - 1:
                    sp_ref[c // UNIT_CHUNKS, 1, hd] = st
                st = tb["cd_b"] * st + kv_b[c]


def _ret_state_call(x2, mod, s0, rope, nmix, w_in, dec, *, n_seq, seq_len, even_index):
    n_chunk = seq_len // CHUNK
    units = seq_len // UNIT
    seq = lambda i: jnp.maximum(i - 1, 0)
    return pl.pallas_call(
        functools.partial(_ret_state_kernel, n_chunk=n_chunk),
        grid=(n_seq + 1,),
        in_specs=[
            pl.BlockSpec((seq_len, D_MODEL), lambda i: (seq(i), 0)),
            pl.BlockSpec((1, 1, N_MOD), lambda i: (seq(i), 0, 0)),
            pl.BlockSpec((1, 1, 2, HEADS, HEAD_DIM, HEAD_DIM), lambda i: (seq(i), 0, 0, 0, 0, 0)),
            _resident((seq_len, HEAD_DIM)), _resident((seq_len, HEAD_DIM)),
            _resident((1, D_MODEL)),
            pl.BlockSpec((None, D_MODEL, QK_W), lambda i: (even_index, 0, K_COL // QK_W),
                         pipeline_mode=pl.Buffered(1)),
            pl.BlockSpec((None, D_MODEL, V_W), lambda i: (even_index, 0, V_COL // V_W),
                         pipeline_mode=pl.Buffered(1)),
            _resident(dec.shape),
        ],
        out_specs=pl.BlockSpec((units, 2, HEADS, HEAD_DIM, HEAD_DIM), lambda i: (i, 0, 0, 0, 0)),
        out_shape=jax.ShapeDtypeStruct(((n_seq + 1) * units, 2, HEADS, HEAD_DIM, HEAD_DIM), F32),
        compiler_params=pltpu.CompilerParams(
            dimension_semantics=("arbitrary",),
            vmem_limit_bytes=V7X_VMEM_LIMIT_BYTES),
        name="ret_state",
    )(x2, mod, s0, rope[0], rope[1], nmix, w_in, w_in, dec)


def _even_kernel(xa_short, xa_long, xc_short, xc_long, moda_ref, modc_ref, sp_ref, cos_ref, sin_ref,
                 nmix_ref, nmlp_ref, w_s_ref, b_s_ref, vgain_ref, dec_ref, w_in_hbm, w_out_hbm, w1_hbm, w2_hbm,
                 y_ref, st_ref, ug_s, vg_s, q_s, k_s, v_s, gf_s, gb_s, cat_s, tab_s,
                 w_in_ref, w_out_ref, w1_ref, w2_ref, stage, sem, *, units, layer, even_index):
    n_units, n_short = units.total, units.n_short
    step = pl.program_id(0)
    slot_a = lax.rem(step, 2)
    slot_b = 1 - slot_a

    @pl.when(step == 0)
    def _():
        _load_weights_bf16(_weight_chunks(w_in_hbm.at[even_index], w_in_ref)
                           + _weight_chunks(w_out_hbm.at[even_index], w_out_ref)
                           + _weight_chunks(w1_hbm.at[layer], w1_ref)
                           + _weight_chunks(w2_hbm.at[layer], w2_ref), stage, sem)
        for hd in range(HEADS):
            tb = _decay_tables(dec_ref, hd)
            for n, name in enumerate(TABLE_NAMES):
                tab_s[hd, n] = tb[name]

    def stage_in():
        sh1, sc1 = _split_mod(moda_ref[0])[:2]
        x = _pick_block(step < n_short, xa_short, xa_long)
        hb = _rms_mod(x, nmix_ref[...], sh1, sc1).astype(BF16)
        z = _dot(hb, w_in_ref[...])
        yield
        ug_s[slot_a] = _gelu_tanh(z[:, 0:A_WIDTH])
        yield
        zv = _gelu_tanh(z[:, A_WIDTH:2 * A_WIDTH])
        vgn = [_layer_norm(zv[:, g * A_GROUP_DIM:(g + 1) * A_GROUP_DIM]) for g in range(A_GROUPS)]
        vg_s[slot_a] = (jnp.concatenate(vgn, axis=-1) * vgain_ref[...]).astype(BF16)
        yield
        cos2, sin2 = cos_ref[...], sin_ref[...]
        heads = lambda zz: jnp.concatenate(
            [_rope(zz[:, h * HEAD_DIM:(h + 1) * HEAD_DIM], cos2, sin2) for h in range(HEADS)], -1)
        q_s[slot_a] = heads(z[:, 2 * A_WIDTH:K_COL]).astype(BF16)
        yield
        k_s[slot_a] = heads(z[:, K_COL:V_COL] * (HEAD_DIM ** -0.5))
        v_s[slot_a] = z[:, V_COL:V_COL + V_W].astype(BF16)
        yield
        gf_s[slot_a] = _silu(z[:, V_COL + V_W:V_COL + 2 * V_W])
        yield
        gb_s[slot_a] = _silu(z[:, V_COL + 2 * V_W:V_COL + 3 * V_W])

    IN_LIGHT = 6

    def stage_mix(final_states):
        s = slot_b
        rows = [slice(c * CHUNK, (c + 1) * CHUNK) for c in range(UNIT_CHUNKS)]
        head_cols = [slice(hd * HEAD_DIM, (hd + 1) * HEAD_DIM) for hd in range(HEADS)]
        table = lambda hd, name: tab_s[hd, TABLE_NAMES.index(name)]

        sp2 = []
        for g in range(A_GROUPS):
            lo = g * A_GROUP_DIM
            vg2 = jnp.concatenate([vg_s[s, r, lo:lo + A_GROUP_DIM] for r in rows], axis=-1)
            sp2.append(_dot(w_s_ref[g], vg2))
        yield
        scores, kv_f, kv_b = [], [], []
        for hd in range(HEADS):
            kc = [k_s[s, r, head_cols[hd]] for r in rows]
            vc = [v_s[s, r, head_cols[hd]] for r in rows]
            scores.append([_dot_nt(q_s[s, r, head_cols[hd]], kc[c].astype(BF16)) for c, r in enumerate(rows)])
            kv_f.append([_dot((kc[c] * table(hd, "kd_f")).T.astype(BF16), vc[c]) for c in range(UNIT_CHUNKS)])
            kv_b.append([_dot((kc[c] * table(hd, "kd_b")).T.astype(BF16), vc[c]) for c in range(UNIT_CHUNKS)])
            yield
        for g in range(A_GROUPS):
            lo = g * A_GROUP_DIM
            for c, r in enumerate(rows):
                gate = sp2[g][:, c * CHUNK:(c + 1) * CHUNK] + b_s_ref[g]
                cat_s[s, r, lo:lo + A_GROUP_DIM] = (ug_s[s, r, lo:lo + A_GROUP_DIM] * gate).astype(BF16)
            if g % 2 == 1:
                yield
        y_raw = []
        for hd in range(HEADS):
            enter_f, st = [], sp_ref[0, 0, hd]
            for c in range(UNIT_CHUNKS):
                enter_f.append(st)
                st = table(hd, "cd_f") * st + kv_f[hd][c]
            final_f = st
            enter_b, st = [None] * UNIT_CHUNKS, sp_ref[0, 1, hd]
            for c in reversed(range(UNIT_CHUNKS)):
                enter_b[c] = st
                st = table(hd, "cd_b") * st + kv_b[hd][c]
            final_states.append((final_f, st))
            per_chunk = []
            for c, r in enumerate(rows):
                vc = v_s[s, r, head_cols[hd]]
                qf = q_s[s, r, head_cols[hd]].astype(F32)

                def direction(decay, qd, enter):
                    p = (scores[hd][c] * table(hd, decay)).astype(BF16)
                    lhs = jnp.concatenate([p, (qf * table(hd, qd)).astype(BF16)], axis=-1)
                    return _dot(lhs, jnp.concatenate([vc, enter.astype(BF16)], axis=0))

                per_chunk.append((direction("decay_f", "qd_f", enter_f[c]), direction("decay_b", "qd_b", enter_b[c])))
            y_raw.append(per_chunk)
            yield
        for hd in range(HEADS):
            for c, r in enumerate(rows):
                y_f, y_b = y_raw[hd][c]
                cols = slice(A_WIDTH + hd * HEAD_DIM, A_WIDTH + (hd + 1) * HEAD_DIM)
                cat_s[s, r, cols] = (gf_s[s, r, head_cols[hd]] * _layer_norm(y_f)
                                     + gb_s[s, r, head_cols[hd]] * _layer_norm(y_b)).astype(BF16)
            yield

    MIX_LIGHT = 1 + HEADS + A_GROUPS // 2 + 2 * HEADS

    def emit_states(final_states):
        @pl.when(step - 1 < n_short)
        def _():
            for hd, (final_f, final_b) in enumerate(final_states):
                st_ref[0, 0, 0, hd] = final_f
                st_ref[0, 0, 1, hd] = final_b

    def stage_mlp():
        _, _, g1, sh2, sc2, g2 = _split_mod(modc_ref[0])
        x = _pick_block(step - 2 < n_short, xc_short, xc_long) + g1 * _dot(cat_s[slot_a], w_out_ref[...])
        hb = _rms_mod(x, nmlp_ref[...], sh2, sc2).astype(BF16)
        yield
        acc = yield from _mlp_pieces(hb, w1_ref, w2_ref)
        y_ref[...] = x + g2 * acc

    @pl.when(step == 0)
    def _():
        _run(stage_in())

    @pl.when(step == 1)
    def _():
        a, finals = stage_in(), []
        _weave(_pieces(a, 1), [a, stage_mix(finals)], IN_LIGHT + MIX_LIGHT)
        emit_states(finals)

    @pl.when((step >= 2) & (step < n_units))
    def _():
        a, c, finals = stage_in(), stage_mlp(), []
        heavy = _pieces(c, 1) + _pieces(a, 1) + _pieces(c, MLP_DOT_PIECES + 1)
        _weave(heavy, [a, stage_mix(finals)], IN_LIGHT + MIX_LIGHT)
        emit_states(finals)

    @pl.when(step == n_units)
    def _():
        finals = []
        _weave(_pieces(stage_mlp(), MLP_DOT_PIECES + 2), [stage_mix(finals)], MIX_LIGHT)
        emit_states(finals)

    @pl.when(step == n_units + 1)
    def _():
        _run(stage_mlp())


_HBM = pl.BlockSpec(memory_space=pl.ANY)
_WEIGHT_STAGING = [pltpu.VMEM((STAGE_SLOTS, STAGE_ROWS, STAGE_COLS), F32), pltpu.SemaphoreType.DMA((STAGE_SLOTS,))]


def _even_call(x_short, x_long, mod, sp, rope, params, weights, units, *, layer, even_index):
    n_units = units.total
    unit_a = lambda i: units.clamp(i)
    unit_b = lambda i: units.clamp(i - 1)
    unit_c = lambda i: units.clamp(i - 2)
    state_block = (1, 1, 2, HEADS, HEAD_DIM, HEAD_DIM)

    in_specs = [pl.BlockSpec((UNIT, D_MODEL), lambda i: (units.short_block(unit_a(i)), 0)),
                pl.BlockSpec((UNIT, D_MODEL), lambda i: (units.long_block(unit_a(i)), 0)),
                pl.BlockSpec((UNIT, D_MODEL), lambda i: (units.short_block(unit_c(i)), 0)),
                pl.BlockSpec((UNIT, D_MODEL), lambda i: (units.long_block(unit_c(i)), 0)),
                pl.BlockSpec((1, 1, N_MOD), lambda i: (units.mod_row(unit_a(i)), 0, 0)),
                pl.BlockSpec((1, 1, N_MOD), lambda i: (units.mod_row(unit_c(i)), 0, 0)),
                pl.BlockSpec((1, 2, HEADS, HEAD_DIM, HEAD_DIM), lambda i: (units.state_block(unit_b(i)), 0, 0, 0, 0)),
                pl.BlockSpec((UNIT, HEAD_DIM), lambda i: (units.rope_block(unit_a(i)), 0)),
                pl.BlockSpec((UNIT, HEAD_DIM), lambda i: (units.rope_block(unit_a(i)), 0))]
    in_specs += [_resident(p.shape) for p in params] + [_HBM] * len(weights)
    scratch = [
        pltpu.VMEM((2, UNIT, A_WIDTH), F32),
        pltpu.VMEM((2, UNIT, A_WIDTH), BF16),
        pltpu.VMEM((2, UNIT, QK_W), BF16),
        pltpu.VMEM((2, UNIT, QK_W), F32),
        pltpu.VMEM((2, UNIT, V_W), BF16),
        pltpu.VMEM((2, UNIT, V_W), F32),
        pltpu.VMEM((2, UNIT, V_W), F32),
        pltpu.VMEM((2, UNIT, OUT_WIDTH), BF16),
        pltpu.VMEM((HEADS, len(TABLE_NAMES), CHUNK, CHUNK), F32),
    ]
    scratch += [pltpu.VMEM(w.shape[1:], BF16) for w in weights] + _WEIGHT_STAGING
    return pl.pallas_call(
        functools.partial(_even_kernel, units=units, layer=layer, even_index=even_index),
        grid=(n_units + 2,),
        in_specs=in_specs,
        out_specs=[pl.BlockSpec((UNIT, D_MODEL), lambda i: (unit_c(i), 0)),
                   pl.BlockSpec(state_block, lambda i: (units.short_block(unit_b(i)), 0, 0, 0, 0, 0))],
        out_shape=[jax.ShapeDtypeStruct((n_units * UNIT, D_MODEL), F32),
                   jax.ShapeDtypeStruct((units.n_short,) + state_block[1:], F32)],
        scratch_shapes=scratch,
        compiler_params=pltpu.CompilerParams(
            dimension_semantics=("arbitrary",),
            vmem_limit_bytes=V7X_VMEM_LIMIT_BYTES),
        name="even_layer",
    )(x_short, x_long, x_short, x_long, mod, mod, sp, rope[0], rope[1], *params, *weights)


def _odd_kernel(xp_ref, x_ref, xn_ref, moda_ref, modc_ref, nmix_ref, nmlp_ref, fin_ref, pscale_ref,
                w_pool_hbm, w1_hbm, w2_hbm, y_short, y_long, x1_s, hb_s, pre_s, w_pool_ref, w1_ref, w2_ref, stage, sem,
                *, n_units, n_short, short_len, long_len, layer, odd_index):
    step = pl.program_id(0)
    slot_a = lax.rem(step, 2)
    slot_b = 1 - slot_a
    unit = x_ref.shape[0]
    rows_w = unit + 2 * POOL_HALO

    @pl.when(step == 0)
    def _():
        _load_weights_bf16(_weight_chunks(w_pool_hbm.at[odd_index], w_pool_ref)
                           + _weight_chunks(w1_hbm.at[layer], w1_ref)
                           + _weight_chunks(w2_hbm.at[layer], w2_ref), stage, sem)

    def stage_pool():
        sh1, sc1, g1, sh2, sc2, _ = _split_mod(moda_ref[0])
        xw = jnp.concatenate([xp_ref[...], x_ref[...], xn_ref[...]], axis=0)
        h = _rms_mod(xw, nmix_ref[...], sh1, sc1)
        seq_len = jnp.where(step < n_short, short_len, long_len)
        first = jnp.minimum(step, n_units - 1) * unit - POOL_HALO
        t = (lax.broadcasted_iota(jnp.int32, (rows_w, 1), 0) + first) & (seq_len - 1)
        yield

        def shift_down(a, k):
            return jnp.where(t >= k, pltpu.roll(a, k, axis=0), 0.0)

        def shift_up(a, k):
            return jnp.where(t < seq_len - k, pltpu.roll(a, rows_w - k, axis=0), 0.0)

        outs = []
        for g, w in enumerate(POOL_WINDOWS):
            hg = h[:, g * POOL_DIM:(g + 1) * POOL_DIM]
            half = w // 2
            left = shift_down(hg, 1)
            right = hg
            span = 1
            while span < half:
                left = left + shift_down(left, span)
                right = right + shift_up(right, span)
                span *= 2
            cnt = (jnp.minimum(t + half, seq_len) - jnp.maximum(t - half, 0)).astype(F32)
            pooled = ((left + right) / cnt - hg)[POOL_HALO:POOL_HALO + unit].astype(BF16)
            outs.append(_dot(pooled, w_pool_ref[g * POOL_DIM:(g + 1) * POOL_DIM, :]))
            yield
        x1 = x_ref[...] + g1 * (jnp.concatenate(outs, axis=-1) * pscale_ref[...])
        x1_s[slot_a] = x1
        hb_s[slot_a] = _rms_mod(x1, nmlp_ref[...], sh2, sc2).astype(BF16)

    def stage_mlp():
        g2 = _split_mod(modc_ref[0])[5]
        acc = yield from _mlp_pieces(hb_s[slot_b], w1_ref, w2_ref)
        pre_s[slot_b] = x1_s[slot_b] + g2 * acc

    def stage_final(result):
        result.append(_rms(pre_s[slot_a]) * fin_ref[...])
        yield

    def store(result):
        @pl.when((step >= 2) & (step - 2 < n_short))
        def _():
            y_short[...] = result[0]

        @pl.when(step - 2 >= n_short)
        def _():
            y_long[...] = result[0]

    pool_light = 2 + len(POOL_WINDOWS)

    @pl.when(step == 0)
    def _():
        pre_s[...] = jnp.zeros_like(pre_s)
        _run(stage_pool())

    @pl.when((step >= 1) & (step < n_units))
    def _():
        result = []
        _weave(_pieces(stage_mlp(), MLP_DOT_PIECES + 1), [stage_final(result), stage_pool()], pool_light + 1)
        store(result)

    @pl.when(step == n_units)
    def _():
        result = []
        _weave(_pieces(stage_mlp(), MLP_DOT_PIECES + 1), [stage_final(result)], 1)
        store(result)

    @pl.when(step == n_units + 1)
    def _():
        result = []
        _run(stage_final(result))
        store(result)


def _odd_call(x2, mod, params, weights, units, *, unit, short_len, long_len, layer, odd_index):
    n_units = units.total
    halo_per_unit = unit // POOL_HALO
    last_halo = x2.shape[0] // POOL_HALO - 1
    unit_a = lambda i: units.clamp(i)
    unit_b = lambda i: units.clamp(i - 1)
    unit_c = lambda i: units.clamp(i - 2)
    assert (units.n_short * unit) % long_len == 0
    in_specs = [pl.BlockSpec((POOL_HALO, D_MODEL), lambda i: (jnp.maximum(unit_a(i) * halo_per_unit - 1, 0), 0)),
                pl.BlockSpec((unit, D_MODEL), lambda i: (unit_a(i), 0)),
                pl.BlockSpec((POOL_HALO, D_MODEL),
                             lambda i: (jnp.minimum((unit_a(i) + 1) * halo_per_unit, last_halo), 0)),
                pl.BlockSpec((1, 1, N_MOD), lambda i: (units.mod_row(unit_a(i)), 0, 0)),
                pl.BlockSpec((1, 1, N_MOD), lambda i: (units.mod_row(unit_b(i)), 0, 0))]
    in_specs += [_resident(p.shape) for p in params] + [_HBM] * len(weights)
    scratch = [pltpu.VMEM((2, unit, D_MODEL), F32),
               pltpu.VMEM((2, unit, D_MODEL), BF16),
               pltpu.VMEM((2, unit, D_MODEL), F32)]
    scratch += [pltpu.VMEM(w.shape[1:], BF16) for w in weights] + _WEIGHT_STAGING
    return pl.pallas_call(
        functools.partial(_odd_kernel, n_units=n_units, n_short=units.n_short,
                          short_len=short_len, long_len=long_len, layer=layer, odd_index=odd_index),
        grid=(n_units + 2,),
        in_specs=in_specs,
        out_specs=[pl.BlockSpec((unit, D_MODEL), lambda i: (units.short_block(unit_c(i)), 0)),
                   pl.BlockSpec((unit, D_MODEL), lambda i: (units.long_block(unit_c(i)), 0))],
        out_shape=[jax.ShapeDtypeStruct((units.n_short * unit, D_MODEL), F32),
                   jax.ShapeDtypeStruct((units.n_long * unit, D_MODEL), F32)],
        scratch_shapes=scratch,
        compiler_params=pltpu.CompilerParams(
            dimension_semantics=("arbitrary",),
            vmem_limit_bytes=V7X_VMEM_LIMIT_BYTES),
        name="odd_layer",
    )(x2, x2, x2, mod, mod, *params, *weights)


def _rope_tables(n_tokens):
    rows = n_tokens // GRID_W
    pos_r = jnp.repeat(jnp.arange(rows, dtype=F32), GRID_W)
    pos_c = jnp.tile(jnp.arange(GRID_W, dtype=F32), rows)
    inv = ROPE_BASE ** (-jnp.arange(ROPE_PAIRS_AXIS, dtype=F32) / ROPE_PAIRS_AXIS)
    ang = jnp.concatenate([pos_r[:, None] * inv, pos_c[:, None] * inv], axis=-1)
    cos, sin = jnp.cos(ang), jnp.sin(ang)
    return jnp.concatenate([cos, cos], axis=-1), jnp.concatenate([-sin, sin], axis=-1)


def kernel(x_prompt, x_sample, state_ret, c, c_ctx, ada_w, ada_b, norm_mix, norm_mlp, mlp_w1, mlp_w2,
           ev_w_in, ev_w_s, ev_b_s, ev_vnorm, ev_decay, ev_w_out, od_w_pool, od_pool_scale, final_norm):
    n_prompt, prompt_len, _ = x_prompt.shape
    n_sample, sample_len, _ = x_sample.shape
    assert DEPTH == 2 and n_sample + 1 <= COND_ROWS
    assert prompt_len == UNIT and sample_len % UNIT == 0
    units = _Units(n_prompt, n_sample, sample_len // UNIT)

    cond = jnp.zeros((COND_ROWS, D_MODEL), F32).at[0].set(c_ctx).at[1:1 + n_sample].set(c)
    mod = _ada_call(cond, ada_w, ada_b)[:, :1 + n_sample].reshape(DEPTH, 1 + n_sample, 1, N_MOD)

    row = lambda v: v.reshape(1, -1)
    dec = jnp.broadcast_to(ev_decay[0].reshape(2 * HEADS, 1), (2 * HEADS, CHUNK))
    even_params = (
        row(norm_mix[0]), row(norm_mlp[0]),
        ev_w_s[0].astype(BF16),
        jnp.broadcast_to(ev_b_s[0][:, :, None], (A_GROUPS, CHUNK, A_GROUP_DIM)),
        row(ev_vnorm[0]), dec,
    )
    odd_params = (row(norm_mix[1]), row(norm_mlp[1]), row(final_norm), row(od_pool_scale[0]))
    even_weights = (ev_w_in, ev_w_out, mlp_w1, mlp_w2)
    odd_weights = (od_w_pool.reshape(od_w_pool.shape[0], D_MODEL, POOL_DIM), mlp_w1, mlp_w2)

    xp2 = x_prompt.reshape(n_prompt * prompt_len, D_MODEL)
    xs2 = x_sample.reshape(n_sample * sample_len, D_MODEL)
    cos2, sin2 = _rope_tables(sample_len)
    sp = _ret_state_call(xs2, mod[0, 1:], state_ret, (cos2, sin2), row(norm_mix[0]), ev_w_in, dec,
                         n_seq=n_sample, seq_len=sample_len, even_index=0)
    rope = (jnp.concatenate([jnp.ones((UNIT, HEAD_DIM), F32), cos2]),
            jnp.concatenate([jnp.zeros((UNIT, HEAD_DIM), F32), sin2]))
    x_all, new_state = _even_call(xp2, xs2, mod[0], sp, rope, even_params, even_weights, units,
                                  layer=0, even_index=0)
    y_prompt, y_sample = _odd_call(x_all, mod[1], odd_params, odd_weights, units, unit=UNIT,
                                   short_len=prompt_len, long_len=sample_len, layer=1, odd_index=0)

    return (y_prompt.reshape(n_prompt, prompt_len, D_MODEL),
            y_sample.reshape(n_sample, sample_len, D_MODEL),
            new_state)
```

```python
import functools

import jax
import jax.numpy as jnp
import numpy as np
from jax import lax
from jax.experimental import pallas as pl
from jax.experimental.pallas import tpu as pltpu

D_MODEL = 1024
DEPTH = 2
GRID_W = 64
A_WIDTH = D_MODEL // 2
A_GROUPS = 4
A_GROUP_DIM = A_WIDTH // A_GROUPS
CHUNK = 128
HEADS = 4
HEAD_DIM = (D_MODEL // 2) // HEADS
ROPE_BASE = 10000.0
ROPE_PAIRS_AXIS = HEAD_DIM // 4
POOL_WINDOWS = (2, 4, 8, 16)
POOL_DIM = D_MODEL // len(POOL_WINDOWS)
POOL_HALO = max(POOL_WINDOWS) // 2
D_FF = 4 * D_MODEL
EPS = 1e-6
QK_W = HEADS * HEAD_DIM
V_W = HEADS * HEAD_DIM
IN_WIDTH = 2 * A_WIDTH + 2 * QK_W + 3 * V_W
OUT_WIDTH = A_WIDTH + V_W
N_MOD = 6 * D_MODEL
K_COL = 2 * A_WIDTH + QK_W
V_COL = K_COL + QK_W

UNIT = 2 * CHUNK
UNIT_CHUNKS = UNIT // CHUNK
FF_CHUNK = 1024
ADA_BLOCK_N = 3072
COND_ROWS = 8
STAGE_ROWS = 512
STAGE_COLS = 1024
STAGE_SLOTS = 3
V7X_VMEM_LIMIT_BYTES = 60 * 1024 * 1024

F32 = jnp.float32
BF16 = jnp.bfloat16


def _dot(a, b):
    return jnp.dot(a, b, preferred_element_type=F32)


def _dot_nt(a, b):
    return lax.dot_general(a, b, (((1,), (1,)), ((), ())), preferred_element_type=F32)


def _silu(x):
    return x * jax.nn.sigmoid(x)


def _gelu_tanh(x):
    return 0.5 * x * (1.0 + jnp.tanh(0.7978845608028654 * (x + 0.044715 * (x * x * x))))


def _rms(x):
    return x * lax.rsqrt(jnp.mean(x * x, axis=-1, keepdims=True) + EPS)


def _rms_mod(x, gain, shift, scale):
    return (_rms(x) * gain) * (1.0 + scale) + shift


def _layer_norm(x):
    mu = jnp.mean(x, axis=-1, keepdims=True)
    d = x - mu
    return d * lax.rsqrt(jnp.mean(d * d, axis=-1, keepdims=True) + EPS)


def _split_mod(mod):
    return [mod[:, i * D_MODEL:(i + 1) * D_MODEL] for i in range(6)]


def _rope(x, cos2, sin2):
    return x * cos2 + pltpu.roll(x, HEAD_DIM // 2, axis=1) * sin2


def _pick_block(take_first, a_ref, b_ref):
    rows = lax.broadcasted_iota(jnp.int32, (a_ref.shape[0], 1), 0)
    return jnp.where(rows < jnp.where(take_first, a_ref.shape[0], 0), a_ref[...], b_ref[...])


def _run(gen, n=None):
    done = 0
    while n is None or done < n:
        try:
            next(gen)
        except StopIteration:
            break
        done += 1
    return done


def _weave(heavy, lights, n_light):
    heavy = list(heavy)
    pending = _roundrobin(lights)
    ran = 0
    for k, piece in enumerate(heavy):
        piece()
        want = ((k + 1) * n_light + len(heavy) - 1) // len(heavy)
        ran += _run(pending, want - ran)
    _run(pending)


def _pieces(gen, n):
    return [functools.partial(next, gen, None) for _ in range(n)]


def _roundrobin(gens):
    live = list(gens)
    while live:
        for g in list(live):
            try:
                next(g)
            except StopIteration:
                live.remove(g)
                continue
            yield


MLP_DOT_PIECES = 2 * (D_FF // FF_CHUNK)


def _mlp_pieces(hb, w1_ref, w2_ref):
    n_ff = D_FF // FF_CHUNK
    up = lambda j: _dot(hb, w1_ref[:, j * FF_CHUNK:(j + 1) * FF_CHUNK])
    a = up(0)
    yield
    acc = None
    for j in range(n_ff):
        a_next = None
        if j + 1 < n_ff:
            a_next = up(j + 1)
            yield
        p = _dot(jnp.square(jnp.maximum(a, 0.0)).astype(BF16), w2_ref[j * FF_CHUNK:(j + 1) * FF_CHUNK, :])
        acc = p if acc is None else acc + p
        a = a_next
        yield
    return acc


def _resident(shape):
    return pl.BlockSpec(shape, lambda i: (0,) * len(shape), pipeline_mode=pl.Buffered(1))


def _weight_chunks(w_hbm, w_vmem):
    rows, cols = w_hbm.shape
    jobs = []
    for r0 in range(0, rows, STAGE_ROWS):
        for c0 in range(0, cols, STAGE_COLS):
            window = (pl.ds(r0, min(STAGE_ROWS, rows - r0)), pl.ds(c0, min(STAGE_COLS, cols - c0)))
            jobs.append((w_hbm.at[window], w_vmem.at[window]))
    return jobs


def _load_weights_bf16(jobs, stage, sem):
    n_slots = stage.shape[0]

    def copy(k):
        r, c = jobs[k][0].shape
        slot = k % n_slots
        return pltpu.make_async_copy(jobs[k][0], stage.at[slot, pl.ds(0, r), pl.ds(0, c)], sem.at[slot])

    for k in range(min(n_slots - 1, len(jobs))):
        copy(k).start()
    for k, (src, dst) in enumerate(jobs):
        if k + n_slots - 1 < len(jobs):
            copy(k + n_slots - 1).start()
        copy(k).wait()
        r, c = src.shape
        dst[...] = stage[k % n_slots, 0:r, 0:c].astype(BF16)


class _Units:
    def __init__(self, n_short, n_long, units_per_long):
        self.n_short, self.n_long, self.units_per_long = n_short, n_long * units_per_long, units_per_long
        self.total = self.n_short + self.n_long

    def clamp(self, unit):
        return jnp.clip(unit, 0, self.total - 1)

    def short_block(self, unit):
        return jnp.minimum(unit, self.n_short - 1)

    def long_block(self, unit):
        return jnp.clip(unit - self.n_short, 0, self.n_long - 1)

    def mod_row(self, unit):
        return jnp.where(unit < self.n_short, 0, 1 + (unit - self.n_short) // self.units_per_long)

    def rope_block(self, unit):
        return jnp.where(unit < self.n_short, 0, 1 + (unit - self.n_short) % self.units_per_long)

    def state_block(self, unit):
        return jnp.where(unit < self.n_short, 0, self.units_per_long + unit - self.n_short)


def _ada_kernel(cond_ref, w_ref, b_ref, out_ref):
    s = _silu(cond_ref[...]).astype(BF16)
    out_ref[0] = _dot(s, w_ref[0].astype(BF16)) + b_ref[0]


def _ada_call(cond, ada_w, ada_b):
    return pl.pallas_call(
        _ada_kernel,
        grid=(DEPTH, N_MOD // ADA_BLOCK_N),
        in_specs=[
            pl.BlockSpec((COND_ROWS, D_MODEL), lambda i, j: (0, 0)),
            pl.BlockSpec((1, D_MODEL, ADA_BLOCK_N), lambda i, j: (i, 0, j)),
            pl.BlockSpec((1, 1, ADA_BLOCK_N), lambda i, j: (i, 0, j)),
        ],
        out_specs=pl.BlockSpec((1, COND_ROWS, ADA_BLOCK_N), lambda i, j: (i, 0, j)),
        out_shape=jax.ShapeDtypeStruct((DEPTH, COND_ROWS, N_MOD), F32),
        compiler_params=pltpu.CompilerParams(
            dimension_semantics=("arbitrary", "arbitrary"),
            vmem_limit_bytes=V7X_VMEM_LIMIT_BYTES),
        name="ada_mod",
    )(cond, ada_w, ada_b.reshape(DEPTH, 1, N_MOD))


def _decay_tables(dec_ref, hd):
    row = lax.broadcasted_iota(jnp.int32, (CHUNK, CHUNK), 0).astype(F32)
    col = lax.broadcasted_iota(jnp.int32, (CHUNK, CHUNK), 1).astype(F32)
    lg_f = jnp.broadcast_to(-jnp.exp(dec_ref[hd:hd + 1, :]), (CHUNK, CHUNK))
    lg_b = jnp.broadcast_to(-jnp.exp(dec_ref[HEADS + hd:HEADS + hd + 1, :]), (CHUNK, CHUNK))
    d_f = row - col
    d_b = col - row
    return dict(
        decay_f=jnp.where(d_f >= 0, jnp.exp(lg_f * jnp.maximum(d_f, 0.0)), 0.0),
        decay_b=jnp.where(d_b >= 0, jnp.exp(lg_b * jnp.maximum(d_b, 0.0)), 0.0),
        qd_f=jnp.exp(lg_f * (row + 1.0)),
        qd_b=jnp.exp(lg_b * (CHUNK - row)),
        kd_f=jnp.exp(lg_f * (CHUNK - 1.0 - row)),
        kd_b=jnp.exp(lg_b * row),
        cd_f=jnp.exp(lg_f * CHUNK),
        cd_b=jnp.exp(lg_b * CHUNK),
    )


TABLE_NAMES = ("decay_f", "decay_b", "qd_f", "qd_b", "kd_f", "kd_b", "cd_f", "cd_b")


def _ret_state_kernel(x_ref, mod_ref, s0_ref, cos_ref, sin_ref, nmix_ref, wk_ref, wv_ref, dec_ref, sp_ref,
                      *, n_chunk):
    step = pl.program_id(0)

    @pl.when(step == 0)
    def _():
        sp_ref[...] = jnp.zeros_like(sp_ref)

    @pl.when(step > 0)
    def _():
        sh1, sc1 = _split_mod(mod_ref[0])[:2]
        hb = _rms_mod(x_ref[...], nmix_ref[...], sh1, sc1).astype(BF16)
        zk = _dot(hb, wk_ref[...].astype(BF16)) * (HEAD_DIM ** -0.5)
        zv = _dot(hb, wv_ref[...].astype(BF16)).astype(BF16)
        cos2, sin2 = cos_ref[...], sin_ref[...]
        for hd in range(HEADS):
            lo = hd * HEAD_DIM
            tb = _decay_tables(dec_ref, hd)
            kh = _rope(zk[:, lo:lo + HEAD_DIM], cos2, sin2)
            kv_f, kv_b = [], []
            for c in range(n_chunk):
                kc = kh[c * CHUNK:(c + 1) * CHUNK]
                vc = zv[c * CHUNK:(c + 1) * CHUNK, lo:lo + HEAD_DIM]
                kv_f.append(_dot((kc * tb["kd_f"]).T.astype(BF16), vc))
                kv_b.append(_dot((kc * tb["kd_b"]).T.astype(BF16), vc))
            st = s0_ref[0, 0, 0, hd]
            for c in range(n_chunk):
                if c % UNIT_CHUNKS == 0:
                    sp_ref[c // UNIT_CHUNKS, 0, hd] = st
                st = tb["cd_f"] * st + kv_f[c]
            st = s0_ref[0, 0, 1, hd]
            for c in reversed(range(n_chunk)):
                if c % UNIT_CHUNKS == UNIT_CHUNKS - 1:
                    sp_ref[c // UNIT_CHUNKS, 1, hd] = st
                st = tb["cd_b"] * st + kv_b[c]


def _ret_state_call(x2, mod, s0, rope, nmix, w_in, dec, *, n_seq, seq_len, layer, even_index):
    n_chunk = seq_len // CHUNK
    units = seq_len // UNIT
    seq = lambda i: jnp.maximum(i - 1, 0)
    mod_row = lambda i: layer * COND_ROWS + 1 + seq(i)
    return pl.pallas_call(
        functools.partial(_ret_state_kernel, n_chunk=n_chunk),
        grid=(n_seq + 1,),
        in_specs=[
            pl.BlockSpec((seq_len, D_MODEL), lambda i: (seq(i), 0)),
            pl.BlockSpec((1, 1, N_MOD), lambda i: (mod_row(i), 0, 0)),
            pl.BlockSpec((1, 1, 2, HEADS, HEAD_DIM, HEAD_DIM), lambda i: (seq(i), 0, 0, 0, 0, 0)),
            _resident((seq_len, HEAD_DIM)), _resident((seq_len, HEAD_DIM)),
            _resident((1, D_MODEL)),
            pl.BlockSpec((None, D_MODEL, QK_W), lambda i: (even_index, 0, K_COL // QK_W),
                         pipeline_mode=pl.Buffered(1)),
            pl.BlockSpec((None, D_MODEL, V_W), lambda i: (even_index, 0, V_COL // V_W),
                         pipeline_mode=pl.Buffered(1)),
            _resident(dec.shape),
        ],
        out_specs=pl.BlockSpec((units, 2, HEADS, HEAD_DIM, HEAD_DIM), lambda i: (i, 0, 0, 0, 0)),
        out_shape=jax.ShapeDtypeStruct(((n_seq + 1) * units, 2, HEADS, HEAD_DIM, HEAD_DIM), F32),
        compiler_params=pltpu.CompilerParams(
            dimension_semantics=("arbitrary",),
            vmem_limit_bytes=V7X_VMEM_LIMIT_BYTES),
        name="ret_state",
    )(x2, mod, s0, rope[0], rope[1], nmix, w_in, w_in, dec)


def _even_kernel(xa_short, xa_long, xc_short, xc_long, moda_ref, modc_ref, sp_ref, cos_ref, sin_ref,
                 nmix_ref, nmlp_ref, w_s_ref, b_s_ref, vgain_ref, dec_ref, w_in_hbm, w_out_hbm, w1_hbm, w2_hbm,
                 y_ref, st_ref, ug_s, vg_s, q_s, k_s, v_s, gf_s, gb_s, cat_s, tab_s,
                 w_in_ref, w_out_ref, w1_ref, w2_ref, stage, sem, *, units, layer, even_index):
    n_units, n_short = units.total, units.n_short
    step = pl.program_id(0)
    slot_a = lax.rem(step, 2)
    slot_b = 1 - slot_a

    @pl.when(step == 0)
    def _():
        _load_weights_bf16(_weight_chunks(w_in_hbm.at[even_index], w_in_ref)
                           + _weight_chunks(w_out_hbm.at[even_index], w_out_ref)
                           + _weight_chunks(w1_hbm.at[layer], w1_ref)
                           + _weight_chunks(w2_hbm.at[layer], w2_ref), stage, sem)
        for hd in range(HEADS):
            tb = _decay_tables(dec_ref, hd)
            for n, name in enumerate(TABLE_NAMES):
                tab_s[hd, n] = tb[name]

    def stage_in():
        sh1, sc1 = _split_mod(moda_ref[0])[:2]
        x = _pick_block(step < n_short, xa_short, xa_long)
        hb = _rms_mod(x, nmix_ref[...], sh1, sc1).astype(BF16)
        z = _dot(hb, w_in_ref[...])
        yield
        ug_s[slot_a] = _gelu_tanh(z[:, 0:A_WIDTH])
        yield
        zv = _gelu_tanh(z[:, A_WIDTH:2 * A_WIDTH])
        vgn = [_layer_norm(zv[:, g * A_GROUP_DIM:(g + 1) * A_GROUP_DIM]) for g in range(A_GROUPS)]
        vg_s[slot_a] = (jnp.concatenate(vgn, axis=-1) * vgain_ref[...]).astype(BF16)
        yield
        cos2, sin2 = cos_ref[...], sin_ref[...]
        heads = lambda zz: jnp.concatenate(
            [_rope(zz[:, h * HEAD_DIM:(h + 1) * HEAD_DIM], cos2, sin2) for h in range(HEADS)], -1)
        q_s[slot_a] = heads(z[:, 2 * A_WIDTH:K_COL]).astype(BF16)
        yield
        k_s[slot_a] = heads(z[:, K_COL:V_COL] * (HEAD_DIM ** -0.5))
        v_s[slot_a] = z[:, V_COL:V_COL + V_W].astype(BF16)
        yield
        gf_s[slot_a] = _silu(z[:, V_COL + V_W:V_COL + 2 * V_W])
        yield
        gb_s[slot_a] = _silu(z[:, V_COL + 2 * V_W:V_COL + 3 * V_W])

    IN_LIGHT = 6

    def stage_mix(final_states):
        s = slot_b
        rows = [slice(c * CHUNK, (c + 1) * CHUNK) for c in range(UNIT_CHUNKS)]
        head_cols = [slice(hd * HEAD_DIM, (hd + 1) * HEAD_DIM) for hd in range(HEADS)]
        table = lambda hd, name: tab_s[hd, TABLE_NAMES.index(name)]

        sp2 = []
        for g in range(A_GROUPS):
            lo = g * A_GROUP_DIM
            vg2 = jnp.concatenate([vg_s[s, r, lo:lo + A_GROUP_DIM] for r in rows], axis=-1)
            sp2.append(_dot(w_s_ref[g], vg2))
        yield
        scores, kv_f, kv_b = [], [], []
        for hd in range(HEADS):
            kc = [k_s[s, r, head_cols[hd]] for r in rows]
            vc = [v_s[s, r, head_cols[hd]] for r in rows]
            scores.append([_dot_nt(q_s[s, r, head_cols[hd]], kc[c].astype(BF16)) for c, r in enumerate(rows)])
            kv_f.append([_dot((kc[c] * table(hd, "kd_f")).T.astype(BF16), vc[c]) for c in range(UNIT_CHUNKS)])
            kv_b.append([_dot((kc[c] * table(hd, "kd_b")).T.astype(BF16), vc[c]) for c in range(UNIT_CHUNKS)])
            yield
        for g in range(A_GROUPS):
            lo = g * A_GROUP_DIM
            for c, r in enumerate(rows):
                gate = sp2[g][:, c * CHUNK:(c + 1) * CHUNK] + b_s_ref[g]
                cat_s[s, r, lo:lo + A_GROUP_DIM] = (ug_s[s, r, lo:lo + A_GROUP_DIM] * gate).astype(BF16)
            if g % 2 == 1:
                yield
        y_raw = []
        for hd in range(HEADS):
            enter_f, st = [], sp_ref[0, 0, hd]
            for c in range(UNIT_CHUNKS):
                enter_f.append(st)
                st = table(hd, "cd_f") * st + kv_f[hd][c]
            final_f = st
            enter_b, st = [None] * UNIT_CHUNKS, sp_ref[0, 1, hd]
            for c in reversed(range(UNIT_CHUNKS)):
                enter_b[c] = st
                st = table(hd, "cd_b") * st + kv_b[hd][c]
            final_states.append((final_f, st))
            per_chunk = []
            for c, r in enumerate(rows):
                vc = v_s[s, r, head_cols[hd]]
                qf = q_s[s, r, head_cols[hd]].astype(F32)

                def direction(decay, qd, enter):
                    p = (scores[hd][c] * table(hd, decay)).astype(BF16)
                    lhs = jnp.concatenate([p, (qf * table(hd, qd)).astype(BF16)], axis=-1)
                    return _dot(lhs, jnp.concatenate([vc, enter.astype(BF16)], axis=0))

                per_chunk.append((direction("decay_f", "qd_f", enter_f[c]), direction("decay_b", "qd_b", enter_b[c])))
            y_raw.append(per_chunk)
            yield
        for hd in range(HEADS):
            for c, r in enumerate(rows):
                y_f, y_b = y_raw[hd][c]
                cols = slice(A_WIDTH + hd * HEAD_DIM, A_WIDTH + (hd + 1) * HEAD_DIM)
                cat_s[s, r, cols] = (gf_s[s, r, head_cols[hd]] * _layer_norm(y_f)
                                     + gb_s[s, r, head_cols[hd]] * _layer_norm(y_b)).astype(BF16)
            yield

    MIX_LIGHT = 1 + HEADS + A_GROUPS // 2 + 2 * HEADS

    def emit_states(final_states):
        @pl.when(step - 1 < n_short)
        def _():
            for hd, (final_f, final_b) in enumerate(final_states):
                st_ref[0, 0, 0, hd] = final_f
                st_ref[0, 0, 1, hd] = final_b

    def stage_mlp():
        _, _, g1, sh2, sc2, g2 = _split_mod(modc_ref[0])
        x = _pick_block(step - 2 < n_short, xc_short, xc_long) + g1 * _dot(cat_s[slot_a], w_out_ref[...])
        hb = _rms_mod(x, nmlp_ref[...], sh2, sc2).astype(BF16)
        yield
        acc = yield from _mlp_pieces(hb, w1_ref, w2_ref)
        y_ref[...] = x + g2 * acc

    @pl.when(step == 0)
    def _():
        _run(stage_in())

    @pl.when(step == 1)
    def _():
        a, finals = stage_in(), []
        _weave(_pieces(a, 1), [a, stage_mix(finals)], IN_LIGHT + MIX_LIGHT)
        emit_states(finals)

    @pl.when((step >= 2) & (step < n_units))
    def _():
        a, c, finals = stage_in(), stage_mlp(), []
        heavy = _pieces(c, 1) + _pieces(a, 1) + _pieces(c, MLP_DOT_PIECES + 1)
        _weave(heavy, [a, stage_mix(finals)], IN_LIGHT + MIX_LIGHT)
        emit_states(finals)

    @pl.when(step == n_units)
    def _():
        finals = []
        _weave(_pieces(stage_mlp(), MLP_DOT_PIECES + 2), [stage_mix(finals)], MIX_LIGHT)
        emit_states(finals)

    @pl.when(step == n_units + 1)
    def _():
        _run(stage_mlp())


_HBM = pl.BlockSpec(memory_space=pl.ANY)
_WEIGHT_STAGING = [pltpu.VMEM((STAGE_SLOTS, STAGE_ROWS, STAGE_COLS), F32), pltpu.SemaphoreType.DMA((STAGE_SLOTS,))]


def _even_call(x_short, x_long, mod, sp, rope, params, weights, units, *, layer, even_index):
    n_units = units.total
    unit_a = lambda i: units.clamp(i)
    unit_b = lambda i: units.clamp(i - 1)
    unit_c = lambda i: units.clamp(i - 2)
    mod_row = lambda unit: layer * COND_ROWS + units.mod_row(unit)
    state_block = (1, 1, 2, HEADS, HEAD_DIM, HEAD_DIM)

    in_specs = [pl.BlockSpec((UNIT, D_MODEL), lambda i: (units.short_block(unit_a(i)), 0)),
                pl.BlockSpec((UNIT, D_MODEL), lambda i: (units.long_block(unit_a(i)), 0)),
                pl.BlockSpec((UNIT, D_MODEL), lambda i: (units.short_block(unit_c(i)), 0)),
                pl.BlockSpec((UNIT, D_MODEL), lambda i: (units.long_block(unit_c(i)), 0)),
                pl.BlockSpec((1, 1, N_MOD), lambda i: (mod_row(unit_a(i)), 0, 0)),
                pl.BlockSpec((1, 1, N_MOD), lambda i: (mod_row(unit_c(i)), 0, 0)),
                pl.BlockSpec((1, 2, HEADS, HEAD_DIM, HEAD_DIM), lambda i: (units.state_block(unit_b(i)), 0, 0, 0, 0)),
                pl.BlockSpec((UNIT, HEAD_DIM), lambda i: (units.rope_block(unit_a(i)), 0)),
                pl.BlockSpec((UNIT, HEAD_DIM), lambda i: (units.rope_block(unit_a(i)), 0))]
    in_specs += [_resident(p.shape) for p in params] + [_HBM] * len(weights)
    scratch = [
        pltpu.VMEM((2, UNIT, A_WIDTH), F32),
        pltpu.VMEM((2, UNIT, A_WIDTH), BF16),
        pltpu.VMEM((2, UNIT, QK_W), BF16),
        pltpu.VMEM((2, UNIT, QK_W), F32),
        pltpu.VMEM((2, UNIT, V_W), BF16),
        pltpu.VMEM((2, UNIT, V_W), F32),
        pltpu.VMEM((2, UNIT, V_W), F32),
        pltpu.VMEM((2, UNIT, OUT_WIDTH), BF16),
        pltpu.VMEM((HEADS, len(TABLE_NAMES), CHUNK, CHUNK), F32),
    ]
    scratch += [pltpu.VMEM(w.shape[1:], BF16) for w in weights] + _WEIGHT_STAGING
    return pl.pallas_call(
        functools.partial(_even_kernel, units=units, layer=layer, even_index=even_index),
        grid=(n_units + 2,),
        in_specs=in_specs,
        out_specs=[pl.BlockSpec((UNIT, D_MODEL), lambda i: (unit_c(i), 0)),
                   pl.BlockSpec(state_block, lambda i: (units.short_block(unit_b(i)), 0, 0, 0, 0, 0))],
        out_shape=[jax.ShapeDtypeStruct((n_units * UNIT, D_MODEL), F32),
                   jax.ShapeDtypeStruct((units.n_short,) + state_block[1:], F32)],
        scratch_shapes=scratch,
        compiler_params=pltpu.CompilerParams(
            dimension_semantics=("arbitrary",),
            vmem_limit_bytes=V7X_VMEM_LIMIT_BYTES),
        name="even_layer",
    )(x_short, x_long, x_short, x_long, mod, mod, sp, rope[0], rope[1], *params, *weights)


def _odd_kernel(xp_ref, x_ref, xn_ref, moda_ref, modc_ref, nmix_ref, nmlp_ref, fin_ref, pscale_ref,
                w_pool_hbm, w1_hbm, w2_hbm, y_short, y_long, x1_s, hb_s, w_pool_ref, w1_ref, w2_ref, stage, sem,
                *, n_units, n_short, short_len, long_len, layer, odd_index):
    step = pl.program_id(0)
    slot_a = lax.rem(step, 2)
    slot_b = 1 - slot_a
    unit = x_ref.shape[0]
    rows_w = unit + 2 * POOL_HALO

    @pl.when(step == 0)
    def _():
        _load_weights_bf16(_weight_chunks(w_pool_hbm.at[odd_index], w_pool_ref)
                           + _weight_chunks(w1_hbm.at[layer], w1_ref)
                           + _weight_chunks(w2_hbm.at[layer], w2_ref), stage, sem)

    def stage_pool():
        sh1, sc1, g1, sh2, sc2, _ = _split_mod(moda_ref[0])
        xw = jnp.concatenate([xp_ref[...], x_ref[...], xn_ref[...]], axis=0)
        h = _rms_mod(xw, nmix_ref[...], sh1, sc1)
        seq_len = jnp.where(step < n_short, short_len, long_len)
        first = jnp.minimum(step, n_units - 1) * unit - POOL_HALO
        t = (lax.broadcasted_iota(jnp.int32, (rows_w, 1), 0) + first) & (seq_len - 1)
        yield

        def shift_down(a, k):
            return jnp.where(t >= k, pltpu.roll(a, k, axis=0), 0.0)

        def shift_up(a, k):
            return jnp.where(t < seq_len - k, pltpu.roll(a, rows_w - k, axis=0), 0.0)

        outs = []
        for g, w in enumerate(POOL_WINDOWS):
            hg = h[:, g * POOL_DIM:(g + 1) * POOL_DIM]
            half = w // 2
            left = shift_down(hg, 1)
            right = hg
            span = 1
            while span < half:
                left = left + shift_down(left, span)
                right = right + shift_up(right, span)
                span *= 2
            cnt = (jnp.minimum(t + half, seq_len) - jnp.maximum(t - half, 0)).astype(F32)
            pooled = ((left + right) / cnt - hg)[POOL_HALO:POOL_HALO + unit].astype(BF16)
            outs.append(_dot(pooled, w_pool_ref[g * POOL_DIM:(g + 1) * POOL_DIM, :]))
            yield
        x1 = x_ref[...] + g1 * (jnp.concatenate(outs, axis=-1) * pscale_ref[...])
        x1_s[slot_a] = x1
        hb_s[slot_a] = _rms_mod(x1, nmlp_ref[...], sh2, sc2).astype(BF16)

    def stage_mlp(result):
        g2 = _split_mod(modc_ref[0])[5]
        acc = yield from _mlp_pieces(hb_s[slot_b], w1_ref, w2_ref)
        result.append(_rms(x1_s[slot_b] + g2 * acc) * fin_ref[...])

    def store(result):
        @pl.when(step - 1 < n_short)
        def _():
            y_short[...] = result[0]

        @pl.when(step - 1 >= n_short)
        def _():
            y_long[...] = result[0]

    pool_light = 2 + len(POOL_WINDOWS)

    @pl.when(step == 0)
    def _():
        _run(stage_pool())

    @pl.when((step >= 1) & (step < n_units))
    def _():
        result = []
        _weave(_pieces(stage_mlp(result), MLP_DOT_PIECES + 1), [stage_pool()], pool_light)
        store(result)

    @pl.when(step == n_units)
    def _():
        result = []
        _run(stage_mlp(result))
        store(result)


def _odd_call(x2, mod, params, weights, units, *, unit, short_len, long_len, layer, odd_index):
    n_units = units.total
    halo_per_unit = unit // POOL_HALO
    last_halo = x2.shape[0] // POOL_HALO - 1
    unit_a = lambda i: units.clamp(i)
    unit_c = lambda i: units.clamp(i - 1)
    mod_row = lambda u: layer * COND_ROWS + units.mod_row(u)
    assert (units.n_short * unit) % long_len == 0
    in_specs = [pl.BlockSpec((POOL_HALO, D_MODEL), lambda i: (jnp.maximum(unit_a(i) * halo_per_unit - 1, 0), 0)),
                pl.BlockSpec((unit, D_MODEL), lambda i: (unit_a(i), 0)),
                pl.BlockSpec((POOL_HALO, D_MODEL),
                             lambda i: (jnp.minimum((unit_a(i) + 1) * halo_per_unit, last_halo), 0)),
                pl.BlockSpec((1, 1, N_MOD), lambda i: (mod_row(unit_a(i)), 0, 0)),
                pl.BlockSpec((1, 1, N_MOD), lambda i: (mod_row(unit_c(i)), 0, 0))]
    in_specs += [_resident(p.shape) for p in params] + [_HBM] * len(weights)
    scratch = [pltpu.VMEM((2, unit, D_MODEL), F32),
               pltpu.VMEM((2, unit, D_MODEL), BF16)]
    scratch += [pltpu.VMEM(w.shape[1:], BF16) for w in weights] + _WEIGHT_STAGING
    return pl.pallas_call(
        functools.partial(_odd_kernel, n_units=n_units, n_short=units.n_short,
                          short_len=short_len, long_len=long_len, layer=layer, odd_index=odd_index),
        grid=(n_units + 1,),
        in_specs=in_specs,
        out_specs=[pl.BlockSpec((unit, D_MODEL), lambda i: (units.short_block(unit_c(i)), 0)),
                   pl.BlockSpec((unit, D_MODEL), lambda i: (units.long_block(unit_c(i)), 0))],
        out_shape=[jax.ShapeDtypeStruct((units.n_short * unit, D_MODEL), F32),
                   jax.ShapeDtypeStruct((units.n_long * unit, D_MODEL), F32)],
        scratch_shapes=scratch,
        compiler_params=pltpu.CompilerParams(
            dimension_semantics=("arbitrary",),
            vmem_limit_bytes=V7X_VMEM_LIMIT_BYTES),
        name="odd_layer",
    )(x2, x2, x2, mod, mod, *params, *weights)


def _rope_tables(n_tokens):
    f32 = np.float32
    rows = n_tokens // GRID_W
    pos_r = np.repeat(np.arange(rows, dtype=f32), GRID_W)
    pos_c = np.tile(np.arange(GRID_W, dtype=f32), rows)
    inv = (f32(ROPE_BASE) ** (-np.arange(ROPE_PAIRS_AXIS, dtype=f32) / f32(ROPE_PAIRS_AXIS))).astype(f32)
    ang = np.concatenate([pos_r[:, None] * inv, pos_c[:, None] * inv], axis=-1).astype(f32)
    cos, sin = np.cos(ang).astype(f32), np.sin(ang).astype(f32)
    return np.concatenate([cos, cos], axis=-1), np.concatenate([-sin, sin], axis=-1)


def kernel(x_prompt, x_sample, state_ret, c, c_ctx, ada_w, ada_b, norm_mix, norm_mlp, mlp_w1, mlp_w2,
           ev_w_in, ev_w_s, ev_b_s, ev_vnorm, ev_decay, ev_w_out, od_w_pool, od_pool_scale, final_norm):
    n_prompt, prompt_len, _ = x_prompt.shape
    n_sample, sample_len, _ = x_sample.shape
    assert DEPTH == 2 and n_sample + 1 <= COND_ROWS
    assert prompt_len == UNIT and sample_len % UNIT == 0
    units = _Units(n_prompt, n_sample, sample_len // UNIT)

    cond = jnp.concatenate([c_ctx[None, :], c, jnp.zeros((COND_ROWS - 1 - n_sample, D_MODEL), F32)])
    mod = _ada_call(cond, ada_w, ada_b).reshape(DEPTH * COND_ROWS, 1, N_MOD)

    row = lambda v: v.reshape(1, -1)
    dec = jnp.broadcast_to(ev_decay[0].reshape(2 * HEADS, 1), (2 * HEADS, CHUNK))
    even_params = (
        row(norm_mix[0]), row(norm_mlp[0]),
        ev_w_s[0].astype(BF16),
        jnp.broadcast_to(ev_b_s[0][:, :, None], (A_GROUPS, CHUNK, A_GROUP_DIM)),
        row(ev_vnorm[0]), dec,
    )
    odd_params = (row(norm_mix[1]), row(norm_mlp[1]), row(final_norm), row(od_pool_scale[0]))
    even_weights = (ev_w_in, ev_w_out, mlp_w1, mlp_w2)
    odd_weights = (od_w_pool.reshape(od_w_pool.shape[0], D_MODEL, POOL_DIM), mlp_w1, mlp_w2)

    xp2 = x_prompt.reshape(n_prompt * prompt_len, D_MODEL)
    xs2 = x_sample.reshape(n_sample * sample_len, D_MODEL)
    cos2, sin2 = _rope_tables(sample_len)
    sp = _ret_state_call(xs2, mod, state_ret, (cos2, sin2), row(norm_mix[0]), ev_w_in, dec,
                         n_seq=n_sample, seq_len=sample_len, layer=0, even_index=0)
    rope = (np.concatenate([np.ones((UNIT, HEAD_DIM), np.float32), cos2]),
            np.concatenate([np.zeros((UNIT, HEAD_DIM), np.float32), sin2]))
    x_all, new_state = _even_call(xp2, xs2, mod, sp, rope, even_params, even_weights, units,
                                  layer=0, even_index=0)
    y_prompt, y_sample = _odd_call(x_all, mod, odd_params, odd_weights, units, unit=UNIT,
                                   short_len=prompt_len, long_len=sample_len, layer=1, odd_index=0)

    return (y_prompt.reshape(n_prompt, prompt_len, D_MODEL),
            y_sample.reshape(n_sample, sample_len, D_MODEL),
            new_state)
```

```python
import functools

import jax
import jax.numpy as jnp
import numpy as np
from jax import lax
from jax.experimental import pallas as pl
from jax.experimental.pallas import tpu as pltpu

D_MODEL = 1024
DEPTH = 2
GRID_W = 64
A_WIDTH = D_MODEL // 2
A_GROUPS = 4
A_GROUP_DIM = A_WIDTH // A_GROUPS
CHUNK = 128
HEADS = 4
HEAD_DIM = (D_MODEL // 2) // HEADS
ROPE_BASE = 10000.0
ROPE_PAIRS_AXIS = HEAD_DIM // 4
POOL_WINDOWS = (2, 4, 8, 16)
POOL_DIM = D_MODEL // len(POOL_WINDOWS)
POOL_HALO = max(POOL_WINDOWS) // 2
D_FF = 4 * D_MODEL
EPS = 1e-6
QK_W = HEADS * HEAD_DIM
V_W = HEADS * HEAD_DIM
IN_WIDTH = 2 * A_WIDTH + 2 * QK_W + 3 * V_W
OUT_WIDTH = A_WIDTH + V_W
N_MOD = 6 * D_MODEL
K_COL = 2 * A_WIDTH + QK_W
V_COL = K_COL + QK_W

UNIT = 2 * CHUNK
UNIT_CHUNKS = UNIT // CHUNK
FF_CHUNK = 1024
ADA_BLOCK_N = 3072
COND_ROWS = 8
STAGE_ROWS = 512
STAGE_COLS = 1024
STAGE_SLOTS = 3
V7X_VMEM_LIMIT_BYTES = 60 * 1024 * 1024

F32 = jnp.float32
BF16 = jnp.bfloat16


def _dot(a, b):
    return jnp.dot(a, b, preferred_element_type=F32)


def _dot_nt(a, b):
    return lax.dot_general(a, b, (((1,), (1,)), ((), ())), preferred_element_type=F32)


def _silu(x):
    return x * jax.nn.sigmoid(x)


def _gelu_tanh(x):
    return 0.5 * x * (1.0 + jnp.tanh(0.7978845608028654 * (x + 0.044715 * (x * x * x))))


def _rms(x):
    return x * lax.rsqrt(jnp.mean(x * x, axis=-1, keepdims=True) + EPS)


def _rms_mod(x, gain, shift, scale):
    return (_rms(x) * gain) * (1.0 + scale) + shift


def _layer_norm(x):
    mu = jnp.mean(x, axis=-1, keepdims=True)
    d = x - mu
    return d * lax.rsqrt(jnp.mean(d * d, axis=-1, keepdims=True) + EPS)


def _split_mod(mod):
    return [mod[:, i * D_MODEL:(i + 1) * D_MODEL] for i in range(6)]


def _rope(x, cos2, sin2):
    return x * cos2 + pltpu.roll(x, HEAD_DIM // 2, axis=1) * sin2


def _pick_block(take_first, a_ref, b_ref):
    rows = lax.broadcasted_iota(jnp.int32, (a_ref.shape[0], 1), 0)
    return jnp.where(rows < jnp.where(take_first, a_ref.shape[0], 0), a_ref[...], b_ref[...])


def _run(gen, n=None):
    done = 0
    while n is None or done < n:
        try:
            next(gen)
        except StopIteration:
            break
        done += 1
    return done


def _weave(heavy, lights, n_light):
    heavy = list(heavy)
    pending = _roundrobin(lights)
    ran = 0
    for k, piece in enumerate(heavy):
        piece()
        want = ((k + 1) * n_light + len(heavy) - 1) // len(heavy)
        ran += _run(pending, want - ran)
    _run(pending)


def _pieces(gen, n):
    return [functools.partial(next, gen, None) for _ in range(n)]


def _roundrobin(gens):
    live = list(gens)
    while live:
        for g in list(live):
            try:
                next(g)
            except StopIteration:
                live.remove(g)
                continue
            yield


MLP_DOT_PIECES = 2 * (D_FF // FF_CHUNK)


def _mlp_pieces(hb, w1_ref, w2_ref):
    n_ff = D_FF // FF_CHUNK
    up = lambda j: _dot(hb, w1_ref[:, j * FF_CHUNK:(j + 1) * FF_CHUNK])
    a = up(0)
    yield
    acc = None
    for j in range(n_ff):
        a_next = None
        if j + 1 < n_ff:
            a_next = up(j + 1)
            yield
        p = _dot(jnp.square(jnp.maximum(a, 0.0)).astype(BF16), w2_ref[j * FF_CHUNK:(j + 1) * FF_CHUNK, :])
        acc = p if acc is None else acc + p
        a = a_next
        yield
    return acc


def _resident(shape):
    return pl.BlockSpec(shape, lambda i: (0,) * len(shape), pipeline_mode=pl.Buffered(1))


def _weight_chunks(w_hbm, w_vmem):
    rows, cols = w_hbm.shape
    jobs = []
    for r0 in range(0, rows, STAGE_ROWS):
        for c0 in range(0, cols, STAGE_COLS):
            window = (pl.ds(r0, min(STAGE_ROWS, rows - r0)), pl.ds(c0, min(STAGE_COLS, cols - c0)))
            jobs.append((w_hbm.at[window], w_vmem.at[window]))
    return jobs


def _load_weights_bf16(jobs, stage, sem):
    n_slots = stage.shape[0]

    def copy(k):
        r, c = jobs[k][0].shape
        slot = k % n_slots
        return pltpu.make_async_copy(jobs[k][0], stage.at[slot, pl.ds(0, r), pl.ds(0, c)], sem.at[slot])

    for k in range(min(n_slots - 1, len(jobs))):
        copy(k).start()
    for k, (src, dst) in enumerate(jobs):
        if k + n_slots - 1 < len(jobs):
            copy(k + n_slots - 1).start()
        copy(k).wait()
        r, c = src.shape
        dst[...] = stage[k % n_slots, 0:r, 0:c].astype(BF16)


class _Units:
    def __init__(self, n_short, n_long, units_per_long):
        self.n_short, self.n_long, self.units_per_long = n_short, n_long * units_per_long, units_per_long
        self.total = self.n_short + self.n_long

    def clamp(self, unit):
        return jnp.clip(unit, 0, self.total - 1)

    def short_block(self, unit):
        return jnp.minimum(unit, self.n_short - 1)

    def long_block(self, unit):
        return jnp.clip(unit - self.n_short, 0, self.n_long - 1)

    def mod_row(self, unit):
        return jnp.where(unit < self.n_short, 0, 1 + (unit - self.n_short) // self.units_per_long)

    def rope_block(self, unit):
        return jnp.where(unit < self.n_short, 0, 1 + (unit - self.n_short) % self.units_per_long)

    def state_block(self, unit):
        return jnp.where(unit < self.n_short, 0, self.units_per_long + unit - self.n_short)


def _ada_kernel(cond_ref, w_ref, b_ref, out_ref):
    s = _silu(cond_ref[...]).astype(BF16)
    rows = _dot(s, w_ref[0].astype(BF16)) + b_ref[0]
    for r in range(COND_ROWS):
        out_ref[r] = rows[r:r + 1, :]


def _ada_call(cond, ada_w, ada_b):
    return pl.pallas_call(
        _ada_kernel,
        grid=(DEPTH, N_MOD // ADA_BLOCK_N),
        in_specs=[
            pl.BlockSpec((COND_ROWS, D_MODEL), lambda i, j: (0, 0)),
            pl.BlockSpec((1, D_MODEL, ADA_BLOCK_N), lambda i, j: (i, 0, j)),
            pl.BlockSpec((1, 1, ADA_BLOCK_N), lambda i, j: (i, 0, j)),
        ],
        out_specs=pl.BlockSpec((COND_ROWS, 1, ADA_BLOCK_N), lambda i, j: (i, 0, j)),
        out_shape=jax.ShapeDtypeStruct((DEPTH * COND_ROWS, 1, N_MOD), F32),
        compiler_params=pltpu.CompilerParams(
            dimension_semantics=("arbitrary", "arbitrary"),
            vmem_limit_bytes=V7X_VMEM_LIMIT_BYTES),
        name="ada_mod",
    )(cond, ada_w, ada_b.reshape(DEPTH, 1, N_MOD))


def _decay_tables(dec_ref, hd):
    row = lax.broadcasted_iota(jnp.int32, (CHUNK, CHUNK), 0).astype(F32)
    col = lax.broadcasted_iota(jnp.int32, (CHUNK, CHUNK), 1).astype(F32)
    lg_f = jnp.broadcast_to(-jnp.exp(dec_ref[hd:hd + 1, :]), (CHUNK, CHUNK))
    lg_b = jnp.broadcast_to(-jnp.exp(dec_ref[HEADS + hd:HEADS + hd + 1, :]), (CHUNK, CHUNK))
    d_f = row - col
    d_b = col - row
    return dict(
        decay_f=jnp.where(d_f >= 0, jnp.exp(lg_f * jnp.maximum(d_f, 0.0)), 0.0),
        decay_b=jnp.where(d_b >= 0, jnp.exp(lg_b * jnp.maximum(d_b, 0.0)), 0.0),
        qd_f=jnp.exp(lg_f * (row + 1.0)),
        qd_b=jnp.exp(lg_b * (CHUNK - row)),
        kd_f=jnp.exp(lg_f * (CHUNK - 1.0 - row)),
        kd_b=jnp.exp(lg_b * row),
        cd_f=jnp.exp(lg_f * CHUNK),
        cd_b=jnp.exp(lg_b * CHUNK),
    )


TABLE_NAMES = ("decay_f", "decay_b", "qd_f", "qd_b", "kd_f", "kd_b", "cd_f", "cd_b")


def _ret_state_kernel(x_ref, mod_ref, s0_ref, cos_ref, sin_ref, nmix_ref, wk_ref, wv_ref, dec_ref, sp_ref,
                      *, n_chunk):
    step = pl.program_id(0)

    @pl.when(step == 0)
    def _():
        sp_ref[...] = jnp.zeros_like(sp_ref)

    @pl.when(step > 0)
    def _():
        sh1, sc1 = _split_mod(mod_ref[0])[:2]
        hb = _rms_mod(x_ref[...], nmix_ref[...], sh1, sc1).astype(BF16)
        zk = _dot(hb, wk_ref[...].astype(BF16)) * (HEAD_DIM ** -0.5)
        zv = _dot(hb, wv_ref[...].astype(BF16)).astype(BF16)
        cos2, sin2 = cos_ref[...], sin_ref[...]
        for hd in range(HEADS):
            lo = hd * HEAD_DIM
            tb = _decay_tables(dec_ref, hd)
            kh = _rope(zk[:, lo:lo + HEAD_DIM], cos2, sin2)
            kv_f, kv_b = [], []
            for c in range(n_chunk):
                kc = kh[c * CHUNK:(c + 1) * CHUNK]
                vc = zv[c * CHUNK:(c + 1) * CHUNK, lo:lo + HEAD_DIM]
                kv_f.append(_dot((kc * tb["kd_f"]).T.astype(BF16), vc))
                kv_b.append(_dot((kc * tb["kd_b"]).T.astype(BF16), vc))
            st = s0_ref[0, 0, 0, hd]
            for c in range(n_chunk):
                if c % UNIT_CHUNKS == 0:
                    sp_ref[c // UNIT_CHUNKS, 0, hd] = st
                st = tb["cd_f"] * st + kv_f[c]
            st = s0_ref[0, 0, 1, hd]
            for c in reversed(range(n_chunk)):
                if c % UNIT_CHUNKS == UNIT_CHUNKS - 1:
                    sp_ref[c // UNIT_CHUNKS, 1, hd] = st
                st = tb["cd_b"] * st + kv_b[c]


def _ret_state_call(x2, mod, s0, rope, nmix, w_in, dec, *, n_seq, seq_len, layer, even_index):
    n_chunk = seq_len // CHUNK
    units = seq_len // UNIT
    seq = lambda i: jnp.maximum(i - 1, 0)
    mod_row = lambda i: layer * COND_ROWS + 1 + seq(i)
    return pl.pallas_call(
        functools.partial(_ret_state_kernel, n_chunk=n_chunk),
        grid=(n_seq + 1,),
        in_specs=[
            pl.BlockSpec((seq_len, D_MODEL), lambda i: (seq(i), 0)),
            pl.BlockSpec((1, 1, N_MOD), lambda i: (mod_row(i), 0, 0)),
            pl.BlockSpec((1, 1, 2, HEADS, HEAD_DIM, HEAD_DIM), lambda i: (seq(i), 0, 0, 0, 0, 0)),
            _resident((seq_len, HEAD_DIM)), _resident((seq_len, HEAD_DIM)),
            _resident((1, D_MODEL)),
            pl.BlockSpec((None, D_MODEL, QK_W), lambda i: (even_index, 0, K_COL // QK_W),
                         pipeline_mode=pl.Buffered(1)),
            pl.BlockSpec((None, D_MODEL, V_W), lambda i: (even_index, 0, V_COL // V_W),
                         pipeline_mode=pl.Buffered(1)),
            _resident(dec.shape),
        ],
        out_specs=pl.BlockSpec((units, 2, HEADS, HEAD_DIM, HEAD_DIM), lambda i: (i, 0, 0, 0, 0)),
        out_shape=jax.ShapeDtypeStruct(((n_seq + 1) * units, 2, HEADS, HEAD_DIM, HEAD_DIM), F32),
        compiler_params=pltpu.CompilerParams(
            dimension_semantics=("arbitrary",),
            vmem_limit_bytes=V7X_VMEM_LIMIT_BYTES),
        name="ret_state",
    )(x2, mod, s0, rope[0], rope[1], nmix, w_in, w_in, dec)


def _even_kernel(xa_short, xa_long, xc_short, xc_long, moda_ref, modc_ref, sp_ref, cos_ref, sin_ref,
                 nmix_ref, nmlp_ref, b_s_ref, vgain_ref, dec_ref, w_in_hbm, w_out_hbm, w1_hbm, w2_hbm, w_s_hbm,
                 y_ref, st_ref, ug_s, vg_s, q_s, k_s, v_s, gf_s, gb_s, cat_s, tab_s,
                 w_in_ref, w_out_ref, w1_ref, w2_ref, w_s_ref, stage, sem, *, units, layer, even_index):
    n_units, n_short = units.total, units.n_short
    step = pl.program_id(0)
    slot_a = lax.rem(step, 2)
    slot_b = 1 - slot_a

    @pl.when(step == 0)
    def _():
        _load_weights_bf16(_weight_chunks(w_in_hbm.at[even_index], w_in_ref)
                           + _weight_chunks(w_out_hbm.at[even_index], w_out_ref)
                           + _weight_chunks(w1_hbm.at[layer], w1_ref)
                           + _weight_chunks(w2_hbm.at[layer], w2_ref)
                           + sum((_weight_chunks(w_s_hbm.at[even_index, g], w_s_ref.at[g])
                                  for g in range(A_GROUPS)), []), stage, sem)
        for hd in range(HEADS):
            tb = _decay_tables(dec_ref, hd)
            for n, name in enumerate(TABLE_NAMES):
                tab_s[hd, n] = tb[name]

    def stage_in():
        sh1, sc1 = _split_mod(moda_ref[0])[:2]
        x = _pick_block(step < n_short, xa_short, xa_long)
        hb = _rms_mod(x, nmix_ref[...], sh1, sc1).astype(BF16)
        z = _dot(hb, w_in_ref[...])
        yield
        ug_s[slot_a] = _gelu_tanh(z[:, 0:A_WIDTH])
        yield
        zv = _gelu_tanh(z[:, A_WIDTH:2 * A_WIDTH])
        vgn = [_layer_norm(zv[:, g * A_GROUP_DIM:(g + 1) * A_GROUP_DIM]) for g in range(A_GROUPS)]
        vg_s[slot_a] = (jnp.concatenate(vgn, axis=-1) * vgain_ref[...]).astype(BF16)
        yield
        cos2, sin2 = cos_ref[...], sin_ref[...]
        heads = lambda zz: jnp.concatenate(
            [_rope(zz[:, h * HEAD_DIM:(h + 1) * HEAD_DIM], cos2, sin2) for h in range(HEADS)], -1)
        q_s[slot_a] = heads(z[:, 2 * A_WIDTH:K_COL]).astype(BF16)
        yield
        k_s[slot_a] = heads(z[:, K_COL:V_COL] * (HEAD_DIM ** -0.5))
        v_s[slot_a] = z[:, V_COL:V_COL + V_W].astype(BF16)
        yield
        gf_s[slot_a] = _silu(z[:, V_COL + V_W:V_COL + 2 * V_W])
        yield
        gb_s[slot_a] = _silu(z[:, V_COL + 2 * V_W:V_COL + 3 * V_W])

    IN_LIGHT = 6

    def stage_mix(final_states):
        s = slot_b
        rows = [slice(c * CHUNK, (c + 1) * CHUNK) for c in range(UNIT_CHUNKS)]
        head_cols = [slice(hd * HEAD_DIM, (hd + 1) * HEAD_DIM) for hd in range(HEADS)]
        table = lambda hd, name: tab_s[hd, TABLE_NAMES.index(name)]

        sp2 = []
        for g in range(A_GROUPS):
            lo = g * A_GROUP_DIM
            vg2 = jnp.concatenate([vg_s[s, r, lo:lo + A_GROUP_DIM] for r in rows], axis=-1)
            sp2.append(_dot(w_s_ref[g], vg2))
        yield
        scores, kv_f, kv_b = [], [], []
        for hd in range(HEADS):
            kc = [k_s[s, r, head_cols[hd]] for r in rows]
            vc = [v_s[s, r, head_cols[hd]] for r in rows]
            scores.append([_dot_nt(q_s[s, r, head_cols[hd]], kc[c].astype(BF16)) for c, r in enumerate(rows)])
            kv_f.append([_dot((kc[c] * table(hd, "kd_f")).T.astype(BF16), vc[c]) for c in range(UNIT_CHUNKS)])
            kv_b.append([_dot((kc[c] * table(hd, "kd_b")).T.astype(BF16), vc[c]) for c in range(UNIT_CHUNKS)])
            yield
        for g in range(A_GROUPS):
            lo = g * A_GROUP_DIM
            for c, r in enumerate(rows):
                gate = sp2[g][:, c * CHUNK:(c + 1) * CHUNK] + b_s_ref[g]
                cat_s[s, r, lo:lo + A_GROUP_DIM] = (ug_s[s, r, lo:lo + A_GROUP_DIM] * gate).astype(BF16)
            if g % 2 == 1:
                yield
        y_raw = []
        for hd in range(HEADS):
            enter_f, st = [], sp_ref[0, 0, hd]
            for c in range(UNIT_CHUNKS):
                enter_f.append(st)
                st = table(hd, "cd_f") * st + kv_f[hd][c]
            final_f = st
            enter_b, st = [None] * UNIT_CHUNKS, sp_ref[0, 1, hd]
            for c in reversed(range(UNIT_CHUNKS)):
                enter_b[c] = st
                st = table(hd, "cd_b") * st + kv_b[hd][c]
            final_states.append((final_f, st))
            per_chunk = []
            for c, r in enumerate(rows):
                vc = v_s[s, r, head_cols[hd]]
                qf = q_s[s, r, head_cols[hd]].astype(F32)

                def direction(decay, qd, enter):
                    p = (scores[hd][c] * table(hd, decay)).astype(BF16)
                    lhs = jnp.concatenate([p, (qf * table(hd, qd)).astype(BF16)], axis=-1)
                    return _dot(lhs, jnp.concatenate([vc, enter.astype(BF16)], axis=0))

                per_chunk.append((direction("decay_f", "qd_f", enter_f[c]), direction("decay_b", "qd_b", enter_b[c])))
            y_raw.append(per_chunk)
            yield
        for hd in range(HEADS):
            for c, r in enumerate(rows):
                y_f, y_b = y_raw[hd][c]
                cols = slice(A_WIDTH + hd * HEAD_DIM, A_WIDTH + (hd + 1) * HEAD_DIM)
                cat_s[s, r, cols] = (gf_s[s, r, head_cols[hd]] * _layer_norm(y_f)
                                     + gb_s[s, r, head_cols[hd]] * _layer_norm(y_b)).astype(BF16)
            yield

    MIX_LIGHT = 1 + HEADS + A_GROUPS // 2 + 2 * HEADS

    def emit_states(final_states):
        @pl.when(step - 1 < n_short)
        def _():
            for hd, (final_f, final_b) in enumerate(final_states):
                st_ref[0, 0, 0, hd] = final_f
                st_ref[0, 0, 1, hd] = final_b

    def stage_mlp():
        _, _, g1, sh2, sc2, g2 = _split_mod(modc_ref[0])
        x = _pick_block(step - 2 < n_short, xc_short, xc_long) + g1 * _dot(cat_s[slot_a], w_out_ref[...])
        hb = _rms_mod(x, nmlp_ref[...], sh2, sc2).astype(BF16)
        yield
        acc = yield from _mlp_pieces(hb, w1_ref, w2_ref)
        y_ref[...] = x + g2 * acc

    @pl.when(step == 0)
    def _():
        _run(stage_in())

    @pl.when(step == 1)
    def _():
        a, finals = stage_in(), []
        _weave(_pieces(a, 1), [a, stage_mix(finals)], IN_LIGHT + MIX_LIGHT)
        emit_states(finals)

    @pl.when((step >= 2) & (step < n_units))
    def _():
        a, c, finals = stage_in(), stage_mlp(), []
        heavy = _pieces(c, 1) + _pieces(a, 1) + _pieces(c, MLP_DOT_PIECES + 1)
        _weave(heavy, [a, stage_mix(finals)], IN_LIGHT + MIX_LIGHT)
        emit_states(finals)

    @pl.when(step == n_units)
    def _():
        finals = []
        _weave(_pieces(stage_mlp(), MLP_DOT_PIECES + 2), [stage_mix(finals)], MIX_LIGHT)
        emit_states(finals)

    @pl.when(step == n_units + 1)
    def _():
        _run(stage_mlp())


_HBM = pl.BlockSpec(memory_space=pl.ANY)
_WEIGHT_STAGING = [pltpu.VMEM((STAGE_SLOTS, STAGE_ROWS, STAGE_COLS), F32), pltpu.SemaphoreType.DMA((STAGE_SLOTS,))]


def _even_call(x_short, x_long, mod, sp, rope, params, weights, units, *, layer, even_index):
    n_units = units.total
    unit_a = lambda i: units.clamp(i)
    unit_b = lambda i: units.clamp(i - 1)
    unit_c = lambda i: units.clamp(i - 2)
    mod_row = lambda unit: layer * COND_ROWS + units.mod_row(unit)
    state_block = (1, 1, 2, HEADS, HEAD_DIM, HEAD_DIM)

    in_specs = [pl.BlockSpec((UNIT, D_MODEL), lambda i: (units.short_block(unit_a(i)), 0)),
                pl.BlockSpec((UNIT, D_MODEL), lambda i: (units.long_block(unit_a(i)), 0)),
                pl.BlockSpec((UNIT, D_MODEL), lambda i: (units.short_block(unit_c(i)), 0)),
                pl.BlockSpec((UNIT, D_MODEL), lambda i: (units.long_block(unit_c(i)), 0)),
                pl.BlockSpec((1, 1, N_MOD), lambda i: (mod_row(unit_a(i)), 0, 0)),
                pl.BlockSpec((1, 1, N_MOD), lambda i: (mod_row(unit_c(i)), 0, 0)),
                pl.BlockSpec((1, 2, HEADS, HEAD_DIM, HEAD_DIM), lambda i: (units.state_block(unit_b(i)), 0, 0, 0, 0)),
                pl.BlockSpec((UNIT, HEAD_DIM), lambda i: (units.rope_block(unit_a(i)), 0)),
                pl.BlockSpec((UNIT, HEAD_DIM), lambda i: (units.rope_block(unit_a(i)), 0))]
    in_specs += [_resident(p.shape) for p in params] + [_HBM] * len(weights)
    scratch = [
        pltpu.VMEM((2, UNIT, A_WIDTH), F32),
        pltpu.VMEM((2, UNIT, A_WIDTH), BF16),
        pltpu.VMEM((2, UNIT, QK_W), BF16),
        pltpu.VMEM((2, UNIT, QK_W), F32),
        pltpu.VMEM((2, UNIT, V_W), BF16),
        pltpu.VMEM((2, UNIT, V_W), F32),
        pltpu.VMEM((2, UNIT, V_W), F32),
        pltpu.VMEM((2, UNIT, OUT_WIDTH), BF16),
        pltpu.VMEM((HEADS, len(TABLE_NAMES), CHUNK, CHUNK), F32),
    ]
    scratch += [pltpu.VMEM(w.shape[1:], BF16) for w in weights] + _WEIGHT_STAGING
    return pl.pallas_call(
        functools.partial(_even_kernel, units=units, layer=layer, even_index=even_index),
        grid=(n_units + 2,),
        in_specs=in_specs,
        out_specs=[pl.BlockSpec((UNIT, D_MODEL), lambda i: (unit_c(i), 0)),
                   pl.BlockSpec(state_block, lambda i: (units.short_block(unit_b(i)), 0, 0, 0, 0, 0))],
        out_shape=[jax.ShapeDtypeStruct((n_units * UNIT, D_MODEL), F32),
                   jax.ShapeDtypeStruct((units.n_short,) + state_block[1:], F32)],
        scratch_shapes=scratch,
        compiler_params=pltpu.CompilerParams(
            dimension_semantics=("arbitrary",),
            vmem_limit_bytes=V7X_VMEM_LIMIT_BYTES),
        name="even_layer",
    )(x_short, x_long, x_short, x_long, mod, mod, sp, rope[0], rope[1], *params, *weights)


def _odd_kernel(xp_ref, x_ref, xn_ref, moda_ref, modc_ref, nmix_ref, nmlp_ref, fin_ref, pscale_ref,
                w_pool_hbm, w1_hbm, w2_hbm, y_short, y_long, x1_s, hb_s, w_pool_ref, w1_ref, w2_ref, stage, sem,
                *, n_units, n_short, short_len, long_len, layer, odd_index):
    step = pl.program_id(0)
    slot_a = lax.rem(step, 2)
    slot_b = 1 - slot_a
    unit = x_ref.shape[0]
    rows_w = unit + 2 * POOL_HALO

    @pl.when(step == 0)
    def _():
        _load_weights_bf16(sum((_weight_chunks(w_pool_hbm.at[odd_index, g], w_pool_ref.at[g])
                                for g in range(len(POOL_WINDOWS))), [])
                           + _weight_chunks(w1_hbm.at[layer], w1_ref)
                           + _weight_chunks(w2_hbm.at[layer], w2_ref), stage, sem)

    def stage_pool():
        sh1, sc1, g1, sh2, sc2, _ = _split_mod(moda_ref[0])
        xw = jnp.concatenate([xp_ref[...], x_ref[...], xn_ref[...]], axis=0)
        h = _rms_mod(xw, nmix_ref[...], sh1, sc1)
        seq_len = jnp.where(step < n_short, short_len, long_len)
        first = jnp.minimum(step, n_units - 1) * unit - POOL_HALO
        t = (lax.broadcasted_iota(jnp.int32, (rows_w, 1), 0) + first) & (seq_len - 1)
        yield

        def shift_down(a, k):
            return jnp.where(t >= k, pltpu.roll(a, k, axis=0), 0.0)

        def shift_up(a, k):
            return jnp.where(t < seq_len - k, pltpu.roll(a, rows_w - k, axis=0), 0.0)

        outs = []
        for g, w in enumerate(POOL_WINDOWS):
            hg = h[:, g * POOL_DIM:(g + 1) * POOL_DIM]
            half = w // 2
            left = shift_down(hg, 1)
            right = hg
            span = 1
            while span < half:
                left = left + shift_down(left, span)
                right = right + shift_up(right, span)
                span *= 2
            cnt = (jnp.minimum(t + half, seq_len) - jnp.maximum(t - half, 0)).astype(F32)
            pooled = ((left + right) / cnt - hg)[POOL_HALO:POOL_HALO + unit].astype(BF16)
            outs.append(_dot(pooled, w_pool_ref[g]))
            yield
        x1 = x_ref[...] + g1 * (jnp.concatenate(outs, axis=-1) * pscale_ref[...])
        x1_s[slot_a] = x1
        hb_s[slot_a] = _rms_mod(x1, nmlp_ref[...], sh2, sc2).astype(BF16)

    def stage_mlp(result):
        g2 = _split_mod(modc_ref[0])[5]
        acc = yield from _mlp_pieces(hb_s[slot_b], w1_ref, w2_ref)
        result.append(_rms(x1_s[slot_b] + g2 * acc) * fin_ref[...])

    def store(result):
        @pl.when(step - 1 < n_short)
        def _():
            y_short[...] = result[0]

        @pl.when(step - 1 >= n_short)
        def _():
            y_long[...] = result[0]

    pool_light = 2 + len(POOL_WINDOWS)

    @pl.when(step == 0)
    def _():
        _run(stage_pool())

    @pl.when((step >= 1) & (step < n_units))
    def _():
        result = []
        _weave(_pieces(stage_mlp(result), MLP_DOT_PIECES + 1), [stage_pool()], pool_light)
        store(result)

    @pl.when(step == n_units)
    def _():
        result = []
        _run(stage_mlp(result))
        store(result)


def _odd_call(x2, mod, params, weights, units, *, unit, short_len, long_len, layer, odd_index):
    n_units = units.total
    halo_per_unit = unit // POOL_HALO
    last_halo = x2.shape[0] // POOL_HALO - 1
    unit_a = lambda i: units.clamp(i)
    unit_c = lambda i: units.clamp(i - 1)
    mod_row = lambda u: layer * COND_ROWS + units.mod_row(u)
    assert (units.n_short * unit) % long_len == 0
    in_specs = [pl.BlockSpec((POOL_HALO, D_MODEL), lambda i: (jnp.maximum(unit_a(i) * halo_per_unit - 1, 0), 0)),
                pl.BlockSpec((unit, D_MODEL), lambda i: (unit_a(i), 0)),
                pl.BlockSpec((POOL_HALO, D_MODEL),
                             lambda i: (jnp.minimum((unit_a(i) + 1) * halo_per_unit, last_halo), 0)),
                pl.BlockSpec((1, 1, N_MOD), lambda i: (mod_row(unit_a(i)), 0, 0)),
                pl.BlockSpec((1, 1, N_MOD), lambda i: (mod_row(unit_c(i)), 0, 0))]
    in_specs += [_resident(p.shape) for p in params] + [_HBM] * len(weights)
    scratch = [pltpu.VMEM((2, unit, D_MODEL), F32),
               pltpu.VMEM((2, unit, D_MODEL), BF16)]
    scratch += [pltpu.VMEM(w.shape[1:], BF16) for w in weights] + _WEIGHT_STAGING
    return pl.pallas_call(
        functools.partial(_odd_kernel, n_units=n_units, n_short=units.n_short,
                          short_len=short_len, long_len=long_len, layer=layer, odd_index=odd_index),
        grid=(n_units + 1,),
        in_specs=in_specs,
        out_specs=[pl.BlockSpec((unit, D_MODEL), lambda i: (units.short_block(unit_c(i)), 0)),
                   pl.BlockSpec((unit, D_MODEL), lambda i: (units.long_block(unit_c(i)), 0))],
        out_shape=[jax.ShapeDtypeStruct((units.n_short * unit, D_MODEL), F32),
                   jax.ShapeDtypeStruct((units.n_long * unit, D_MODEL), F32)],
        scratch_shapes=scratch,
        compiler_params=pltpu.CompilerParams(
            dimension_semantics=("arbitrary",),
            vmem_limit_bytes=V7X_VMEM_LIMIT_BYTES),
        name="odd_layer",
    )(x2, x2, x2, mod, mod, *params, *weights)


def _rope_tables(n_tokens):
    f32 = np.float32
    rows = n_tokens // GRID_W
    pos_r = np.repeat(np.arange(rows, dtype=f32), GRID_W)
    pos_c = np.tile(np.arange(GRID_W, dtype=f32), rows)
    inv = (f32(ROPE_BASE) ** (-np.arange(ROPE_PAIRS_AXIS, dtype=f32) / f32(ROPE_PAIRS_AXIS))).astype(f32)
    ang = np.concatenate([pos_r[:, None] * inv, pos_c[:, None] * inv], axis=-1).astype(f32)
    cos, sin = np.cos(ang).astype(f32), np.sin(ang).astype(f32)
    return np.concatenate([cos, cos], axis=-1), np.concatenate([-sin, sin], axis=-1)


def kernel(x_prompt, x_sample, state_ret, c, c_ctx, ada_w, ada_b, norm_mix, norm_mlp, mlp_w1, mlp_w2,
           ev_w_in, ev_w_s, ev_b_s, ev_vnorm, ev_decay, ev_w_out, od_w_pool, od_pool_scale, final_norm):
    n_prompt, prompt_len, _ = x_prompt.shape
    n_sample, sample_len, _ = x_sample.shape
    assert DEPTH == 2 and n_sample + 1 <= COND_ROWS
    assert prompt_len == UNIT and sample_len % UNIT == 0
    units = _Units(n_prompt, n_sample, sample_len // UNIT)

    cond = jnp.concatenate([c_ctx[None, :], c, jnp.zeros((COND_ROWS - 1 - n_sample, D_MODEL), F32)])
    mod = _ada_call(cond, ada_w, ada_b)

    row = lambda v: v.reshape(1, -1)
    dec = jnp.broadcast_to(ev_decay[0].reshape(2 * HEADS, 1), (2 * HEADS, CHUNK))
    even_params = (
        row(norm_mix[0]), row(norm_mlp[0]),
        jnp.broadcast_to(ev_b_s[0][:, :, None], (A_GROUPS, CHUNK, A_GROUP_DIM)),
        row(ev_vnorm[0]), dec,
    )
    odd_params = (row(norm_mix[1]), row(norm_mlp[1]), row(final_norm), row(od_pool_scale[0]))
    even_weights = (ev_w_in, ev_w_out, mlp_w1, mlp_w2, ev_w_s)
    odd_weights = (od_w_pool, mlp_w1, mlp_w2)

    xp2 = x_prompt.reshape(n_prompt * prompt_len, D_MODEL)
    xs2 = x_sample.reshape(n_sample * sample_len, D_MODEL)
    cos2, sin2 = _rope_tables(sample_len)
    sp = _ret_state_call(xs2, mod, state_ret, (cos2, sin2), row(norm_mix[0]), ev_w_in, dec,
                         n_seq=n_sample, seq_len=sample_len, layer=0, even_index=0)
    rope = (np.concatenate([np.ones((UNIT, HEAD_DIM), np.float32), cos2]),
            np.concatenate([np.zeros((UNIT, HEAD_DIM), np.float32), sin2]))
    x_all, new_state = _even_call(xp2, xs2, mod, sp, rope, even_params, even_weights, units,
                                  layer=0, even_index=0)
    y_prompt, y_sample = _odd_call(x_all, mod, odd_params, odd_weights, units, unit=UNIT,
                                   short_len=prompt_len, long_len=sample_len, layer=1, odd_index=0)

    return (y_prompt.reshape(n_prompt, prompt_len, D_MODEL),
            y_sample.reshape(n_sample, sample_len, D_MODEL),
            new_state)
```

```python
import functools

import jax
import jax.numpy as jnp
import numpy as np
from jax import lax
from jax.experimental import pallas as pl
from jax.experimental.pallas import tpu as pltpu

D_MODEL = 1024
DEPTH = 2
GRID_W = 64
A_WIDTH = D_MODEL // 2
A_GROUPS = 4
A_GROUP_DIM = A_WIDTH // A_GROUPS
CHUNK = 128
HEADS = 4
HEAD_DIM = (D_MODEL // 2) // HEADS
ROPE_BASE = 10000.0
ROPE_PAIRS_AXIS = HEAD_DIM // 4
POOL_WINDOWS = (2, 4, 8, 16)
POOL_DIM = D_MODEL // len(POOL_WINDOWS)
POOL_HALO = max(POOL_WINDOWS) // 2
D_FF = 4 * D_MODEL
EPS = 1e-6
QK_W = HEADS * HEAD_DIM
V_W = HEADS * HEAD_DIM
IN_WIDTH = 2 * A_WIDTH + 2 * QK_W + 3 * V_W
OUT_WIDTH = A_WIDTH + V_W
N_MOD = 6 * D_MODEL
K_COL = 2 * A_WIDTH + QK_W
V_COL = K_COL + QK_W

UNIT = 2 * CHUNK
UNIT_CHUNKS = UNIT // CHUNK
FF_CHUNK = 1024
ADA_BLOCK_N = 3072
COND_ROWS = 8
STAGE_ROWS = 512
STAGE_COLS = 1024
STAGE_SLOTS = 3
V7X_VMEM_LIMIT_BYTES = 60 * 1024 * 1024

F32 = jnp.float32
BF16 = jnp.bfloat16


def _dot(a, b):
    return jnp.dot(a, b, preferred_element_type=F32)


def _dot_nt(a, b):
    return lax.dot_general(a, b, (((1,), (1,)), ((), ())), preferred_element_type=F32)


def _silu(x):
    return x * jax.nn.sigmoid(x)


def _gelu_tanh(x):
    return 0.5 * x * (1.0 + jnp.tanh(0.7978845608028654 * (x + 0.044715 * (x * x * x))))


def _rms(x):
    return x * lax.rsqrt(jnp.mean(x * x, axis=-1, keepdims=True) + EPS)


def _rms_mod(x, gain, shift, scale):
    return (_rms(x) * gain) * (1.0 + scale) + shift


def _layer_norm(x):
    mu = jnp.mean(x, axis=-1, keepdims=True)
    d = x - mu
    return d * lax.rsqrt(jnp.mean(d * d, axis=-1, keepdims=True) + EPS)


def _split_mod(mod):
    return [mod[:, i * D_MODEL:(i + 1) * D_MODEL] for i in range(6)]


def _rope(x, cos2, sin2):
    return x * cos2 + pltpu.roll(x, HEAD_DIM // 2, axis=1) * sin2


def _pick_block(take_first, a_ref, b_ref):
    rows = lax.broadcasted_iota(jnp.int32, (a_ref.shape[0], 1), 0)
    return jnp.where(rows < jnp.where(take_first, a_ref.shape[0], 0), a_ref[...], b_ref[...])


def _run(gen, n=None):
    done = 0
    while n is None or done < n:
        try:
            next(gen)
        except StopIteration:
            break
        done += 1
    return done


def _weave(heavy, lights, n_light):
    heavy = list(heavy)
    pending = _roundrobin(lights)
    ran = 0
    for k, piece in enumerate(heavy):
        piece()
        want = ((k + 1) * n_light + len(heavy) - 1) // len(heavy)
        ran += _run(pending, want - ran)
    _run(pending)


def _pieces(gen, n):
    return [functools.partial(next, gen, None) for _ in range(n)]


def _roundrobin(gens):
    live = list(gens)
    while live:
        for g in list(live):
            try:
                next(g)
            except StopIteration:
                live.remove(g)
                continue
            yield


MLP_DOT_PIECES = 2 * (D_FF // FF_CHUNK)


def _mlp_pieces(hb, w1_ref, w2_ref):
    n_ff = D_FF // FF_CHUNK
    up = lambda j: _dot(hb, w1_ref[:, j * FF_CHUNK:(j + 1) * FF_CHUNK])
    a = up(0)
    yield
    acc = None
    for j in range(n_ff):
        a_next = None
        if j + 1 < n_ff:
            a_next = up(j + 1)
            yield
        p = _dot(jnp.square(jnp.maximum(a, 0.0)).astype(BF16), w2_ref[j * FF_CHUNK:(j + 1) * FF_CHUNK, :])
        acc = p if acc is None else acc + p
        a = a_next
        yield
    return acc


def _resident(shape):
    return pl.BlockSpec(shape, lambda i: (0,) * len(shape), pipeline_mode=pl.Buffered(1))


def _weight_chunks(w_hbm, w_vmem):
    rows, cols = w_hbm.shape
    jobs = []
    for r0 in range(0, rows, STAGE_ROWS):
        for c0 in range(0, cols, STAGE_COLS):
            window = (pl.ds(r0, min(STAGE_ROWS, rows - r0)), pl.ds(c0, min(STAGE_COLS, cols - c0)))
            jobs.append((w_hbm.at[window], w_vmem.at[window]))
    return jobs


def _load_weights_bf16(jobs, stage, sem):
    n_slots = stage.shape[0]

    def copy(k):
        r, c = jobs[k][0].shape
        slot = k % n_slots
        return pltpu.make_async_copy(jobs[k][0], stage.at[slot, pl.ds(0, r), pl.ds(0, c)], sem.at[slot])

    for k in range(min(n_slots - 1, len(jobs))):
        copy(k).start()
    for k, (src, dst) in enumerate(jobs):
        if k + n_slots - 1 < len(jobs):
            copy(k + n_slots - 1).start()
        copy(k).wait()
        r, c = src.shape
        dst[...] = stage[k % n_slots, 0:r, 0:c].astype(BF16)


class _Units:
    def __init__(self, n_short, n_long, units_per_long):
        self.n_short, self.n_long, self.units_per_long = n_short, n_long * units_per_long, units_per_long
        self.total = self.n_short + self.n_long

    def clamp(self, unit):
        return jnp.clip(unit, 0, self.total - 1)

    def short_block(self, unit):
        return jnp.minimum(unit, self.n_short - 1)

    def long_block(self, unit):
        return jnp.clip(unit - self.n_short, 0, self.n_long - 1)

    def mod_row(self, unit):
        return jnp.where(unit < self.n_short, 0, 1 + (unit - self.n_short) // self.units_per_long)

    def rope_block(self, unit):
        return jnp.where(unit < self.n_short, 0, 1 + (unit - self.n_short) % self.units_per_long)

    def state_block(self, unit):
        return jnp.where(unit < self.n_short, 0, self.units_per_long + unit - self.n_short)


def _ada_kernel(cond_ref, w_ref, b_ref, out_ref):
    s = _silu(cond_ref[...]).astype(BF16)
    rows = _dot(s, w_ref[0].astype(BF16)) + b_ref[pl.ds(pl.program_id(0), 1), :]
    for r in range(COND_ROWS):
        out_ref[r] = rows[r:r + 1, :]


def _ada_call(cond, ada_w, ada_b):
    return pl.pallas_call(
        _ada_kernel,
        grid=(DEPTH, N_MOD // ADA_BLOCK_N),
        in_specs=[
            pl.BlockSpec((COND_ROWS, D_MODEL), lambda i, j: (0, 0)),
            pl.BlockSpec((1, D_MODEL, ADA_BLOCK_N), lambda i, j: (i, 0, j)),
            pl.BlockSpec((DEPTH, ADA_BLOCK_N), lambda i, j: (0, j)),
        ],
        out_specs=pl.BlockSpec((COND_ROWS, 1, ADA_BLOCK_N), lambda i, j: (i, 0, j)),
        out_shape=jax.ShapeDtypeStruct((DEPTH * COND_ROWS, 1, N_MOD), F32),
        compiler_params=pltpu.CompilerParams(
            dimension_semantics=("arbitrary", "arbitrary"),
            vmem_limit_bytes=V7X_VMEM_LIMIT_BYTES),
        name="ada_mod",
    )(cond, ada_w, ada_b)


def _decay_tables(dec_ref, hd):
    row = lax.broadcasted_iota(jnp.int32, (CHUNK, CHUNK), 0).astype(F32)
    col = lax.broadcasted_iota(jnp.int32, (CHUNK, CHUNK), 1).astype(F32)
    lg_f = jnp.broadcast_to(-jnp.exp(dec_ref[hd:hd + 1, :]), (CHUNK, CHUNK))
    lg_b = jnp.broadcast_to(-jnp.exp(dec_ref[HEADS + hd:HEADS + hd + 1, :]), (CHUNK, CHUNK))
    d_f = row - col
    d_b = col - row
    return dict(
        decay_f=jnp.where(d_f >= 0, jnp.exp(lg_f * jnp.maximum(d_f, 0.0)), 0.0),
        decay_b=jnp.where(d_b >= 0, jnp.exp(lg_b * jnp.maximum(d_b, 0.0)), 0.0),
        qd_f=jnp.exp(lg_f * (row + 1.0)),
        qd_b=jnp.exp(lg_b * (CHUNK - row)),
        kd_f=jnp.exp(lg_f * (CHUNK - 1.0 - row)),
        kd_b=jnp.exp(lg_b * row),
        cd_f=jnp.exp(lg_f * CHUNK),
        cd_b=jnp.exp(lg_b * CHUNK),
    )


TABLE_NAMES = ("decay_f", "decay_b", "qd_f", "qd_b", "kd_f", "kd_b", "cd_f", "cd_b")


def _ret_state_kernel(x_ref, mod_ref, s0_ref, cos_ref, sin_ref, nmix_ref, wk_ref, wv_ref, dec_ref, sp_ref,
                      *, n_chunk):
    step = pl.program_id(0)

    @pl.when(step == 0)
    def _():
        sp_ref[...] = jnp.zeros_like(sp_ref)

    @pl.when(step > 0)
    def _():
        sh1, sc1 = _split_mod(mod_ref[0])[:2]
        hb = _rms_mod(x_ref[...], nmix_ref[...], sh1, sc1).astype(BF16)
        zk = _dot(hb, wk_ref[...].astype(BF16)) * (HEAD_DIM ** -0.5)
        zv = _dot(hb, wv_ref[...].astype(BF16)).astype(BF16)
        cos2, sin2 = cos_ref[...], sin_ref[...]
        for hd in range(HEADS):
            lo = hd * HEAD_DIM
            tb = _decay_tables(dec_ref, hd)
            kh = _rope(zk[:, lo:lo + HEAD_DIM], cos2, sin2)
            kv_f, kv_b = [], []
            for c in range(n_chunk):
                kc = kh[c * CHUNK:(c + 1) * CHUNK]
                vc = zv[c * CHUNK:(c + 1) * CHUNK, lo:lo + HEAD_DIM]
                kv_f.append(_dot((kc * tb["kd_f"]).T.astype(BF16), vc))
                kv_b.append(_dot((kc * tb["kd_b"]).T.astype(BF16), vc))
            st = s0_ref[0, 0, 0, hd]
            for c in range(n_chunk):
                if c % UNIT_CHUNKS == 0:
                    sp_ref[c // UNIT_CHUNKS, 0, hd] = st
                st = tb["cd_f"] * st + kv_f[c]
            st = s0_ref[0, 0, 1, hd]
            for c in reversed(range(n_chunk)):
                if c % UNIT_CHUNKS == UNIT_CHUNKS - 1:
                    sp_ref[c // UNIT_CHUNKS, 1, hd] = st
                st = tb["cd_b"] * st + kv_b[c]


def _ret_state_call(x2, mod, s0, rope, nmix, w_in, dec, *, n_seq, seq_len, layer, even_index):
    n_chunk = seq_len // CHUNK
    units = seq_len // UNIT
    seq = lambda i: jnp.maximum(i - 1, 0)
    mod_row = lambda i: layer * COND_ROWS + 1 + seq(i)
    return pl.pallas_call(
        functools.partial(_ret_state_kernel, n_chunk=n_chunk),
        grid=(n_seq + 1,),
        in_specs=[
            pl.BlockSpec((seq_len, D_MODEL), lambda i: (seq(i), 0)),
            pl.BlockSpec((1, 1, N_MOD), lambda i: (mod_row(i), 0, 0)),
            pl.BlockSpec((1, 1, 2, HEADS, HEAD_DIM, HEAD_DIM), lambda i: (seq(i), 0, 0, 0, 0, 0)),
            _resident((seq_len, HEAD_DIM)), _resident((seq_len, HEAD_DIM)),
            _resident((1, D_MODEL)),
            pl.BlockSpec((None, D_MODEL, QK_W), lambda i: (even_index, 0, K_COL // QK_W),
                         pipeline_mode=pl.Buffered(1)),
            pl.BlockSpec((None, D_MODEL, V_W), lambda i: (even_index, 0, V_COL // V_W),
                         pipeline_mode=pl.Buffered(1)),
            _resident(dec.shape),
        ],
        out_specs=pl.BlockSpec((units, 2, HEADS, HEAD_DIM, HEAD_DIM), lambda i: (i, 0, 0, 0, 0)),
        out_shape=jax.ShapeDtypeStruct(((n_seq + 1) * units, 2, HEADS, HEAD_DIM, HEAD_DIM), F32),
        compiler_params=pltpu.CompilerParams(
            dimension_semantics=("arbitrary",),
            vmem_limit_bytes=V7X_VMEM_LIMIT_BYTES),
        name="ret_state",
    )(x2, mod, s0, rope[0], rope[1], nmix, w_in, w_in, dec)


def _even_kernel(xa_short, xa_long, xc_short, xc_long, moda_ref, modc_ref, sp_ref, cos_ref, sin_ref,
                 nmix_ref, nmlp_ref, b_s_ref, vgain_ref, dec_ref, w_in_hbm, w_out_hbm, w1_hbm, w2_hbm, w_s_hbm,
                 y_ref, st_ref, ug_s, vg_s, q_s, k_s, v_s, gf_s, gb_s, cat_s, tab_s,
                 w_in_ref, w_out_ref, w1_ref, w2_ref, w_s_ref, stage, sem, *, units, layer, even_index):
    n_units, n_short = units.total, units.n_short
    step = pl.program_id(0)
    slot_a = lax.rem(step, 2)
    slot_b = 1 - slot_a

    @pl.when(step == 0)
    def _():
        _load_weights_bf16(_weight_chunks(w_in_hbm.at[even_index], w_in_ref)
                           + _weight_chunks(w_out_hbm.at[even_index], w_out_ref)
                           + _weight_chunks(w1_hbm.at[layer], w1_ref)
                           + _weight_chunks(w2_hbm.at[layer], w2_ref)
                           + sum((_weight_chunks(w_s_hbm.at[even_index, g], w_s_ref.at[g])
                                  for g in range(A_GROUPS)), []), stage, sem)
        for hd in range(HEADS):
            tb = _decay_tables(dec_ref, hd)
            for n, name in enumerate(TABLE_NAMES):
                tab_s[hd, n] = tb[name]

    def stage_in():
        sh1, sc1 = _split_mod(moda_ref[0])[:2]
        x = _pick_block(step < n_short, xa_short, xa_long)
        hb = _rms_mod(x, nmix_ref[...], sh1, sc1).astype(BF16)
        z = _dot(hb, w_in_ref[...])
        yield
        ug_s[slot_a] = _gelu_tanh(z[:, 0:A_WIDTH])
        yield
        zv = _gelu_tanh(z[:, A_WIDTH:2 * A_WIDTH])
        vgn = [_layer_norm(zv[:, g * A_GROUP_DIM:(g + 1) * A_GROUP_DIM]) for g in range(A_GROUPS)]
        vg_s[slot_a] = (jnp.concatenate(vgn, axis=-1) * vgain_ref[...]).astype(BF16)
        yield
        cos2, sin2 = cos_ref[...], sin_ref[...]
        heads = lambda zz: jnp.concatenate(
            [_rope(zz[:, h * HEAD_DIM:(h + 1) * HEAD_DIM], cos2, sin2) for h in range(HEADS)], -1)
        q_s[slot_a] = heads(z[:, 2 * A_WIDTH:K_COL]).astype(BF16)
        yield
        k_s[slot_a] = heads(z[:, K_COL:V_COL] * (HEAD_DIM ** -0.5))
        v_s[slot_a] = z[:, V_COL:V_COL + V_W].astype(BF16)
        yield
        gf_s[slot_a] = _silu(z[:, V_COL + V_W:V_COL + 2 * V_W])
        yield
        gb_s[slot_a] = _silu(z[:, V_COL + 2 * V_W:V_COL + 3 * V_W])

    IN_LIGHT = 6

    def stage_mix(final_states):
        s = slot_b
        rows = [slice(c * CHUNK, (c + 1) * CHUNK) for c in range(UNIT_CHUNKS)]
        head_cols = [slice(hd * HEAD_DIM, (hd + 1) * HEAD_DIM) for hd in range(HEADS)]
        table = lambda hd, name: tab_s[hd, TABLE_NAMES.index(name)]

        sp2 = []
        for g in range(A_GROUPS):
            lo = g * A_GROUP_DIM
            vg2 = jnp.concatenate([vg_s[s, r, lo:lo + A_GROUP_DIM] for r in rows], axis=-1)
            sp2.append(_dot(w_s_ref[g], vg2))
        yield
        scores, kv_f, kv_b = [], [], []
        for hd in range(HEADS):
            kc = [k_s[s, r, head_cols[hd]] for r in rows]
            vc = [v_s[s, r, head_cols[hd]] for r in rows]
            scores.append([_dot_nt(q_s[s, r, head_cols[hd]], kc[c].astype(BF16)) for c, r in enumerate(rows)])
            kv_f.append([_dot((kc[c] * table(hd, "kd_f")).T.astype(BF16), vc[c]) for c in range(UNIT_CHUNKS)])
            kv_b.append([_dot((kc[c] * table(hd, "kd_b")).T.astype(BF16), vc[c]) for c in range(UNIT_CHUNKS)])
            yield
        for g in range(A_GROUPS):
            lo = g * A_GROUP_DIM
            for c, r in enumerate(rows):
                gate = sp2[g][:, c * CHUNK:(c + 1) * CHUNK] + b_s_ref[g]
                cat_s[s, r, lo:lo + A_GROUP_DIM] = (ug_s[s, r, lo:lo + A_GROUP_DIM] * gate).astype(BF16)
            if g % 2 == 1:
                yield
        y_raw = []
        for hd in range(HEADS):
            enter_f, st = [], sp_ref[0, 0, hd]
            for c in range(UNIT_CHUNKS):
                enter_f.append(st)
                st = table(hd, "cd_f") * st + kv_f[hd][c]
            final_f = st
            enter_b, st = [None] * UNIT_CHUNKS, sp_ref[0, 1, hd]
            for c in reversed(range(UNIT_CHUNKS)):
                enter_b[c] = st
                st = table(hd, "cd_b") * st + kv_b[hd][c]
            final_states.append((final_f, st))
            per_chunk = []
            for c, r in enumerate(rows):
                vc = v_s[s, r, head_cols[hd]]
                qf = q_s[s, r, head_cols[hd]].astype(F32)

                def direction(decay, qd, enter):
                    p = (scores[hd][c] * table(hd, decay)).astype(BF16)
                    lhs = jnp.concatenate([p, (qf * table(hd, qd)).astype(BF16)], axis=-1)
                    return _dot(lhs, jnp.concatenate([vc, enter.astype(BF16)], axis=0))

                per_chunk.append((direction("decay_f", "qd_f", enter_f[c]), direction("decay_b", "qd_b", enter_b[c])))
            y_raw.append(per_chunk)
            yield
        for hd in range(HEADS):
            for c, r in enumerate(rows):
                y_f, y_b = y_raw[hd][c]
                cols = slice(A_WIDTH + hd * HEAD_DIM, A_WIDTH + (hd + 1) * HEAD_DIM)
                cat_s[s, r, cols] = (gf_s[s, r, head_cols[hd]] * _layer_norm(y_f)
                                     + gb_s[s, r, head_cols[hd]] * _layer_norm(y_b)).astype(BF16)
            yield

    MIX_LIGHT = 1 + HEADS + A_GROUPS // 2 + 2 * HEADS

    def emit_states(final_states):
        @pl.when(step - 1 < n_short)
        def _():
            for hd, (final_f, final_b) in enumerate(final_states):
                st_ref[0, 0, 0, hd] = final_f
                st_ref[0, 0, 1, hd] = final_b

    def stage_mlp():
        _, _, g1, sh2, sc2, g2 = _split_mod(modc_ref[0])
        x = _pick_block(step - 2 < n_short, xc_short, xc_long) + g1 * _dot(cat_s[slot_a], w_out_ref[...])
        hb = _rms_mod(x, nmlp_ref[...], sh2, sc2).astype(BF16)
        yield
        acc = yield from _mlp_pieces(hb, w1_ref, w2_ref)
        y_ref[...] = x + g2 * acc

    @pl.when(step == 0)
    def _():
        _run(stage_in())

    @pl.when(step == 1)
    def _():
        a, finals = stage_in(), []
        _weave(_pieces(a, 1), [a, stage_mix(finals)], IN_LIGHT + MIX_LIGHT)
        emit_states(finals)

    @pl.when((step >= 2) & (step < n_units))
    def _():
        a, c, finals = stage_in(), stage_mlp(), []
        heavy = _pieces(c, 1) + _pieces(a, 1) + _pieces(c, MLP_DOT_PIECES + 1)
        _weave(heavy, [a, stage_mix(finals)], IN_LIGHT + MIX_LIGHT)
        emit_states(finals)

    @pl.when(step == n_units)
    def _():
        finals = []
        _weave(_pieces(stage_mlp(), MLP_DOT_PIECES + 2), [stage_mix(finals)], MIX_LIGHT)
        emit_states(finals)

    @pl.when(step == n_units + 1)
    def _():
        _run(stage_mlp())


_HBM = pl.BlockSpec(memory_space=pl.ANY)
_WEIGHT_STAGING = [pltpu.VMEM((STAGE_SLOTS, STAGE_ROWS, STAGE_COLS), F32), pltpu.SemaphoreType.DMA((STAGE_SLOTS,))]


def _even_call(x_short, x_long, mod, sp, rope, params, weights, units, *, layer, even_index):
    n_units = units.total
    unit_a = lambda i: units.clamp(i)
    unit_b = lambda i: units.clamp(i - 1)
    unit_c = lambda i: units.clamp(i - 2)
    mod_row = lambda unit: layer * COND_ROWS + units.mod_row(unit)
    state_block = (1, 1, 2, HEADS, HEAD_DIM, HEAD_DIM)

    in_specs = [pl.BlockSpec((UNIT, D_MODEL), lambda i: (units.short_block(unit_a(i)), 0)),
                pl.BlockSpec((UNIT, D_MODEL), lambda i: (units.long_block(unit_a(i)), 0)),
                pl.BlockSpec((UNIT, D_MODEL), lambda i: (units.short_block(unit_c(i)), 0)),
                pl.BlockSpec((UNIT, D_MODEL), lambda i: (units.long_block(unit_c(i)), 0)),
                pl.BlockSpec((1, 1, N_MOD), lambda i: (mod_row(unit_a(i)), 0, 0)),
                pl.BlockSpec((1, 1, N_MOD), lambda i: (mod_row(unit_c(i)), 0, 0)),
                pl.BlockSpec((1, 2, HEADS, HEAD_DIM, HEAD_DIM), lambda i: (units.state_block(unit_b(i)), 0, 0, 0, 0)),
                pl.BlockSpec((UNIT, HEAD_DIM), lambda i: (units.rope_block(unit_a(i)), 0)),
                pl.BlockSpec((UNIT, HEAD_DIM), lambda i: (units.rope_block(unit_a(i)), 0))]
    in_specs += [_resident(p.shape) for p in params] + [_HBM] * len(weights)
    scratch = [
        pltpu.VMEM((2, UNIT, A_WIDTH), F32),
        pltpu.VMEM((2, UNIT, A_WIDTH), BF16),
        pltpu.VMEM((2, UNIT, QK_W), BF16),
        pltpu.VMEM((2, UNIT, QK_W), F32),
        pltpu.VMEM((2, UNIT, V_W), BF16),
        pltpu.VMEM((2, UNIT, V_W), F32),
        pltpu.VMEM((2, UNIT, V_W), F32),
        pltpu.VMEM((2, UNIT, OUT_WIDTH), BF16),
        pltpu.VMEM((HEADS, len(TABLE_NAMES), CHUNK, CHUNK), F32),
    ]
    scratch += [pltpu.VMEM(w.shape[1:], BF16) for w in weights] + _WEIGHT_STAGING
    return pl.pallas_call(
        functools.partial(_even_kernel, units=units, layer=layer, even_index=even_index),
        grid=(n_units + 2,),
        in_specs=in_specs,
        out_specs=[pl.BlockSpec((UNIT, D_MODEL), lambda i: (unit_c(i), 0)),
                   pl.BlockSpec(state_block, lambda i: (units.short_block(unit_b(i)), 0, 0, 0, 0, 0))],
        out_shape=[jax.ShapeDtypeStruct((n_units * UNIT, D_MODEL), F32),
                   jax.ShapeDtypeStruct((units.n_short,) + state_block[1:], F32)],
        scratch_shapes=scratch,
        compiler_params=pltpu.CompilerParams(
            dimension_semantics=("arbitrary",),
            vmem_limit_bytes=V7X_VMEM_LIMIT_BYTES),
        name="even_layer",
    )(x_short, x_long, x_short, x_long, mod, mod, sp, rope[0], rope[1], *params, *weights)


def _odd_kernel(xp_ref, x_ref, xn_ref, moda_ref, modc_ref, nmix_ref, nmlp_ref, fin_ref, pscale_ref,
                w_pool_hbm, w1_hbm, w2_hbm, y_short, y_long, x1_s, hb_s, w_pool_ref, w1_ref, w2_ref, stage, sem,
                *, n_units, n_short, short_len, long_len, layer, odd_index):
    step = pl.program_id(0)
    slot_a = lax.rem(step, 2)
    slot_b = 1 - slot_a
    unit = x_ref.shape[0]
    rows_w = unit + 2 * POOL_HALO

    @pl.when(step == 0)
    def _():
        _load_weights_bf16(sum((_weight_chunks(w_pool_hbm.at[odd_index, g], w_pool_ref.at[g])
                                for g in range(len(POOL_WINDOWS))), [])
                           + _weight_chunks(w1_hbm.at[layer], w1_ref)
                           + _weight_chunks(w2_hbm.at[layer], w2_ref), stage, sem)

    def stage_pool():
        sh1, sc1, g1, sh2, sc2, _ = _split_mod(moda_ref[0])
        xw = jnp.concatenate([xp_ref[...], x_ref[...], xn_ref[...]], axis=0)
        h = _rms_mod(xw, nmix_ref[...], sh1, sc1)
        seq_len = jnp.where(step < n_short, short_len, long_len)
        first = jnp.minimum(step, n_units - 1) * unit - POOL_HALO
        t = (lax.broadcasted_iota(jnp.int32, (rows_w, 1), 0) + first) & (seq_len - 1)
        yield

        def shift_down(a, k):
            return jnp.where(t >= k, pltpu.roll(a, k, axis=0), 0.0)

        def shift_up(a, k):
            return jnp.where(t < seq_len - k, pltpu.roll(a, rows_w - k, axis=0), 0.0)

        outs = []
        for g, w in enumerate(POOL_WINDOWS):
            hg = h[:, g * POOL_DIM:(g + 1) * POOL_DIM]
            half = w // 2
            left = shift_down(hg, 1)
            right = hg
            span = 1
            while span < half:
                left = left + shift_down(left, span)
                right = right + shift_up(right, span)
                span *= 2
            cnt = (jnp.minimum(t + half, seq_len) - jnp.maximum(t - half, 0)).astype(F32)
            pooled = ((left + right) / cnt - hg)[POOL_HALO:POOL_HALO + unit].astype(BF16)
            outs.append(_dot(pooled, w_pool_ref[g]))
            yield
        x1 = x_ref[...] + g1 * (jnp.concatenate(outs, axis=-1) * pscale_ref[...])
        x1_s[slot_a] = x1
        hb_s[slot_a] = _rms_mod(x1, nmlp_ref[...], sh2, sc2).astype(BF16)

    def stage_mlp(result):
        g2 = _split_mod(modc_ref[0])[5]
        acc = yield from _mlp_pieces(hb_s[slot_b], w1_ref, w2_ref)
        result.append(_rms(x1_s[slot_b] + g2 * acc) * fin_ref[...])

    def store(result):
        @pl.when(step - 1 < n_short)
        def _():
            y_short[...] = result[0]

        @pl.when(step - 1 >= n_short)
        def _():
            y_long[...] = result[0]

    pool_light = 2 + len(POOL_WINDOWS)

    @pl.when(step == 0)
    def _():
        _run(stage_pool())

    @pl.when((step >= 1) & (step < n_units))
    def _():
        result = []
        _weave(_pieces(stage_mlp(result), MLP_DOT_PIECES + 1), [stage_pool()], pool_light)
        store(result)

    @pl.when(step == n_units)
    def _():
        result = []
        _run(stage_mlp(result))
        store(result)


def _odd_call(x2, mod, params, weights, units, *, unit, short_len, long_len, layer, odd_index):
    n_units = units.total
    halo_per_unit = unit // POOL_HALO
    last_halo = x2.shape[0] // POOL_HALO - 1
    unit_a = lambda i: units.clamp(i)
    unit_c = lambda i: units.clamp(i - 1)
    mod_row = lambda u: layer * COND_ROWS + units.mod_row(u)
    assert (units.n_short * unit) % long_len == 0
    in_specs = [pl.BlockSpec((POOL_HALO, D_MODEL), lambda i: (jnp.maximum(unit_a(i) * halo_per_unit - 1, 0), 0)),
                pl.BlockSpec((unit, D_MODEL), lambda i: (unit_a(i), 0)),
                pl.BlockSpec((POOL_HALO, D_MODEL),
                             lambda i: (jnp.minimum((unit_a(i) + 1) * halo_per_unit, last_halo), 0)),
                pl.BlockSpec((1, 1, N_MOD), lambda i: (mod_row(unit_a(i)), 0, 0)),
                pl.BlockSpec((1, 1, N_MOD), lambda i: (mod_row(unit_c(i)), 0, 0))]
    in_specs += [_resident(p.shape) for p in params] + [_HBM] * len(weights)
    scratch = [pltpu.VMEM((2, unit, D_MODEL), F32),
               pltpu.VMEM((2, unit, D_MODEL), BF16)]
    scratch += [pltpu.VMEM(w.shape[1:], BF16) for w in weights] + _WEIGHT_STAGING
    return pl.pallas_call(
        functools.partial(_odd_kernel, n_units=n_units, n_short=units.n_short,
                          short_len=short_len, long_len=long_len, layer=layer, odd_index=odd_index),
        grid=(n_units + 1,),
        in_specs=in_specs,
        out_specs=[pl.BlockSpec((unit, D_MODEL), lambda i: (units.short_block(unit_c(i)), 0)),
                   pl.BlockSpec((unit, D_MODEL), lambda i: (units.long_block(unit_c(i)), 0))],
        out_shape=[jax.ShapeDtypeStruct((units.n_short * unit, D_MODEL), F32),
                   jax.ShapeDtypeStruct((units.n_long * unit, D_MODEL), F32)],
        scratch_shapes=scratch,
        compiler_params=pltpu.CompilerParams(
            dimension_semantics=("arbitrary",),
            vmem_limit_bytes=V7X_VMEM_LIMIT_BYTES),
        name="odd_layer",
    )(x2, x2, x2, mod, mod, *params, *weights)


def _rope_tables(n_tokens):
    f32 = np.float32
    rows = n_tokens // GRID_W
    pos_r = np.repeat(np.arange(rows, dtype=f32), GRID_W)
    pos_c = np.tile(np.arange(GRID_W, dtype=f32), rows)
    inv = (f32(ROPE_BASE) ** (-np.arange(ROPE_PAIRS_AXIS, dtype=f32) / f32(ROPE_PAIRS_AXIS))).astype(f32)
    ang = np.concatenate([pos_r[:, None] * inv, pos_c[:, None] * inv], axis=-1).astype(f32)
    cos, sin = np.cos(ang).astype(f32), np.sin(ang).astype(f32)
    return np.concatenate([cos, cos], axis=-1), np.concatenate([-sin, sin], axis=-1)


def kernel(x_prompt, x_sample, state_ret, c, c_ctx, ada_w, ada_b, norm_mix, norm_mlp, mlp_w1, mlp_w2,
           ev_w_in, ev_w_s, ev_b_s, ev_vnorm, ev_decay, ev_w_out, od_w_pool, od_pool_scale, final_norm):
    n_prompt, prompt_len, _ = x_prompt.shape
    n_sample, sample_len, _ = x_sample.shape
    assert DEPTH == 2 and n_sample + 1 <= COND_ROWS
    assert prompt_len == UNIT and sample_len % UNIT == 0
    units = _Units(n_prompt, n_sample, sample_len // UNIT)

    cond = jnp.concatenate([c_ctx[None, :], c, jnp.zeros((COND_ROWS - 1 - n_sample, D_MODEL), F32)])
    mod = _ada_call(cond, ada_w, ada_b)

    row = lambda v: v.reshape(1, -1)
    dec = jnp.broadcast_to(ev_decay[0].reshape(2 * HEADS, 1), (2 * HEADS, CHUNK))
    even_params = (
        row(norm_mix[0]), row(norm_mlp[0]),
        jnp.broadcast_to(ev_b_s[0][:, :, None], (A_GROUPS, CHUNK, A_GROUP_DIM)),
        row(ev_vnorm[0]), dec,
    )
    odd_params = (row(norm_mix[1]), row(norm_mlp[1]), row(final_norm), row(od_pool_scale[0]))
    even_weights = (ev_w_in, ev_w_out, mlp_w1, mlp_w2, ev_w_s)
    odd_weights = (od_w_pool, mlp_w1, mlp_w2)

    xp2 = x_prompt.reshape(n_prompt * prompt_len, D_MODEL)
    xs2 = x_sample.reshape(n_sample * sample_len, D_MODEL)
    cos2, sin2 = _rope_tables(sample_len)
    sp = _ret_state_call(xs2, mod, state_ret, (cos2, sin2), row(norm_mix[0]), ev_w_in, dec,
                         n_seq=n_sample, seq_len=sample_len, layer=0, even_index=0)
    rope = (np.concatenate([np.ones((UNIT, HEAD_DIM), np.float32), cos2]),
            np.concatenate([np.zeros((UNIT, HEAD_DIM), np.float32), sin2]))
    x_all, new_state = _even_call(xp2, xs2, mod, sp, rope, even_params, even_weights, units,
                                  layer=0, even_index=0)
    y_prompt, y_sample = _odd_call(x_all, mod, odd_params, odd_weights, units, unit=UNIT,
                                   short_len=prompt_len, long_len=sample_len, layer=1, odd_index=0)

    return (y_prompt.reshape(n_prompt, prompt_len, D_MODEL),
            y_sample.reshape(n_sample, sample_len, D_MODEL),
            new_state)
```

```python
import functools

import jax
import jax.numpy as jnp
import numpy as np
from jax import lax
from jax.experimental import pallas as pl
from jax.experimental.pallas import tpu as pltpu

D_MODEL = 1024
DEPTH = 2
GRID_W = 64
A_WIDTH = D_MODEL // 2
A_GROUPS = 4
A_GROUP_DIM = A_WIDTH // A_GROUPS
CHUNK = 128
HEADS = 4
HEAD_DIM = (D_MODEL // 2) // HEADS
ROPE_BASE = 10000.0
ROPE_PAIRS_AXIS = HEAD_DIM // 4
POOL_WINDOWS = (2, 4, 8, 16)
POOL_DIM = D_MODEL // len(POOL_WINDOWS)
POOL_HALO = max(POOL_WINDOWS) // 2
D_FF = 4 * D_MODEL
EPS = 1e-6
QK_W = HEADS * HEAD_DIM
V_W = HEADS * HEAD_DIM
IN_WIDTH = 2 * A_WIDTH + 2 * QK_W + 3 * V_W
OUT_WIDTH = A_WIDTH + V_W
N_MOD = 6 * D_MODEL
K_COL = 2 * A_WIDTH + QK_W
V_COL = K_COL + QK_W

UNIT = 2 * CHUNK
UNIT_CHUNKS = UNIT // CHUNK
FF_CHUNK = 1024
ADA_BLOCK_N = 3072
COND_ROWS = 8
STAGE_ROWS = 512
STAGE_COLS = 1024
STAGE_SLOTS = 3
V7X_VMEM_LIMIT_BYTES = 60 * 1024 * 1024

F32 = jnp.float32
BF16 = jnp.bfloat16


def _dot(a, b):
    return jnp.dot(a, b, preferred_element_type=F32)


def _dot_nt(a, b):
    return lax.dot_general(a, b, (((1,), (1,)), ((), ())), preferred_element_type=F32)


def _silu(x):
    return x * jax.nn.sigmoid(x)


def _gelu_tanh(x):
    return 0.5 * x * (1.0 + jnp.tanh(0.7978845608028654 * (x + 0.044715 * (x * x * x))))


def _rms(x):
    return x * lax.rsqrt(jnp.mean(x * x, axis=-1, keepdims=True) + EPS)


def _rms_mod(x, gain, shift, scale):
    return (_rms(x) * gain) * (1.0 + scale) + shift


def _layer_norm(x):
    mu = jnp.mean(x, axis=-1, keepdims=True)
    d = x - mu
    return d * lax.rsqrt(jnp.mean(d * d, axis=-1, keepdims=True) + EPS)


def _split_mod(mod):
    return [mod[:, i * D_MODEL:(i + 1) * D_MODEL] for i in range(6)]


def _rope(x, cos2, sin2):
    return x * cos2 + pltpu.roll(x, HEAD_DIM // 2, axis=1) * sin2


def _pick_block(take_first, a_ref, b_ref):
    rows = lax.broadcasted_iota(jnp.int32, (a_ref.shape[0], 1), 0)
    return jnp.where(rows < jnp.where(take_first, a_ref.shape[0], 0), a_ref[...], b_ref[...])


def _run(gen, n=None):
    done = 0
    while n is None or done < n:
        try:
            next(gen)
        except StopIteration:
            break
        done += 1
    return done


def _weave(heavy, lights, n_light):
    heavy = list(heavy)
    pending = _roundrobin(lights)
    ran = 0
    for k, piece in enumerate(heavy):
        piece()
        want = ((k + 1) * n_light + len(heavy) - 1) // len(heavy)
        ran += _run(pending, want - ran)
    _run(pending)


def _pieces(gen, n):
    return [functools.partial(next, gen, None) for _ in range(n)]


def _roundrobin(gens):
    live = list(gens)
    while live:
        for g in list(live):
            try:
                next(g)
            except StopIteration:
                live.remove(g)
                continue
            yield


MLP_DOT_PIECES = 2 * (D_FF // FF_CHUNK)


def _mlp_pieces(hb, w1_ref, w2_ref):
    n_ff = D_FF // FF_CHUNK
    up = lambda j: _dot(hb, w1_ref[:, j * FF_CHUNK:(j + 1) * FF_CHUNK])
    a = up(0)
    yield
    acc = None
    for j in range(n_ff):
        a_next = None
        if j + 1 < n_ff:
            a_next = up(j + 1)
            yield
        p = _dot(jnp.square(jnp.maximum(a, 0.0)).astype(BF16), w2_ref[j * FF_CHUNK:(j + 1) * FF_CHUNK, :])
        acc = p if acc is None else acc + p
        a = a_next
        yield
    return acc


def _resident(shape):
    return pl.BlockSpec(shape, lambda i: (0,) * len(shape), pipeline_mode=pl.Buffered(1))


def _weight_chunks(w_hbm, w_vmem):
    rows, cols = w_hbm.shape
    jobs = []
    for r0 in range(0, rows, STAGE_ROWS):
        for c0 in range(0, cols, STAGE_COLS):
            window = (pl.ds(r0, min(STAGE_ROWS, rows - r0)), pl.ds(c0, min(STAGE_COLS, cols - c0)))
            jobs.append((w_hbm.at[window], w_vmem.at[window]))
    return jobs


def _load_weights_bf16(jobs, stage, sem):
    n_slots = stage.shape[0]

    def copy(k):
        r, c = jobs[k][0].shape
        slot = k % n_slots
        return pltpu.make_async_copy(jobs[k][0], stage.at[slot, pl.ds(0, r), pl.ds(0, c)], sem.at[slot])

    for k in range(min(n_slots - 1, len(jobs))):
        copy(k).start()
    for k, (src, dst) in enumerate(jobs):
        if k + n_slots - 1 < len(jobs):
            copy(k + n_slots - 1).start()
        copy(k).wait()
        r, c = src.shape
        dst[...] = stage[k % n_slots, 0:r, 0:c].astype(BF16)


class _Units:
    def __init__(self, n_short, n_long, units_per_long):
        self.n_short, self.n_long, self.units_per_long = n_short, n_long * units_per_long, units_per_long
        self.total = self.n_short + self.n_long

    def clamp(self, unit):
        return jnp.clip(unit, 0, self.total - 1)

    def short_block(self, unit):
        return jnp.minimum(unit, self.n_short - 1)

    def long_block(self, unit):
        return jnp.clip(unit - self.n_short, 0, self.n_long - 1)

    def mod_row(self, unit):
        return jnp.where(unit < self.n_short, 0, 1 + (unit - self.n_short) // self.units_per_long)

    def rope_block(self, unit):
        return jnp.where(unit < self.n_short, 0, 1 + (unit - self.n_short) % self.units_per_long)

    def state_block(self, unit):
        return jnp.where(unit < self.n_short, 0, self.units_per_long + unit - self.n_short)


def _ada_kernel(shared_ref, each_ref, w_ref, b_ref, out_ref):
    row_id = lax.broadcasted_iota(jnp.int32, (COND_ROWS, 1), 0)
    cond = jnp.where(row_id == 0, shared_ref[...], 0.0)
    for r in range(each_ref.shape[0]):
        cond = jnp.where(row_id == 1 + r, each_ref[r:r + 1, :], cond)
    s = _silu(cond).astype(BF16)
    rows = _dot(s, w_ref[0].astype(BF16)) + b_ref[pl.ds(pl.program_id(0), 1), :]
    for r in range(COND_ROWS):
        out_ref[r] = rows[r:r + 1, :]


def _ada_call(cond_shared, cond_each, ada_w, ada_b):
    return pl.pallas_call(
        _ada_kernel,
        grid=(DEPTH, N_MOD // ADA_BLOCK_N),
        in_specs=[
            pl.BlockSpec(cond_shared.shape, lambda i, j: (0, 0)),
            pl.BlockSpec(cond_each.shape, lambda i, j: (0, 0)),
            pl.BlockSpec((1, D_MODEL, ADA_BLOCK_N), lambda i, j: (i, 0, j)),
            pl.BlockSpec((DEPTH, ADA_BLOCK_N), lambda i, j: (0, j)),
        ],
        out_specs=pl.BlockSpec((COND_ROWS, 1, ADA_BLOCK_N), lambda i, j: (i, 0, j)),
        out_shape=jax.ShapeDtypeStruct((DEPTH * COND_ROWS, 1, N_MOD), F32),
        compiler_params=pltpu.CompilerParams(
            dimension_semantics=("arbitrary", "arbitrary"),
            vmem_limit_bytes=V7X_VMEM_LIMIT_BYTES),
        name="ada_mod",
    )(cond_shared, cond_each, ada_w, ada_b)


def _decay_tables(dec_ref, hd):
    row = lax.broadcasted_iota(jnp.int32, (CHUNK, CHUNK), 0).astype(F32)
    col = lax.broadcasted_iota(jnp.int32, (CHUNK, CHUNK), 1).astype(F32)
    lg_f = jnp.broadcast_to(-jnp.exp(dec_ref[hd:hd + 1, :]), (CHUNK, CHUNK))
    lg_b = jnp.broadcast_to(-jnp.exp(dec_ref[HEADS + hd:HEADS + hd + 1, :]), (CHUNK, CHUNK))
    d_f = row - col
    d_b = col - row
    return dict(
        decay_f=jnp.where(d_f >= 0, jnp.exp(lg_f * jnp.maximum(d_f, 0.0)), 0.0),
        decay_b=jnp.where(d_b >= 0, jnp.exp(lg_b * jnp.maximum(d_b, 0.0)), 0.0),
        qd_f=jnp.exp(lg_f * (row + 1.0)),
        qd_b=jnp.exp(lg_b * (CHUNK - row)),
        kd_f=jnp.exp(lg_f * (CHUNK - 1.0 - row)),
        kd_b=jnp.exp(lg_b * row),
        cd_f=jnp.exp(lg_f * CHUNK),
        cd_b=jnp.exp(lg_b * CHUNK),
    )


TABLE_NAMES = ("decay_f", "decay_b", "qd_f", "qd_b", "kd_f", "kd_b", "cd_f", "cd_b")


def _ret_state_kernel(x_ref, mod_ref, s0_ref, cos_ref, sin_ref, nmix_ref, wk_ref, wv_ref, dec_ref, sp_ref,
                      *, n_chunk):
    step = pl.program_id(0)

    @pl.when(step == 0)
    def _():
        sp_ref[...] = jnp.zeros_like(sp_ref)

    @pl.when(step > 0)
    def _():
        sh1, sc1 = _split_mod(mod_ref[0])[:2]
        hb = _rms_mod(x_ref[...], nmix_ref[...], sh1, sc1).astype(BF16)
        zk = _dot(hb, wk_ref[...].astype(BF16)) * (HEAD_DIM ** -0.5)
        zv = _dot(hb, wv_ref[...].astype(BF16)).astype(BF16)
        cos2, sin2 = cos_ref[...], sin_ref[...]
        for hd in range(HEADS):
            lo = hd * HEAD_DIM
            tb = _decay_tables(dec_ref, hd)
            kh = _rope(zk[:, lo:lo + HEAD_DIM], cos2, sin2)
            kv_f, kv_b = [], []
            for c in range(n_chunk):
                kc = kh[c * CHUNK:(c + 1) * CHUNK]
                vc = zv[c * CHUNK:(c + 1) * CHUNK, lo:lo + HEAD_DIM]
                kv_f.append(_dot((kc * tb["kd_f"]).T.astype(BF16), vc))
                kv_b.append(_dot((kc * tb["kd_b"]).T.astype(BF16), vc))
            st = s0_ref[0, 0, 0, hd]
            for c in range(n_chunk):
                if c % UNIT_CHUNKS == 0:
                    sp_ref[c // UNIT_CHUNKS, 0, hd] = st
                st = tb["cd_f"] * st + kv_f[c]
            st = s0_ref[0, 0, 1, hd]
            for c in reversed(range(n_chunk)):
                if c % UNIT_CHUNKS == UNIT_CHUNKS - 1:
                    sp_ref[c // UNIT_CHUNKS, 1, hd] = st
                st = tb["cd_b"] * st + kv_b[c]


def _ret_state_call(x2, mod, s0, rope, nmix, w_in, dec, *, n_seq, seq_len, layer, even_index):
    n_chunk = seq_len // CHUNK
    units = seq_len // UNIT
    seq = lambda i: jnp.maximum(i - 1, 0)
    mod_row = lambda i: layer * COND_ROWS + 1 + seq(i)
    return pl.pallas_call(
        functools.partial(_ret_state_kernel, n_chunk=n_chunk),
        grid=(n_seq + 1,),
        in_specs=[
            pl.BlockSpec((seq_len, D_MODEL), lambda i: (seq(i), 0)),
            pl.BlockSpec((1, 1, N_MOD), lambda i: (mod_row(i), 0, 0)),
            pl.BlockSpec((1, 1, 2, HEADS, HEAD_DIM, HEAD_DIM), lambda i: (seq(i), 0, 0, 0, 0, 0)),
            _resident((seq_len, HEAD_DIM)), _resident((seq_len, HEAD_DIM)),
            _resident((1, D_MODEL)),
            pl.BlockSpec((None, D_MODEL, QK_W), lambda i: (even_index, 0, K_COL // QK_W),
                         pipeline_mode=pl.Buffered(1)),
            pl.BlockSpec((None, D_MODEL, V_W), lambda i: (even_index, 0, V_COL // V_W),
                         pipeline_mode=pl.Buffered(1)),
            _resident(dec.shape),
        ],
        out_specs=pl.BlockSpec((units, 2, HEADS, HEAD_DIM, HEAD_DIM), lambda i: (i, 0, 0, 0, 0)),
        out_shape=jax.ShapeDtypeStruct(((n_seq + 1) * units, 2, HEADS, HEAD_DIM, HEAD_DIM), F32),
        compiler_params=pltpu.CompilerParams(
            dimension_semantics=("arbitrary",),
            vmem_limit_bytes=V7X_VMEM_LIMIT_BYTES),
        name="ret_state",
    )(x2, mod, s0, rope[0], rope[1], nmix, w_in, w_in, dec)


def _even_kernel(xa_short, xa_long, xc_short, xc_long, moda_ref, modc_ref, sp_ref, cos_ref, sin_ref,
                 nmix_ref, nmlp_ref, b_s_ref, vgain_ref, dec_ref, w_in_hbm, w_out_hbm, w1_hbm, w2_hbm, w_s_hbm,
                 y_ref, st_ref, ug_s, vg_s, q_s, k_s, v_s, gf_s, gb_s, cat_s, tab_s,
                 w_in_ref, w_out_ref, w1_ref, w2_ref, w_s_ref, stage, sem, *, units, layer, even_index):
    n_units, n_short = units.total, units.n_short
    step = pl.program_id(0)
    slot_a = lax.rem(step, 2)
    slot_b = 1 - slot_a

    @pl.when(step == 0)
    def _():
        _load_weights_bf16(_weight_chunks(w_in_hbm.at[even_index], w_in_ref)
                           + _weight_chunks(w_out_hbm.at[even_index], w_out_ref)
                           + _weight_chunks(w1_hbm.at[layer], w1_ref)
                           + _weight_chunks(w2_hbm.at[layer], w2_ref)
                           + sum((_weight_chunks(w_s_hbm.at[even_index, g], w_s_ref.at[g])
                                  for g in range(A_GROUPS)), []), stage, sem)
        for hd in range(HEADS):
            tb = _decay_tables(dec_ref, hd)
            for n, name in enumerate(TABLE_NAMES):
                tab_s[hd, n] = tb[name]

    def stage_in():
        sh1, sc1 = _split_mod(moda_ref[0])[:2]
        x = _pick_block(step < n_short, xa_short, xa_long)
        hb = _rms_mod(x, nmix_ref[...], sh1, sc1).astype(BF16)
        z = _dot(hb, w_in_ref[...])
        yield
        ug_s[slot_a] = _gelu_tanh(z[:, 0:A_WIDTH])
        yield
        zv = _gelu_tanh(z[:, A_WIDTH:2 * A_WIDTH])
        vgn = [_layer_norm(zv[:, g * A_GROUP_DIM:(g + 1) * A_GROUP_DIM]) for g in range(A_GROUPS)]
        vg_s[slot_a] = (jnp.concatenate(vgn, axis=-1) * vgain_ref[...]).astype(BF16)
        yield
        cos2, sin2 = cos_ref[...], sin_ref[...]
        heads = lambda zz: jnp.concatenate(
            [_rope(zz[:, h * HEAD_DIM:(h + 1) * HEAD_DIM], cos2, sin2) for h in range(HEADS)], -1)
        q_s[slot_a] = heads(z[:, 2 * A_WIDTH:K_COL]).astype(BF16)
        yield
        k_s[slot_a] = heads(z[:, K_COL:V_COL] * (HEAD_DIM ** -0.5))
        v_s[slot_a] = z[:, V_COL:V_COL + V_W].astype(BF16)
        yield
        gf_s[slot_a] = _silu(z[:, V_COL + V_W:V_COL + 2 * V_W])
        yield
        gb_s[slot_a] = _silu(z[:, V_COL + 2 * V_W:V_COL + 3 * V_W])

    IN_LIGHT = 6

    def stage_mix(final_states):
        s = slot_b
        rows = [slice(c * CHUNK, (c + 1) * CHUNK) for c in range(UNIT_CHUNKS)]
        head_cols = [slice(hd * HEAD_DIM, (hd + 1) * HEAD_DIM) for hd in range(HEADS)]
        table = lambda hd, name: tab_s[hd, TABLE_NAMES.index(name)]

        sp2 = []
        for g in range(A_GROUPS):
            lo = g * A_GROUP_DIM
            vg2 = jnp.concatenate([vg_s[s, r, lo:lo + A_GROUP_DIM] for r in rows], axis=-1)
            sp2.append(_dot(w_s_ref[g], vg2))
        yield
        scores, kv_f, kv_b = [], [], []
        for hd in range(HEADS):
            kc = [k_s[s, r, head_cols[hd]] for r in rows]
            vc = [v_s[s, r, head_cols[hd]] for r in rows]
            scores.append([_dot_nt(q_s[s, r, head_cols[hd]], kc[c].astype(BF16)) for c, r in enumerate(rows)])
            kv_f.append([_dot((kc[c] * table(hd, "kd_f")).T.astype(BF16), vc[c]) for c in range(UNIT_CHUNKS)])
            kv_b.append([_dot((kc[c] * table(hd, "kd_b")).T.astype(BF16), vc[c]) for c in range(UNIT_CHUNKS)])
            yield
        for g in range(A_GROUPS):
            lo = g * A_GROUP_DIM
            for c, r in enumerate(rows):
                gate = sp2[g][:, c * CHUNK:(c + 1) * CHUNK] + b_s_ref[g]
                cat_s[s, r, lo:lo + A_GROUP_DIM] = (ug_s[s, r, lo:lo + A_GROUP_DIM] * gate).astype(BF16)
            if g % 2 == 1:
                yield
        y_raw = []
        for hd in range(HEADS):
            enter_f, st = [], sp_ref[0, 0, hd]
            for c in range(UNIT_CHUNKS):
                enter_f.append(st)
                st = table(hd, "cd_f") * st + kv_f[hd][c]
            final_f = st
            enter_b, st = [None] * UNIT_CHUNKS, sp_ref[0, 1, hd]
            for c in reversed(range(UNIT_CHUNKS)):
                enter_b[c] = st
                st = table(hd, "cd_b") * st + kv_b[hd][c]
            final_states.append((final_f, st))
            per_chunk = []
            for c, r in enumerate(rows):
                vc = v_s[s, r, head_cols[hd]]
                qf = q_s[s, r, head_cols[hd]].astype(F32)

                def direction(decay, qd, enter):
                    p = (scores[hd][c] * table(hd, decay)).astype(BF16)
                    lhs = jnp.concatenate([p, (qf * table(hd, qd)).astype(BF16)], axis=-1)
                    return _dot(lhs, jnp.concatenate([vc, enter.astype(BF16)], axis=0))

                per_chunk.append((direction("decay_f", "qd_f", enter_f[c]), direction("decay_b", "qd_b", enter_b[c])))
            y_raw.append(per_chunk)
            yield
        for hd in range(HEADS):
            for c, r in enumerate(rows):
                y_f, y_b = y_raw[hd][c]
                cols = slice(A_WIDTH + hd * HEAD_DIM, A_WIDTH + (hd + 1) * HEAD_DIM)
                cat_s[s, r, cols] = (gf_s[s, r, head_cols[hd]] * _layer_norm(y_f)
                                     + gb_s[s, r, head_cols[hd]] * _layer_norm(y_b)).astype(BF16)
            yield

    MIX_LIGHT = 1 + HEADS + A_GROUPS // 2 + 2 * HEADS

    def emit_states(final_states):
        @pl.when(step - 1 < n_short)
        def _():
            for hd, (final_f, final_b) in enumerate(final_states):
                st_ref[0, 0, 0, hd] = final_f
                st_ref[0, 0, 1, hd] = final_b

    def stage_mlp():
        _, _, g1, sh2, sc2, g2 = _split_mod(modc_ref[0])
        x = _pick_block(step - 2 < n_short, xc_short, xc_long) + g1 * _dot(cat_s[slot_a], w_out_ref[...])
        hb = _rms_mod(x, nmlp_ref[...], sh2, sc2).astype(BF16)
        yield
        acc = yield from _mlp_pieces(hb, w1_ref, w2_ref)
        y_ref[...] = x + g2 * acc

    @pl.when(step == 0)
    def _():
        _run(stage_in())

    @pl.when(step == 1)
    def _():
        a, finals = stage_in(), []
        _weave(_pieces(a, 1), [a, stage_mix(finals)], IN_LIGHT + MIX_LIGHT)
        emit_states(finals)

    @pl.when((step >= 2) & (step < n_units))
    def _():
        a, c, finals = stage_in(), stage_mlp(), []
        heavy = _pieces(c, 1) + _pieces(a, 1) + _pieces(c, MLP_DOT_PIECES + 1)
        _weave(heavy, [a, stage_mix(finals)], IN_LIGHT + MIX_LIGHT)
        emit_states(finals)

    @pl.when(step == n_units)
    def _():
        finals = []
        _weave(_pieces(stage_mlp(), MLP_DOT_PIECES + 2), [stage_mix(finals)], MIX_LIGHT)
        emit_states(finals)

    @pl.when(step == n_units + 1)
    def _():
        _run(stage_mlp())


_HBM = pl.BlockSpec(memory_space=pl.ANY)
_WEIGHT_STAGING = [pltpu.VMEM((STAGE_SLOTS, STAGE_ROWS, STAGE_COLS), F32), pltpu.SemaphoreType.DMA((STAGE_SLOTS,))]


def _even_call(x_short, x_long, mod, sp, rope, params, weights, units, *, layer, even_index):
    n_units = units.total
    unit_a = lambda i: units.clamp(i)
    unit_b = lambda i: units.clamp(i - 1)
    unit_c = lambda i: units.clamp(i - 2)
    mod_row = lambda unit: layer * COND_ROWS + units.mod_row(unit)
    state_block = (1, 1, 2, HEADS, HEAD_DIM, HEAD_DIM)

    in_specs = [pl.BlockSpec((UNIT, D_MODEL), lambda i: (units.short_block(unit_a(i)), 0)),
                pl.BlockSpec((UNIT, D_MODEL), lambda i: (units.long_block(unit_a(i)), 0)),
                pl.BlockSpec((UNIT, D_MODEL), lambda i: (units.short_block(unit_c(i)), 0)),
                pl.BlockSpec((UNIT, D_MODEL), lambda i: (units.long_block(unit_c(i)), 0)),
                pl.BlockSpec((1, 1, N_MOD), lambda i: (mod_row(unit_a(i)), 0, 0)),
                pl.BlockSpec((1, 1, N_MOD), lambda i: (mod_row(unit_c(i)), 0, 0)),
                pl.BlockSpec((1, 2, HEADS, HEAD_DIM, HEAD_DIM), lambda i: (units.state_block(unit_b(i)), 0, 0, 0, 0)),
                pl.BlockSpec((UNIT, HEAD_DIM), lambda i: (units.rope_block(unit_a(i)), 0)),
                pl.BlockSpec((UNIT, HEAD_DIM), lambda i: (units.rope_block(unit_a(i)), 0))]
    in_specs += [_resident(p.shape) for p in params] + [_HBM] * len(weights)
    scratch = [
        pltpu.VMEM((2, UNIT, A_WIDTH), F32),
        pltpu.VMEM((2, UNIT, A_WIDTH), BF16),
        pltpu.VMEM((2, UNIT, QK_W), BF16),
        pltpu.VMEM((2, UNIT, QK_W), F32),
        pltpu.VMEM((2, UNIT, V_W), BF16),
        pltpu.VMEM((2, UNIT, V_W), F32),
        pltpu.VMEM((2, UNIT, V_W), F32),
        pltpu.VMEM((2, UNIT, OUT_WIDTH), BF16),
        pltpu.VMEM((HEADS, len(TABLE_NAMES), CHUNK, CHUNK), F32),
    ]
    scratch += [pltpu.VMEM(w.shape[1:], BF16) for w in weights] + _WEIGHT_STAGING
    return pl.pallas_call(
        functools.partial(_even_kernel, units=units, layer=layer, even_index=even_index),
        grid=(n_units + 2,),
        in_specs=in_specs,
        out_specs=[pl.BlockSpec((UNIT, D_MODEL), lambda i: (unit_c(i), 0)),
                   pl.BlockSpec(state_block, lambda i: (units.short_block(unit_b(i)), 0, 0, 0, 0, 0))],
        out_shape=[jax.ShapeDtypeStruct((n_units * UNIT, D_MODEL), F32),
                   jax.ShapeDtypeStruct((units.n_short,) + state_block[1:], F32)],
        scratch_shapes=scratch,
        compiler_params=pltpu.CompilerParams(
            dimension_semantics=("arbitrary",),
            vmem_limit_bytes=V7X_VMEM_LIMIT_BYTES),
        name="even_layer",
    )(x_short, x_long, x_short, x_long, mod, mod, sp, rope[0], rope[1], *params, *weights)


def _odd_kernel(xp_ref, x_ref, xn_ref, moda_ref, modc_ref, nmix_ref, nmlp_ref, fin_ref, pscale_ref,
                w_pool_hbm, w1_hbm, w2_hbm, y_short, y_long, x1_s, hb_s, w_pool_ref, w1_ref, w2_ref, stage, sem,
                *, n_units, n_short, short_len, long_len, layer, odd_index):
    step = pl.program_id(0)
    slot_a = lax.rem(step, 2)
    slot_b = 1 - slot_a
    unit = x_ref.shape[0]
    rows_w = unit + 2 * POOL_HALO

    @pl.when(step == 0)
    def _():
        _load_weights_bf16(sum((_weight_chunks(w_pool_hbm.at[odd_index, g], w_pool_ref.at[g])
                                for g in range(len(POOL_WINDOWS))), [])
                           + _weight_chunks(w1_hbm.at[layer], w1_ref)
                           + _weight_chunks(w2_hbm.at[layer], w2_ref), stage, sem)

    def stage_pool():
        sh1, sc1, g1, sh2, sc2, _ = _split_mod(moda_ref[0])
        xw = jnp.concatenate([xp_ref[...], x_ref[...], xn_ref[...]], axis=0)
        h = _rms_mod(xw, nmix_ref[...], sh1, sc1)
        seq_len = jnp.where(step < n_short, short_len, long_len)
        first = jnp.minimum(step, n_units - 1) * unit - POOL_HALO
        t = (lax.broadcasted_iota(jnp.int32, (rows_w, 1), 0) + first) & (seq_len - 1)
        yield

        def shift_down(a, k):
            return jnp.where(t >= k, pltpu.roll(a, k, axis=0), 0.0)

        def shift_up(a, k):
            return jnp.where(t < seq_len - k, pltpu.roll(a, rows_w - k, axis=0), 0.0)

        outs = []
        for g, w in enumerate(POOL_WINDOWS):
            hg = h[:, g * POOL_DIM:(g + 1) * POOL_DIM]
            half = w // 2
            left = shift_down(hg, 1)
            right = hg
            span = 1
            while span < half:
                left = left + shift_down(left, span)
                right = right + shift_up(right, span)
                span *= 2
            cnt = (jnp.minimum(t + half, seq_len) - jnp.maximum(t - half, 0)).astype(F32)
            pooled = ((left + right) / cnt - hg)[POOL_HALO:POOL_HALO + unit].astype(BF16)
            outs.append(_dot(pooled, w_pool_ref[g]))
            yield
        x1 = x_ref[...] + g1 * (jnp.concatenate(outs, axis=-1) * pscale_ref[...])
        x1_s[slot_a] = x1
        hb_s[slot_a] = _rms_mod(x1, nmlp_ref[...], sh2, sc2).astype(BF16)

    def stage_mlp(result):
        g2 = _split_mod(modc_ref[0])[5]
        acc = yield from _mlp_pieces(hb_s[slot_b], w1_ref, w2_ref)
        result.append(_rms(x1_s[slot_b] + g2 * acc) * fin_ref[...])

    def store(result):
        @pl.when(step - 1 < n_short)
        def _():
            y_short[...] = result[0]

        @pl.when(step - 1 >= n_short)
        def _():
            y_long[...] = result[0]

    pool_light = 2 + len(POOL_WINDOWS)

    @pl.when(step == 0)
    def _():
        _run(stage_pool())

    @pl.when((step >= 1) & (step < n_units))
    def _():
        result = []
        _weave(_pieces(stage_mlp(result), MLP_DOT_PIECES + 1), [stage_pool()], pool_light)
        store(result)

    @pl.when(step == n_units)
    def _():
        result = []
        _run(stage_mlp(result))
        store(result)


def _odd_call(x2, mod, params, weights, units, *, unit, short_len, long_len, layer, odd_index):
    n_units = units.total
    halo_per_unit = unit // POOL_HALO
    last_halo = x2.shape[0] // POOL_HALO - 1
    unit_a = lambda i: units.clamp(i)
    unit_c = lambda i: units.clamp(i - 1)
    mod_row = lambda u: layer * COND_ROWS + units.mod_row(u)
    assert (units.n_short * unit) % long_len == 0
    in_specs = [pl.BlockSpec((POOL_HALO, D_MODEL), lambda i: (jnp.maximum(unit_a(i) * halo_per_unit - 1, 0), 0)),
                pl.BlockSpec((unit, D_MODEL), lambda i: (unit_a(i), 0)),
                pl.BlockSpec((POOL_HALO, D_MODEL),
                             lambda i: (jnp.minimum((unit_a(i) + 1) * halo_per_unit, last_halo), 0)),
                pl.BlockSpec((1, 1, N_MOD), lambda i: (mod_row(unit_a(i)), 0, 0)),
                pl.BlockSpec((1, 1, N_MOD), lambda i: (mod_row(unit_c(i)), 0, 0))]
    in_specs += [_resident(p.shape) for p in params] + [_HBM] * len(weights)
    scratch = [pltpu.VMEM((2, unit, D_MODEL), F32),
               pltpu.VMEM((2, unit, D_MODEL), BF16)]
    scratch += [pltpu.VMEM(w.shape[1:], BF16) for w in weights] + _WEIGHT_STAGING
    return pl.pallas_call(
        functools.partial(_odd_kernel, n_units=n_units, n_short=units.n_short,
                          short_len=short_len, long_len=long_len, layer=layer, odd_index=odd_index),
        grid=(n_units + 1,),
        in_specs=in_specs,
        out_specs=[pl.BlockSpec((unit, D_MODEL), lambda i: (units.short_block(unit_c(i)), 0)),
                   pl.BlockSpec((unit, D_MODEL), lambda i: (units.long_block(unit_c(i)), 0))],
        out_shape=[jax.ShapeDtypeStruct((units.n_short * unit, D_MODEL), F32),
                   jax.ShapeDtypeStruct((units.n_long * unit, D_MODEL), F32)],
        scratch_shapes=scratch,
        compiler_params=pltpu.CompilerParams(
            dimension_semantics=("arbitrary",),
            vmem_limit_bytes=V7X_VMEM_LIMIT_BYTES),
        name="odd_layer",
    )(x2, x2, x2, mod, mod, *params, *weights)


def _rope_tables(n_tokens):
    f32 = np.float32
    rows = n_tokens // GRID_W
    pos_r = np.repeat(np.arange(rows, dtype=f32), GRID_W)
    pos_c = np.tile(np.arange(GRID_W, dtype=f32), rows)
    inv = (f32(ROPE_BASE) ** (-np.arange(ROPE_PAIRS_AXIS, dtype=f32) / f32(ROPE_PAIRS_AXIS))).astype(f32)
    ang = np.concatenate([pos_r[:, None] * inv, pos_c[:, None] * inv], axis=-1).astype(f32)
    cos, sin = np.cos(ang).astype(f32), np.sin(ang).astype(f32)
    return np.concatenate([cos, cos], axis=-1), np.concatenate([-sin, sin], axis=-1)


def kernel(x_prompt, x_sample, state_ret, c, c_ctx, ada_w, ada_b, norm_mix, norm_mlp, mlp_w1, mlp_w2,
           ev_w_in, ev_w_s, ev_b_s, ev_vnorm, ev_decay, ev_w_out, od_w_pool, od_pool_scale, final_norm):
    n_prompt, prompt_len, _ = x_prompt.shape
    n_sample, sample_len, _ = x_sample.shape
    assert DEPTH == 2 and n_sample + 1 <= COND_ROWS
    assert prompt_len == UNIT and sample_len % UNIT == 0
    units = _Units(n_prompt, n_sample, sample_len // UNIT)

    mod = _ada_call(c_ctx.reshape(1, D_MODEL), c, ada_w, ada_b)

    row = lambda v: v.reshape(1, -1)
    dec = jnp.broadcast_to(ev_decay[0].reshape(2 * HEADS, 1), (2 * HEADS, CHUNK))
    even_params = (
        row(norm_mix[0]), row(norm_mlp[0]),
        jnp.broadcast_to(ev_b_s[0][:, :, None], (A_GROUPS, CHUNK, A_GROUP_DIM)),
        row(ev_vnorm[0]), dec,
    )
    odd_params = (row(norm_mix[1]), row(norm_mlp[1]), row(final_norm), row(od_pool_scale[0]))
    even_weights = (ev_w_in, ev_w_out, mlp_w1, mlp_w2, ev_w_s)
    odd_weights = (od_w_pool, mlp_w1, mlp_w2)

    xp2 = x_prompt.reshape(n_prompt * prompt_len, D_MODEL)
    xs2 = x_sample.reshape(n_sample * sample_len, D_MODEL)
    cos2, sin2 = _rope_tables(sample_len)
    sp = _ret_state_call(xs2, mod, state_ret, (cos2, sin2), row(norm_mix[0]), ev_w_in, dec,
                         n_seq=n_sample, seq_len=sample_len, layer=0, even_index=0)
    rope = (np.concatenate([np.ones((UNIT, HEAD_DIM), np.float32), cos2]),
            np.concatenate([np.zeros((UNIT, HEAD_DIM), np.float32), sin2]))
    x_all, new_state = _even_call(xp2, xs2, mod, sp, rope, even_params, even_weights, units,
                                  layer=0, even_index=0)
    y_prompt, y_sample = _odd_call(x_all, mod, odd_params, odd_weights, units, unit=UNIT,
                                   short_len=prompt_len, long_len=sample_len, layer=1, odd_index=0)

    return (y_prompt.reshape(n_prompt, prompt_len, D_MODEL),
            y_sample.reshape(n_sample, sample_len, D_MODEL),
            new_state)
```

```python
import functools

import jax
import jax.numpy as jnp
import numpy as np
from jax import lax
from jax.experimental import pallas as pl
from jax.experimental.pallas import tpu as pltpu

D_MODEL = 1024
DEPTH = 2
GRID_W = 64
A_WIDTH = D_MODEL // 2
A_GROUPS = 4
A_GROUP_DIM = A_WIDTH // A_GROUPS
CHUNK = 128
HEADS = 4
HEAD_DIM = (D_MODEL // 2) // HEADS
ROPE_BASE = 10000.0
ROPE_PAIRS_AXIS = HEAD_DIM // 4
POOL_WINDOWS = (2, 4, 8, 16)
POOL_DIM = D_MODEL // len(POOL_WINDOWS)
POOL_HALO = max(POOL_WINDOWS) // 2
D_FF = 4 * D_MODEL
EPS = 1e-6
QK_W = HEADS * HEAD_DIM
V_W = HEADS * HEAD_DIM
IN_WIDTH = 2 * A_WIDTH + 2 * QK_W + 3 * V_W
OUT_WIDTH = A_WIDTH + V_W
N_MOD = 6 * D_MODEL
K_COL = 2 * A_WIDTH + QK_W
V_COL = K_COL + QK_W

UNIT = 2 * CHUNK
UNIT_CHUNKS = UNIT // CHUNK
FF_CHUNK = 1024
ADA_BLOCK_N = 3072
COND_ROWS = 8
STAGE_ROWS = 512
STAGE_COLS = 1024
STAGE_SLOTS = 3
V7X_VMEM_LIMIT_BYTES = 60 * 1024 * 1024

F32 = jnp.float32
BF16 = jnp.bfloat16


def _dot(a, b):
    return jnp.dot(a, b, preferred_element_type=F32)


def _dot_nt(a, b):
    return lax.dot_general(a, b, (((1,), (1,)), ((), ())), preferred_element_type=F32)


def _silu(x):
    return x * jax.nn.sigmoid(x)


def _gelu_tanh(x):
    return 0.5 * x * (1.0 + jnp.tanh(0.7978845608028654 * (x + 0.044715 * (x * x * x))))


def _rms(x):
    return x * lax.rsqrt(jnp.mean(x * x, axis=-1, keepdims=True) + EPS)


def _rms_mod(x, gain, shift, scale):
    return (_rms(x) * gain) * (1.0 + scale) + shift


def _layer_norm(x):
    mu = jnp.mean(x, axis=-1, keepdims=True)
    d = x - mu
    return d * lax.rsqrt(jnp.mean(d * d, axis=-1, keepdims=True) + EPS)


def _split_mod(mod):
    return [mod[:, i * D_MODEL:(i + 1) * D_MODEL] for i in range(6)]


def _rope(x, cos2, sin2):
    return x * cos2 + pltpu.roll(x, HEAD_DIM // 2, axis=1) * sin2


def _pick_block(take_first, a_ref, b_ref):
    rows = lax.broadcasted_iota(jnp.int32, (a_ref.shape[0], 1), 0)
    return jnp.where(rows < jnp.where(take_first, a_ref.shape[0], 0), a_ref[...], b_ref[...])


def _run(gen, n=None):
    done = 0
    while n is None or done < n:
        try:
            next(gen)
        except StopIteration:
            break
        done += 1
    return done


def _weave(heavy, lights, n_light):
    heavy = list(heavy)
    pending = _roundrobin(lights)
    ran = 0
    for k, piece in enumerate(heavy):
        piece()
        want = ((k + 1) * n_light + len(heavy) - 1) // len(heavy)
        ran += _run(pending, want - ran)
    _run(pending)


def _pieces(gen, n):
    return [functools.partial(next, gen, None) for _ in range(n)]


def _roundrobin(gens):
    live = list(gens)
    while live:
        for g in list(live):
            try:
                next(g)
            except StopIteration:
                live.remove(g)
                continue
            yield


MLP_DOT_PIECES = 2 * (D_FF // FF_CHUNK)


def _mlp_pieces(hb, w1_ref, w2_ref):
    n_ff = D_FF // FF_CHUNK
    up = lambda j: _dot(hb, w1_ref[:, j * FF_CHUNK:(j + 1) * FF_CHUNK])
    a = up(0)
    yield
    acc = None
    for j in range(n_ff):
        a_next = None
        if j + 1 < n_ff:
            a_next = up(j + 1)
            yield
        p = _dot(jnp.square(jnp.maximum(a, 0.0)).astype(BF16), w2_ref[j * FF_CHUNK:(j + 1) * FF_CHUNK, :])
        acc = p if acc is None else acc + p
        a = a_next
        yield
    return acc


def _resident(shape):
    return pl.BlockSpec(shape, lambda i: (0,) * len(shape), pipeline_mode=pl.Buffered(1))


def _weight_chunks(w_hbm, w_vmem):
    rows, cols = w_hbm.shape
    jobs = []
    for r0 in range(0, rows, STAGE_ROWS):
        for c0 in range(0, cols, STAGE_COLS):
            window = (pl.ds(r0, min(STAGE_ROWS, rows - r0)), pl.ds(c0, min(STAGE_COLS, cols - c0)))
            jobs.append((w_hbm.at[window], w_vmem.at[window]))
    return jobs


def _load_weights_bf16(jobs, stage, sem):
    n_slots = stage.shape[0]

    def copy(k):
        r, c = jobs[k][0].shape
        slot = k % n_slots
        return pltpu.make_async_copy(jobs[k][0], stage.at[slot, pl.ds(0, r), pl.ds(0, c)], sem.at[slot])

    for k in range(min(n_slots - 1, len(jobs))):
        copy(k).start()
    for k, (src, dst) in enumerate(jobs):
        if k + n_slots - 1 < len(jobs):
            copy(k + n_slots - 1).start()
        copy(k).wait()
        r, c = src.shape
        dst[...] = stage[k % n_slots, 0:r, 0:c].astype(BF16)


class _Units:
    def __init__(self, n_short, n_long, units_per_long):
        self.n_short, self.n_long, self.units_per_long = n_short, n_long * units_per_long, units_per_long
        self.total = self.n_short + self.n_long

    def clamp(self, unit):
        return jnp.clip(unit, 0, self.total - 1)

    def short_block(self, unit):
        return jnp.minimum(unit, self.n_short - 1)

    def long_block(self, unit):
        return jnp.clip(unit - self.n_short, 0, self.n_long - 1)

    def mod_row(self, unit):
        return jnp.where(unit < self.n_short, 0, 1 + (unit - self.n_short) // self.units_per_long)

    def rope_block(self, unit):
        return jnp.where(unit < self.n_short, 0, 1 + (unit - self.n_short) % self.units_per_long)

    def state_block(self, unit):
        return jnp.where(unit < self.n_short, 0, self.units_per_long + unit - self.n_short)


def _ada_kernel(cond_ref, w_ref, b_ref, out_ref):
    s = _silu(cond_ref[...]).astype(BF16)
    rows = _dot(s, w_ref[0].astype(BF16)) + b_ref[pl.ds(pl.program_id(0), 1), :]
    for r in range(COND_ROWS):
        out_ref[r] = rows[r:r + 1, :]


def _ada_call(cond, ada_w, ada_b):
    return pl.pallas_call(
        _ada_kernel,
        grid=(DEPTH, N_MOD // ADA_BLOCK_N),
        in_specs=[
            pl.BlockSpec((COND_ROWS, D_MODEL), lambda i, j: (0, 0)),
            pl.BlockSpec((1, D_MODEL, ADA_BLOCK_N), lambda i, j: (i, 0, j)),
            pl.BlockSpec((DEPTH, ADA_BLOCK_N), lambda i, j: (0, j)),
        ],
        out_specs=pl.BlockSpec((COND_ROWS, 1, ADA_BLOCK_N), lambda i, j: (i, 0, j)),
        out_shape=jax.ShapeDtypeStruct((DEPTH * COND_ROWS, 1, N_MOD), F32),
        compiler_params=pltpu.CompilerParams(
            dimension_semantics=("arbitrary", "arbitrary"),
            vmem_limit_bytes=V7X_VMEM_LIMIT_BYTES),
        name="ada_mod",
    )(cond, ada_w, ada_b)


def _decay_tables(dec_ref, hd):
    row = lax.broadcasted_iota(jnp.int32, (CHUNK, CHUNK), 0).astype(F32)
    col = lax.broadcasted_iota(jnp.int32, (CHUNK, CHUNK), 1).astype(F32)
    lg_f = jnp.broadcast_to(-jnp.exp(dec_ref[hd:hd + 1, :]), (CHUNK, CHUNK))
    lg_b = jnp.broadcast_to(-jnp.exp(dec_ref[HEADS + hd:HEADS + hd + 1, :]), (CHUNK, CHUNK))
    d_f = row - col
    d_b = col - row
    return dict(
        decay_f=jnp.where(d_f >= 0, jnp.exp(lg_f * jnp.maximum(d_f, 0.0)), 0.0),
        decay_b=jnp.where(d_b >= 0, jnp.exp(lg_b * jnp.maximum(d_b, 0.0)), 0.0),
        qd_f=jnp.exp(lg_f * (row + 1.0)),
        qd_b=jnp.exp(lg_b * (CHUNK - row)),
        kd_f=jnp.exp(lg_f * (CHUNK - 1.0 - row)),
        kd_b=jnp.exp(lg_b * row),
        cd_f=jnp.exp(lg_f * CHUNK),
        cd_b=jnp.exp(lg_b * CHUNK),
    )


TABLE_NAMES = ("decay_f", "decay_b", "qd_f", "qd_b", "kd_f", "kd_b", "cd_f", "cd_b")


def _ret_state_kernel(x_ref, mod_ref, s0_ref, cos_ref, sin_ref, nmix_ref, wk_ref, wv_ref, dec_ref, sp_ref,
                      *, n_chunk):
    step = pl.program_id(0)

    @pl.when(step == 0)
    def _():
        sp_ref[...] = jnp.zeros_like(sp_ref)

    @pl.when(step > 0)
    def _():
        sh1, sc1 = _split_mod(mod_ref[0])[:2]
        hb = _rms_mod(x_ref[...], nmix_ref[...], sh1, sc1).astype(BF16)
        zk = _dot(hb, wk_ref[...].astype(BF16)) * (HEAD_DIM ** -0.5)
        zv = _dot(hb, wv_ref[...].astype(BF16)).astype(BF16)
        cos2, sin2 = cos_ref[...], sin_ref[...]
        for hd in range(HEADS):
            lo = hd * HEAD_DIM
            tb = _decay_tables(dec_ref, hd)
            kh = _rope(zk[:, lo:lo + HEAD_DIM], cos2, sin2)
            kv_f, kv_b = [], []
            for c in range(n_chunk):
                kc = kh[c * CHUNK:(c + 1) * CHUNK]
                vc = zv[c * CHUNK:(c + 1) * CHUNK, lo:lo + HEAD_DIM]
                kv_f.append(_dot((kc * tb["kd_f"]).T.astype(BF16), vc))
                kv_b.append(_dot((kc * tb["kd_b"]).T.astype(BF16), vc))
            st = s0_ref[0, 0, 0, hd]
            for c in range(n_chunk):
                if c % UNIT_CHUNKS == 0:
                    sp_ref[c // UNIT_CHUNKS, 0, hd] = st
                st = tb["cd_f"] * st + kv_f[c]
            st = s0_ref[0, 0, 1, hd]
            for c in reversed(range(n_chunk)):
                if c % UNIT_CHUNKS == UNIT_CHUNKS - 1:
                    sp_ref[c // UNIT_CHUNKS, 1, hd] = st
                st = tb["cd_b"] * st + kv_b[c]


def _ret_state_call(x2, mod, s0, rope, nmix, w_in, dec, *, n_seq, seq_len, layer, even_index):
    n_chunk = seq_len // CHUNK
    units = seq_len // UNIT
    seq = lambda i: jnp.maximum(i - 1, 0)
    mod_row = lambda i: layer * COND_ROWS + 1 + seq(i)
    return pl.pallas_call(
        functools.partial(_ret_state_kernel, n_chunk=n_chunk),
        grid=(n_seq + 1,),
        in_specs=[
            pl.BlockSpec((seq_len, D_MODEL), lambda i: (seq(i), 0)),
            pl.BlockSpec((1, 1, N_MOD), lambda i: (mod_row(i), 0, 0)),
            pl.BlockSpec((1, 1, 2, HEADS, HEAD_DIM, HEAD_DIM), lambda i: (seq(i), 0, 0, 0, 0, 0)),
            _resident((seq_len, HEAD_DIM)), _resident((seq_len, HEAD_DIM)),
            _resident((1, D_MODEL)),
            pl.BlockSpec((None, D_MODEL, QK_W), lambda i: (even_index, 0, K_COL // QK_W),
                         pipeline_mode=pl.Buffered(1)),
            pl.BlockSpec((None, D_MODEL, V_W), lambda i: (even_index, 0, V_COL // V_W),
                         pipeline_mode=pl.Buffered(1)),
            _resident(dec.shape),
        ],
        out_specs=pl.BlockSpec((units, 2, HEADS, HEAD_DIM, HEAD_DIM), lambda i: (i, 0, 0, 0, 0)),
        out_shape=jax.ShapeDtypeStruct(((n_seq + 1) * units, 2, HEADS, HEAD_DIM, HEAD_DIM), F32),
        compiler_params=pltpu.CompilerParams(
            dimension_semantics=("arbitrary",),
            vmem_limit_bytes=V7X_VMEM_LIMIT_BYTES),
        name="ret_state",
    )(x2, mod, s0, rope[0], rope[1], nmix, w_in, w_in, dec)


def _even_kernel(xa_short, xa_long, xc_short, xc_long, moda_ref, modc_ref, sp_ref, cos_ref, sin_ref,
                 nmix_ref, nmlp_ref, b_s_ref, vgain_ref, dec_ref, w_in_hbm, w_out_hbm, w1_hbm, w2_hbm, w_s_hbm,
                 y_ref, st_ref, ug_s, vg_s, q_s, k_s, v_s, gf_s, gb_s, cat_s, tab_s,
                 w_in_ref, w_out_ref, w1_ref, w2_ref, w_s_ref, stage, sem, *, units, layer, even_index):
    n_units, n_short = units.total, units.n_short
    step = pl.program_id(0)
    slot_a = lax.rem(step, 2)
    slot_b = 1 - slot_a

    @pl.when(step == 0)
    def _():
        _load_weights_bf16(_weight_chunks(w_in_hbm.at[even_index], w_in_ref)
                           + _weight_chunks(w_out_hbm.at[even_index], w_out_ref)
                           + _weight_chunks(w1_hbm.at[layer], w1_ref)
                           + _weight_chunks(w2_hbm.at[layer], w2_ref)
                           + sum((_weight_chunks(w_s_hbm.at[even_index, g], w_s_ref.at[g])
                                  for g in range(A_GROUPS)), []), stage, sem)
        for hd in range(HEADS):
            tb = _decay_tables(dec_ref, hd)
            for n, name in enumerate(TABLE_NAMES):
                tab_s[hd, n] = tb[name]

    def stage_in():
        sh1, sc1 = _split_mod(moda_ref[0])[:2]
        x = _pick_block(step < n_short, xa_short, xa_long)
        hb = _rms_mod(x, nmix_ref[...], sh1, sc1).astype(BF16)
        z = _dot(hb, w_in_ref[...])
        yield
        ug_s[slot_a] = z[:, 0:A_WIDTH]
        yield
        zv = _gelu_tanh(z[:, A_WIDTH:2 * A_WIDTH])
        vgn = [_layer_norm(zv[:, g * A_GROUP_DIM:(g + 1) * A_GROUP_DIM]) for g in range(A_GROUPS)]
        vg_s[slot_a] = (jnp.concatenate(vgn, axis=-1) * vgain_ref[...]).astype(BF16)
        yield
        cos2, sin2 = cos_ref[...], sin_ref[...]
        heads = lambda zz: jnp.concatenate(
            [_rope(zz[:, h * HEAD_DIM:(h + 1) * HEAD_DIM], cos2, sin2) for h in range(HEADS)], -1)
        q_s[slot_a] = heads(z[:, 2 * A_WIDTH:K_COL]).astype(BF16)
        yield
        k_s[slot_a] = heads(z[:, K_COL:V_COL] * (HEAD_DIM ** -0.5))
        v_s[slot_a] = z[:, V_COL:V_COL + V_W].astype(BF16)
        yield
        gf_s[slot_a] = z[:, V_COL + V_W:V_COL + 2 * V_W]
        yield
        gb_s[slot_a] = z[:, V_COL + 2 * V_W:V_COL + 3 * V_W]

    IN_LIGHT = 6

    def stage_mix(final_states):
        s = slot_b
        rows = [slice(c * CHUNK, (c + 1) * CHUNK) for c in range(UNIT_CHUNKS)]
        head_cols = [slice(hd * HEAD_DIM, (hd + 1) * HEAD_DIM) for hd in range(HEADS)]
        table = lambda hd, name: tab_s[hd, TABLE_NAMES.index(name)]

        sp2 = []
        for g in range(A_GROUPS):
            lo = g * A_GROUP_DIM
            vg2 = jnp.concatenate([vg_s[s, r, lo:lo + A_GROUP_DIM] for r in rows], axis=-1)
            sp2.append(_dot(w_s_ref[g], vg2))
        yield
        scores, kv_f, kv_b = [], [], []
        for hd in range(HEADS):
            kc = [k_s[s, r, head_cols[hd]] for r in rows]
            vc = [v_s[s, r, head_cols[hd]] for r in rows]
            scores.append([_dot_nt(q_s[s, r, head_cols[hd]], kc[c].astype(BF16)) for c, r in enumerate(rows)])
            kv_f.append([_dot((kc[c] * table(hd, "kd_f")).T.astype(BF16), vc[c]) for c in range(UNIT_CHUNKS)])
            kv_b.append([_dot((kc[c] * table(hd, "kd_b")).T.astype(BF16), vc[c]) for c in range(UNIT_CHUNKS)])
            yield
        for g in range(A_GROUPS):
            lo = g * A_GROUP_DIM
            for c, r in enumerate(rows):
                gate = sp2[g][:, c * CHUNK:(c + 1) * CHUNK] + b_s_ref[g]
                cat_s[s, r, lo:lo + A_GROUP_DIM] = (_gelu_tanh(ug_s[s, r, lo:lo + A_GROUP_DIM]) * gate).astype(BF16)
            if g % 2 == 1:
                yield
        y_raw = []
        for hd in range(HEADS):
            enter_f, st = [], sp_ref[0, 0, hd]
            for c in range(UNIT_CHUNKS):
                enter_f.append(st)
                st = table(hd, "cd_f") * st + kv_f[hd][c]
            final_f = st
            enter_b, st = [None] * UNIT_CHUNKS, sp_ref[0, 1, hd]
            for c in reversed(range(UNIT_CHUNKS)):
                enter_b[c] = st
                st = table(hd, "cd_b") * st + kv_b[hd][c]
            final_states.append((final_f, st))
            per_chunk = []
            for c, r in enumerate(rows):
                vc = v_s[s, r, head_cols[hd]]
                qf = q_s[s, r, head_cols[hd]].astype(F32)

                def direction(decay, qd, enter):
                    p = (scores[hd][c] * table(hd, decay)).astype(BF16)
                    lhs = jnp.concatenate([p, (qf * table(hd, qd)).astype(BF16)], axis=-1)
                    return _dot(lhs, jnp.concatenate([vc, enter.astype(BF16)], axis=0))

                per_chunk.append((direction("decay_f", "qd_f", enter_f[c]), direction("decay_b", "qd_b", enter_b[c])))
            y_raw.append(per_chunk)
            yield
        for hd in range(HEADS):
            for c, r in enumerate(rows):
                y_f, y_b = y_raw[hd][c]
                cols = slice(A_WIDTH + hd * HEAD_DIM, A_WIDTH + (hd + 1) * HEAD_DIM)
                cat_s[s, r, cols] = (_silu(gf_s[s, r, head_cols[hd]]) * _layer_norm(y_f)
                                     + _silu(gb_s[s, r, head_cols[hd]]) * _layer_norm(y_b)).astype(BF16)
            yield

    MIX_LIGHT = 1 + HEADS + A_GROUPS // 2 + 2 * HEADS

    def emit_states(final_states):
        @pl.when(step - 1 < n_short)
        def _():
            for hd, (final_f, final_b) in enumerate(final_states):
                st_ref[0, 0, 0, hd] = final_f
                st_ref[0, 0, 1, hd] = final_b

    def stage_mlp():
        _, _, g1, sh2, sc2, g2 = _split_mod(modc_ref[0])
        x = _pick_block(step - 2 < n_short, xc_short, xc_long) + g1 * _dot(cat_s[slot_a], w_out_ref[...])
        hb = _rms_mod(x, nmlp_ref[...], sh2, sc2).astype(BF16)
        yield
        acc = yield from _mlp_pieces(hb, w1_ref, w2_ref)
        y_ref[...] = x + g2 * acc

    @pl.when(step == 0)
    def _():
        _run(stage_in())

    @pl.when(step == 1)
    def _():
        a, finals = stage_in(), []
        _weave(_pieces(a, 1), [a, stage_mix(finals)], IN_LIGHT + MIX_LIGHT)
        emit_states(finals)

    @pl.when((step >= 2) & (step < n_units))
    def _():
        a, c, finals = stage_in(), stage_mlp(), []
        heavy = _pieces(c, 1) + _pieces(a, 1) + _pieces(c, MLP_DOT_PIECES + 1)
        _weave(heavy, [a, stage_mix(finals)], IN_LIGHT + MIX_LIGHT)
        emit_states(finals)

    @pl.when(step == n_units)
    def _():
        finals = []
        _weave(_pieces(stage_mlp(), MLP_DOT_PIECES + 2), [stage_mix(finals)], MIX_LIGHT)
        emit_states(finals)

    @pl.when(step == n_units + 1)
    def _():
        _run(stage_mlp())


_HBM = pl.BlockSpec(memory_space=pl.ANY)
_WEIGHT_STAGING = [pltpu.VMEM((STAGE_SLOTS, STAGE_ROWS, STAGE_COLS), F32), pltpu.SemaphoreType.DMA((STAGE_SLOTS,))]


def _even_call(x_short, x_long, mod, sp, rope, params, weights, units, *, layer, even_index):
    n_units = units.total
    unit_a = lambda i: units.clamp(i)
    unit_b = lambda i: units.clamp(i - 1)
    unit_c = lambda i: units.clamp(i - 2)
    mod_row = lambda unit: layer * COND_ROWS + units.mod_row(unit)
    state_block = (1, 1, 2, HEADS, HEAD_DIM, HEAD_DIM)

    in_specs = [pl.BlockSpec((UNIT, D_MODEL), lambda i: (units.short_block(unit_a(i)), 0)),
                pl.BlockSpec((UNIT, D_MODEL), lambda i: (units.long_block(unit_a(i)), 0)),
                pl.BlockSpec((UNIT, D_MODEL), lambda i: (units.short_block(unit_c(i)), 0)),
                pl.BlockSpec((UNIT, D_MODEL), lambda i: (units.long_block(unit_c(i)), 0)),
                pl.BlockSpec((1, 1, N_MOD), lambda i: (mod_row(unit_a(i)), 0, 0)),
                pl.BlockSpec((1, 1, N_MOD), lambda i: (mod_row(unit_c(i)), 0, 0)),
                pl.BlockSpec((1, 2, HEADS, HEAD_DIM, HEAD_DIM), lambda i: (units.state_block(unit_b(i)), 0, 0, 0, 0)),
                pl.BlockSpec((UNIT, HEAD_DIM), lambda i: (units.rope_block(unit_a(i)), 0)),
                pl.BlockSpec((UNIT, HEAD_DIM), lambda i: (units.rope_block(unit_a(i)), 0))]
    in_specs += [_resident(p.shape) for p in params] + [_HBM] * len(weights)
    scratch = [
        pltpu.VMEM((2, UNIT, A_WIDTH), F32),
        pltpu.VMEM((2, UNIT, A_WIDTH), BF16),
        pltpu.VMEM((2, UNIT, QK_W), BF16),
        pltpu.VMEM((2, UNIT, QK_W), F32),
        pltpu.VMEM((2, UNIT, V_W), BF16),
        pltpu.VMEM((2, UNIT, V_W), F32),
        pltpu.VMEM((2, UNIT, V_W), F32),
        pltpu.VMEM((2, UNIT, OUT_WIDTH), BF16),
        pltpu.VMEM((HEADS, len(TABLE_NAMES), CHUNK, CHUNK), F32),
    ]
    scratch += [pltpu.VMEM(w.shape[1:], BF16) for w in weights] + _WEIGHT_STAGING
    return pl.pallas_call(
        functools.partial(_even_kernel, units=units, layer=layer, even_index=even_index),
        grid=(n_units + 2,),
        in_specs=in_specs,
        out_specs=[pl.BlockSpec((UNIT, D_MODEL), lambda i: (unit_c(i), 0)),
                   pl.BlockSpec(state_block, lambda i: (units.short_block(unit_b(i)), 0, 0, 0, 0, 0))],
        out_shape=[jax.ShapeDtypeStruct((n_units * UNIT, D_MODEL), F32),
                   jax.ShapeDtypeStruct((units.n_short,) + state_block[1:], F32)],
        scratch_shapes=scratch,
        compiler_params=pltpu.CompilerParams(
            dimension_semantics=("arbitrary",),
            vmem_limit_bytes=V7X_VMEM_LIMIT_BYTES),
        name="even_layer",
    )(x_short, x_long, x_short, x_long, mod, mod, sp, rope[0], rope[1], *params, *weights)


def _odd_kernel(xp_ref, x_ref, xn_ref, moda_ref, modc_ref, nmix_ref, nmlp_ref, fin_ref, pscale_ref,
                w_pool_hbm, w1_hbm, w2_hbm, y_short, y_long, x1_s, hb_s, w_pool_ref, w1_ref, w2_ref, stage, sem,
                *, n_units, n_short, short_len, long_len, layer, odd_index):
    step = pl.program_id(0)
    slot_a = lax.rem(step, 2)
    slot_b = 1 - slot_a
    unit = x_ref.shape[0]
    rows_w = unit + 2 * POOL_HALO

    @pl.when(step == 0)
    def _():
        _load_weights_bf16(sum((_weight_chunks(w_pool_hbm.at[odd_index, g], w_pool_ref.at[g])
                                for g in range(len(POOL_WINDOWS))), [])
                           + _weight_chunks(w1_hbm.at[layer], w1_ref)
                           + _weight_chunks(w2_hbm.at[layer], w2_ref), stage, sem)

    def stage_pool():
        sh1, sc1, g1, sh2, sc2, _ = _split_mod(moda_ref[0])
        xw = jnp.concatenate([xp_ref[...], x_ref[...], xn_ref[...]], axis=0)
        h = _rms_mod(xw, nmix_ref[...], sh1, sc1)
        seq_len = jnp.where(step < n_short, short_len, long_len)
        first = jnp.minimum(step, n_units - 1) * unit - POOL_HALO
        t = (lax.broadcasted_iota(jnp.int32, (rows_w, 1), 0) + first) & (seq_len - 1)
        yield

        def shift_down(a, k):
            return jnp.where(t >= k, pltpu.roll(a, k, axis=0), 0.0)

        def shift_up(a, k):
            return jnp.where(t < seq_len - k, pltpu.roll(a, rows_w - k, axis=0), 0.0)

        outs = []
        for g, w in enumerate(POOL_WINDOWS):
            hg = h[:, g * POOL_DIM:(g + 1) * POOL_DIM]
            half = w // 2
            left = shift_down(hg, 1)
            right = hg
            span = 1
            while span < half:
                left = left + shift_down(left, span)
                right = right + shift_up(right, span)
                span *= 2
            cnt = (jnp.minimum(t + half, seq_len) - jnp.maximum(t - half, 0)).astype(F32)
            pooled = ((left + right) / cnt - hg)[POOL_HALO:POOL_HALO + unit].astype(BF16)
            outs.append(_dot(pooled, w_pool_ref[g]))
            yield
        x1 = x_ref[...] + g1 * (jnp.concatenate(outs, axis=-1) * pscale_ref[...])
        x1_s[slot_a] = x1
        hb_s[slot_a] = _rms_mod(x1, nmlp_ref[...], sh2, sc2).astype(BF16)

    def stage_mlp(result):
        g2 = _split_mod(modc_ref[0])[5]
        acc = yield from _mlp_pieces(hb_s[slot_b], w1_ref, w2_ref)
        result.append(_rms(x1_s[slot_b] + g2 * acc) * fin_ref[...])

    def store(result):
        @pl.when(step - 1 < n_short)
        def _():
            y_short[...] = result[0]

        @pl.when(step - 1 >= n_short)
        def _():
            y_long[...] = result[0]

    pool_light = 2 + len(POOL_WINDOWS)

    @pl.when(step == 0)
    def _():
        _run(stage_pool())

    @pl.when((step >= 1) & (step < n_units))
    def _():
        result = []
        _weave(_pieces(stage_mlp(result), MLP_DOT_PIECES + 1), [stage_pool()], pool_light)
        store(result)

    @pl.when(step == n_units)
    def _():
        result = []
        _run(stage_mlp(result))
        store(result)


def _odd_call(x2, mod, params, weights, units, *, unit, short_len, long_len, layer, odd_index):
    n_units = units.total
    halo_per_unit = unit // POOL_HALO
    last_halo = x2.shape[0] // POOL_HALO - 1
    unit_a = lambda i: units.clamp(i)
    unit_c = lambda i: units.clamp(i - 1)
    mod_row = lambda u: layer * COND_ROWS + units.mod_row(u)
    assert (units.n_short * unit) % long_len == 0
    in_specs = [pl.BlockSpec((POOL_HALO, D_MODEL), lambda i: (jnp.maximum(unit_a(i) * halo_per_unit - 1, 0), 0)),
                pl.BlockSpec((unit, D_MODEL), lambda i: (unit_a(i), 0)),
                pl.BlockSpec((POOL_HALO, D_MODEL),
                             lambda i: (jnp.minimum((unit_a(i) + 1) * halo_per_unit, last_halo), 0)),
                pl.BlockSpec((1, 1, N_MOD), lambda i: (mod_row(unit_a(i)), 0, 0)),
                pl.BlockSpec((1, 1, N_MOD), lambda i: (mod_row(unit_c(i)), 0, 0))]
    in_specs += [_resident(p.shape) for p in params] + [_HBM] * len(weights)
    scratch = [pltpu.VMEM((2, unit, D_MODEL), F32),
               pltpu.VMEM((2, unit, D_MODEL), BF16)]
    scratch += [pltpu.VMEM(w.shape[1:], BF16) for w in weights] + _WEIGHT_STAGING
    return pl.pallas_call(
        functools.partial(_odd_kernel, n_units=n_units, n_short=units.n_short,
                          short_len=short_len, long_len=long_len, layer=layer, odd_index=odd_index),
        grid=(n_units + 1,),
        in_specs=in_specs,
        out_specs=[pl.BlockSpec((unit, D_MODEL), lambda i: (units.short_block(unit_c(i)), 0)),
                   pl.BlockSpec((unit, D_MODEL), lambda i: (units.long_block(unit_c(i)), 0))],
        out_shape=[jax.ShapeDtypeStruct((units.n_short * unit, D_MODEL), F32),
                   jax.ShapeDtypeStruct((units.n_long * unit, D_MODEL), F32)],
        scratch_shapes=scratch,
        compiler_params=pltpu.CompilerParams(
            dimension_semantics=("arbitrary",),
            vmem_limit_bytes=V7X_VMEM_LIMIT_BYTES),
        name="odd_layer",
    )(x2, x2, x2, mod, mod, *params, *weights)


def _rope_tables(n_tokens):
    f32 = np.float32
    rows = n_tokens // GRID_W
    pos_r = np.repeat(np.arange(rows, dtype=f32), GRID_W)
    pos_c = np.tile(np.arange(GRID_W, dtype=f32), rows)
    inv = (f32(ROPE_BASE) ** (-np.arange(ROPE_PAIRS_AXIS, dtype=f32) / f32(ROPE_PAIRS_AXIS))).astype(f32)
    ang = np.concatenate([pos_r[:, None] * inv, pos_c[:, None] * inv], axis=-1).astype(f32)
    cos, sin = np.cos(ang).astype(f32), np.sin(ang).astype(f32)
    return np.concatenate([cos, cos], axis=-1), np.concatenate([-sin, sin], axis=-1)


def kernel(x_prompt, x_sample, state_ret, c, c_ctx, ada_w, ada_b, norm_mix, norm_mlp, mlp_w1, mlp_w2,
           ev_w_in, ev_w_s, ev_b_s, ev_vnorm, ev_decay, ev_w_out, od_w_pool, od_pool_scale, final_norm):
    n_prompt, prompt_len, _ = x_prompt.shape
    n_sample, sample_len, _ = x_sample.shape
    assert DEPTH == 2 and n_sample + 1 <= COND_ROWS
    assert prompt_len == UNIT and sample_len % UNIT == 0
    units = _Units(n_prompt, n_sample, sample_len // UNIT)

    cond = jnp.concatenate([c_ctx[None, :], c, jnp.zeros((COND_ROWS - 1 - n_sample, D_MODEL), F32)])
    mod = _ada_call(cond, ada_w, ada_b)

    row = lambda v: v.reshape(1, -1)
    dec = jnp.broadcast_to(ev_decay[0].reshape(2 * HEADS, 1), (2 * HEADS, CHUNK))
    even_params = (
        row(norm_mix[0]), row(norm_mlp[0]),
        jnp.broadcast_to(ev_b_s[0][:, :, None], (A_GROUPS, CHUNK, A_GROUP_DIM)),
        row(ev_vnorm[0]), dec,
    )
    odd_params = (row(norm_mix[1]), row(norm_mlp[1]), row(final_norm), row(od_pool_scale[0]))
    even_weights = (ev_w_in, ev_w_out, mlp_w1, mlp_w2, ev_w_s)
    odd_weights = (od_w_pool, mlp_w1, mlp_w2)

    xp2 = x_prompt.reshape(n_prompt * prompt_len, D_MODEL)
    xs2 = x_sample.reshape(n_sample * sample_len, D_MODEL)
    cos2, sin2 = _rope_tables(sample_len)
    sp = _ret_state_call(xs2, mod, state_ret, (cos2, sin2), row(norm_mix[0]), ev_w_in, dec,
                         n_seq=n_sample, seq_len=sample_len, layer=0, even_index=0)
    rope = (np.concatenate([np.ones((UNIT, HEAD_DIM), np.float32), cos2]),
            np.concatenate([np.zeros((UNIT, HEAD_DIM), np.float32), sin2]))
    x_all, new_state = _even_call(xp2, xs2, mod, sp, rope, even_params, even_weights, units,
                                  layer=0, even_index=0)
    y_prompt, y_sample = _odd_call(x_all, mod, odd_params, odd_weights, units, unit=UNIT,
                                   short_len=prompt_len, long_len=sample_len, layer=1, odd_index=0)

    return (y_prompt.reshape(n_prompt, prompt_len, D_MODEL),
            y_sample.reshape(n_sample, sample_len, D_MODEL),
            new_state)
```

```python
import functools

import jax
import jax.numpy as jnp
import numpy as np
from jax import lax
from jax.experimental import pallas as pl
from jax.experimental.pallas import tpu as pltpu

D_MODEL = 1024
DEPTH = 2
GRID_W = 64
A_WIDTH = D_MODEL // 2
A_GROUPS = 4
A_GROUP_DIM = A_WIDTH // A_GROUPS
CHUNK = 128
HEADS = 4
HEAD_DIM = (D_MODEL // 2) // HEADS
ROPE_BASE = 10000.0
ROPE_PAIRS_AXIS = HEAD_DIM // 4
POOL_WINDOWS = (2, 4, 8, 16)
POOL_DIM = D_MODEL // len(POOL_WINDOWS)
POOL_HALO = max(POOL_WINDOWS) // 2
D_FF = 4 * D_MODEL
EPS = 1e-6
QK_W = HEADS * HEAD_DIM
V_W = HEADS * HEAD_DIM
IN_WIDTH = 2 * A_WIDTH + 2 * QK_W + 3 * V_W
OUT_WIDTH = A_WIDTH + V_W
N_MOD = 6 * D_MODEL
K_COL = 2 * A_WIDTH + QK_W
V_COL = K_COL + QK_W

UNIT = 2 * CHUNK
UNIT_CHUNKS = UNIT // CHUNK
FF_CHUNK = 1024
ADA_BLOCK_N = 3072
COND_ROWS = 8
STAGE_ROWS = 512
STAGE_COLS = 1024
STAGE_SLOTS = 3
V7X_VMEM_LIMIT_BYTES = 60 * 1024 * 1024

F32 = jnp.float32
BF16 = jnp.bfloat16


def _dot(a, b):
    return jnp.dot(a, b, preferred_element_type=F32)


def _dot_nt(a, b):
    return lax.dot_general(a, b, (((1,), (1,)), ((), ())), preferred_element_type=F32)


def _silu(x):
    return x * jax.nn.sigmoid(x)


def _gelu_tanh(x):
    return 0.5 * x * (1.0 + jnp.tanh(0.7978845608028654 * (x + 0.044715 * (x * x * x))))


def _rms(x):
    return x * lax.rsqrt(jnp.mean(x * x, axis=-1, keepdims=True) + EPS)


def _rms_mod(x, gain, shift, scale):
    return (_rms(x) * gain) * (1.0 + scale) + shift


def _layer_norm(x):
    mu = jnp.mean(x, axis=-1, keepdims=True)
    d = x - mu
    return d * lax.rsqrt(jnp.mean(d * d, axis=-1, keepdims=True) + EPS)


def _split_mod(mod):
    return [mod[:, i * D_MODEL:(i + 1) * D_MODEL] for i in range(6)]


def _rope(x, cos2, sin2):
    return x * cos2 + pltpu.roll(x, HEAD_DIM // 2, axis=1) * sin2


def _pick_block(take_first, a_ref, b_ref):
    rows = lax.broadcasted_iota(jnp.int32, (a_ref.shape[0], 1), 0)
    return jnp.where(rows < jnp.where(take_first, a_ref.shape[0], 0), a_ref[...], b_ref[...])


def _run(gen, n=None):
    done = 0
    while n is None or done < n:
        try:
            next(gen)
        except StopIteration:
            break
        done += 1
    return done


def _weave(heavy, lights, n_light):
    heavy = list(heavy)
    pending = _roundrobin(lights)
    ran = 0
    for k, piece in enumerate(heavy):
        piece()
        want = ((k + 1) * n_light + len(heavy) - 1) // len(heavy)
        ran += _run(pending, want - ran)
    _run(pending)


def _pieces(gen, n):
    return [functools.partial(next, gen, None) for _ in range(n)]


def _roundrobin(gens):
    live = list(gens)
    while live:
        for g in list(live):
            try:
                next(g)
            except StopIteration:
                live.remove(g)
                continue
            yield


MLP_DOT_PIECES = 2 * (D_FF // FF_CHUNK)


def _mlp_pieces(hb, w1_ref, w2_ref):
    n_ff = D_FF // FF_CHUNK
    up = lambda j: _dot(hb, w1_ref[:, j * FF_CHUNK:(j + 1) * FF_CHUNK])
    a = up(0)
    yield
    acc = None
    for j in range(n_ff):
        a_next = None
        if j + 1 < n_ff:
            a_next = up(j + 1)
            yield
        p = _dot(jnp.square(jnp.maximum(a, 0.0)).astype(BF16), w2_ref[j * FF_CHUNK:(j + 1) * FF_CHUNK, :])
        acc = p if acc is None else acc + p
        a = a_next
        yield
    return acc


def _resident(shape):
    return pl.BlockSpec(shape, lambda i: (0,) * len(shape), pipeline_mode=pl.Buffered(1))


def _weight_chunks(w_hbm, w_vmem):
    rows, cols = w_hbm.shape
    jobs = []
    for r0 in range(0, rows, STAGE_ROWS):
        for c0 in range(0, cols, STAGE_COLS):
            window = (pl.ds(r0, min(STAGE_ROWS, rows - r0)), pl.ds(c0, min(STAGE_COLS, cols - c0)))
            jobs.append((w_hbm.at[window], w_vmem.at[window]))
    return jobs


def _load_weights_bf16(jobs, stage, sem):
    n_slots = stage.shape[0]

    def copy(k):
        r, c = jobs[k][0].shape
        slot = k % n_slots
        return pltpu.make_async_copy(jobs[k][0], stage.at[slot, pl.ds(0, r), pl.ds(0, c)], sem.at[slot])

    for k in range(min(n_slots - 1, len(jobs))):
        copy(k).start()
    for k, (src, dst) in enumerate(jobs):
        if k + n_slots - 1 < len(jobs):
            copy(k + n_slots - 1).start()
        copy(k).wait()
        r, c = src.shape
        dst[...] = stage[k % n_slots, 0:r, 0:c].astype(BF16)


class _Units:
    def __init__(self, n_short, n_long, units_per_long):
        self.n_short, self.n_long, self.units_per_long = n_short, n_long * units_per_long, units_per_long
        self.total = self.n_short + self.n_long

    def clamp(self, unit):
        return jnp.clip(unit, 0, self.total - 1)

    def short_block(self, unit):
        return jnp.minimum(unit, self.n_short - 1)

    def long_block(self, unit):
        return jnp.clip(unit - self.n_short, 0, self.n_long - 1)

    def mod_row(self, unit):
        return jnp.where(unit < self.n_short, 0, 1 + (unit - self.n_short) // self.units_per_long)

    def rope_block(self, unit):
        return jnp.where(unit < self.n_short, 0, 1 + (unit - self.n_short) % self.units_per_long)

    def state_block(self, unit):
        return jnp.where(unit < self.n_short, 0, self.units_per_long + unit - self.n_short)


def _ada_kernel(shared_ref, each_ref, w_ref, b_ref, out_ref):
    row_id = lax.broadcasted_iota(jnp.int32, (COND_ROWS, 1), 0)
    cond = jnp.where(row_id == 0, shared_ref[...], 0.0)
    for r in range(each_ref.shape[0]):
        cond = jnp.where(row_id == 1 + r, each_ref[r:r + 1, :], cond)
    s = _silu(cond).astype(BF16)
    rows = _dot(s, w_ref[0].astype(BF16)) + b_ref[pl.ds(pl.program_id(0), 1), :]
    for r in range(COND_ROWS):
        out_ref[r] = rows[r:r + 1, :]


def _ada_call(cond_shared, cond_each, ada_w, ada_b):
    return pl.pallas_call(
        _ada_kernel,
        grid=(DEPTH, N_MOD // ADA_BLOCK_N),
        in_specs=[
            pl.BlockSpec(cond_shared.shape, lambda i, j: (0, 0)),
            pl.BlockSpec(cond_each.shape, lambda i, j: (0, 0)),
            pl.BlockSpec((1, D_MODEL, ADA_BLOCK_N), lambda i, j: (i, 0, j)),
            pl.BlockSpec((DEPTH, ADA_BLOCK_N), lambda i, j: (0, j)),
        ],
        out_specs=pl.BlockSpec((COND_ROWS, 1, ADA_BLOCK_N), lambda i, j: (i, 0, j)),
        out_shape=jax.ShapeDtypeStruct((DEPTH * COND_ROWS, 1, N_MOD), F32),
        compiler_params=pltpu.CompilerParams(
            dimension_semantics=("arbitrary", "arbitrary"),
            vmem_limit_bytes=V7X_VMEM_LIMIT_BYTES),
        name="ada_mod",
    )(cond_shared, cond_each, ada_w, ada_b)


def _decay_tables(dec_ref, hd):
    row = lax.broadcasted_iota(jnp.int32, (CHUNK, CHUNK), 0).astype(F32)
    col = lax.broadcasted_iota(jnp.int32, (CHUNK, CHUNK), 1).astype(F32)
    lg_f = jnp.broadcast_to(-jnp.exp(dec_ref[hd:hd + 1, :]), (CHUNK, CHUNK))
    lg_b = jnp.broadcast_to(-jnp.exp(dec_ref[HEADS + hd:HEADS + hd + 1, :]), (CHUNK, CHUNK))
    d_f = row - col
    d_b = col - row
    return dict(
        decay_f=jnp.where(d_f >= 0, jnp.exp(lg_f * jnp.maximum(d_f, 0.0)), 0.0),
        decay_b=jnp.where(d_b >= 0, jnp.exp(lg_b * jnp.maximum(d_b, 0.0)), 0.0),
        qd_f=jnp.exp(lg_f * (row + 1.0)),
        qd_b=jnp.exp(lg_b * (CHUNK - row)),
        kd_f=jnp.exp(lg_f * (CHUNK - 1.0 - row)),
        kd_b=jnp.exp(lg_b * row),
        cd_f=jnp.exp(lg_f * CHUNK),
        cd_b=jnp.exp(lg_b * CHUNK),
    )


TABLE_NAMES = ("decay_f", "decay_b", "qd_f", "qd_b", "kd_f", "kd_b", "cd_f", "cd_b")


def _ret_state_kernel(x_ref, mod_ref, s0_ref, cos_ref, sin_ref, nmix_ref, wk_ref, wv_ref, dec_ref, sp_ref,
                      *, n_chunk):
    step = pl.program_id(0)

    @pl.when(step == 0)
    def _():
        sp_ref[...] = jnp.zeros_like(sp_ref)

    @pl.when(step > 0)
    def _():
        sh1, sc1 = _split_mod(mod_ref[0])[:2]
        hb = _rms_mod(x_ref[...], nmix_ref[...], sh1, sc1).astype(BF16)
        zk = _dot(hb, wk_ref[...].astype(BF16)) * (HEAD_DIM ** -0.5)
        zv = _dot(hb, wv_ref[...].astype(BF16)).astype(BF16)
        cos2, sin2 = cos_ref[...], sin_ref[...]
        for hd in range(HEADS):
            lo = hd * HEAD_DIM
            tb = _decay_tables(dec_ref, hd)
            kh = _rope(zk[:, lo:lo + HEAD_DIM], cos2, sin2)
            kv_f, kv_b = [], []
            for c in range(n_chunk):
                kc = kh[c * CHUNK:(c + 1) * CHUNK]
                vc = zv[c * CHUNK:(c + 1) * CHUNK, lo:lo + HEAD_DIM]
                kv_f.append(_dot((kc * tb["kd_f"]).T.astype(BF16), vc))
                kv_b.append(_dot((kc * tb["kd_b"]).T.astype(BF16), vc))
            st = s0_ref[0, 0, 0, hd]
            for c in range(n_chunk):
                if c % UNIT_CHUNKS == 0:
                    sp_ref[c // UNIT_CHUNKS, 0, hd] = st
                st = tb["cd_f"] * st + kv_f[c]
            st = s0_ref[0, 0, 1, hd]
            for c in reversed(range(n_chunk)):
                if c % UNIT_CHUNKS == UNIT_CHUNKS - 1:
                    sp_ref[c // UNIT_CHUNKS, 1, hd] = st
                st = tb["cd_b"] * st + kv_b[c]


def _ret_state_call(x2, mod, s0, rope, nmix, w_in, dec, *, n_seq, seq_len, layer, even_index):
    n_chunk = seq_len // CHUNK
    units = seq_len // UNIT
    seq = lambda i: jnp.maximum(i - 1, 0)
    mod_row = lambda i: layer * COND_ROWS + 1 + seq(i)
    return pl.pallas_call(
        functools.partial(_ret_state_kernel, n_chunk=n_chunk),
        grid=(n_seq + 1,),
        in_specs=[
            pl.BlockSpec((seq_len, D_MODEL), lambda i: (seq(i), 0)),
            pl.BlockSpec((1, 1, N_MOD), lambda i: (mod_row(i), 0, 0)),
            pl.BlockSpec((1, 1, 2, HEADS, HEAD_DIM, HEAD_DIM), lambda i: (seq(i), 0, 0, 0, 0, 0)),
            _resident((seq_len, HEAD_DIM)), _resident((seq_len, HEAD_DIM)),
            _resident((1, D_MODEL)),
            pl.BlockSpec((None, D_MODEL, QK_W), lambda i: (even_index, 0, K_COL // QK_W),
                         pipeline_mode=pl.Buffered(1)),
            pl.BlockSpec((None, D_MODEL, V_W), lambda i: (even_index, 0, V_COL // V_W),
                         pipeline_mode=pl.Buffered(1)),
            _resident(dec.shape),
        ],
        out_specs=pl.BlockSpec((units, 2, HEADS, HEAD_DIM, HEAD_DIM), lambda i: (i, 0, 0, 0, 0)),
        out_shape=jax.ShapeDtypeStruct(((n_seq + 1) * units, 2, HEADS, HEAD_DIM, HEAD_DIM), F32),
        compiler_params=pltpu.CompilerParams(
            dimension_semantics=("arbitrary",),
            vmem_limit_bytes=V7X_VMEM_LIMIT_BYTES),
        name="ret_state",
    )(x2, mod, s0, rope[0], rope[1], nmix, w_in, w_in, dec)


def _even_kernel(xa_short, xa_long, xc_short, xc_long, moda_ref, modc_ref, sp_ref, cos_ref, sin_ref,
                 nmix_ref, nmlp_ref, b_s_ref, vgain_ref, dec_ref, w_in_hbm, w_out_hbm, w1_hbm, w2_hbm, w_s_hbm,
                 y_ref, st_ref, ug_s, vg_s, q_s, k_s, v_s, gf_s, gb_s, cat_s, tab_s,
                 w_in_ref, w_out_ref, w1_ref, w2_ref, w_s_ref, stage, sem, *, units, layer, even_index):
    n_units, n_short = units.total, units.n_short
    step = pl.program_id(0)
    slot_a = lax.rem(step, 2)
    slot_b = 1 - slot_a

    @pl.when(step == 0)
    def _():
        _load_weights_bf16(_weight_chunks(w_in_hbm.at[even_index], w_in_ref)
                           + _weight_chunks(w_out_hbm.at[even_index], w_out_ref)
                           + _weight_chunks(w1_hbm.at[layer], w1_ref)
                           + _weight_chunks(w2_hbm.at[layer], w2_ref)
                           + sum((_weight_chunks(w_s_hbm.at[even_index, g], w_s_ref.at[g])
                                  for g in range(A_GROUPS)), []), stage, sem)
        for hd in range(HEADS):
            tb = _decay_tables(dec_ref, hd)
            for n, name in enumerate(TABLE_NAMES):
                tab_s[hd, n] = tb[name]

    def stage_in():
        sh1, sc1 = _split_mod(moda_ref[0])[:2]
        x = _pick_block(step < n_short, xa_short, xa_long)
        hb = _rms_mod(x, nmix_ref[...], sh1, sc1).astype(BF16)
        z = _dot(hb, w_in_ref[...])
        yield
        ug_s[slot_a] = z[:, 0:A_WIDTH]
        yield
        zv = _gelu_tanh(z[:, A_WIDTH:2 * A_WIDTH])
        vgn = [_layer_norm(zv[:, g * A_GROUP_DIM:(g + 1) * A_GROUP_DIM]) for g in range(A_GROUPS)]
        vg_s[slot_a] = (jnp.concatenate(vgn, axis=-1) * vgain_ref[...]).astype(BF16)
        yield
        cos2, sin2 = cos_ref[...], sin_ref[...]
        heads = lambda zz: jnp.concatenate(
            [_rope(zz[:, h * HEAD_DIM:(h + 1) * HEAD_DIM], cos2, sin2) for h in range(HEADS)], -1)
        q_s[slot_a] = heads(z[:, 2 * A_WIDTH:K_COL]).astype(BF16)
        yield
        k_s[slot_a] = heads(z[:, K_COL:V_COL] * (HEAD_DIM ** -0.5))
        v_s[slot_a] = z[:, V_COL:V_COL + V_W].astype(BF16)
        yield
        gf_s[slot_a] = z[:, V_COL + V_W:V_COL + 2 * V_W]
        yield
        gb_s[slot_a] = z[:, V_COL + 2 * V_W:V_COL + 3 * V_W]

    IN_LIGHT = 6

    def stage_mix(final_states):
        s = slot_b
        rows = [slice(c * CHUNK, (c + 1) * CHUNK) for c in range(UNIT_CHUNKS)]
        head_cols = [slice(hd * HEAD_DIM, (hd + 1) * HEAD_DIM) for hd in range(HEADS)]
        table = lambda hd, name: tab_s[hd, TABLE_NAMES.index(name)]

        sp2 = []
        for g in range(A_GROUPS):
            lo = g * A_GROUP_DIM
            vg2 = jnp.concatenate([vg_s[s, r, lo:lo + A_GROUP_DIM] for r in rows], axis=-1)
            sp2.append(_dot(w_s_ref[g], vg2))
        yield
        scores, kv_f, kv_b = [], [], []
        for hd in range(HEADS):
            kc = [k_s[s, r, head_cols[hd]] for r in rows]
            vc = [v_s[s, r, head_cols[hd]] for r in rows]
            scores.append([_dot_nt(q_s[s, r, head_cols[hd]], kc[c].astype(BF16)) for c, r in enumerate(rows)])
            kv_f.append([_dot((kc[c] * table(hd, "kd_f")).T.astype(BF16), vc[c]) for c in range(UNIT_CHUNKS)])
            kv_b.append([_dot((kc[c] * table(hd, "kd_b")).T.astype(BF16), vc[c]) for c in range(UNIT_CHUNKS)])
            yield
        for g in range(A_GROUPS):
            lo = g * A_GROUP_DIM
            for c, r in enumerate(rows):
                gate = sp2[g][:, c * CHUNK:(c + 1) * CHUNK] + b_s_ref[g]
                cat_s[s, r, lo:lo + A_GROUP_DIM] = (_gelu_tanh(ug_s[s, r, lo:lo + A_GROUP_DIM]) * gate).astype(BF16)
            if g % 2 == 1:
                yield
        y_raw = []
        for hd in range(HEADS):
            enter_f, st = [], sp_ref[0, 0, hd]
            for c in range(UNIT_CHUNKS):
                enter_f.append(st)
                st = table(hd, "cd_f") * st + kv_f[hd][c]
            final_f = st
            enter_b, st = [None] * UNIT_CHUNKS, sp_ref[0, 1, hd]
            for c in reversed(range(UNIT_CHUNKS)):
                enter_b[c] = st
                st = table(hd, "cd_b") * st + kv_b[hd][c]
            final_states.append((final_f, st))
            per_chunk = []
            for c, r in enumerate(rows):
                vc = v_s[s, r, head_cols[hd]]
                qf = q_s[s, r, head_cols[hd]].astype(F32)

                def direction(decay, qd, enter):
                    p = (scores[hd][c] * table(hd, decay)).astype(BF16)
                    lhs = jnp.concatenate([p, (qf * table(hd, qd)).astype(BF16)], axis=-1)
                    return _dot(lhs, jnp.concatenate([vc, enter.astype(BF16)], axis=0))

                per_chunk.append((direction("decay_f", "qd_f", enter_f[c]), direction("decay_b", "qd_b", enter_b[c])))
            y_raw.append(per_chunk)
            yield
        for hd in range(HEADS):
            for c, r in enumerate(rows):
                y_f, y_b = y_raw[hd][c]
                cols = slice(A_WIDTH + hd * HEAD_DIM, A_WIDTH + (hd + 1) * HEAD_DIM)
                cat_s[s, r, cols] = (_silu(gf_s[s, r, head_cols[hd]]) * _layer_norm(y_f)
                                     + _silu(gb_s[s, r, head_cols[hd]]) * _layer_norm(y_b)).astype(BF16)
            yield

    MIX_LIGHT = 1 + HEADS + A_GROUPS // 2 + 2 * HEADS

    def emit_states(final_states):
        @pl.when(step - 1 < n_short)
        def _():
            for hd, (final_f, final_b) in enumerate(final_states):
                st_ref[0, 0, 0, hd] = final_f
                st_ref[0, 0, 1, hd] = final_b

    def stage_mlp():
        _, _, g1, sh2, sc2, g2 = _split_mod(modc_ref[0])
        x = _pick_block(step - 2 < n_short, xc_short, xc_long) + g1 * _dot(cat_s[slot_a], w_out_ref[...])
        hb = _rms_mod(x, nmlp_ref[...], sh2, sc2).astype(BF16)
        yield
        acc = yield from _mlp_pieces(hb, w1_ref, w2_ref)
        y_ref[...] = x + g2 * acc

    @pl.when(step == 0)
    def _():
        _run(stage_in())

    @pl.when(step == 1)
    def _():
        a, finals = stage_in(), []
        _weave(_pieces(a, 1), [a, stage_mix(finals)], IN_LIGHT + MIX_LIGHT)
        emit_states(finals)

    @pl.when((step >= 2) & (step < n_units))
    def _():
        a, c, finals = stage_in(), stage_mlp(), []
        heavy = _pieces(c, 1) + _pieces(a, 1) + _pieces(c, MLP_DOT_PIECES + 1)
        _weave(heavy, [a, stage_mix(finals)], IN_LIGHT + MIX_LIGHT)
        emit_states(finals)

    @pl.when(step == n_units)
    def _():
        finals = []
        _weave(_pieces(stage_mlp(), MLP_DOT_PIECES + 2), [stage_mix(finals)], MIX_LIGHT)
        emit_states(finals)

    @pl.when(step == n_units + 1)
    def _():
        _run(stage_mlp())


_HBM = pl.BlockSpec(memory_space=pl.ANY)
_WEIGHT_STAGING = [pltpu.VMEM((STAGE_SLOTS, STAGE_ROWS, STAGE_COLS), F32), pltpu.SemaphoreType.DMA((STAGE_SLOTS,))]


def _even_call(x_short, x_long, mod, sp, rope, params, weights, units, *, layer, even_index):
    n_units = units.total
    unit_a = lambda i: units.clamp(i)
    unit_b = lambda i: units.clamp(i - 1)
    unit_c = lambda i: units.clamp(i - 2)
    mod_row = lambda unit: layer * COND_ROWS + units.mod_row(unit)
    state_block = (1, 1, 2, HEADS, HEAD_DIM, HEAD_DIM)

    in_specs = [pl.BlockSpec((UNIT, D_MODEL), lambda i: (units.short_block(unit_a(i)), 0)),
                pl.BlockSpec((UNIT, D_MODEL), lambda i: (units.long_block(unit_a(i)), 0)),
                pl.BlockSpec((UNIT, D_MODEL), lambda i: (units.short_block(unit_c(i)), 0)),
                pl.BlockSpec((UNIT, D_MODEL), lambda i: (units.long_block(unit_c(i)), 0)),
                pl.BlockSpec((1, 1, N_MOD), lambda i: (mod_row(unit_a(i)), 0, 0)),
                pl.BlockSpec((1, 1, N_MOD), lambda i: (mod_row(unit_c(i)), 0, 0)),
                pl.BlockSpec((1, 2, HEADS, HEAD_DIM, HEAD_DIM), lambda i: (units.state_block(unit_b(i)), 0, 0, 0, 0)),
                pl.BlockSpec((UNIT, HEAD_DIM), lambda i: (units.rope_block(unit_a(i)), 0)),
                pl.BlockSpec((UNIT, HEAD_DIM), lambda i: (units.rope_block(unit_a(i)), 0))]
    in_specs += [_resident(p.shape) for p in params] + [_HBM] * len(weights)
    scratch = [
        pltpu.VMEM((2, UNIT, A_WIDTH), F32),
        pltpu.VMEM((2, UNIT, A_WIDTH), BF16),
        pltpu.VMEM((2, UNIT, QK_W), BF16),
        pltpu.VMEM((2, UNIT, QK_W), F32),
        pltpu.VMEM((2, UNIT, V_W), BF16),
        pltpu.VMEM((2, UNIT, V_W), F32),
        pltpu.VMEM((2, UNIT, V_W), F32),
        pltpu.VMEM((2, UNIT, OUT_WIDTH), BF16),
        pltpu.VMEM((HEADS, len(TABLE_NAMES), CHUNK, CHUNK), F32),
    ]
    scratch += [pltpu.VMEM(w.shape[1:], BF16) for w in weights] + _WEIGHT_STAGING
    return pl.pallas_call(
        functools.partial(_even_kernel, units=units, layer=layer, even_index=even_index),
        grid=(n_units + 2,),
        in_specs=in_specs,
        out_specs=[pl.BlockSpec((UNIT, D_MODEL), lambda i: (unit_c(i), 0)),
                   pl.BlockSpec(state_block, lambda i: (units.short_block(unit_b(i)), 0, 0, 0, 0, 0))],
        out_shape=[jax.ShapeDtypeStruct((n_units * UNIT, D_MODEL), F32),
                   jax.ShapeDtypeStruct((units.n_short,) + state_block[1:], F32)],
        scratch_shapes=scratch,
        compiler_params=pltpu.CompilerParams(
            dimension_semantics=("arbitrary",),
            vmem_limit_bytes=V7X_VMEM_LIMIT_BYTES),
        name="even_layer",
    )(x_short, x_long, x_short, x_long, mod, mod, sp, rope[0], rope[1], *params, *weights)


def _odd_kernel(xp_ref, x_ref, xn_ref, moda_ref, modc_ref, nmix_ref, nmlp_ref, fin_ref, pscale_ref,
                w_pool_hbm, w1_hbm, w2_hbm, y_short, y_long, x1_s, hb_s, w_pool_ref, w1_ref, w2_ref, stage, sem,
                *, n_units, n_short, short_len, long_len, layer, odd_index):
    step = pl.program_id(0)
    slot_a = lax.rem(step, 2)
    slot_b = 1 - slot_a
    unit = x_ref.shape[0]
    rows_w = unit + 2 * POOL_HALO

    @pl.when(step == 0)
    def _():
        _load_weights_bf16(sum((_weight_chunks(w_pool_hbm.at[odd_index, g], w_pool_ref.at[g])
                                for g in range(len(POOL_WINDOWS))), [])
                           + _weight_chunks(w1_hbm.at[layer], w1_ref)
                           + _weight_chunks(w2_hbm.at[layer], w2_ref), stage, sem)

    def stage_pool():
        sh1, sc1, g1, sh2, sc2, _ = _split_mod(moda_ref[0])
        xw = jnp.concatenate([xp_ref[...], x_ref[...], xn_ref[...]], axis=0)
        h = _rms_mod(xw, nmix_ref[...], sh1, sc1)
        seq_len = jnp.where(step < n_short, short_len, long_len)
        first = jnp.minimum(step, n_units - 1) * unit - POOL_HALO
        t = (lax.broadcasted_iota(jnp.int32, (rows_w, 1), 0) + first) & (seq_len - 1)
        yield

        def shift_down(a, k):
            return jnp.where(t >= k, pltpu.roll(a, k, axis=0), 0.0)

        def shift_up(a, k):
            return jnp.where(t < seq_len - k, pltpu.roll(a, rows_w - k, axis=0), 0.0)

        outs = []
        for g, w in enumerate(POOL_WINDOWS):
            hg = h[:, g * POOL_DIM:(g + 1) * POOL_DIM]
            half = w // 2
            left = shift_down(hg, 1)
            right = hg
            span = 1
            while span < half:
                left = left + shift_down(left, span)
                right = right + shift_up(right, span)
                span *= 2
            cnt = (jnp.minimum(t + half, seq_len) - jnp.maximum(t - half, 0)).astype(F32)
            pooled = ((left + right) / cnt - hg)[POOL_HALO:POOL_HALO + unit].astype(BF16)
            outs.append(_dot(pooled, w_pool_ref[g]))
            yield
        x1 = x_ref[...] + g1 * (jnp.concatenate(outs, axis=-1) * pscale_ref[...])
        x1_s[slot_a] = x1
        hb_s[slot_a] = _rms_mod(x1, nmlp_ref[...], sh2, sc2).astype(BF16)

    def stage_mlp(result):
        g2 = _split_mod(modc_ref[0])[5]
        acc = yield from _mlp_pieces(hb_s[slot_b], w1_ref, w2_ref)
        result.append(_rms(x1_s[slot_b] + g2 * acc) * fin_ref[...])

    def store(result):
        @pl.when(step - 1 < n_short)
        def _():
            y_short[...] = result[0]

        @pl.when(step - 1 >= n_short)
        def _():
            y_long[...] = result[0]

    pool_light = 2 + len(POOL_WINDOWS)

    @pl.when(step == 0)
    def _():
        _run(stage_pool())

    @pl.when((step >= 1) & (step < n_units))
    def _():
        result = []
        _weave(_pieces(stage_mlp(result), MLP_DOT_PIECES + 1), [stage_pool()], pool_light)
        store(result)

    @pl.when(step == n_units)
    def _():
        result = []
        _run(stage_mlp(result))
        store(result)


def _odd_call(x2, mod, params, weights, units, *, unit, short_len, long_len, layer, odd_index):
    n_units = units.total
    halo_per_unit = unit // POOL_HALO
    last_halo = x2.shape[0] // POOL_HALO - 1
    unit_a = lambda i: units.clamp(i)
    unit_c = lambda i: units.clamp(i - 1)
    mod_row = lambda u: layer * COND_ROWS + units.mod_row(u)
    assert (units.n_short * unit) % long_len == 0
    in_specs = [pl.BlockSpec((POOL_HALO, D_MODEL), lambda i: (jnp.maximum(unit_a(i) * halo_per_unit - 1, 0), 0)),
                pl.BlockSpec((unit, D_MODEL), lambda i: (unit_a(i), 0)),
                pl.BlockSpec((POOL_HALO, D_MODEL),
                             lambda i: (jnp.minimum((unit_a(i) + 1) * halo_per_unit, last_halo), 0)),
                pl.BlockSpec((1, 1, N_MOD), lambda i: (mod_row(unit_a(i)), 0, 0)),
                pl.BlockSpec((1, 1, N_MOD), lambda i: (mod_row(unit_c(i)), 0, 0))]
    in_specs += [_resident(p.shape) for p in params] + [_HBM] * len(weights)
    scratch = [pltpu.VMEM((2, unit, D_MODEL), F32),
               pltpu.VMEM((2, unit, D_MODEL), BF16)]
    scratch += [pltpu.VMEM(w.shape[1:], BF16) for w in weights] + _WEIGHT_STAGING
    return pl.pallas_call(
        functools.partial(_odd_kernel, n_units=n_units, n_short=units.n_short,
                          short_len=short_len, long_len=long_len, layer=layer, odd_index=odd_index),
        grid=(n_units + 1,),
        in_specs=in_specs,
        out_specs=[pl.BlockSpec((unit, D_MODEL), lambda i: (units.short_block(unit_c(i)), 0)),
                   pl.BlockSpec((unit, D_MODEL), lambda i: (units.long_block(unit_c(i)), 0))],
        out_shape=[jax.ShapeDtypeStruct((units.n_short * unit, D_MODEL), F32),
                   jax.ShapeDtypeStruct((units.n_long * unit, D_MODEL), F32)],
        scratch_shapes=scratch,
        compiler_params=pltpu.CompilerParams(
            dimension_semantics=("arbitrary",),
            vmem_limit_bytes=V7X_VMEM_LIMIT_BYTES),
        name="odd_layer",
    )(x2, x2, x2, mod, mod, *params, *weights)


def _rope_tables(n_tokens):
    f32 = np.float32
    rows = n_tokens // GRID_W
    pos_r = np.repeat(np.arange(rows, dtype=f32), GRID_W)
    pos_c = np.tile(np.arange(GRID_W, dtype=f32), rows)
    inv = (f32(ROPE_BASE) ** (-np.arange(ROPE_PAIRS_AXIS, dtype=f32) / f32(ROPE_PAIRS_AXIS))).astype(f32)
    ang = np.concatenate([pos_r[:, None] * inv, pos_c[:, None] * inv], axis=-1).astype(f32)
    cos, sin = np.cos(ang).astype(f32), np.sin(ang).astype(f32)
    return np.concatenate([cos, cos], axis=-1), np.concatenate([-sin, sin], axis=-1)


def kernel(x_prompt, x_sample, state_ret, c, c_ctx, ada_w, ada_b, norm_mix, norm_mlp, mlp_w1, mlp_w2,
           ev_w_in, ev_w_s, ev_b_s, ev_vnorm, ev_decay, ev_w_out, od_w_pool, od_pool_scale, final_norm):
    n_prompt, prompt_len, _ = x_prompt.shape
    n_sample, sample_len, _ = x_sample.shape
    assert DEPTH == 2 and n_sample + 1 <= COND_ROWS
    assert prompt_len == UNIT and sample_len % UNIT == 0
    units = _Units(n_prompt, n_sample, sample_len // UNIT)

    mod = _ada_call(c_ctx.reshape(1, D_MODEL), c, ada_w, ada_b)

    row = lambda v: v.reshape(1, -1)
    dec = jnp.broadcast_to(ev_decay[0].reshape(2 * HEADS, 1), (2 * HEADS, CHUNK))
    even_params = (
        row(norm_mix[0]), row(norm_mlp[0]),
        jnp.broadcast_to(ev_b_s[0][:, :, None], (A_GROUPS, CHUNK, A_GROUP_DIM)),
        row(ev_vnorm[0]), dec,
    )
    odd_params = (row(norm_mix[1]), row(norm_mlp[1]), row(final_norm), row(od_pool_scale[0]))
    even_weights = (ev_w_in, ev_w_out, mlp_w1, mlp_w2, ev_w_s)
    odd_weights = (od_w_pool, mlp_w1, mlp_w2)

    xp2 = x_prompt.reshape(n_prompt * prompt_len, D_MODEL)
    xs2 = x_sample.reshape(n_sample * sample_len, D_MODEL)
    cos2, sin2 = _rope_tables(sample_len)
    sp = _ret_state_call(xs2, mod, state_ret, (cos2, sin2), row(norm_mix[0]), ev_w_in, dec,
                         n_seq=n_sample, seq_len=sample_len, layer=0, even_index=0)
    rope = (np.concatenate([np.ones((UNIT, HEAD_DIM), np.float32), cos2]),
            np.concatenate([np.zeros((UNIT, HEAD_DIM), np.float32), sin2]))
    x_all, new_state = _even_call(xp2, xs2, mod, sp, rope, even_params, even_weights, units,
                                  layer=0, even_index=0)
    y_prompt, y_sample = _odd_call(x_all, mod, odd_params, odd_weights, units, unit=UNIT,
                                   short_len=prompt_len, long_len=sample_len, layer=1, odd_index=0)

    return (y_prompt.reshape(n_prompt, prompt_len, D_MODEL),
            y_sample.reshape(n_sample, sample_len, D_MODEL),
            new_state)
```

```python
import functools

import jax
import jax.numpy as jnp
import numpy as np
from jax import lax
from jax.experimental import pallas as pl
from jax.experimental.pallas import tpu as pltpu

D_MODEL = 1024
DEPTH = 2
GRID_W = 64
A_WIDTH = D_MODEL // 2
A_GROUPS = 4
A_GROUP_DIM = A_WIDTH // A_GROUPS
CHUNK = 128
HEADS = 4
HEAD_DIM = (D_MODEL // 2) // HEADS
ROPE_BASE = 10000.0
ROPE_PAIRS_AXIS = HEAD_DIM // 4
POOL_WINDOWS = (2, 4, 8, 16)
POOL_DIM = D_MODEL // len(POOL_WINDOWS)
POOL_HALO = max(POOL_WINDOWS) // 2
D_FF = 4 * D_MODEL
EPS = 1e-6
QK_W = HEADS * HEAD_DIM
V_W = HEADS * HEAD_DIM
IN_WIDTH = 2 * A_WIDTH + 2 * QK_W + 3 * V_W
OUT_WIDTH = A_WIDTH + V_W
N_MOD = 6 * D_MODEL
K_COL = 2 * A_WIDTH + QK_W
V_COL = K_COL + QK_W

UNIT = 2 * CHUNK
UNIT_CHUNKS = UNIT // CHUNK
ODD_UNIT = 2 * UNIT
FF_CHUNK = 1024
ADA_BLOCK_N = 3072
COND_ROWS = 8
STAGE_ROWS = 512
STAGE_COLS = 1024
STAGE_SLOTS = 3
V7X_VMEM_LIMIT_BYTES = 60 * 1024 * 1024

F32 = jnp.float32
BF16 = jnp.bfloat16


def _dot(a, b):
    return jnp.dot(a, b, preferred_element_type=F32)


def _dot_nt(a, b):
    return lax.dot_general(a, b, (((1,), (1,)), ((), ())), preferred_element_type=F32)


def _silu(x):
    return x * jax.nn.sigmoid(x)


def _gelu_tanh(x):
    return 0.5 * x * (1.0 + jnp.tanh(0.7978845608028654 * (x + 0.044715 * (x * x * x))))


def _rms(x):
    return x * lax.rsqrt(jnp.mean(x * x, axis=-1, keepdims=True) + EPS)


def _rms_mod(x, gain, shift, scale):
    return (_rms(x) * gain) * (1.0 + scale) + shift


def _layer_norm(x):
    mu = jnp.mean(x, axis=-1, keepdims=True)
    d = x - mu
    return d * lax.rsqrt(jnp.mean(d * d, axis=-1, keepdims=True) + EPS)


def _split_mod(mod):
    return [mod[:, i * D_MODEL:(i + 1) * D_MODEL] for i in range(6)]


def _rope(x, cos2, sin2):
    return x * cos2 + pltpu.roll(x, HEAD_DIM // 2, axis=1) * sin2


def _pick_block(take_first, a_ref, b_ref):
    rows = lax.broadcasted_iota(jnp.int32, (a_ref.shape[0], 1), 0)
    return jnp.where(rows < jnp.where(take_first, a_ref.shape[0], 0), a_ref[...], b_ref[...])


def _run(gen, n=None):
    done = 0
    while n is None or done < n:
        try:
            next(gen)
        except StopIteration:
            break
        done += 1
    return done


def _weave(heavy, lights, n_light):
    heavy = list(heavy)
    pending = _roundrobin(lights)
    ran = 0
    for k, piece in enumerate(heavy):
        piece()
        want = ((k + 1) * n_light + len(heavy) - 1) // len(heavy)
        ran += _run(pending, want - ran)
    _run(pending)


def _pieces(gen, n):
    return [functools.partial(next, gen, None) for _ in range(n)]


def _roundrobin(gens):
    live = list(gens)
    while live:
        for g in list(live):
            try:
                next(g)
            except StopIteration:
                live.remove(g)
                continue
            yield


MLP_DOT_PIECES = 2 * (D_FF // FF_CHUNK)


def _mlp_pieces(hb, w1_ref, w2_ref):
    n_ff = D_FF // FF_CHUNK
    up = lambda j: _dot(hb, w1_ref[:, j * FF_CHUNK:(j + 1) * FF_CHUNK])
    a = up(0)
    yield
    acc = None
    for j in range(n_ff):
        a_next = None
        if j + 1 < n_ff:
            a_next = up(j + 1)
            yield
        p = _dot(jnp.square(jnp.maximum(a, 0.0)).astype(BF16), w2_ref[j * FF_CHUNK:(j + 1) * FF_CHUNK, :])
        acc = p if acc is None else acc + p
        a = a_next
        yield
    return acc


def _resident(shape):
    return pl.BlockSpec(shape, lambda i: (0,) * len(shape), pipeline_mode=pl.Buffered(1))


def _weight_chunks(w_hbm, w_vmem):
    rows, cols = w_hbm.shape
    jobs = []
    for r0 in range(0, rows, STAGE_ROWS):
        for c0 in range(0, cols, STAGE_COLS):
            window = (pl.ds(r0, min(STAGE_ROWS, rows - r0)), pl.ds(c0, min(STAGE_COLS, cols - c0)))
            jobs.append((w_hbm.at[window], w_vmem.at[window]))
    return jobs


def _load_weights_bf16(jobs, stage, sem):
    n_slots = stage.shape[0]

    def copy(k):
        r, c = jobs[k][0].shape
        slot = k % n_slots
        return pltpu.make_async_copy(jobs[k][0], stage.at[slot, pl.ds(0, r), pl.ds(0, c)], sem.at[slot])

    for k in range(min(n_slots - 1, len(jobs))):
        copy(k).start()
    for k, (src, dst) in enumerate(jobs):
        if k + n_slots - 1 < len(jobs):
            copy(k + n_slots - 1).start()
        copy(k).wait()
        r, c = src.shape
        dst[...] = stage[k % n_slots, 0:r, 0:c].astype(BF16)


class _Units:
    def __init__(self, n_short, n_long, units_per_long):
        self.n_short, self.n_long, self.units_per_long = n_short, n_long * units_per_long, units_per_long
        self.total = self.n_short + self.n_long

    def clamp(self, unit):
        return jnp.clip(unit, 0, self.total - 1)

    def short_block(self, unit):
        return jnp.minimum(unit, self.n_short - 1)

    def long_block(self, unit):
        return jnp.clip(unit - self.n_short, 0, self.n_long - 1)

    def mod_row(self, unit):
        return jnp.where(unit < self.n_short, 0, 1 + (unit - self.n_short) // self.units_per_long)

    def rope_block(self, unit):
        return jnp.where(unit < self.n_short, 0, 1 + (unit - self.n_short) % self.units_per_long)

    def state_block(self, unit):
        return jnp.where(unit < self.n_short, 0, self.units_per_long + unit - self.n_short)


def _ada_kernel(shared_ref, each_ref, w_ref, b_ref, out_ref):
    row_id = lax.broadcasted_iota(jnp.int32, (COND_ROWS, 1), 0)
    cond = jnp.where(row_id == 0, shared_ref[...], 0.0)
    for r in range(each_ref.shape[0]):
        cond = jnp.where(row_id == 1 + r, each_ref[r:r + 1, :], cond)
    s = _silu(cond).astype(BF16)
    rows = _dot(s, w_ref[0].astype(BF16)) + b_ref[pl.ds(pl.program_id(0), 1), :]
    for r in range(COND_ROWS):
        out_ref[r] = rows[r:r + 1, :]


def _ada_call(cond_shared, cond_each, ada_w, ada_b):
    return pl.pallas_call(
        _ada_kernel,
        grid=(DEPTH, N_MOD // ADA_BLOCK_N),
        in_specs=[
            pl.BlockSpec(cond_shared.shape, lambda i, j: (0, 0)),
            pl.BlockSpec(cond_each.shape, lambda i, j: (0, 0)),
            pl.BlockSpec((1, D_MODEL, ADA_BLOCK_N), lambda i, j: (i, 0, j)),
            pl.BlockSpec((DEPTH, ADA_BLOCK_N), lambda i, j: (0, j)),
        ],
        out_specs=pl.BlockSpec((COND_ROWS, 1, ADA_BLOCK_N), lambda i, j: (i, 0, j)),
        out_shape=jax.ShapeDtypeStruct((DEPTH * COND_ROWS, 1, N_MOD), F32),
        compiler_params=pltpu.CompilerParams(
            dimension_semantics=("arbitrary", "arbitrary"),
            vmem_limit_bytes=V7X_VMEM_LIMIT_BYTES),
        name="ada_mod",
    )(cond_shared, cond_each, ada_w, ada_b)


def _decay_tables(dec_ref, hd):
    row = lax.broadcasted_iota(jnp.int32, (CHUNK, CHUNK), 0).astype(F32)
    col = lax.broadcasted_iota(jnp.int32, (CHUNK, CHUNK), 1).astype(F32)
    lg_f = jnp.broadcast_to(-jnp.exp(dec_ref[hd:hd + 1, :]), (CHUNK, CHUNK))
    lg_b = jnp.broadcast_to(-jnp.exp(dec_ref[HEADS + hd:HEADS + hd + 1, :]), (CHUNK, CHUNK))
    d_f = row - col
    d_b = col - row
    return dict(
        decay_f=jnp.where(d_f >= 0, jnp.exp(lg_f * jnp.maximum(d_f, 0.0)), 0.0),
        decay_b=jnp.where(d_b >= 0, jnp.exp(lg_b * jnp.maximum(d_b, 0.0)), 0.0),
        qd_f=jnp.exp(lg_f * (row + 1.0)),
        qd_b=jnp.exp(lg_b * (CHUNK - row)),
        kd_f=jnp.exp(lg_f * (CHUNK - 1.0 - row)),
        kd_b=jnp.exp(lg_b * row),
        cd_f=jnp.exp(lg_f * CHUNK),
        cd_b=jnp.exp(lg_b * CHUNK),
    )


TABLE_NAMES = ("decay_f", "decay_b", "qd_f", "qd_b", "kd_f", "kd_b", "cd_f", "cd_b")


def _ret_state_kernel(x_ref, mod_ref, s0_ref, cos_ref, sin_ref, nmix_ref, wk_ref, wv_ref, dec_ref, sp_ref,
                      *, n_chunk):
    step = pl.program_id(0)

    @pl.when(step == 0)
    def _():
        sp_ref[...] = jnp.zeros_like(sp_ref)

    @pl.when(step > 0)
    def _():
        sh1, sc1 = _split_mod(mod_ref[0])[:2]
        hb = _rms_mod(x_ref[...], nmix_ref[...], sh1, sc1).astype(BF16)
        zk = _dot(hb, wk_ref[...].astype(BF16)) * (HEAD_DIM ** -0.5)
        zv = _dot(hb, wv_ref[...].astype(BF16)).astype(BF16)
        cos2, sin2 = cos_ref[...], sin_ref[...]
        for hd in range(HEADS):
            lo = hd * HEAD_DIM
            tb = _decay_tables(dec_ref, hd)
            kh = _rope(zk[:, lo:lo + HEAD_DIM], cos2, sin2)
            kv_f, kv_b = [], []
            for c in range(n_chunk):
                kc = kh[c * CHUNK:(c + 1) * CHUNK]
                vc = zv[c * CHUNK:(c + 1) * CHUNK, lo:lo + HEAD_DIM]
                kv_f.append(_dot((kc * tb["kd_f"]).T.astype(BF16), vc))
                kv_b.append(_dot((kc * tb["kd_b"]).T.astype(BF16), vc))
            st = s0_ref[0, 0, 0, hd]
            for c in range(n_chunk):
                if c % UNIT_CHUNKS == 0:
                    sp_ref[c // UNIT_CHUNKS, 0, hd] = st
                st = tb["cd_f"] * st + kv_f[c]
            st = s0_ref[0, 0, 1, hd]
            for c in reversed(range(n_chunk)):
                if c % UNIT_CHUNKS == UNIT_CHUNKS - 1:
                    sp_ref[c // UNIT_CHUNKS, 1, hd] = st
                st = tb["cd_b"] * st + kv_b[c]


def _ret_state_call(x2, mod, s0, rope, nmix, w_in, dec, *, n_seq, seq_len, layer, even_index):
    n_chunk = seq_len // CHUNK
    units = seq_len // UNIT
    seq = lambda i: jnp.maximum(i - 1, 0)
    mod_row = lambda i: layer * COND_ROWS + 1 + seq(i)
    return pl.pallas_call(
        functools.partial(_ret_state_kernel, n_chunk=n_chunk),
        grid=(n_seq + 1,),
        in_specs=[
            pl.BlockSpec((seq_len, D_MODEL), lambda i: (seq(i), 0)),
            pl.BlockSpec((1, 1, N_MOD), lambda i: (mod_row(i), 0, 0)),
            pl.BlockSpec((1, 1, 2, HEADS, HEAD_DIM, HEAD_DIM), lambda i: (seq(i), 0, 0, 0, 0, 0)),
            _resident((seq_len, HEAD_DIM)), _resident((seq_len, HEAD_DIM)),
            _resident((1, D_MODEL)),
            pl.BlockSpec((None, D_MODEL, QK_W), lambda i: (even_index, 0, K_COL // QK_W),
                         pipeline_mode=pl.Buffered(1)),
            pl.BlockSpec((None, D_MODEL, V_W), lambda i: (even_index, 0, V_COL // V_W),
                         pipeline_mode=pl.Buffered(1)),
            _resident(dec.shape),
        ],
        out_specs=pl.BlockSpec((units, 2, HEADS, HEAD_DIM, HEAD_DIM), lambda i: (i, 0, 0, 0, 0)),
        out_shape=jax.ShapeDtypeStruct(((n_seq + 1) * units, 2, HEADS, HEAD_DIM, HEAD_DIM), F32),
        compiler_params=pltpu.CompilerParams(
            dimension_semantics=("arbitrary",),
            vmem_limit_bytes=V7X_VMEM_LIMIT_BYTES),
        name="ret_state",
    )(x2, mod, s0, rope[0], rope[1], nmix, w_in, w_in, dec)


def _even_kernel(xa_short, xa_long, xc_short, xc_long, moda_ref, modc_ref, sp_ref, cos_ref, sin_ref,
                 nmix_ref, nmlp_ref, b_s_ref, vgain_ref, dec_ref, w_in_hbm, w_out_hbm, w1_hbm, w2_hbm, w_s_hbm,
                 y_ref, st_ref, ug_s, vg_s, q_s, k_s, v_s, gf_s, gb_s, cat_s, tab_s,
                 w_in_ref, w_out_ref, w1_ref, w2_ref, w_s_ref, stage, sem, *, units, layer, even_index):
    n_units, n_short = units.total, units.n_short
    step = pl.program_id(0)
    slot_a = lax.rem(step, 2)
    slot_b = 1 - slot_a

    @pl.when(step == 0)
    def _():
        _load_weights_bf16(_weight_chunks(w_in_hbm.at[even_index], w_in_ref)
                           + _weight_chunks(w_out_hbm.at[even_index], w_out_ref)
                           + _weight_chunks(w1_hbm.at[layer], w1_ref)
                           + _weight_chunks(w2_hbm.at[layer], w2_ref)
                           + sum((_weight_chunks(w_s_hbm.at[even_index, g], w_s_ref.at[g])
                                  for g in range(A_GROUPS)), []), stage, sem)
        for hd in range(HEADS):
            tb = _decay_tables(dec_ref, hd)
            for n, name in enumerate(TABLE_NAMES):
                tab_s[hd, n] = tb[name]

    def stage_in():
        sh1, sc1 = _split_mod(moda_ref[0])[:2]
        x = _pick_block(step < n_short, xa_short, xa_long)
        hb = _rms_mod(x, nmix_ref[...], sh1, sc1).astype(BF16)
        z = _dot(hb, w_in_ref[...])
        yield
        ug_s[slot_a] = z[:, 0:A_WIDTH]
        yield
        zv = _gelu_tanh(z[:, A_WIDTH:2 * A_WIDTH])
        vgn = [_layer_norm(zv[:, g * A_GROUP_DIM:(g + 1) * A_GROUP_DIM]) for g in range(A_GROUPS)]
        vg_s[slot_a] = (jnp.concatenate(vgn, axis=-1) * vgain_ref[...]).astype(BF16)
        yield
        cos2, sin2 = cos_ref[...], sin_ref[...]
        heads = lambda zz: jnp.concatenate(
            [_rope(zz[:, h * HEAD_DIM:(h + 1) * HEAD_DIM], cos2, sin2) for h in range(HEADS)], -1)
        q_s[slot_a] = heads(z[:, 2 * A_WIDTH:K_COL]).astype(BF16)
        yield
        k_s[slot_a] = heads(z[:, K_COL:V_COL] * (HEAD_DIM ** -0.5))
        v_s[slot_a] = z[:, V_COL:V_COL + V_W].astype(BF16)
        yield
        gf_s[slot_a] = z[:, V_COL + V_W:V_COL + 2 * V_W]
        yield
        gb_s[slot_a] = z[:, V_COL + 2 * V_W:V_COL + 3 * V_W]

    IN_LIGHT = 6

    def stage_mix(final_states):
        s = slot_b
        rows = [slice(c * CHUNK, (c + 1) * CHUNK) for c in range(UNIT_CHUNKS)]
        head_cols = [slice(hd * HEAD_DIM, (hd + 1) * HEAD_DIM) for hd in range(HEADS)]
        table = lambda hd, name: tab_s[hd, TABLE_NAMES.index(name)]

        sp2 = []
        for g in range(A_GROUPS):
            lo = g * A_GROUP_DIM
            vg2 = jnp.concatenate([vg_s[s, r, lo:lo + A_GROUP_DIM] for r in rows], axis=-1)
            sp2.append(_dot(w_s_ref[g], vg2))
        yield
        scores, kv_f, kv_b = [], [], []
        for hd in range(HEADS):
            kc = [k_s[s, r, head_cols[hd]] for r in rows]
            vc = [v_s[s, r, head_cols[hd]] for r in rows]
            scores.append([_dot_nt(q_s[s, r, head_cols[hd]], kc[c].astype(BF16)) for c, r in enumerate(rows)])
            kv_f.append([_dot((kc[c] * table(hd, "kd_f")).T.astype(BF16), vc[c]) for c in range(UNIT_CHUNKS)])
            kv_b.append([_dot((kc[c] * table(hd, "kd_b")).T.astype(BF16), vc[c]) for c in range(UNIT_CHUNKS)])
            yield
        for g in range(A_GROUPS):
            lo = g * A_GROUP_DIM
            for c, r in enumerate(rows):
                gate = sp2[g][:, c * CHUNK:(c + 1) * CHUNK] + b_s_ref[g]
                cat_s[s, r, lo:lo + A_GROUP_DIM] = (_gelu_tanh(ug_s[s, r, lo:lo + A_GROUP_DIM]) * gate).astype(BF16)
            if g % 2 == 1:
                yield
        y_raw = []
        for hd in range(HEADS):
            enter_f, st = [], sp_ref[0, 0, hd]
            for c in range(UNIT_CHUNKS):
                enter_f.append(st)
                st = table(hd, "cd_f") * st + kv_f[hd][c]
            final_f = st
            enter_b, st = [None] * UNIT_CHUNKS, sp_ref[0, 1, hd]
            for c in reversed(range(UNIT_CHUNKS)):
                enter_b[c] = st
                st = table(hd, "cd_b") * st + kv_b[hd][c]
            final_states.append((final_f, st))
            per_chunk = []
            for c, r in enumerate(rows):
                vc = v_s[s, r, head_cols[hd]]
                qf = q_s[s, r, head_cols[hd]].astype(F32)

                def direction(decay, qd, enter):
                    p = (scores[hd][c] * table(hd, decay)).astype(BF16)
                    lhs = jnp.concatenate([p, (qf * table(hd, qd)).astype(BF16)], axis=-1)
                    return _dot(lhs, jnp.concatenate([vc, enter.astype(BF16)], axis=0))

                per_chunk.append((direction("decay_f", "qd_f", enter_f[c]), direction("decay_b", "qd_b", enter_b[c])))
            y_raw.append(per_chunk)
            yield
        for hd in range(HEADS):
            for c, r in enumerate(rows):
                y_f, y_b = y_raw[hd][c]
                cols = slice(A_WIDTH + hd * HEAD_DIM, A_WIDTH + (hd + 1) * HEAD_DIM)
                cat_s[s, r, cols] = (_silu(gf_s[s, r, head_cols[hd]]) * _layer_norm(y_f)
                                     + _silu(gb_s[s, r, head_cols[hd]]) * _layer_norm(y_b)).astype(BF16)
            yield

    MIX_LIGHT = 1 + HEADS + A_GROUPS // 2 + 2 * HEADS

    def emit_states(final_states):
        @pl.when(step - 1 < n_short)
        def _():
            for hd, (final_f, final_b) in enumerate(final_states):
                st_ref[0, 0, 0, hd] = final_f
                st_ref[0, 0, 1, hd] = final_b

    def stage_mlp():
        _, _, g1, sh2, sc2, g2 = _split_mod(modc_ref[0])
        x = _pick_block(step - 2 < n_short, xc_short, xc_long) + g1 * _dot(cat_s[slot_a], w_out_ref[...])
        hb = _rms_mod(x, nmlp_ref[...], sh2, sc2).astype(BF16)
        yield
        acc = yield from _mlp_pieces(hb, w1_ref, w2_ref)
        y_ref[...] = x + g2 * acc

    @pl.when(step == 0)
    def _():
        _run(stage_in())

    @pl.when(step == 1)
    def _():
        a, finals = stage_in(), []
        _weave(_pieces(a, 1), [a, stage_mix(finals)], IN_LIGHT + MIX_LIGHT)
        emit_states(finals)

    @pl.when((step >= 2) & (step < n_units))
    def _():
        a, c, finals = stage_in(), stage_mlp(), []
        heavy = _pieces(c, 1) + _pieces(a, 1) + _pieces(c, MLP_DOT_PIECES + 1)
        _weave(heavy, [a, stage_mix(finals)], IN_LIGHT + MIX_LIGHT)
        emit_states(finals)

    @pl.when(step == n_units)
    def _():
        finals = []
        _weave(_pieces(stage_mlp(), MLP_DOT_PIECES + 2), [stage_mix(finals)], MIX_LIGHT)
        emit_states(finals)

    @pl.when(step == n_units + 1)
    def _():
        _run(stage_mlp())


_HBM = pl.BlockSpec(memory_space=pl.ANY)
_WEIGHT_STAGING = [pltpu.VMEM((STAGE_SLOTS, STAGE_ROWS, STAGE_COLS), F32), pltpu.SemaphoreType.DMA((STAGE_SLOTS,))]


def _even_call(x_short, x_long, mod, sp, rope, params, weights, units, *, layer, even_index):
    n_units = units.total
    unit_a = lambda i: units.clamp(i)
    unit_b = lambda i: units.clamp(i - 1)
    unit_c = lambda i: units.clamp(i - 2)
    mod_row = lambda unit: layer * COND_ROWS + units.mod_row(unit)
    state_block = (1, 1, 2, HEADS, HEAD_DIM, HEAD_DIM)

    in_specs = [pl.BlockSpec((UNIT, D_MODEL), lambda i: (units.short_block(unit_a(i)), 0)),
                pl.BlockSpec((UNIT, D_MODEL), lambda i: (units.long_block(unit_a(i)), 0)),
                pl.BlockSpec((UNIT, D_MODEL), lambda i: (units.short_block(unit_c(i)), 0)),
                pl.BlockSpec((UNIT, D_MODEL), lambda i: (units.long_block(unit_c(i)), 0)),
                pl.BlockSpec((1, 1, N_MOD), lambda i: (mod_row(unit_a(i)), 0, 0)),
                pl.BlockSpec((1, 1, N_MOD), lambda i: (mod_row(unit_c(i)), 0, 0)),
                pl.BlockSpec((1, 2, HEADS, HEAD_DIM, HEAD_DIM), lambda i: (units.state_block(unit_b(i)), 0, 0, 0, 0)),
                pl.BlockSpec((UNIT, HEAD_DIM), lambda i: (units.rope_block(unit_a(i)), 0)),
                pl.BlockSpec((UNIT, HEAD_DIM), lambda i: (units.rope_block(unit_a(i)), 0))]
    in_specs += [_resident(p.shape) for p in params] + [_HBM] * len(weights)
    scratch = [
        pltpu.VMEM((2, UNIT, A_WIDTH), F32),
        pltpu.VMEM((2, UNIT, A_WIDTH), BF16),
        pltpu.VMEM((2, UNIT, QK_W), BF16),
        pltpu.VMEM((2, UNIT, QK_W), F32),
        pltpu.VMEM((2, UNIT, V_W), BF16),
        pltpu.VMEM((2, UNIT, V_W), F32),
        pltpu.VMEM((2, UNIT, V_W), F32),
        pltpu.VMEM((2, UNIT, OUT_WIDTH), BF16),
        pltpu.VMEM((HEADS, len(TABLE_NAMES), CHUNK, CHUNK), F32),
    ]
    scratch += [pltpu.VMEM(w.shape[1:], BF16) for w in weights] + _WEIGHT_STAGING
    return pl.pallas_call(
        functools.partial(_even_kernel, units=units, layer=layer, even_index=even_index),
        grid=(n_units + 2,),
        in_specs=in_specs,
        out_specs=[pl.BlockSpec((UNIT, D_MODEL), lambda i: (unit_c(i), 0)),
                   pl.BlockSpec(state_block, lambda i: (units.short_block(unit_b(i)), 0, 0, 0, 0, 0))],
        out_shape=[jax.ShapeDtypeStruct((n_units * UNIT, D_MODEL), F32),
                   jax.ShapeDtypeStruct((units.n_short,) + state_block[1:], F32)],
        scratch_shapes=scratch,
        compiler_params=pltpu.CompilerParams(
            dimension_semantics=("arbitrary",),
            vmem_limit_bytes=V7X_VMEM_LIMIT_BYTES),
        name="even_layer",
    )(x_short, x_long, x_short, x_long, mod, mod, sp, rope[0], rope[1], *params, *weights)


def _odd_kernel(xp_ref, x_ref, xn_ref, moda_ref, modc_ref, nmix_ref, nmlp_ref, fin_ref, pscale_ref,
                w_pool_hbm, w1_hbm, w2_hbm, y_short, y_long, x1_s, hb_s, w_pool_ref, w1_ref, w2_ref, stage, sem,
                *, n_units, n_short, short_len, long_len, layer, odd_index):
    step = pl.program_id(0)
    slot_a = lax.rem(step, 2)
    slot_b = 1 - slot_a
    unit = x_ref.shape[0]
    rows_w = unit + 2 * POOL_HALO

    @pl.when(step == 0)
    def _():
        _load_weights_bf16(sum((_weight_chunks(w_pool_hbm.at[odd_index, g], w_pool_ref.at[g])
                                for g in range(len(POOL_WINDOWS))), [])
                           + _weight_chunks(w1_hbm.at[layer], w1_ref)
                           + _weight_chunks(w2_hbm.at[layer], w2_ref), stage, sem)

    def stage_pool():
        sh1, sc1, g1, sh2, sc2, _ = _split_mod(moda_ref[0])
        xw = jnp.concatenate([xp_ref[...], x_ref[...], xn_ref[...]], axis=0)
        h = _rms_mod(xw, nmix_ref[...], sh1, sc1)
        seq_len = jnp.where(step < n_short, short_len, long_len)
        first = jnp.minimum(step, n_units - 1) * unit - POOL_HALO
        t = (lax.broadcasted_iota(jnp.int32, (rows_w, 1), 0) + first) & (seq_len - 1)
        yield

        def shift_down(a, k):
            return jnp.where(t >= k, pltpu.roll(a, k, axis=0), 0.0)

        def shift_up(a, k):
            return jnp.where(t < seq_len - k, pltpu.roll(a, rows_w - k, axis=0), 0.0)

        outs = []
        for g, w in enumerate(POOL_WINDOWS):
            hg = h[:, g * POOL_DIM:(g + 1) * POOL_DIM]
            half = w // 2
            left = shift_down(hg, 1)
            right = hg
            span = 1
            while span < half:
                left = left + shift_down(left, span)
                right = right + shift_up(right, span)
                span *= 2
            cnt = (jnp.minimum(t + half, seq_len) - jnp.maximum(t - half, 0)).astype(F32)
            pooled = ((left + right) / cnt - hg)[POOL_HALO:POOL_HALO + unit].astype(BF16)
            outs.append(_dot(pooled, w_pool_ref[g]))
            yield
        x1 = x_ref[...] + g1 * (jnp.concatenate(outs, axis=-1) * pscale_ref[...])
        x1_s[slot_a] = x1
        hb_s[slot_a] = _rms_mod(x1, nmlp_ref[...], sh2, sc2).astype(BF16)

    def stage_mlp(result):
        g2 = _split_mod(modc_ref[0])[5]
        acc = yield from _mlp_pieces(hb_s[slot_b], w1_ref, w2_ref)
        result.append(_rms(x1_s[slot_b] + g2 * acc) * fin_ref[...])

    def store(result):
        @pl.when(step - 1 < n_short)
        def _():
            y_short[...] = result[0]

        @pl.when(step - 1 >= n_short)
        def _():
            y_long[...] = result[0]

    pool_light = 2 + len(POOL_WINDOWS)

    @pl.when(step == 0)
    def _():
        _run(stage_pool())

    @pl.when((step >= 1) & (step < n_units))
    def _():
        result = []
        _weave(_pieces(stage_mlp(result), MLP_DOT_PIECES + 1), [stage_pool()], pool_light)
        store(result)

    @pl.when(step == n_units)
    def _():
        result = []
        _run(stage_mlp(result))
        store(result)


def _odd_call(x2, mod, params, weights, units, *, unit, short_len, long_len, layer, odd_index):
    n_units = units.total
    halo_per_unit = unit // POOL_HALO
    last_halo = x2.shape[0] // POOL_HALO - 1
    unit_a = lambda i: units.clamp(i)
    unit_c = lambda i: units.clamp(i - 1)
    mod_row = lambda u: layer * COND_ROWS + units.mod_row(u)
    assert (units.n_short * unit) % long_len == 0
    in_specs = [pl.BlockSpec((POOL_HALO, D_MODEL), lambda i: (jnp.maximum(unit_a(i) * halo_per_unit - 1, 0), 0)),
                pl.BlockSpec((unit, D_MODEL), lambda i: (unit_a(i), 0)),
                pl.BlockSpec((POOL_HALO, D_MODEL),
                             lambda i: (jnp.minimum((unit_a(i) + 1) * halo_per_unit, last_halo), 0)),
                pl.BlockSpec((1, 1, N_MOD), lambda i: (mod_row(unit_a(i)), 0, 0)),
                pl.BlockSpec((1, 1, N_MOD), lambda i: (mod_row(unit_c(i)), 0, 0))]
    in_specs += [_resident(p.shape) for p in params] + [_HBM] * len(weights)
    scratch = [pltpu.VMEM((2, unit, D_MODEL), F32),
               pltpu.VMEM((2, unit, D_MODEL), BF16)]
    scratch += [pltpu.VMEM(w.shape[1:], BF16) for w in weights] + _WEIGHT_STAGING
    return pl.pallas_call(
        functools.partial(_odd_kernel, n_units=n_units, n_short=units.n_short,
                          short_len=short_len, long_len=long_len, layer=layer, odd_index=odd_index),
        grid=(n_units + 1,),
        in_specs=in_specs,
        out_specs=[pl.BlockSpec((unit, D_MODEL), lambda i: (units.short_block(unit_c(i)), 0)),
                   pl.BlockSpec((unit, D_MODEL), lambda i: (units.long_block(unit_c(i)), 0))],
        out_shape=[jax.ShapeDtypeStruct((units.n_short * unit, D_MODEL), F32),
                   jax.ShapeDtypeStruct((units.n_long * unit, D_MODEL), F32)],
        scratch_shapes=scratch,
        compiler_params=pltpu.CompilerParams(
            dimension_semantics=("arbitrary",),
            vmem_limit_bytes=V7X_VMEM_LIMIT_BYTES),
        name="odd_layer",
    )(x2, x2, x2, mod, mod, *params, *weights)


def _rope_tables(n_tokens):
    f32 = np.float32
    rows = n_tokens // GRID_W
    pos_r = np.repeat(np.arange(rows, dtype=f32), GRID_W)
    pos_c = np.tile(np.arange(GRID_W, dtype=f32), rows)
    inv = (f32(ROPE_BASE) ** (-np.arange(ROPE_PAIRS_AXIS, dtype=f32) / f32(ROPE_PAIRS_AXIS))).astype(f32)
    ang = np.concatenate([pos_r[:, None] * inv, pos_c[:, None] * inv], axis=-1).astype(f32)
    cos, sin = np.cos(ang).astype(f32), np.sin(ang).astype(f32)
    return np.concatenate([cos, cos], axis=-1), np.concatenate([-sin, sin], axis=-1)


def kernel(x_prompt, x_sample, state_ret, c, c_ctx, ada_w, ada_b, norm_mix, norm_mlp, mlp_w1, mlp_w2,
           ev_w_in, ev_w_s, ev_b_s, ev_vnorm, ev_decay, ev_w_out, od_w_pool, od_pool_scale, final_norm):
    n_prompt, prompt_len, _ = x_prompt.shape
    n_sample, sample_len, _ = x_sample.shape
    assert DEPTH == 2 and n_sample + 1 <= COND_ROWS
    assert prompt_len == UNIT and sample_len % UNIT == 0
    units = _Units(n_prompt, n_sample, sample_len // UNIT)

    mod = _ada_call(c_ctx.reshape(1, D_MODEL), c, ada_w, ada_b)

    row = lambda v: v.reshape(1, -1)
    dec = jnp.broadcast_to(ev_decay[0].reshape(2 * HEADS, 1), (2 * HEADS, CHUNK))
    even_params = (
        row(norm_mix[0]), row(norm_mlp[0]),
        jnp.broadcast_to(ev_b_s[0][:, :, None], (A_GROUPS, CHUNK, A_GROUP_DIM)),
        row(ev_vnorm[0]), dec,
    )
    odd_params = (row(norm_mix[1]), row(norm_mlp[1]), row(final_norm), row(od_pool_scale[0]))
    even_weights = (ev_w_in, ev_w_out, mlp_w1, mlp_w2, ev_w_s)
    odd_weights = (od_w_pool, mlp_w1, mlp_w2)

    xp2 = x_prompt.reshape(n_prompt * prompt_len, D_MODEL)
    xs2 = x_sample.reshape(n_sample * sample_len, D_MODEL)
    cos2, sin2 = _rope_tables(sample_len)
    sp = _ret_state_call(xs2, mod, state_ret, (cos2, sin2), row(norm_mix[0]), ev_w_in, dec,
                         n_seq=n_sample, seq_len=sample_len, layer=0, even_index=0)
    rope = (np.concatenate([np.ones((UNIT, HEAD_DIM), np.float32), cos2]),
            np.concatenate([np.zeros((UNIT, HEAD_DIM), np.float32), sin2]))
    x_all, new_state = _even_call(xp2, xs2, mod, sp, rope, even_params, even_weights, units,
                                  layer=0, even_index=0)
    odd_units = _Units(n_prompt * prompt_len // ODD_UNIT, n_sample, sample_len // ODD_UNIT)
    y_prompt, y_sample = _odd_call(x_all, mod, odd_params, odd_weights, odd_units, unit=ODD_UNIT,
                                   short_len=prompt_len, long_len=sample_len, layer=1, odd_index=0)

    return (y_prompt.reshape(n_prompt, prompt_len, D_MODEL),
            y_sample.reshape(n_sample, sample_len, D_MODEL),
            new_state)
```

```python
import functools

import jax
import jax.numpy as jnp
import numpy as np
from jax import lax
from jax.experimental import pallas as pl
from jax.experimental.pallas import tpu as pltpu

D_MODEL = 1024
DEPTH = 2
GRID_W = 64
A_WIDTH = D_MODEL // 2
A_GROUPS = 4
A_GROUP_DIM = A_WIDTH // A_GROUPS
CHUNK = 128
HEADS = 4
HEAD_DIM = (D_MODEL // 2) // HEADS
ROPE_BASE = 10000.0
ROPE_PAIRS_AXIS = HEAD_DIM // 4
POOL_WINDOWS = (2, 4, 8, 16)
POOL_DIM = D_MODEL // len(POOL_WINDOWS)
POOL_HALO = max(POOL_WINDOWS) // 2
D_FF = 4 * D_MODEL
EPS = 1e-6
QK_W = HEADS * HEAD_DIM
V_W = HEADS * HEAD_DIM
IN_WIDTH = 2 * A_WIDTH + 2 * QK_W + 3 * V_W
OUT_WIDTH = A_WIDTH + V_W
N_MOD = 6 * D_MODEL
K_COL = 2 * A_WIDTH + QK_W
V_COL = K_COL + QK_W

UNIT = 2 * CHUNK
UNIT_CHUNKS = UNIT // CHUNK
FF_CHUNK = 1024
ADA_BLOCK_N = 3072
COND_ROWS = 8
STAGE_ROWS = 512
STAGE_COLS = 1024
STAGE_SLOTS = 3
V7X_VMEM_LIMIT_BYTES = 60 * 1024 * 1024

F32 = jnp.float32
BF16 = jnp.bfloat16


def _dot(a, b):
    return jnp.dot(a, b, preferred_element_type=F32)


def _dot_nt(a, b):
    return lax.dot_general(a, b, (((1,), (1,)), ((), ())), preferred_element_type=F32)


def _silu(x):
    return x * jax.nn.sigmoid(x)


def _gelu_tanh(x):
    return 0.5 * x * (1.0 + jnp.tanh(0.7978845608028654 * (x + 0.044715 * (x * x * x))))


def _rms(x):
    return x * lax.rsqrt(jnp.mean(x * x, axis=-1, keepdims=True) + EPS)


def _rms_mod(x, gain, shift, scale):
    return (_rms(x) * gain) * (1.0 + scale) + shift


def _layer_norm(x):
    mu = jnp.mean(x, axis=-1, keepdims=True)
    d = x - mu
    return d * lax.rsqrt(jnp.mean(d * d, axis=-1, keepdims=True) + EPS)


def _split_mod(mod):
    return [mod[:, i * D_MODEL:(i + 1) * D_MODEL] for i in range(6)]


def _rope(x, cos2, sin2):
    return x * cos2 + pltpu.roll(x, HEAD_DIM // 2, axis=1) * sin2


def _pick_block(take_first, a_ref, b_ref):
    rows = lax.broadcasted_iota(jnp.int32, (a_ref.shape[0], 1), 0)
    return jnp.where(rows < jnp.where(take_first, a_ref.shape[0], 0), a_ref[...], b_ref[...])


def _run(gen, n=None):
    done = 0
    while n is None or done < n:
        try:
            next(gen)
        except StopIteration:
            break
        done += 1
    return done


def _weave(heavy, lights, n_light):
    heavy = list(heavy)
    pending = _roundrobin(lights)
    ran = 0
    for k, piece in enumerate(heavy):
        piece()
        want = ((k + 1) * n_light + len(heavy) - 1) // len(heavy)
        ran += _run(pending, want - ran)
    _run(pending)


def _pieces(gen, n):
    return [functools.partial(next, gen, None) for _ in range(n)]


def _roundrobin(gens):
    live = list(gens)
    while live:
        for g in list(live):
            try:
                next(g)
            except StopIteration:
                live.remove(g)
                continue
            yield


MLP_DOT_PIECES = 2 * (D_FF // FF_CHUNK)


def _mlp_pieces(hb, w1_ref, w2_ref):
    n_ff = D_FF // FF_CHUNK
    up = lambda j: _dot(hb, w1_ref[:, j * FF_CHUNK:(j + 1) * FF_CHUNK])
    a = up(0)
    yield
    acc = None
    for j in range(n_ff):
        a_next = None
        if j + 1 < n_ff:
            a_next = up(j + 1)
            yield
        p = _dot(jnp.square(jnp.maximum(a, 0.0)).astype(BF16), w2_ref[j * FF_CHUNK:(j + 1) * FF_CHUNK, :])
        acc = p if acc is None else acc + p
        a = a_next
        yield
    return acc


def _resident(shape):
    return pl.BlockSpec(shape, lambda i: (0,) * len(shape), pipeline_mode=pl.Buffered(1))


def _weight_chunks(w_hbm, w_vmem):
    rows, cols = w_hbm.shape
    jobs = []
    for r0 in range(0, rows, STAGE_ROWS):
        for c0 in range(0, cols, STAGE_COLS):
            window = (pl.ds(r0, min(STAGE_ROWS, rows - r0)), pl.ds(c0, min(STAGE_COLS, cols - c0)))
            jobs.append((w_hbm.at[window], w_vmem.at[window]))
    return jobs


class _WeightRing:
    def __init__(self, jobs, stage, sem):
        self.jobs, self.stage, self.sem, self.n_slots = jobs, stage, sem, stage.shape[0]

    def _copy(self, k):
        r, c = self.jobs[k][0].shape
        slot = k % self.n_slots
        return pltpu.make_async_copy(self.jobs[k][0], self.stage.at[slot, pl.ds(0, r), pl.ds(0, c)],
                                     self.sem.at[slot])

    def prefill(self):
        for k in range(min(self.n_slots - 1, len(self.jobs))):
            self._copy(k).start()

    def finish(self, first, last):
        for k in range(first, last):
            if k + self.n_slots - 1 < len(self.jobs):
                self._copy(k + self.n_slots - 1).start()
            self._copy(k).wait()
            src, dst = self.jobs[k]
            r, c = src.shape
            dst[...] = self.stage[k % self.n_slots, 0:r, 0:c].astype(BF16)
            yield


def _load_weights_bf16(jobs, stage, sem):
    ring = _WeightRing(jobs, stage, sem)
    ring.prefill()
    _run(ring.finish(0, len(jobs)))


class _Units:
    def __init__(self, n_short, n_long, units_per_long):
        self.n_short, self.n_long, self.units_per_long = n_short, n_long * units_per_long, units_per_long
        self.total = self.n_short + self.n_long

    def clamp(self, unit):
        return jnp.clip(unit, 0, self.total - 1)

    def short_block(self, unit):
        return jnp.minimum(unit, self.n_short - 1)

    def long_block(self, unit):
        return jnp.clip(unit - self.n_short, 0, self.n_long - 1)

    def mod_row(self, unit):
        return jnp.where(unit < self.n_short, 0, 1 + (unit - self.n_short) // self.units_per_long)

    def rope_block(self, unit):
        return jnp.where(unit < self.n_short, 0, 1 + (unit - self.n_short) % self.units_per_long)

    def state_block(self, unit):
        return jnp.where(unit < self.n_short, 0, self.units_per_long + unit - self.n_short)


def _ada_kernel(shared_ref, each_ref, w_ref, b_ref, out_ref):
    row_id = lax.broadcasted_iota(jnp.int32, (COND_ROWS, 1), 0)
    cond = jnp.where(row_id == 0, shared_ref[...], 0.0)
    for r in range(each_ref.shape[0]):
        cond = jnp.where(row_id == 1 + r, each_ref[r:r + 1, :], cond)
    s = _silu(cond).astype(BF16)
    rows = _dot(s, w_ref[0].astype(BF16)) + b_ref[pl.ds(pl.program_id(0), 1), :]
    for r in range(COND_ROWS):
        out_ref[r] = rows[r:r + 1, :]


def _ada_call(cond_shared, cond_each, ada_w, ada_b):
    return pl.pallas_call(
        _ada_kernel,
        grid=(DEPTH, N_MOD // ADA_BLOCK_N),
        in_specs=[
            pl.BlockSpec(cond_shared.shape, lambda i, j: (0, 0)),
            pl.BlockSpec(cond_each.shape, lambda i, j: (0, 0)),
            pl.BlockSpec((1, D_MODEL, ADA_BLOCK_N), lambda i, j: (i, 0, j)),
            pl.BlockSpec((DEPTH, ADA_BLOCK_N), lambda i, j: (0, j)),
        ],
        out_specs=pl.BlockSpec((COND_ROWS, 1, ADA_BLOCK_N), lambda i, j: (i, 0, j)),
        out_shape=jax.ShapeDtypeStruct((DEPTH * COND_ROWS, 1, N_MOD), F32),
        compiler_params=pltpu.CompilerParams(
            dimension_semantics=("arbitrary", "arbitrary"),
            vmem_limit_bytes=V7X_VMEM_LIMIT_BYTES),
        name="ada_mod",
    )(cond_shared, cond_each, ada_w, ada_b)


def _decay_tables(dec_ref, hd):
    row = lax.broadcasted_iota(jnp.int32, (CHUNK, CHUNK), 0).astype(F32)
    col = lax.broadcasted_iota(jnp.int32, (CHUNK, CHUNK), 1).astype(F32)
    lg_f = jnp.broadcast_to(-jnp.exp(dec_ref[hd:hd + 1, :]), (CHUNK, CHUNK))
    lg_b = jnp.broadcast_to(-jnp.exp(dec_ref[HEADS + hd:HEADS + hd + 1, :]), (CHUNK, CHUNK))
    d_f = row - col
    d_b = col - row
    return dict(
        decay_f=jnp.where(d_f >= 0, jnp.exp(lg_f * jnp.maximum(d_f, 0.0)), 0.0),
        decay_b=jnp.where(d_b >= 0, jnp.exp(lg_b * jnp.maximum(d_b, 0.0)), 0.0),
        qd_f=jnp.exp(lg_f * (row + 1.0)),
        qd_b=jnp.exp(lg_b * (CHUNK - row)),
        kd_f=jnp.exp(lg_f * (CHUNK - 1.0 - row)),
        kd_b=jnp.exp(lg_b * row),
        cd_f=jnp.exp(lg_f * CHUNK),
        cd_b=jnp.exp(lg_b * CHUNK),
    )


TABLE_NAMES = ("decay_f", "decay_b", "qd_f", "qd_b", "kd_f", "kd_b", "cd_f", "cd_b")


def _ret_state_kernel(x_ref, mod_ref, s0_ref, cos_ref, sin_ref, nmix_ref, wk_ref, wv_ref, dec_ref, sp_ref,
                      *, n_chunk):
    step = pl.program_id(0)

    @pl.when(step == 0)
    def _():
        sp_ref[...] = jnp.zeros_like(sp_ref)

    @pl.when(step > 0)
    def _():
        sh1, sc1 = _split_mod(mod_ref[0])[:2]
        hb = _rms_mod(x_ref[...], nmix_ref[...], sh1, sc1).astype(BF16)
        zk = _dot(hb, wk_ref[...].astype(BF16)) * (HEAD_DIM ** -0.5)
        zv = _dot(hb, wv_ref[...].astype(BF16)).astype(BF16)
        cos2, sin2 = cos_ref[...], sin_ref[...]
        for hd in range(HEADS):
            lo = hd * HEAD_DIM
            tb = _decay_tables(dec_ref, hd)
            kh = _rope(zk[:, lo:lo + HEAD_DIM], cos2, sin2)
            kv_f, kv_b = [], []
            for c in range(n_chunk):
                kc = kh[c * CHUNK:(c + 1) * CHUNK]
                vc = zv[c * CHUNK:(c + 1) * CHUNK, lo:lo + HEAD_DIM]
                kv_f.append(_dot((kc * tb["kd_f"]).T.astype(BF16), vc))
                kv_b.append(_dot((kc * tb["kd_b"]).T.astype(BF16), vc))
            st = s0_ref[0, 0, 0, hd]
            for c in range(n_chunk):
                if c % UNIT_CHUNKS == 0:
                    sp_ref[c // UNIT_CHUNKS, 0, hd] = st
                st = tb["cd_f"] * st + kv_f[c]
            st = s0_ref[0, 0, 1, hd]
            for c in reversed(range(n_chunk)):
                if c % UNIT_CHUNKS == UNIT_CHUNKS - 1:
                    sp_ref[c // UNIT_CHUNKS, 1, hd] = st
                st = tb["cd_b"] * st + kv_b[c]


def _ret_state_call(x2, mod, s0, rope, nmix, w_in, dec, *, n_seq, seq_len, layer, even_index):
    n_chunk = seq_len // CHUNK
    units = seq_len // UNIT
    seq = lambda i: jnp.maximum(i - 1, 0)
    mod_row = lambda i: layer * COND_ROWS + 1 + seq(i)
    return pl.pallas_call(
        functools.partial(_ret_state_kernel, n_chunk=n_chunk),
        grid=(n_seq + 1,),
        in_specs=[
            pl.BlockSpec((seq_len, D_MODEL), lambda i: (seq(i), 0)),
            pl.BlockSpec((1, 1, N_MOD), lambda i: (mod_row(i), 0, 0)),
            pl.BlockSpec((1, 1, 2, HEADS, HEAD_DIM, HEAD_DIM), lambda i: (seq(i), 0, 0, 0, 0, 0)),
            _resident((seq_len, HEAD_DIM)), _resident((seq_len, HEAD_DIM)),
            _resident((1, D_MODEL)),
            pl.BlockSpec((None, D_MODEL, QK_W), lambda i: (even_index, 0, K_COL // QK_W),
                         pipeline_mode=pl.Buffered(1)),
            pl.BlockSpec((None, D_MODEL, V_W), lambda i: (even_index, 0, V_COL // V_W),
                         pipeline_mode=pl.Buffered(1)),
            _resident(dec.shape),
        ],
        out_specs=pl.BlockSpec((units, 2, HEADS, HEAD_DIM, HEAD_DIM), lambda i: (i, 0, 0, 0, 0)),
        out_shape=jax.ShapeDtypeStruct(((n_seq + 1) * units, 2, HEADS, HEAD_DIM, HEAD_DIM), F32),
        compiler_params=pltpu.CompilerParams(
            dimension_semantics=("arbitrary",),
            vmem_limit_bytes=V7X_VMEM_LIMIT_BYTES),
        name="ret_state",
    )(x2, mod, s0, rope[0], rope[1], nmix, w_in, w_in, dec)


def _even_kernel(xa_short, xa_long, xc_short, xc_long, moda_ref, modc_ref, sp_ref, cos_ref, sin_ref,
                 nmix_ref, nmlp_ref, b_s_ref, vgain_ref, dec_ref, w_in_hbm, w_out_hbm, w1_hbm, w2_hbm, w_s_hbm,
                 y_ref, st_ref, ug_s, vg_s, q_s, k_s, v_s, gf_s, gb_s, cat_s, tab_s,
                 w_in_ref, w_out_ref, w1_ref, w2_ref, w_s_ref, stage, sem, *, units, layer, even_index):
    n_units, n_short = units.total, units.n_short
    step = pl.program_id(0)
    slot_a = lax.rem(step, 2)
    slot_b = 1 - slot_a

    early_weights = (_weight_chunks(w_in_hbm.at[even_index], w_in_ref)
                     + sum((_weight_chunks(w_s_hbm.at[even_index, g], w_s_ref.at[g]) for g in range(A_GROUPS)), []))
    late_weights = _WeightRing(_weight_chunks(w_out_hbm.at[even_index], w_out_ref)
                               + _weight_chunks(w1_hbm.at[layer], w1_ref)
                               + _weight_chunks(w2_hbm.at[layer], w2_ref), stage, sem)
    n_late = len(late_weights.jobs)
    n_late_step0 = min(n_late, 7)

    @pl.when(step == 0)
    def _():
        _load_weights_bf16(early_weights, stage, sem)
        late_weights.prefill()
        for hd in range(HEADS):
            tb = _decay_tables(dec_ref, hd)
            for n, name in enumerate(TABLE_NAMES):
                tab_s[hd, n] = tb[name]

    def stage_in():
        sh1, sc1 = _split_mod(moda_ref[0])[:2]
        x = _pick_block(step < n_short, xa_short, xa_long)
        hb = _rms_mod(x, nmix_ref[...], sh1, sc1).astype(BF16)
        z = _dot(hb, w_in_ref[...])
        yield
        ug_s[slot_a] = z[:, 0:A_WIDTH]
        yield
        zv = _gelu_tanh(z[:, A_WIDTH:2 * A_WIDTH])
        vgn = [_layer_norm(zv[:, g * A_GROUP_DIM:(g + 1) * A_GROUP_DIM]) for g in range(A_GROUPS)]
        vg_s[slot_a] = (jnp.concatenate(vgn, axis=-1) * vgain_ref[...]).astype(BF16)
        yield
        cos2, sin2 = cos_ref[...], sin_ref[...]
        heads = lambda zz: jnp.concatenate(
            [_rope(zz[:, h * HEAD_DIM:(h + 1) * HEAD_DIM], cos2, sin2) for h in range(HEADS)], -1)
        q_s[slot_a] = heads(z[:, 2 * A_WIDTH:K_COL]).astype(BF16)
        yield
        k_s[slot_a] = heads(z[:, K_COL:V_COL] * (HEAD_DIM ** -0.5))
        v_s[slot_a] = z[:, V_COL:V_COL + V_W].astype(BF16)
        yield
        gf_s[slot_a] = z[:, V_COL + V_W:V_COL + 2 * V_W]
        yield
        gb_s[slot_a] = z[:, V_COL + 2 * V_W:V_COL + 3 * V_W]

    IN_LIGHT = 6

    def stage_mix(final_states):
        s = slot_b
        rows = [slice(c * CHUNK, (c + 1) * CHUNK) for c in range(UNIT_CHUNKS)]
        head_cols = [slice(hd * HEAD_DIM, (hd + 1) * HEAD_DIM) for hd in range(HEADS)]
        table = lambda hd, name: tab_s[hd, TABLE_NAMES.index(name)]

        sp2 = []
        for g in range(A_GROUPS):
            lo = g * A_GROUP_DIM
            vg2 = jnp.concatenate([vg_s[s, r, lo:lo + A_GROUP_DIM] for r in rows], axis=-1)
            sp2.append(_dot(w_s_ref[g], vg2))
        yield
        scores, kv_f, kv_b = [], [], []
        for hd in range(HEADS):
            kc = [k_s[s, r, head_cols[hd]] for r in rows]
            vc = [v_s[s, r, head_cols[hd]] for r in rows]
            scores.append([_dot_nt(q_s[s, r, head_cols[hd]], kc[c].astype(BF16)) for c, r in enumerate(rows)])
            kv_f.append([_dot((kc[c] * table(hd, "kd_f")).T.astype(BF16), vc[c]) for c in range(UNIT_CHUNKS)])
            kv_b.append([_dot((kc[c] * table(hd, "kd_b")).T.astype(BF16), vc[c]) for c in range(UNIT_CHUNKS)])
            yield
        for g in range(A_GROUPS):
            lo = g * A_GROUP_DIM
            for c, r in enumerate(rows):
                gate = sp2[g][:, c * CHUNK:(c + 1) * CHUNK] + b_s_ref[g]
                cat_s[s, r, lo:lo + A_GROUP_DIM] = (_gelu_tanh(ug_s[s, r, lo:lo + A_GROUP_DIM]) * gate).astype(BF16)
            if g % 2 == 1:
                yield
        y_raw = []
        for hd in range(HEADS):
            enter_f, st = [], sp_ref[0, 0, hd]
            for c in range(UNIT_CHUNKS):
                enter_f.append(st)
                st = table(hd, "cd_f") * st + kv_f[hd][c]
            final_f = st
            enter_b, st = [None] * UNIT_CHUNKS, sp_ref[0, 1, hd]
            for c in reversed(range(UNIT_CHUNKS)):
                enter_b[c] = st
                st = table(hd, "cd_b") * st + kv_b[hd][c]
            final_states.append((final_f, st))
            per_chunk = []
            for c, r in enumerate(rows):
                vc = v_s[s, r, head_cols[hd]]
                qf = q_s[s, r, head_cols[hd]].astype(F32)

                def direction(decay, qd, enter):
                    p = (scores[hd][c] * table(hd, decay)).astype(BF16)
                    lhs = jnp.concatenate([p, (qf * table(hd, qd)).astype(BF16)], axis=-1)
                    return _dot(lhs, jnp.concatenate([vc, enter.astype(BF16)], axis=0))

                per_chunk.append((direction("decay_f", "qd_f", enter_f[c]), direction("decay_b", "qd_b", enter_b[c])))
            y_raw.append(per_chunk)
            yield
        for hd in range(HEADS):
            for c, r in enumerate(rows):
                y_f, y_b = y_raw[hd][c]
                cols = slice(A_WIDTH + hd * HEAD_DIM, A_WIDTH + (hd + 1) * HEAD_DIM)
                cat_s[s, r, cols] = (_silu(gf_s[s, r, head_cols[hd]]) * _layer_norm(y_f)
                                     + _silu(gb_s[s, r, head_cols[hd]]) * _layer_norm(y_b)).astype(BF16)
            yield

    MIX_LIGHT = 1 + HEADS + A_GROUPS // 2 + 2 * HEADS

    def emit_states(final_states):
        @pl.when(step - 1 < n_short)
        def _():
            for hd, (final_f, final_b) in enumerate(final_states):
                st_ref[0, 0, 0, hd] = final_f
                st_ref[0, 0, 1, hd] = final_b

    def stage_mlp():
        _, _, g1, sh2, sc2, g2 = _split_mod(modc_ref[0])
        x = _pick_block(step - 2 < n_short, xc_short, xc_long) + g1 * _dot(cat_s[slot_a], w_out_ref[...])
        hb = _rms_mod(x, nmlp_ref[...], sh2, sc2).astype(BF16)
        yield
        acc = yield from _mlp_pieces(hb, w1_ref, w2_ref)
        y_ref[...] = x + g2 * acc

    @pl.when(step == 0)
    def _():
        a = stage_in()
        _weave(_pieces(a, 1), [a, late_weights.finish(0, n_late_step0)], IN_LIGHT + n_late_step0)

    @pl.when(step == 1)
    def _():
        a, finals = stage_in(), []
        _weave(_pieces(a, 1), [a, stage_mix(finals), late_weights.finish(n_late_step0, n_late)],
               IN_LIGHT + MIX_LIGHT + n_late - n_late_step0)
        emit_states(finals)

    @pl.when((step >= 2) & (step < n_units))
    def _():
        a, c, finals = stage_in(), stage_mlp(), []
        heavy = _pieces(c, 1) + _pieces(a, 1) + _pieces(c, MLP_DOT_PIECES + 1)
        _weave(heavy, [a, stage_mix(finals)], IN_LIGHT + MIX_LIGHT)
        emit_states(finals)

    @pl.when(step == n_units)
    def _():
        finals = []
        _weave(_pieces(stage_mlp(), MLP_DOT_PIECES + 2), [stage_mix(finals)], MIX_LIGHT)
        emit_states(finals)

    @pl.when(step == n_units + 1)
    def _():
        _run(stage_mlp())


_HBM = pl.BlockSpec(memory_space=pl.ANY)
_WEIGHT_STAGING = [pltpu.VMEM((STAGE_SLOTS, STAGE_ROWS, STAGE_COLS), F32), pltpu.SemaphoreType.DMA((STAGE_SLOTS,))]


def _even_call(x_short, x_long, mod, sp, rope, params, weights, units, *, layer, even_index):
    n_units = units.total
    unit_a = lambda i: units.clamp(i)
    unit_b = lambda i: units.clamp(i - 1)
    unit_c = lambda i: units.clamp(i - 2)
    mod_row = lambda unit: layer * COND_ROWS + units.mod_row(unit)
    state_block = (1, 1, 2, HEADS, HEAD_DIM, HEAD_DIM)

    in_specs = [pl.BlockSpec((UNIT, D_MODEL), lambda i: (units.short_block(unit_a(i)), 0)),
                pl.BlockSpec((UNIT, D_MODEL), lambda i: (units.long_block(unit_a(i)), 0)),
                pl.BlockSpec((UNIT, D_MODEL), lambda i: (units.short_block(unit_c(i)), 0)),
                pl.BlockSpec((UNIT, D_MODEL), lambda i: (units.long_block(unit_c(i)), 0)),
                pl.BlockSpec((1, 1, N_MOD), lambda i: (mod_row(unit_a(i)), 0, 0)),
                pl.BlockSpec((1, 1, N_MOD), lambda i: (mod_row(unit_c(i)), 0, 0)),
                pl.BlockSpec((1, 2, HEADS, HEAD_DIM, HEAD_DIM), lambda i: (units.state_block(unit_b(i)), 0, 0, 0, 0)),
                pl.BlockSpec((UNIT, HEAD_DIM), lambda i: (units.rope_block(unit_a(i)), 0)),
                pl.BlockSpec((UNIT, HEAD_DIM), lambda i: (units.rope_block(unit_a(i)), 0))]
    in_specs += [_resident(p.shape) for p in params] + [_HBM] * len(weights)
    scratch = [
        pltpu.VMEM((2, UNIT, A_WIDTH), F32),
        pltpu.VMEM((2, UNIT, A_WIDTH), BF16),
        pltpu.VMEM((2, UNIT, QK_W), BF16),
        pltpu.VMEM((2, UNIT, QK_W), F32),
        pltpu.VMEM((2, UNIT, V_W), BF16),
        pltpu.VMEM((2, UNIT, V_W), F32),
        pltpu.VMEM((2, UNIT, V_W), F32),
        pltpu.VMEM((2, UNIT, OUT_WIDTH), BF16),
        pltpu.VMEM((HEADS, len(TABLE_NAMES), CHUNK, CHUNK), F32),
    ]
    scratch += [pltpu.VMEM(w.shape[1:], BF16) for w in weights] + _WEIGHT_STAGING
    return pl.pallas_call(
        functools.partial(_even_kernel, units=units, layer=layer, even_index=even_index),
        grid=(n_units + 2,),
        in_specs=in_specs,
        out_specs=[pl.BlockSpec((UNIT, D_MODEL), lambda i: (unit_c(i), 0)),
                   pl.BlockSpec(state_block, lambda i: (units.short_block(unit_b(i)), 0, 0, 0, 0, 0))],
        out_shape=[jax.ShapeDtypeStruct((n_units * UNIT, D_MODEL), F32),
                   jax.ShapeDtypeStruct((units.n_short,) + state_block[1:], F32)],
        scratch_shapes=scratch,
        compiler_params=pltpu.CompilerParams(
            dimension_semantics=("arbitrary",),
            vmem_limit_bytes=V7X_VMEM_LIMIT_BYTES),
        name="even_layer",
    )(x_short, x_long, x_short, x_long, mod, mod, sp, rope[0], rope[1], *params, *weights)


def _odd_kernel(xp_ref, x_ref, xn_ref, moda_ref, modc_ref, nmix_ref, nmlp_ref, fin_ref, pscale_ref,
                w_pool_hbm, w1_hbm, w2_hbm, y_short, y_long, x1_s, hb_s, w_pool_ref, w1_ref, w2_ref, stage, sem,
                *, n_units, n_short, short_len, long_len, layer, odd_index):
    step = pl.program_id(0)
    slot_a = lax.rem(step, 2)
    slot_b = 1 - slot_a
    unit = x_ref.shape[0]
    rows_w = unit + 2 * POOL_HALO

    @pl.when(step == 0)
    def _():
        _load_weights_bf16(sum((_weight_chunks(w_pool_hbm.at[odd_index, g], w_pool_ref.at[g])
                                for g in range(len(POOL_WINDOWS))), [])
                           + _weight_chunks(w1_hbm.at[layer], w1_ref)
                           + _weight_chunks(w2_hbm.at[layer], w2_ref), stage, sem)

    def stage_pool():
        sh1, sc1, g1, sh2, sc2, _ = _split_mod(moda_ref[0])
        xw = jnp.concatenate([xp_ref[...], x_ref[...], xn_ref[...]], axis=0)
        h = _rms_mod(xw, nmix_ref[...], sh1, sc1)
        seq_len = jnp.where(step < n_short, short_len, long_len)
        first = jnp.minimum(step, n_units - 1) * unit - POOL_HALO
        t = (lax.broadcasted_iota(jnp.int32, (rows_w, 1), 0) + first) & (seq_len - 1)
        yield

        def shift_down(a, k):
            return jnp.where(t >= k, pltpu.roll(a, k, axis=0), 0.0)

        def shift_up(a, k):
            return jnp.where(t < seq_len - k, pltpu.roll(a, rows_w - k, axis=0), 0.0)

        outs = []
        for g, w in enumerate(POOL_WINDOWS):
            hg = h[:, g * POOL_DIM:(g + 1) * POOL_DIM]
            half = w // 2
            left = shift_down(hg, 1)
            right = hg
            span = 1
            while span < half:
                left = left + shift_down(left, span)
                right = right + shift_up(right, span)
                span *= 2
            cnt = (jnp.minimum(t + half, seq_len) - jnp.maximum(t - half, 0)).astype(F32)
            pooled = ((left + right) / cnt - hg)[POOL_HALO:POOL_HALO + unit].astype(BF16)
            outs.append(_dot(pooled, w_pool_ref[g]))
            yield
        x1 = x_ref[...] + g1 * (jnp.concatenate(outs, axis=-1) * pscale_ref[...])
        x1_s[slot_a] = x1
        hb_s[slot_a] = _rms_mod(x1, nmlp_ref[...], sh2, sc2).astype(BF16)

    def stage_mlp(result):
        g2 = _split_mod(modc_ref[0])[5]
        acc = yield from _mlp_pieces(hb_s[slot_b], w1_ref, w2_ref)
        result.append(_rms(x1_s[slot_b] + g2 * acc) * fin_ref[...])

    def store(result):
        @pl.when(step - 1 < n_short)
        def _():
            y_short[...] = result[0]

        @pl.when(step - 1 >= n_short)
        def _():
            y_long[...] = result[0]

    pool_light = 2 + len(POOL_WINDOWS)

    @pl.when(step == 0)
    def _():
        _run(stage_pool())

    @pl.when((step >= 1) & (step < n_units))
    def _():
        result = []
        _weave(_pieces(stage_mlp(result), MLP_DOT_PIECES + 1), [stage_pool()], pool_light)
        store(result)

    @pl.when(step == n_units)
    def _():
        result = []
        _run(stage_mlp(result))
        store(result)


def _odd_call(x2, mod, params, weights, units, *, unit, short_len, long_len, layer, odd_index):
    n_units = units.total
    halo_per_unit = unit // POOL_HALO
    last_halo = x2.shape[0] // POOL_HALO - 1
    unit_a = lambda i: units.clamp(i)
    unit_c = lambda i: units.clamp(i - 1)
    mod_row = lambda u: layer * COND_ROWS + units.mod_row(u)
    assert (units.n_short * unit) % long_len == 0
    in_specs = [pl.BlockSpec((POOL_HALO, D_MODEL), lambda i: (jnp.maximum(unit_a(i) * halo_per_unit - 1, 0), 0)),
                pl.BlockSpec((unit, D_MODEL), lambda i: (unit_a(i), 0)),
                pl.BlockSpec((POOL_HALO, D_MODEL),
                             lambda i: (jnp.minimum((unit_a(i) + 1) * halo_per_unit, last_halo), 0)),
                pl.BlockSpec((1, 1, N_MOD), lambda i: (mod_row(unit_a(i)), 0, 0)),
                pl.BlockSpec((1, 1, N_MOD), lambda i: (mod_row(unit_c(i)), 0, 0))]
    in_specs += [_resident(p.shape) for p in params] + [_HBM] * len(weights)
    scratch = [pltpu.VMEM((2, unit, D_MODEL), F32),
               pltpu.VMEM((2, unit, D_MODEL), BF16)]
    scratch += [pltpu.VMEM(w.shape[1:], BF16) for w in weights] + _WEIGHT_STAGING
    return pl.pallas_call(
        functools.partial(_odd_kernel, n_units=n_units, n_short=units.n_short,
                          short_len=short_len, long_len=long_len, layer=layer, odd_index=odd_index),
        grid=(n_units + 1,),
        in_specs=in_specs,
        out_specs=[pl.BlockSpec((unit, D_MODEL), lambda i: (units.short_block(unit_c(i)), 0)),
                   pl.BlockSpec((unit, D_MODEL), lambda i: (units.long_block(unit_c(i)), 0))],
        out_shape=[jax.ShapeDtypeStruct((units.n_short * unit, D_MODEL), F32),
                   jax.ShapeDtypeStruct((units.n_long * unit, D_MODEL), F32)],
        scratch_shapes=scratch,
        compiler_params=pltpu.CompilerParams(
            dimension_semantics=("arbitrary",),
            vmem_limit_bytes=V7X_VMEM_LIMIT_BYTES),
        name="odd_layer",
    )(x2, x2, x2, mod, mod, *params, *weights)


def _rope_tables(n_tokens):
    f32 = np.float32
    rows = n_tokens // GRID_W
    pos_r = np.repeat(np.arange(rows, dtype=f32), GRID_W)
    pos_c = np.tile(np.arange(GRID_W, dtype=f32), rows)
    inv = (f32(ROPE_BASE) ** (-np.arange(ROPE_PAIRS_AXIS, dtype=f32) / f32(ROPE_PAIRS_AXIS))).astype(f32)
    ang = np.concatenate([pos_r[:, None] * inv, pos_c[:, None] * inv], axis=-1).astype(f32)
    cos, sin = np.cos(ang).astype(f32), np.sin(ang).astype(f32)
    return np.concatenate([cos, cos], axis=-1), np.concatenate([-sin, sin], axis=-1)


def kernel(x_prompt, x_sample, state_ret, c, c_ctx, ada_w, ada_b, norm_mix, norm_mlp, mlp_w1, mlp_w2,
           ev_w_in, ev_w_s, ev_b_s, ev_vnorm, ev_decay, ev_w_out, od_w_pool, od_pool_scale, final_norm):
    n_prompt, prompt_len, _ = x_prompt.shape
    n_sample, sample_len, _ = x_sample.shape
    assert DEPTH == 2 and n_sample + 1 <= COND_ROWS
    assert prompt_len == UNIT and sample_len % UNIT == 0
    units = _Units(n_prompt, n_sample, sample_len // UNIT)

    mod = _ada_call(c_ctx.reshape(1, D_MODEL), c, ada_w, ada_b)

    row = lambda v: v.reshape(1, -1)
    dec = jnp.broadcast_to(ev_decay[0].reshape(2 * HEADS, 1), (2 * HEADS, CHUNK))
    even_params = (
        row(norm_mix[0]), row(norm_mlp[0]),
        jnp.broadcast_to(ev_b_s[0][:, :, None], (A_GROUPS, CHUNK, A_GROUP_DIM)),
        row(ev_vnorm[0]), dec,
    )
    odd_params = (row(norm_mix[1]), row(norm_mlp[1]), row(final_norm), row(od_pool_scale[0]))
    even_weights = (ev_w_in, ev_w_out, mlp_w1, mlp_w2, ev_w_s)
    odd_weights = (od_w_pool, mlp_w1, mlp_w2)

    xp2 = x_prompt.reshape(n_prompt * prompt_len, D_MODEL)
    xs2 = x_sample.reshape(n_sample * sample_len, D_MODEL)
    cos2, sin2 = _rope_tables(sample_len)
    sp = _ret_state_call(xs2, mod, state_ret, (cos2, sin2), row(norm_mix[0]), ev_w_in, dec,
                         n_seq=n_sample, seq_len=sample_len, layer=0, even_index=0)
    rope = (np.concatenate([np.ones((UNIT, HEAD_DIM), np.float32), cos2]),
            np.concatenate([np.zeros((UNIT, HEAD_DIM), np.float32), sin2]))
    x_all, new_state = _even_call(xp2, xs2, mod, sp, rope, even_params, even_weights, units,
                                  layer=0, even_index=0)
    y_prompt, y_sample = _odd_call(x_all, mod, odd_params, odd_weights, units, unit=UNIT,
                                   short_len=prompt_len, long_len=sample_len, layer=1, odd_index=0)

    return (y_prompt.reshape(n_prompt, prompt_len, D_MODEL),
            y_sample.reshape(n_sample, sample_len, D_MODEL),
            new_state)
```

```python
import functools

import jax
import jax.numpy as jnp
import numpy as np
from jax import lax
from jax.experimental import pallas as pl
from jax.experimental.pallas import tpu as pltpu

D_MODEL = 1024
DEPTH = 2
GRID_W = 64
A_WIDTH = D_MODEL // 2
A_GROUPS = 4
A_GROUP_DIM = A_WIDTH // A_GROUPS
CHUNK = 128
HEADS = 4
HEAD_DIM = (D_MODEL // 2) // HEADS
ROPE_BASE = 10000.0
ROPE_PAIRS_AXIS = HEAD_DIM // 4
POOL_WINDOWS = (2, 4, 8, 16)
POOL_DIM = D_MODEL // len(POOL_WINDOWS)
POOL_HALO = max(POOL_WINDOWS) // 2
D_FF = 4 * D_MODEL
EPS = 1e-6
QK_W = HEADS * HEAD_DIM
V_W = HEADS * HEAD_DIM
IN_WIDTH = 2 * A_WIDTH + 2 * QK_W + 3 * V_W
OUT_WIDTH = A_WIDTH + V_W
N_MOD = 6 * D_MODEL
K_COL = 2 * A_WIDTH + QK_W
V_COL = K_COL + QK_W

UNIT = 2 * CHUNK
UNIT_CHUNKS = UNIT // CHUNK
ODD_UNIT = 2 * UNIT
FF_CHUNK = 1024
ADA_BLOCK_N = 3072
COND_ROWS = 8
STAGE_ROWS = 512
STAGE_COLS = 1024
STAGE_SLOTS = 3
V7X_VMEM_LIMIT_BYTES = 60 * 1024 * 1024

F32 = jnp.float32
BF16 = jnp.bfloat16


def _dot(a, b):
    return jnp.dot(a, b, preferred_element_type=F32)


def _dot_nt(a, b):
    return lax.dot_general(a, b, (((1,), (1,)), ((), ())), preferred_element_type=F32)


def _silu(x):
    return x * jax.nn.sigmoid(x)


def _gelu_tanh(x):
    return 0.5 * x * (1.0 + jnp.tanh(0.7978845608028654 * (x + 0.044715 * (x * x * x))))


def _rms(x):
    return x * lax.rsqrt(jnp.mean(x * x, axis=-1, keepdims=True) + EPS)


def _rms_mod(x, gain, shift, scale):
    return (_rms(x) * gain) * (1.0 + scale) + shift


def _layer_norm(x):
    mu = jnp.mean(x, axis=-1, keepdims=True)
    d = x - mu
    return d * lax.rsqrt(jnp.mean(d * d, axis=-1, keepdims=True) + EPS)


def _split_mod(mod):
    return [mod[:, i * D_MODEL:(i + 1) * D_MODEL] for i in range(6)]


def _rope(x, cos2, sin2):
    return x * cos2 + pltpu.roll(x, HEAD_DIM // 2, axis=1) * sin2


def _pick_block(take_first, a_ref, b_ref):
    rows = lax.broadcasted_iota(jnp.int32, (a_ref.shape[0], 1), 0)
    return jnp.where(rows < jnp.where(take_first, a_ref.shape[0], 0), a_ref[...], b_ref[...])


def _run(gen, n=None):
    done = 0
    while n is None or done < n:
        try:
            next(gen)
        except StopIteration:
            break
        done += 1
    return done


def _weave(heavy, lights, n_light):
    heavy = list(heavy)
    pending = _roundrobin(lights)
    ran = 0
    for k, piece in enumerate(heavy):
        piece()
        want = ((k + 1) * n_light + len(heavy) - 1) // len(heavy)
        ran += _run(pending, want - ran)
    _run(pending)


def _pieces(gen, n):
    return [functools.partial(next, gen, None) for _ in range(n)]


def _roundrobin(gens):
    live = list(gens)
    while live:
        for g in list(live):
            try:
                next(g)
            except StopIteration:
                live.remove(g)
                continue
            yield


MLP_DOT_PIECES = 2 * (D_FF // FF_CHUNK)


def _mlp_pieces(hb, w1_ref, w2_ref):
    n_ff = D_FF // FF_CHUNK
    up = lambda j: _dot(hb, w1_ref[:, j * FF_CHUNK:(j + 1) * FF_CHUNK])
    a = up(0)
    yield
    acc = None
    for j in range(n_ff):
        a_next = None
        if j + 1 < n_ff:
            a_next = up(j + 1)
            yield
        p = _dot(jnp.square(jnp.maximum(a, 0.0)).astype(BF16), w2_ref[j * FF_CHUNK:(j + 1) * FF_CHUNK, :])
        acc = p if acc is None else acc + p
        a = a_next
        yield
    return acc


def _resident(shape):
    return pl.BlockSpec(shape, lambda i: (0,) * len(shape), pipeline_mode=pl.Buffered(1))


def _weight_chunks(w_hbm, w_vmem):
    rows, cols = w_hbm.shape
    jobs = []
    for r0 in range(0, rows, STAGE_ROWS):
        for c0 in range(0, cols, STAGE_COLS):
            window = (pl.ds(r0, min(STAGE_ROWS, rows - r0)), pl.ds(c0, min(STAGE_COLS, cols - c0)))
            jobs.append((w_hbm.at[window], w_vmem.at[window]))
    return jobs


class _WeightRing:
    def __init__(self, jobs, stage, sem):
        self.jobs, self.stage, self.sem, self.n_slots = jobs, stage, sem, stage.shape[0]

    def _copy(self, k):
        r, c = self.jobs[k][0].shape
        slot = k % self.n_slots
        return pltpu.make_async_copy(self.jobs[k][0], self.stage.at[slot, pl.ds(0, r), pl.ds(0, c)],
                                     self.sem.at[slot])

    def prefill(self):
        for k in range(min(self.n_slots - 1, len(self.jobs))):
            self._copy(k).start()

    def finish(self, first, last):
        for k in range(first, last):
            if k + self.n_slots - 1 < len(self.jobs):
                self._copy(k + self.n_slots - 1).start()
            self._copy(k).wait()
            src, dst = self.jobs[k]
            r, c = src.shape
            dst[...] = self.stage[k % self.n_slots, 0:r, 0:c].astype(BF16)
            yield


def _load_weights_bf16(jobs, stage, sem):
    ring = _WeightRing(jobs, stage, sem)
    ring.prefill()
    _run(ring.finish(0, len(jobs)))


class _Units:
    def __init__(self, n_short, n_long, units_per_long):
        self.n_short, self.n_long, self.units_per_long = n_short, n_long * units_per_long, units_per_long
        self.total = self.n_short + self.n_long

    def clamp(self, unit):
        return jnp.clip(unit, 0, self.total - 1)

    def short_block(self, unit):
        return jnp.minimum(unit, self.n_short - 1)

    def long_block(self, unit):
        return jnp.clip(unit - self.n_short, 0, self.n_long - 1)

    def mod_row(self, unit):
        return jnp.where(unit < self.n_short, 0, 1 + (unit - self.n_short) // self.units_per_long)

    def rope_block(self, unit):
        return jnp.where(unit < self.n_short, 0, 1 + (unit - self.n_short) % self.units_per_long)

    def state_block(self, unit):
        return jnp.where(unit < self.n_short, 0, self.units_per_long + unit - self.n_short)


def _ada_kernel(shared_ref, each_ref, w_ref, b_ref, out_ref):
    row_id = lax.broadcasted_iota(jnp.int32, (COND_ROWS, 1), 0)
    cond = jnp.where(row_id == 0, shared_ref[...], 0.0)
    for r in range(each_ref.shape[0]):
        cond = jnp.where(row_id == 1 + r, each_ref[r:r + 1, :], cond)
    s = _silu(cond).astype(BF16)
    rows = _dot(s, w_ref[0].astype(BF16)) + b_ref[pl.ds(pl.program_id(0), 1), :]
    for r in range(COND_ROWS):
        out_ref[r] = rows[r:r + 1, :]


def _ada_call(cond_shared, cond_each, ada_w, ada_b):
    return pl.pallas_call(
        _ada_kernel,
        grid=(DEPTH, N_MOD // ADA_BLOCK_N),
        in_specs=[
            pl.BlockSpec(cond_shared.shape, lambda i, j: (0, 0)),
            pl.BlockSpec(cond_each.shape, lambda i, j: (0, 0)),
            pl.BlockSpec((1, D_MODEL, ADA_BLOCK_N), lambda i, j: (i, 0, j)),
            pl.BlockSpec((DEPTH, ADA_BLOCK_N), lambda i, j: (0, j)),
        ],
        out_specs=pl.BlockSpec((COND_ROWS, 1, ADA_BLOCK_N), lambda i, j: (i, 0, j)),
        out_shape=jax.ShapeDtypeStruct((DEPTH * COND_ROWS, 1, N_MOD), F32),
        compiler_params=pltpu.CompilerParams(
            dimension_semantics=("arbitrary", "arbitrary"),
            vmem_limit_bytes=V7X_VMEM_LIMIT_BYTES),
        name="ada_mod",
    )(cond_shared, cond_each, ada_w, ada_b)


def _decay_tables(dec_ref, hd):
    row = lax.broadcasted_iota(jnp.int32, (CHUNK, CHUNK), 0).astype(F32)
    col = lax.broadcasted_iota(jnp.int32, (CHUNK, CHUNK), 1).astype(F32)
    lg_f = jnp.broadcast_to(-jnp.exp(dec_ref[hd:hd + 1, :]), (CHUNK, CHUNK))
    lg_b = jnp.broadcast_to(-jnp.exp(dec_ref[HEADS + hd:HEADS + hd + 1, :]), (CHUNK, CHUNK))
    d_f = row - col
    d_b = col - row
    return dict(
        decay_f=jnp.where(d_f >= 0, jnp.exp(lg_f * jnp.maximum(d_f, 0.0)), 0.0),
        decay_b=jnp.where(d_b >= 0, jnp.exp(lg_b * jnp.maximum(d_b, 0.0)), 0.0),
        qd_f=jnp.exp(lg_f * (row + 1.0)),
        qd_b=jnp.exp(lg_b * (CHUNK - row)),
        kd_f=jnp.exp(lg_f * (CHUNK - 1.0 - row)),
        kd_b=jnp.exp(lg_b * row),
        cd_f=jnp.exp(lg_f * CHUNK),
        cd_b=jnp.exp(lg_b * CHUNK),
    )


TABLE_NAMES = ("decay_f", "decay_b", "qd_f", "qd_b", "kd_f", "kd_b", "cd_f", "cd_b")


def _ret_state_kernel(x_ref, mod_ref, s0_ref, cos_ref, sin_ref, nmix_ref, wk_ref, wv_ref, dec_ref, sp_ref,
                      *, n_chunk):
    step = pl.program_id(0)

    @pl.when(step == 0)
    def _():
        sp_ref[...] = jnp.zeros_like(sp_ref)

    @pl.when(step > 0)
    def _():
        sh1, sc1 = _split_mod(mod_ref[0])[:2]
        hb = _rms_mod(x_ref[...], nmix_ref[...], sh1, sc1).astype(BF16)
        zk = _dot(hb, wk_ref[...].astype(BF16)) * (HEAD_DIM ** -0.5)
        zv = _dot(hb, wv_ref[...].astype(BF16)).astype(BF16)
        cos2, sin2 = cos_ref[...], sin_ref[...]
        for hd in range(HEADS):
            lo = hd * HEAD_DIM
            tb = _decay_tables(dec_ref, hd)
            kh = _rope(zk[:, lo:lo + HEAD_DIM], cos2, sin2)
            kv_f, kv_b = [], []
            for c in range(n_chunk):
                kc = kh[c * CHUNK:(c + 1) * CHUNK]
                vc = zv[c * CHUNK:(c + 1) * CHUNK, lo:lo + HEAD_DIM]
                kv_f.append(_dot((kc * tb["kd_f"]).T.astype(BF16), vc))
                kv_b.append(_dot((kc * tb["kd_b"]).T.astype(BF16), vc))
            st = s0_ref[0, 0, 0, hd]
            for c in range(n_chunk):
                if c % UNIT_CHUNKS == 0:
                    sp_ref[c // UNIT_CHUNKS, 0, hd] = st
                st = tb["cd_f"] * st + kv_f[c]
            st = s0_ref[0, 0, 1, hd]
            for c in reversed(range(n_chunk)):
                if c % UNIT_CHUNKS == UNIT_CHUNKS - 1:
                    sp_ref[c // UNIT_CHUNKS, 1, hd] = st
                st = tb["cd_b"] * st + kv_b[c]


def _ret_state_call(x2, mod, s0, rope, nmix, w_in, dec, *, n_seq, seq_len, layer, even_index):
    n_chunk = seq_len // CHUNK
    units = seq_len // UNIT
    seq = lambda i: jnp.maximum(i - 1, 0)
    mod_row = lambda i: layer * COND_ROWS + 1 + seq(i)
    return pl.pallas_call(
        functools.partial(_ret_state_kernel, n_chunk=n_chunk),
        grid=(n_seq + 1,),
        in_specs=[
            pl.BlockSpec((seq_len, D_MODEL), lambda i: (seq(i), 0)),
            pl.BlockSpec((1, 1, N_MOD), lambda i: (mod_row(i), 0, 0)),
            pl.BlockSpec((1, 1, 2, HEADS, HEAD_DIM, HEAD_DIM), lambda i: (seq(i), 0, 0, 0, 0, 0)),
            _resident((seq_len, HEAD_DIM)), _resident((seq_len, HEAD_DIM)),
            _resident((1, D_MODEL)),
            pl.BlockSpec((None, D_MODEL, QK_W), lambda i: (even_index, 0, K_COL // QK_W),
                         pipeline_mode=pl.Buffered(1)),
            pl.BlockSpec((None, D_MODEL, V_W), lambda i: (even_index, 0, V_COL // V_W),
                         pipeline_mode=pl.Buffered(1)),
            _resident(dec.shape),
        ],
        out_specs=pl.BlockSpec((units, 2, HEADS, HEAD_DIM, HEAD_DIM), lambda i: (i, 0, 0, 0, 0)),
        out_shape=jax.ShapeDtypeStruct(((n_seq + 1) * units, 2, HEADS, HEAD_DIM, HEAD_DIM), F32),
        compiler_params=pltpu.CompilerParams(
            dimension_semantics=("arbitrary",),
            vmem_limit_bytes=V7X_VMEM_LIMIT_BYTES),
        name="ret_state",
    )(x2, mod, s0, rope[0], rope[1], nmix, w_in, w_in, dec)


def _even_kernel(xa_short, xa_long, xc_short, xc_long, moda_ref, modc_ref, sp_ref, cos_ref, sin_ref,
                 nmix_ref, nmlp_ref, b_s_ref, vgain_ref, dec_ref, w_in_hbm, w_out_hbm, w1_hbm, w2_hbm, w_s_hbm,
                 y_ref, st_ref, ug_s, vg_s, q_s, k_s, v_s, gf_s, gb_s, cat_s, tab_s,
                 w_in_ref, w_out_ref, w1_ref, w2_ref, w_s_ref, stage, sem, *, units, layer, even_index):
    n_units, n_short = units.total, units.n_short
    step = pl.program_id(0)
    slot_a = lax.rem(step, 2)
    slot_b = 1 - slot_a

    early_weights = (_weight_chunks(w_in_hbm.at[even_index], w_in_ref)
                     + sum((_weight_chunks(w_s_hbm.at[even_index, g], w_s_ref.at[g]) for g in range(A_GROUPS)), []))
    late_weights = _WeightRing(_weight_chunks(w_out_hbm.at[even_index], w_out_ref)
                               + _weight_chunks(w1_hbm.at[layer], w1_ref)
                               + _weight_chunks(w2_hbm.at[layer], w2_ref), stage, sem)
    n_late = len(late_weights.jobs)
    n_late_step0 = min(n_late, 7)

    @pl.when(step == 0)
    def _():
        _load_weights_bf16(early_weights, stage, sem)
        late_weights.prefill()
        for hd in range(HEADS):
            tb = _decay_tables(dec_ref, hd)
            for n, name in enumerate(TABLE_NAMES):
                tab_s[hd, n] = tb[name]

    def stage_in():
        sh1, sc1 = _split_mod(moda_ref[0])[:2]
        x = _pick_block(step < n_short, xa_short, xa_long)
        hb = _rms_mod(x, nmix_ref[...], sh1, sc1).astype(BF16)
        z = _dot(hb, w_in_ref[...])
        yield
        ug_s[slot_a] = z[:, 0:A_WIDTH]
        yield
        zv = _gelu_tanh(z[:, A_WIDTH:2 * A_WIDTH])
        vgn = [_layer_norm(zv[:, g * A_GROUP_DIM:(g + 1) * A_GROUP_DIM]) for g in range(A_GROUPS)]
        vg_s[slot_a] = (jnp.concatenate(vgn, axis=-1) * vgain_ref[...]).astype(BF16)
        yield
        cos2, sin2 = cos_ref[...], sin_ref[...]
        heads = lambda zz: jnp.concatenate(
            [_rope(zz[:, h * HEAD_DIM:(h + 1) * HEAD_DIM], cos2, sin2) for h in range(HEADS)], -1)
        q_s[slot_a] = heads(z[:, 2 * A_WIDTH:K_COL]).astype(BF16)
        yield
        k_s[slot_a] = heads(z[:, K_COL:V_COL] * (HEAD_DIM ** -0.5))
        v_s[slot_a] = z[:, V_COL:V_COL + V_W].astype(BF16)
        yield
        gf_s[slot_a] = z[:, V_COL + V_W:V_COL + 2 * V_W]
        yield
        gb_s[slot_a] = z[:, V_COL + 2 * V_W:V_COL + 3 * V_W]

    IN_LIGHT = 6

    def stage_mix(final_states):
        s = slot_b
        rows = [slice(c * CHUNK, (c + 1) * CHUNK) for c in range(UNIT_CHUNKS)]
        head_cols = [slice(hd * HEAD_DIM, (hd + 1) * HEAD_DIM) for hd in range(HEADS)]
        table = lambda hd, name: tab_s[hd, TABLE_NAMES.index(name)]

        sp2 = []
        for g in range(A_GROUPS):
            lo = g * A_GROUP_DIM
            vg2 = jnp.concatenate([vg_s[s, r, lo:lo + A_GROUP_DIM] for r in rows], axis=-1)
            sp2.append(_dot(w_s_ref[g], vg2))
        yield
        scores, kv_f, kv_b = [], [], []
        for hd in range(HEADS):
            kc = [k_s[s, r, head_cols[hd]] for r in rows]
            vc = [v_s[s, r, head_cols[hd]] for r in rows]
            scores.append([_dot_nt(q_s[s, r, head_cols[hd]], kc[c].astype(BF16)) for c, r in enumerate(rows)])
            kv_f.append([_dot((kc[c] * table(hd, "kd_f")).T.astype(BF16), vc[c]) for c in range(UNIT_CHUNKS)])
            kv_b.append([_dot((kc[c] * table(hd, "kd_b")).T.astype(BF16), vc[c]) for c in range(UNIT_CHUNKS)])
            yield
        for g in range(A_GROUPS):
            lo = g * A_GROUP_DIM
            for c, r in enumerate(rows):
                gate = sp2[g][:, c * CHUNK:(c + 1) * CHUNK] + b_s_ref[g]
                cat_s[s, r, lo:lo + A_GROUP_DIM] = (_gelu_tanh(ug_s[s, r, lo:lo + A_GROUP_DIM]) * gate).astype(BF16)
            if g % 2 == 1:
                yield
        y_raw = []
        for hd in range(HEADS):
            enter_f, st = [], sp_ref[0, 0, hd]
            for c in range(UNIT_CHUNKS):
                enter_f.append(st)
                st = table(hd, "cd_f") * st + kv_f[hd][c]
            final_f = st
            enter_b, st = [None] * UNIT_CHUNKS, sp_ref[0, 1, hd]
            for c in reversed(range(UNIT_CHUNKS)):
                enter_b[c] = st
                st = table(hd, "cd_b") * st + kv_b[hd][c]
            final_states.append((final_f, st))
            per_chunk = []
            for c, r in enumerate(rows):
                vc = v_s[s, r, head_cols[hd]]
                qf = q_s[s, r, head_cols[hd]].astype(F32)

                def direction(decay, qd, enter):
                    p = (scores[hd][c] * table(hd, decay)).astype(BF16)
                    lhs = jnp.concatenate([p, (qf * table(hd, qd)).astype(BF16)], axis=-1)
                    return _dot(lhs, jnp.concatenate([vc, enter.astype(BF16)], axis=0))

                per_chunk.append((direction("decay_f", "qd_f", enter_f[c]), direction("decay_b", "qd_b", enter_b[c])))
            y_raw.append(per_chunk)
            yield
        for hd in range(HEADS):
            for c, r in enumerate(rows):
                y_f, y_b = y_raw[hd][c]
                cols = slice(A_WIDTH + hd * HEAD_DIM, A_WIDTH + (hd + 1) * HEAD_DIM)
                cat_s[s, r, cols] = (_silu(gf_s[s, r, head_cols[hd]]) * _layer_norm(y_f)
                                     + _silu(gb_s[s, r, head_cols[hd]]) * _layer_norm(y_b)).astype(BF16)
            yield

    MIX_LIGHT = 1 + HEADS + A_GROUPS // 2 + 2 * HEADS

    def emit_states(final_states):
        @pl.when(step - 1 < n_short)
        def _():
            for hd, (final_f, final_b) in enumerate(final_states):
                st_ref[0, 0, 0, hd] = final_f
                st_ref[0, 0, 1, hd] = final_b

    def stage_mlp():
        _, _, g1, sh2, sc2, g2 = _split_mod(modc_ref[0])
        x = _pick_block(step - 2 < n_short, xc_short, xc_long) + g1 * _dot(cat_s[slot_a], w_out_ref[...])
        hb = _rms_mod(x, nmlp_ref[...], sh2, sc2).astype(BF16)
        yield
        acc = yield from _mlp_pieces(hb, w1_ref, w2_ref)
        y_ref[...] = x + g2 * acc

    @pl.when(step == 0)
    def _():
        a = stage_in()
        _weave(_pieces(a, 1), [a, late_weights.finish(0, n_late_step0)], IN_LIGHT + n_late_step0)

    @pl.when(step == 1)
    def _():
        a, finals = stage_in(), []
        _weave(_pieces(a, 1), [a, stage_mix(finals), late_weights.finish(n_late_step0, n_late)],
               IN_LIGHT + MIX_LIGHT + n_late - n_late_step0)
        emit_states(finals)

    @pl.when((step >= 2) & (step < n_units))
    def _():
        a, c, finals = stage_in(), stage_mlp(), []
        heavy = _pieces(c, 1) + _pieces(a, 1) + _pieces(c, MLP_DOT_PIECES + 1)
        _weave(heavy, [a, stage_mix(finals)], IN_LIGHT + MIX_LIGHT)
        emit_states(finals)

    @pl.when(step == n_units)
    def _():
        finals = []
        _weave(_pieces(stage_mlp(), MLP_DOT_PIECES + 2), [stage_mix(finals)], MIX_LIGHT)
        emit_states(finals)

    @pl.when(step == n_units + 1)
    def _():
        _run(stage_mlp())


_HBM = pl.BlockSpec(memory_space=pl.ANY)
_WEIGHT_STAGING = [pltpu.VMEM((STAGE_SLOTS, STAGE_ROWS, STAGE_COLS), F32), pltpu.SemaphoreType.DMA((STAGE_SLOTS,))]


def _even_call(x_short, x_long, mod, sp, rope, params, weights, units, *, layer, even_index):
    n_units = units.total
    unit_a = lambda i: units.clamp(i)
    unit_b = lambda i: units.clamp(i - 1)
    unit_c = lambda i: units.clamp(i - 2)
    mod_row = lambda unit: layer * COND_ROWS + units.mod_row(unit)
    state_block = (1, 1, 2, HEADS, HEAD_DIM, HEAD_DIM)

    in_specs = [pl.BlockSpec((UNIT, D_MODEL), lambda i: (units.short_block(unit_a(i)), 0)),
                pl.BlockSpec((UNIT, D_MODEL), lambda i: (units.long_block(unit_a(i)), 0)),
                pl.BlockSpec((UNIT, D_MODEL), lambda i: (units.short_block(unit_c(i)), 0)),
                pl.BlockSpec((UNIT, D_MODEL), lambda i: (units.long_block(unit_c(i)), 0)),
                pl.BlockSpec((1, 1, N_MOD), lambda i: (mod_row(unit_a(i)), 0, 0)),
                pl.BlockSpec((1, 1, N_MOD), lambda i: (mod_row(unit_c(i)), 0, 0)),
                pl.BlockSpec((1, 2, HEADS, HEAD_DIM, HEAD_DIM), lambda i: (units.state_block(unit_b(i)), 0, 0, 0, 0)),
                pl.BlockSpec((UNIT, HEAD_DIM), lambda i: (units.rope_block(unit_a(i)), 0)),
                pl.BlockSpec((UNIT, HEAD_DIM), lambda i: (units.rope_block(unit_a(i)), 0))]
    in_specs += [_resident(p.shape) for p in params] + [_HBM] * len(weights)
    scratch = [
        pltpu.VMEM((2, UNIT, A_WIDTH), F32),
        pltpu.VMEM((2, UNIT, A_WIDTH), BF16),
        pltpu.VMEM((2, UNIT, QK_W), BF16),
        pltpu.VMEM((2, UNIT, QK_W), F32),
        pltpu.VMEM((2, UNIT, V_W), BF16),
        pltpu.VMEM((2, UNIT, V_W), F32),
        pltpu.VMEM((2, UNIT, V_W), F32),
        pltpu.VMEM((2, UNIT, OUT_WIDTH), BF16),
        pltpu.VMEM((HEADS, len(TABLE_NAMES), CHUNK, CHUNK), F32),
    ]
    scratch += [pltpu.VMEM(w.shape[1:], BF16) for w in weights] + _WEIGHT_STAGING
    return pl.pallas_call(
        functools.partial(_even_kernel, units=units, layer=layer, even_index=even_index),
        grid=(n_units + 2,),
        in_specs=in_specs,
        out_specs=[pl.BlockSpec((UNIT, D_MODEL), lambda i: (unit_c(i), 0)),
                   pl.BlockSpec(state_block, lambda i: (units.short_block(unit_b(i)), 0, 0, 0, 0, 0))],
        out_shape=[jax.ShapeDtypeStruct((n_units * UNIT, D_MODEL), F32),
                   jax.ShapeDtypeStruct((units.n_short,) + state_block[1:], F32)],
        scratch_shapes=scratch,
        compiler_params=pltpu.CompilerParams(
            dimension_semantics=("arbitrary",),
            vmem_limit_bytes=V7X_VMEM_LIMIT_BYTES),
        name="even_layer",
    )(x_short, x_long, x_short, x_long, mod, mod, sp, rope[0], rope[1], *params, *weights)


def _odd_kernel(xp_ref, x_ref, xn_ref, moda_ref, modc_ref, nmix_ref, nmlp_ref, fin_ref, pscale_ref,
                w_pool_hbm, w1_hbm, w2_hbm, y_short, y_long, x1_s, hb_s, w_pool_ref, w1_ref, w2_ref, stage, sem,
                *, n_units, n_short, short_len, long_len, layer, odd_index):
    step = pl.program_id(0)
    slot_a = lax.rem(step, 2)
    slot_b = 1 - slot_a
    unit = x_ref.shape[0]
    rows_w = unit + 2 * POOL_HALO

    late_weights = _WeightRing(_weight_chunks(w1_hbm.at[layer], w1_ref) + _weight_chunks(w2_hbm.at[layer], w2_ref),
                               stage, sem)

    @pl.when(step == 0)
    def _():
        _load_weights_bf16(sum((_weight_chunks(w_pool_hbm.at[odd_index, g], w_pool_ref.at[g])
                                for g in range(len(POOL_WINDOWS))), []), stage, sem)
        late_weights.prefill()

    def stage_pool():
        sh1, sc1, g1, sh2, sc2, _ = _split_mod(moda_ref[0])
        xw = jnp.concatenate([xp_ref[...], x_ref[...], xn_ref[...]], axis=0)
        h = _rms_mod(xw, nmix_ref[...], sh1, sc1)
        seq_len = jnp.where(step < n_short, short_len, long_len)
        first = jnp.minimum(step, n_units - 1) * unit - POOL_HALO
        t = (lax.broadcasted_iota(jnp.int32, (rows_w, 1), 0) + first) & (seq_len - 1)
        yield

        def shift_down(a, k):
            return jnp.where(t >= k, pltpu.roll(a, k, axis=0), 0.0)

        def shift_up(a, k):
            return jnp.where(t < seq_len - k, pltpu.roll(a, rows_w - k, axis=0), 0.0)

        outs = []
        for g, w in enumerate(POOL_WINDOWS):
            hg = h[:, g * POOL_DIM:(g + 1) * POOL_DIM]
            half = w // 2
            left = shift_down(hg, 1)
            right = hg
            span = 1
            while span < half:
                left = left + shift_down(left, span)
                right = right + shift_up(right, span)
                span *= 2
            cnt = (jnp.minimum(t + half, seq_len) - jnp.maximum(t - half, 0)).astype(F32)
            pooled = ((left + right) / cnt - hg)[POOL_HALO:POOL_HALO + unit].astype(BF16)
            outs.append(_dot(pooled, w_pool_ref[g]))
            yield
        x1 = x_ref[...] + g1 * (jnp.concatenate(outs, axis=-1) * pscale_ref[...])
        x1_s[slot_a] = x1
        hb_s[slot_a] = _rms_mod(x1, nmlp_ref[...], sh2, sc2).astype(BF16)

    def stage_mlp(result):
        g2 = _split_mod(modc_ref[0])[5]
        acc = yield from _mlp_pieces(hb_s[slot_b], w1_ref, w2_ref)
        result.append(_rms(x1_s[slot_b] + g2 * acc) * fin_ref[...])

    def store(result):
        @pl.when(step - 1 < n_short)
        def _():
            y_short[...] = result[0]

        @pl.when(step - 1 >= n_short)
        def _():
            y_long[...] = result[0]

    pool_light = 2 + len(POOL_WINDOWS)

    @pl.when(step == 0)
    def _():
        _run(_roundrobin([stage_pool(), late_weights.finish(0, len(late_weights.jobs))]))

    @pl.when((step >= 1) & (step < n_units))
    def _():
        result = []
        _weave(_pieces(stage_mlp(result), MLP_DOT_PIECES + 1), [stage_pool()], pool_light)
        store(result)

    @pl.when(step == n_units)
    def _():
        result = []
        _run(stage_mlp(result))
        store(result)


def _odd_call(x2, mod, params, weights, units, *, unit, short_len, long_len, layer, odd_index):
    n_units = units.total
    halo_per_unit = unit // POOL_HALO
    last_halo = x2.shape[0] // POOL_HALO - 1
    unit_a = lambda i: units.clamp(i)
    unit_c = lambda i: units.clamp(i - 1)
    mod_row = lambda u: layer * COND_ROWS + units.mod_row(u)
    assert (units.n_short * unit) % long_len == 0
    in_specs = [pl.BlockSpec((POOL_HALO, D_MODEL), lambda i: (jnp.maximum(unit_a(i) * halo_per_unit - 1, 0), 0)),
                pl.BlockSpec((unit, D_MODEL), lambda i: (unit_a(i), 0)),
                pl.BlockSpec((POOL_HALO, D_MODEL),
                             lambda i: (jnp.minimum((unit_a(i) + 1) * halo_per_unit, last_halo), 0)),
                pl.BlockSpec((1, 1, N_MOD), lambda i: (mod_row(unit_a(i)), 0, 0)),
                pl.BlockSpec((1, 1, N_MOD), lambda i: (mod_row(unit_c(i)), 0, 0))]
    in_specs += [_resident(p.shape) for p in params] + [_HBM] * len(weights)
    scratch = [pltpu.VMEM((2, unit, D_MODEL), F32),
               pltpu.VMEM((2, unit, D_MODEL), BF16)]
    scratch += [pltpu.VMEM(w.shape[1:], BF16) for w in weights] + _WEIGHT_STAGING
    return pl.pallas_call(
        functools.partial(_odd_kernel, n_units=n_units, n_short=units.n_short,
                          short_len=short_len, long_len=long_len, layer=layer, odd_index=odd_index),
        grid=(n_units + 1,),
        in_specs=in_specs,
        out_specs=[pl.BlockSpec((unit, D_MODEL), lambda i: (units.short_block(unit_c(i)), 0)),
                   pl.BlockSpec((unit, D_MODEL), lambda i: (units.long_block(unit_c(i)), 0))],
        out_shape=[jax.ShapeDtypeStruct((units.n_short * unit, D_MODEL), F32),
                   jax.ShapeDtypeStruct((units.n_long * unit, D_MODEL), F32)],
        scratch_shapes=scratch,
        compiler_params=pltpu.CompilerParams(
            dimension_semantics=("arbitrary",),
            vmem_limit_bytes=V7X_VMEM_LIMIT_BYTES),
        name="odd_layer",
    )(x2, x2, x2, mod, mod, *params, *weights)


def _rope_tables(n_tokens):
    f32 = np.float32
    rows = n_tokens // GRID_W
    pos_r = np.repeat(np.arange(rows, dtype=f32), GRID_W)
    pos_c = np.tile(np.arange(GRID_W, dtype=f32), rows)
    inv = (f32(ROPE_BASE) ** (-np.arange(ROPE_PAIRS_AXIS, dtype=f32) / f32(ROPE_PAIRS_AXIS))).astype(f32)
    ang = np.concatenate([pos_r[:, None] * inv, pos_c[:, None] * inv], axis=-1).astype(f32)
    cos, sin = np.cos(ang).astype(f32), np.sin(ang).astype(f32)
    return np.concatenate([cos, cos], axis=-1), np.concatenate([-sin, sin], axis=-1)


def kernel(x_prompt, x_sample, state_ret, c, c_ctx, ada_w, ada_b, norm_mix, norm_mlp, mlp_w1, mlp_w2,
           ev_w_in, ev_w_s, ev_b_s, ev_vnorm, ev_decay, ev_w_out, od_w_pool, od_pool_scale, final_norm):
    n_prompt, prompt_len, _ = x_prompt.shape
    n_sample, sample_len, _ = x_sample.shape
    assert DEPTH == 2 and n_sample + 1 <= COND_ROWS
    assert prompt_len == UNIT and sample_len % UNIT == 0
    units = _Units(n_prompt, n_sample, sample_len // UNIT)

    mod = _ada_call(c_ctx.reshape(1, D_MODEL), c, ada_w, ada_b)

    row = lambda v: v.reshape(1, -1)
    dec = jnp.broadcast_to(ev_decay[0].reshape(2 * HEADS, 1), (2 * HEADS, CHUNK))
    even_params = (
        row(norm_mix[0]), row(norm_mlp[0]),
        jnp.broadcast_to(ev_b_s[0][:, :, None], (A_GROUPS, CHUNK, A_GROUP_DIM)),
        row(ev_vnorm[0]), dec,
    )
    odd_params = (row(norm_mix[1]), row(norm_mlp[1]), row(final_norm), row(od_pool_scale[0]))
    even_weights = (ev_w_in, ev_w_out, mlp_w1, mlp_w2, ev_w_s)
    odd_weights = (od_w_pool, mlp_w1, mlp_w2)

    xp2 = x_prompt.reshape(n_prompt * prompt_len, D_MODEL)
    xs2 = x_sample.reshape(n_sample * sample_len, D_MODEL)
    cos2, sin2 = _rope_tables(sample_len)
    sp = _ret_state_call(xs2, mod, state_ret, (cos2, sin2), row(norm_mix[0]), ev_w_in, dec,
                         n_seq=n_sample, seq_len=sample_len, layer=0, even_index=0)
    rope = (np.concatenate([np.ones((UNIT, HEAD_DIM), np.float32), cos2]),
            np.concatenate([np.zeros((UNIT, HEAD_DIM), np.float32), sin2]))
    x_all, new_state = _even_call(xp2, xs2, mod, sp, rope, even_params, even_weights, units,
                                  layer=0, even_index=0)
    odd_units = _Units(n_prompt * prompt_len // ODD_UNIT, n_sample, sample_len // ODD_UNIT)
    y_prompt, y_sample = _odd_call(x_all, mod, odd_params, odd_weights, odd_units, unit=ODD_UNIT,
                                   short_len=prompt_len, long_len=sample_len, layer=1, odd_index=0)

    return (y_prompt.reshape(n_prompt, prompt_len, D_MODEL),
            y_sample.reshape(n_sample, sample_len, D_MODEL),
            new_state)
```

```python
import functools

import jax
import jax.numpy as jnp
import numpy as np
from jax import lax
from jax.experimental import pallas as pl
from jax.experimental.pallas import tpu as pltpu

D_MODEL = 1024
DEPTH = 2
GRID_W = 64
A_WIDTH = D_MODEL // 2
A_GROUPS = 4
A_GROUP_DIM = A_WIDTH // A_GROUPS
CHUNK = 128
HEADS = 4
HEAD_DIM = (D_MODEL // 2) // HEADS
ROPE_BASE = 10000.0
ROPE_PAIRS_AXIS = HEAD_DIM // 4
POOL_WINDOWS = (2, 4, 8, 16)
POOL_DIM = D_MODEL // len(POOL_WINDOWS)
POOL_HALO = max(POOL_WINDOWS) // 2
D_FF = 4 * D_MODEL
EPS = 1e-6
QK_W = HEADS * HEAD_DIM
V_W = HEADS * HEAD_DIM
IN_WIDTH = 2 * A_WIDTH + 2 * QK_W + 3 * V_W
OUT_WIDTH = A_WIDTH + V_W
N_MOD = 6 * D_MODEL
K_COL = 2 * A_WIDTH + QK_W
V_COL = K_COL + QK_W

UNIT = 2 * CHUNK
UNIT_CHUNKS = UNIT // CHUNK
ODD_UNIT = 2 * UNIT
FF_CHUNK = 1024
ADA_BLOCK_N = 3072
COND_ROWS = 8
STAGE_ROWS = 512
STAGE_COLS = 1024
STAGE_SLOTS = 3
V7X_VMEM_LIMIT_BYTES = 60 * 1024 * 1024

F32 = jnp.float32
BF16 = jnp.bfloat16


def _dot(a, b):
    return jnp.dot(a, b, preferred_element_type=F32)


def _dot_nt(a, b):
    return lax.dot_general(a, b, (((1,), (1,)), ((), ())), preferred_element_type=F32)


def _silu(x):
    return x * jax.nn.sigmoid(x)


def _gelu_tanh(x):
    return 0.5 * x * (1.0 + jnp.tanh(0.7978845608028654 * (x + 0.044715 * (x * x * x))))


def _rms(x):
    return x * lax.rsqrt(jnp.mean(x * x, axis=-1, keepdims=True) + EPS)


def _rms_mod(x, gain, shift, scale):
    return (_rms(x) * gain) * (1.0 + scale) + shift


def _layer_norm(x):
    mu = jnp.mean(x, axis=-1, keepdims=True)
    d = x - mu
    return d * lax.rsqrt(jnp.mean(d * d, axis=-1, keepdims=True) + EPS)


def _split_mod(mod):
    return [mod[:, i * D_MODEL:(i + 1) * D_MODEL] for i in range(6)]


def _rope(x, cos2, sin2):
    return x * cos2 + pltpu.roll(x, HEAD_DIM // 2, axis=1) * sin2


def _pick_block(take_first, a_ref, b_ref):
    rows = lax.broadcasted_iota(jnp.int32, (a_ref.shape[0], 1), 0)
    return jnp.where(rows < jnp.where(take_first, a_ref.shape[0], 0), a_ref[...], b_ref[...])


def _run(gen, n=None):
    done = 0
    while n is None or done < n:
        try:
            next(gen)
        except StopIteration:
            break
        done += 1
    return done


def _weave(heavy, lights, n_light):
    heavy = list(heavy)
    pending = _roundrobin(lights)
    ran = 0
    for k, piece in enumerate(heavy):
        piece()
        want = ((k + 1) * n_light + len(heavy) - 1) // len(heavy)
        ran += _run(pending, want - ran)
    _run(pending)


def _pieces(gen, n):
    return [functools.partial(next, gen, None) for _ in range(n)]


def _roundrobin(gens):
    live = list(gens)
    while live:
        for g in list(live):
            try:
                next(g)
            except StopIteration:
                live.remove(g)
                continue
            yield


MLP_DOT_PIECES = 2 * (D_FF // FF_CHUNK)


def _mlp_pieces(hb, w1_ref, w2_ref):
    n_ff = D_FF // FF_CHUNK
    up = lambda j: _dot(hb, w1_ref[:, j * FF_CHUNK:(j + 1) * FF_CHUNK])
    a = up(0)
    yield
    acc = None
    for j in range(n_ff):
        a_next = None
        if j + 1 < n_ff:
            a_next = up(j + 1)
            yield
        p = _dot(jnp.square(jnp.maximum(a, 0.0)).astype(BF16), w2_ref[j * FF_CHUNK:(j + 1) * FF_CHUNK, :])
        acc = p if acc is None else acc + p
        a = a_next
        yield
    return acc


def _resident(shape):
    return pl.BlockSpec(shape, lambda i: (0,) * len(shape), pipeline_mode=pl.Buffered(1))


def _weight_chunks(w_hbm, w_vmem):
    rows, cols = w_hbm.shape
    jobs = []
    for r0 in range(0, rows, STAGE_ROWS):
        for c0 in range(0, cols, STAGE_COLS):
            window = (pl.ds(r0, min(STAGE_ROWS, rows - r0)), pl.ds(c0, min(STAGE_COLS, cols - c0)))
            jobs.append((w_hbm.at[window], w_vmem.at[window]))
    return jobs


class _WeightRing:
    def __init__(self, jobs, stage, sem):
        self.jobs, self.stage, self.sem, self.n_slots = jobs, stage, sem, stage.shape[0]

    def _copy(self, k):
        r, c = self.jobs[k][0].shape
        slot = k % self.n_slots
        return pltpu.make_async_copy(self.jobs[k][0], self.stage.at[slot, pl.ds(0, r), pl.ds(0, c)],
                                     self.sem.at[slot])

    def prefill(self):
        for k in range(min(self.n_slots - 1, len(self.jobs))):
            self._copy(k).start()

    def finish(self, first, last):
        for k in range(first, last):
            if k + self.n_slots - 1 < len(self.jobs):
                self._copy(k + self.n_slots - 1).start()
            self._copy(k).wait()
            src, dst = self.jobs[k]
            r, c = src.shape
            dst[...] = self.stage[k % self.n_slots, 0:r, 0:c].astype(BF16)
            yield


def _load_weights_bf16(jobs, stage, sem):
    ring = _WeightRing(jobs, stage, sem)
    ring.prefill()
    _run(ring.finish(0, len(jobs)))


class _Units:
    def __init__(self, n_short, n_long, units_per_long):
        self.n_short, self.n_long, self.units_per_long = n_short, n_long * units_per_long, units_per_long
        self.total = self.n_short + self.n_long

    def clamp(self, unit):
        return jnp.clip(unit, 0, self.total - 1)

    def short_block(self, unit):
        return jnp.minimum(unit, self.n_short - 1)

    def long_block(self, unit):
        return jnp.clip(unit - self.n_short, 0, self.n_long - 1)

    def mod_row(self, unit):
        return jnp.where(unit < self.n_short, 0, 1 + (unit - self.n_short) // self.units_per_long)

    def rope_block(self, unit):
        return jnp.where(unit < self.n_short, 0, 1 + (unit - self.n_short) % self.units_per_long)

    def state_block(self, unit):
        return jnp.where(unit < self.n_short, 0, self.units_per_long + unit - self.n_short)


def _ada_kernel(shared_ref, each_ref, w_ref, b_ref, out_ref):
    row_id = lax.broadcasted_iota(jnp.int32, (COND_ROWS, 1), 0)
    cond = jnp.where(row_id == 0, shared_ref[...], 0.0)
    for r in range(each_ref.shape[0]):
        cond = jnp.where(row_id == 1 + r, each_ref[r:r + 1, :], cond)
    s = _silu(cond).astype(BF16)
    rows = _dot(s, w_ref[0].astype(BF16)) + b_ref[pl.ds(pl.program_id(0), 1), :]
    for r in range(COND_ROWS):
        out_ref[r] = rows[r:r + 1, :]


def _ada_call(cond_shared, cond_each, ada_w, ada_b):
    return pl.pallas_call(
        _ada_kernel,
        grid=(DEPTH, N_MOD // ADA_BLOCK_N),
        in_specs=[
            pl.BlockSpec(cond_shared.shape, lambda i, j: (0, 0)),
            pl.BlockSpec(cond_each.shape, lambda i, j: (0, 0)),
            pl.BlockSpec((1, D_MODEL, ADA_BLOCK_N), lambda i, j: (i, 0, j)),
            pl.BlockSpec((DEPTH, ADA_BLOCK_N), lambda i, j: (0, j)),
        ],
        out_specs=pl.BlockSpec((COND_ROWS, 1, ADA_BLOCK_N), lambda i, j: (i, 0, j)),
        out_shape=jax.ShapeDtypeStruct((DEPTH * COND_ROWS, 1, N_MOD), F32),
        compiler_params=pltpu.CompilerParams(
            dimension_semantics=("arbitrary", "arbitrary"),
            vmem_limit_bytes=V7X_VMEM_LIMIT_BYTES),
        name="ada_mod",
    )(cond_shared, cond_each, ada_w, ada_b)


def _decay_tables(dec_ref, hd):
    row = lax.broadcasted_iota(jnp.int32, (CHUNK, CHUNK), 0).astype(F32)
    col = lax.broadcasted_iota(jnp.int32, (CHUNK, CHUNK), 1).astype(F32)
    lg_f = jnp.broadcast_to(-jnp.exp(dec_ref[hd:hd + 1, :]), (CHUNK, CHUNK))
    lg_b = jnp.broadcast_to(-jnp.exp(dec_ref[HEADS + hd:HEADS + hd + 1, :]), (CHUNK, CHUNK))
    d_f = row - col
    d_b = col - row
    return dict(
        decay_f=jnp.where(d_f >= 0, jnp.exp(lg_f * jnp.maximum(d_f, 0.0)), 0.0),
        decay_b=jnp.where(d_b >= 0, jnp.exp(lg_b * jnp.maximum(d_b, 0.0)), 0.0),
        qd_f=jnp.exp(lg_f * (row + 1.0)),
        qd_b=jnp.exp(lg_b * (CHUNK - row)),
        kd_f=jnp.exp(lg_f * (CHUNK - 1.0 - row)),
        kd_b=jnp.exp(lg_b * row),
        cd_f=jnp.exp(lg_f * CHUNK),
        cd_b=jnp.exp(lg_b * CHUNK),
    )


TABLE_NAMES = ("decay_f", "decay_b", "qd_f", "qd_b", "kd_f", "kd_b", "cd_f", "cd_b")


def _ret_state_kernel(x_ref, mod_ref, s0_ref, cos_ref, sin_ref, nmix_ref, wk_ref, wv_ref, dec_ref, sp_ref,
                      *, n_chunk):
    step = pl.program_id(0)

    @pl.when(step == 0)
    def _():
        sp_ref[...] = jnp.zeros_like(sp_ref)

    @pl.when(step > 0)
    def _():
        sh1, sc1 = _split_mod(mod_ref[0])[:2]
        hb = _rms_mod(x_ref[...], nmix_ref[...], sh1, sc1).astype(BF16)
        zk = _dot(hb, wk_ref[...].astype(BF16)) * (HEAD_DIM ** -0.5)
        zv = _dot(hb, wv_ref[...].astype(BF16)).astype(BF16)
        cos2, sin2 = cos_ref[...], sin_ref[...]
        for hd in range(HEADS):
            lo = hd * HEAD_DIM
            tb = _decay_tables(dec_ref, hd)
            kh = _rope(zk[:, lo:lo + HEAD_DIM], cos2, sin2)
            kv_f, kv_b = [], []
            for c in range(n_chunk):
                kc = kh[c * CHUNK:(c + 1) * CHUNK]
                vc = zv[c * CHUNK:(c + 1) * CHUNK, lo:lo + HEAD_DIM]
                kv_f.append(_dot((kc * tb["kd_f"]).T.astype(BF16), vc))
                kv_b.append(_dot((kc * tb["kd_b"]).T.astype(BF16), vc))
            st = s0_ref[0, 0, 0, hd]
            for c in range(n_chunk):
                if c % UNIT_CHUNKS == 0:
                    sp_ref[c // UNIT_CHUNKS, 0, hd] = st
                st = tb["cd_f"] * st + kv_f[c]
            st = s0_ref[0, 0, 1, hd]
            for c in reversed(range(n_chunk)):
                if c % UNIT_CHUNKS == UNIT_CHUNKS - 1:
                    sp_ref[c // UNIT_CHUNKS, 1, hd] = st
                st = tb["cd_b"] * st + kv_b[c]


def _ret_state_call(x2, mod, s0, rope, nmix, w_in, dec, *, n_seq, seq_len, layer, even_index):
    n_chunk = seq_len // CHUNK
    units = seq_len // UNIT
    seq = lambda i: jnp.maximum(i - 1, 0)
    mod_row = lambda i: layer * COND_ROWS + 1 + seq(i)
    return pl.pallas_call(
        functools.partial(_ret_state_kernel, n_chunk=n_chunk),
        grid=(n_seq + 1,),
        in_specs=[
            pl.BlockSpec((seq_len, D_MODEL), lambda i: (seq(i), 0)),
            pl.BlockSpec((1, 1, N_MOD), lambda i: (mod_row(i), 0, 0)),
            pl.BlockSpec((1, 1, 2, HEADS, HEAD_DIM, HEAD_DIM), lambda i: (seq(i), 0, 0, 0, 0, 0)),
            _resident((seq_len, HEAD_DIM)), _resident((seq_len, HEAD_DIM)),
            _resident((1, D_MODEL)),
            pl.BlockSpec((None, D_MODEL, QK_W), lambda i: (even_index, 0, K_COL // QK_W),
                         pipeline_mode=pl.Buffered(1)),
            pl.BlockSpec((None, D_MODEL, V_W), lambda i: (even_index, 0, V_COL // V_W),
                         pipeline_mode=pl.Buffered(1)),
            _resident(dec.shape),
        ],
        out_specs=pl.BlockSpec((units, 2, HEADS, HEAD_DIM, HEAD_DIM), lambda i: (i, 0, 0, 0, 0)),
        out_shape=jax.ShapeDtypeStruct(((n_seq + 1) * units, 2, HEADS, HEAD_DIM, HEAD_DIM), F32),
        compiler_params=pltpu.CompilerParams(
            dimension_semantics=("arbitrary",),
            vmem_limit_bytes=V7X_VMEM_LIMIT_BYTES),
        name="ret_state",
    )(x2, mod, s0, rope[0], rope[1], nmix, w_in, w_in, dec)


def _even_kernel(xa_short, xa_long, xc_short, xc_long, moda_ref, modc_ref, sp_ref, cos_ref, sin_ref,
                 nmix_ref, nmlp_ref, b_s_ref, vgain_ref, dec_ref, w_in_hbm, w_out_hbm, w1_hbm, w2_hbm, w_s_hbm,
                 y_ref, st_ref, ug_s, vg_s, q_s, k_s, v_s, gf_s, gb_s, cat_s, tab_s,
                 w_in_ref, w_out_ref, w1_ref, w2_ref, w_s_ref, stage, sem, *, units, layer, even_index):
    n_units, n_short = units.total, units.n_short
    step = pl.program_id(0)
    slot_a = lax.rem(step, 2)
    slot_b = 1 - slot_a

    early_weights = (_weight_chunks(w_in_hbm.at[even_index], w_in_ref)
                     + sum((_weight_chunks(w_s_hbm.at[even_index, g], w_s_ref.at[g]) for g in range(A_GROUPS)), []))
    late_weights = _WeightRing(_weight_chunks(w_out_hbm.at[even_index], w_out_ref)
                               + _weight_chunks(w1_hbm.at[layer], w1_ref)
                               + _weight_chunks(w2_hbm.at[layer], w2_ref), stage, sem)
    n_late = len(late_weights.jobs)
    n_late_step0 = min(n_late, 7)

    @pl.when(step == 0)
    def _():
        _load_weights_bf16(early_weights, stage, sem)
        late_weights.prefill()
        for hd in range(HEADS):
            tb = _decay_tables(dec_ref, hd)
            for n, name in enumerate(TABLE_NAMES):
                tab_s[hd, n] = tb[name]

    def stage_in():
        sh1, sc1 = _split_mod(moda_ref[0])[:2]
        x = _pick_block(step < n_short, xa_short, xa_long)
        hb = _rms_mod(x, nmix_ref[...], sh1, sc1).astype(BF16)
        z = _dot(hb, w_in_ref[...])
        yield
        ug_s[slot_a] = z[:, 0:A_WIDTH]
        yield
        zv = _gelu_tanh(z[:, A_WIDTH:2 * A_WIDTH])
        vgn = [_layer_norm(zv[:, g * A_GROUP_DIM:(g + 1) * A_GROUP_DIM]) for g in range(A_GROUPS)]
        vg_s[slot_a] = (jnp.concatenate(vgn, axis=-1) * vgain_ref[...]).astype(BF16)
        yield
        cos2, sin2 = cos_ref[...], sin_ref[...]
        heads = lambda zz: jnp.concatenate(
            [_rope(zz[:, h * HEAD_DIM:(h + 1) * HEAD_DIM], cos2, sin2) for h in range(HEADS)], -1)
        q_s[slot_a] = heads(z[:, 2 * A_WIDTH:K_COL]).astype(BF16)
        yield
        k_s[slot_a] = heads(z[:, K_COL:V_COL] * (HEAD_DIM ** -0.5))
        v_s[slot_a] = z[:, V_COL:V_COL + V_W].astype(BF16)
        yield
        gf_s[slot_a] = z[:, V_COL + V_W:V_COL + 2 * V_W]
        yield
        gb_s[slot_a] = z[:, V_COL + 2 * V_W:V_COL + 3 * V_W]

    IN_LIGHT = 6

    def stage_mix(final_states):
        s = slot_b
        rows = [slice(c * CHUNK, (c + 1) * CHUNK) for c in range(UNIT_CHUNKS)]
        head_cols = [slice(hd * HEAD_DIM, (hd + 1) * HEAD_DIM) for hd in range(HEADS)]
        table = lambda hd, name: tab_s[hd, TABLE_NAMES.index(name)]

        sp2 = []
        for g in range(A_GROUPS):
            lo = g * A_GROUP_DIM
            vg2 = jnp.concatenate([vg_s[s, r, lo:lo + A_GROUP_DIM] for r in rows], axis=-1)
            sp2.append(_dot(w_s_ref[g], vg2))
        yield
        scores, kv_f, kv_b = [], [], []
        for hd in range(HEADS):
            kc = [k_s[s, r, head_cols[hd]] for r in rows]
            vc = [v_s[s, r, head_cols[hd]] for r in rows]
            scores.append([_dot_nt(q_s[s, r, head_cols[hd]], kc[c].astype(BF16)) for c, r in enumerate(rows)])
            kv_f.append([_dot((kc[c] * table(hd, "kd_f")).T.astype(BF16), vc[c]) for c in range(UNIT_CHUNKS)])
            kv_b.append([_dot((kc[c] * table(hd, "kd_b")).T.astype(BF16), vc[c]) for c in range(UNIT_CHUNKS)])
            yield
        for g in range(A_GROUPS):
            lo = g * A_GROUP_DIM
            for c, r in enumerate(rows):
                gate = sp2[g][:, c * CHUNK:(c + 1) * CHUNK] + b_s_ref[g]
                cat_s[s, r, lo:lo + A_GROUP_DIM] = (_gelu_tanh(ug_s[s, r, lo:lo + A_GROUP_DIM]) * gate).astype(BF16)
            if g % 2 == 1:
                yield
        y_raw = []
        for hd in range(HEADS):
            enter_f, st = [], sp_ref[0, 0, hd]
            for c in range(UNIT_CHUNKS):
                enter_f.append(st)
                st = table(hd, "cd_f") * st + kv_f[hd][c]
            final_f = st
            enter_b, st = [None] * UNIT_CHUNKS, sp_ref[0, 1, hd]
            for c in reversed(range(UNIT_CHUNKS)):
                enter_b[c] = st
                st = table(hd, "cd_b") * st + kv_b[hd][c]
            final_states.append((final_f, st))
            per_chunk = []
            for c, r in enumerate(rows):
                vc = v_s[s, r, head_cols[hd]]
                qf = q_s[s, r, head_cols[hd]].astype(F32)

                def direction(decay, qd, enter):
                    p = (scores[hd][c] * table(hd, decay)).astype(BF16)
                    lhs = jnp.concatenate([p, (qf * table(hd, qd)).astype(BF16)], axis=-1)
                    return _dot(lhs, jnp.concatenate([vc, enter.astype(BF16)], axis=0))

                per_chunk.append((direction("decay_f", "qd_f", enter_f[c]), direction("decay_b", "qd_b", enter_b[c])))
            y_raw.append(per_chunk)
            yield
        for hd in range(HEADS):
            for c, r in enumerate(rows):
                y_f, y_b = y_raw[hd][c]
                cols = slice(A_WIDTH + hd * HEAD_DIM, A_WIDTH + (hd + 1) * HEAD_DIM)
                cat_s[s, r, cols] = (_silu(gf_s[s, r, head_cols[hd]]) * _layer_norm(y_f)
                                     + _silu(gb_s[s, r, head_cols[hd]]) * _layer_norm(y_b)).astype(BF16)
            yield

    MIX_LIGHT = 1 + HEADS + A_GROUPS // 2 + 2 * HEADS

    def emit_states(final_states):
        @pl.when(step - 1 < n_short)
        def _():
            for hd, (final_f, final_b) in enumerate(final_states):
                st_ref[0, 0, 0, hd] = final_f
                st_ref[0, 0, 1, hd] = final_b

    def stage_mlp():
        _, _, g1, sh2, sc2, g2 = _split_mod(modc_ref[0])
        x = _pick_block(step - 2 < n_short, xc_short, xc_long) + g1 * _dot(cat_s[slot_a], w_out_ref[...])
        hb = _rms_mod(x, nmlp_ref[...], sh2, sc2).astype(BF16)
        yield
        acc = yield from _mlp_pieces(hb, w1_ref, w2_ref)
        y_ref[...] = x + g2 * acc

    @pl.when(step == 0)
    def _():
        a = stage_in()
        _weave(_pieces(a, 1), [a, late_weights.finish(0, n_late_step0)], IN_LIGHT + n_late_step0)

    @pl.when(step == 1)
    def _():
        a, finals = stage_in(), []
        _weave(_pieces(a, 1), [a, stage_mix(finals), late_weights.finish(n_late_step0, n_late)],
               IN_LIGHT + MIX_LIGHT + n_late - n_late_step0)
        emit_states(finals)

    @pl.when((step >= 2) & (step < n_units))
    def _():
        a, c, finals = stage_in(), stage_mlp(), []
        heavy = _pieces(c, 1) + _pieces(a, 1) + _pieces(c, MLP_DOT_PIECES + 1)
        _weave(heavy, [a, stage_mix(finals)], IN_LIGHT + MIX_LIGHT)
        emit_states(finals)

    @pl.when(step == n_units)
    def _():
        finals = []
        _weave(_pieces(stage_mlp(), MLP_DOT_PIECES + 2), [stage_mix(finals)], MIX_LIGHT)
        emit_states(finals)

    @pl.when(step == n_units + 1)
    def _():
        _run(stage_mlp())


_HBM = pl.BlockSpec(memory_space=pl.ANY)
_WEIGHT_STAGING = [pltpu.VMEM((STAGE_SLOTS, STAGE_ROWS, STAGE_COLS), F32), pltpu.SemaphoreType.DMA((STAGE_SLOTS,))]


def _even_call(x_short, x_long, mod, sp, rope, params, weights, units, *, layer, even_index):
    n_units = units.total
    unit_a = lambda i: units.clamp(i)
    unit_b = lambda i: units.clamp(i - 1)
    unit_c = lambda i: units.clamp(i - 2)
    mod_row = lambda unit: layer * COND_ROWS + units.mod_row(unit)
    state_block = (1, 1, 2, HEADS, HEAD_DIM, HEAD_DIM)

    in_specs = [pl.BlockSpec((UNIT, D_MODEL), lambda i: (units.short_block(unit_a(i)), 0)),
                pl.BlockSpec((UNIT, D_MODEL), lambda i: (units.long_block(unit_a(i)), 0)),
                pl.BlockSpec((UNIT, D_MODEL), lambda i: (units.short_block(unit_c(i)), 0)),
                pl.BlockSpec((UNIT, D_MODEL), lambda i: (units.long_block(unit_c(i)), 0)),
                pl.BlockSpec((1, 1, N_MOD), lambda i: (mod_row(unit_a(i)), 0, 0)),
                pl.BlockSpec((1, 1, N_MOD), lambda i: (mod_row(unit_c(i)), 0, 0)),
                pl.BlockSpec((1, 2, HEADS, HEAD_DIM, HEAD_DIM), lambda i: (units.state_block(unit_b(i)), 0, 0, 0, 0)),
                pl.BlockSpec((UNIT, HEAD_DIM), lambda i: (units.rope_block(unit_a(i)), 0)),
                pl.BlockSpec((UNIT, HEAD_DIM), lambda i: (units.rope_block(unit_a(i)), 0))]
    in_specs += [_resident(p.shape) for p in params] + [_HBM] * len(weights)
    scratch = [
        pltpu.VMEM((2, UNIT, A_WIDTH), F32),
        pltpu.VMEM((2, UNIT, A_WIDTH), BF16),
        pltpu.VMEM((2, UNIT, QK_W), BF16),
        pltpu.VMEM((2, UNIT, QK_W), F32),
        pltpu.VMEM((2, UNIT, V_W), BF16),
        pltpu.VMEM((2, UNIT, V_W), F32),
        pltpu.VMEM((2, UNIT, V_W), F32),
        pltpu.VMEM((2, UNIT, OUT_WIDTH), BF16),
        pltpu.VMEM((HEADS, len(TABLE_NAMES), CHUNK, CHUNK), F32),
    ]
    scratch += [pltpu.VMEM(w.shape[1:], BF16) for w in weights] + _WEIGHT_STAGING
    return pl.pallas_call(
        functools.partial(_even_kernel, units=units, layer=layer, even_index=even_index),
        grid=(n_units + 2,),
        in_specs=in_specs,
        out_specs=[pl.BlockSpec((UNIT, D_MODEL), lambda i: (unit_c(i), 0)),
                   pl.BlockSpec(state_block, lambda i: (units.short_block(unit_b(i)), 0, 0, 0, 0, 0))],
        out_shape=[jax.ShapeDtypeStruct((n_units * UNIT, D_MODEL), F32),
                   jax.ShapeDtypeStruct((units.n_short,) + state_block[1:], F32)],
        scratch_shapes=scratch,
        compiler_params=pltpu.CompilerParams(
            dimension_semantics=("arbitrary",),
            vmem_limit_bytes=V7X_VMEM_LIMIT_BYTES),
        name="even_layer",
    )(x_short, x_long, x_short, x_long, mod, mod, sp, rope[0], rope[1], *params, *weights)


def _odd_kernel(xp_ref, x_ref, xn_ref, moda_ref, modc_ref, nmix_ref, nmlp_ref, fin_ref, pscale_ref,
                w_pool_hbm, w1_hbm, w2_hbm, y_short, y_long, x1_s, hb_s, w_pool_ref, w1_ref, w2_ref, stage, sem,
                *, n_units, n_short, short_len, long_len, layer, odd_index):
    step = pl.program_id(0)
    slot_a = lax.rem(step, 2)
    slot_b = 1 - slot_a
    unit = x_ref.shape[0]
    rows_w = unit + 2 * POOL_HALO

    late_weights = _WeightRing(_weight_chunks(w1_hbm.at[layer], w1_ref) + _weight_chunks(w2_hbm.at[layer], w2_ref),
                               stage, sem)

    @pl.when(step == 0)
    def _():
        _load_weights_bf16(sum((_weight_chunks(w_pool_hbm.at[odd_index, g], w_pool_ref.at[g])
                                for g in range(len(POOL_WINDOWS))), []), stage, sem)
        late_weights.prefill()

    def stage_pool():
        sh1, sc1, g1, sh2, sc2, _ = _split_mod(moda_ref[0])
        xw = jnp.concatenate([xp_ref[...], x_ref[...], xn_ref[...]], axis=0)
        h = _rms_mod(xw, nmix_ref[...], sh1, sc1)
        seq_len = jnp.where(step < n_short, short_len, long_len)
        first = jnp.minimum(step, n_units - 1) * unit - POOL_HALO
        t = (lax.broadcasted_iota(jnp.int32, (rows_w, 1), 0) + first) & (seq_len - 1)
        yield

        def shift_down(a, k):
            return jnp.where(t >= k, pltpu.roll(a, k, axis=0), 0.0)

        def shift_up(a, k):
            return jnp.where(t < seq_len - k, pltpu.roll(a, rows_w - k, axis=0), 0.0)

        outs = []
        for g, w in enumerate(POOL_WINDOWS):
            hg = h[:, g * POOL_DIM:(g + 1) * POOL_DIM]
            half = w // 2
            left = shift_down(hg, 1)
            right = hg
            span = 1
            while span < half:
                left = left + shift_down(left, span)
                right = right + shift_up(right, span)
                span *= 2
            cnt = (jnp.minimum(t + half, seq_len) - jnp.maximum(t - half, 0)).astype(F32)
            pooled = ((left + right) / cnt - hg)[POOL_HALO:POOL_HALO + unit].astype(BF16)
            outs.append(_dot(pooled, w_pool_ref[g]))
            yield
        x1 = x_ref[...] + g1 * (jnp.concatenate(outs, axis=-1) * pscale_ref[...])
        x1_s[slot_a] = x1
        hb_s[slot_a] = _rms_mod(x1, nmlp_ref[...], sh2, sc2).astype(BF16)

    def stage_mlp(result):
        g2 = _split_mod(modc_ref[0])[5]
        acc = yield from _mlp_pieces(hb_s[slot_b], w1_ref, w2_ref)
        result.append(_rms(x1_s[slot_b] + g2 * acc) * fin_ref[...])

    def store(result):
        @pl.when(step - 1 < n_short)
        def _():
            y_short[...] = result[0]

        @pl.when(step - 1 >= n_short)
        def _():
            y_long[...] = result[0]

    pool_light = 2 + len(POOL_WINDOWS)

    @pl.when(step == 0)
    def _():
        _run(_roundrobin([stage_pool(), late_weights.finish(0, len(late_weights.jobs))]))

    @pl.when(step >= 1)
    def _():
        result = []
        _weave(_pieces(stage_mlp(result), MLP_DOT_PIECES + 1), [stage_pool()], pool_light)
        store(result)


def _odd_call(x2, mod, params, weights, units, *, unit, short_len, long_len, layer, odd_index):
    n_units = units.total
    halo_per_unit = unit // POOL_HALO
    last_halo = x2.shape[0] // POOL_HALO - 1
    unit_a = lambda i: units.clamp(i)
    unit_c = lambda i: units.clamp(i - 1)
    mod_row = lambda u: layer * COND_ROWS + units.mod_row(u)
    assert (units.n_short * unit) % long_len == 0
    in_specs = [pl.BlockSpec((POOL_HALO, D_MODEL), lambda i: (jnp.maximum(unit_a(i) * halo_per_unit - 1, 0), 0)),
                pl.BlockSpec((unit, D_MODEL), lambda i: (unit_a(i), 0)),
                pl.BlockSpec((POOL_HALO, D_MODEL),
                             lambda i: (jnp.minimum((unit_a(i) + 1) * halo_per_unit, last_halo), 0)),
                pl.BlockSpec((1, 1, N_MOD), lambda i: (mod_row(unit_a(i)), 0, 0)),
                pl.BlockSpec((1, 1, N_MOD), lambda i: (mod_row(unit_c(i)), 0, 0))]
    in_specs += [_resident(p.shape) for p in params] + [_HBM] * len(weights)
    scratch = [pltpu.VMEM((2, unit, D_MODEL), F32),
               pltpu.VMEM((2, unit, D_MODEL), BF16)]
    scratch += [pltpu.VMEM(w.shape[1:], BF16) for w in weights] + _WEIGHT_STAGING
    return pl.pallas_call(
        functools.partial(_odd_kernel, n_units=n_units, n_short=units.n_short,
                          short_len=short_len, long_len=long_len, layer=layer, odd_index=odd_index),
        grid=(n_units + 1,),
        in_specs=in_specs,
        out_specs=[pl.BlockSpec((unit, D_MODEL), lambda i: (units.short_block(unit_c(i)), 0)),
                   pl.BlockSpec((unit, D_MODEL), lambda i: (units.long_block(unit_c(i)), 0))],
        out_shape=[jax.ShapeDtypeStruct((units.n_short * unit, D_MODEL), F32),
                   jax.ShapeDtypeStruct((units.n_long * unit, D_MODEL), F32)],
        scratch_shapes=scratch,
        compiler_params=pltpu.CompilerParams(
            dimension_semantics=("arbitrary",),
            vmem_limit_bytes=V7X_VMEM_LIMIT_BYTES),
        name="odd_layer",
    )(x2, x2, x2, mod, mod, *params, *weights)


def _rope_tables(n_tokens):
    f32 = np.float32
    rows = n_tokens // GRID_W
    pos_r = np.repeat(np.arange(rows, dtype=f32), GRID_W)
    pos_c = np.tile(np.arange(GRID_W, dtype=f32), rows)
    inv = (f32(ROPE_BASE) ** (-np.arange(ROPE_PAIRS_AXIS, dtype=f32) / f32(ROPE_PAIRS_AXIS))).astype(f32)
    ang = np.concatenate([pos_r[:, None] * inv, pos_c[:, None] * inv], axis=-1).astype(f32)
    cos, sin = np.cos(ang).astype(f32), np.sin(ang).astype(f32)
    return np.concatenate([cos, cos], axis=-1), np.concatenate([-sin, sin], axis=-1)


def kernel(x_prompt, x_sample, state_ret, c, c_ctx, ada_w, ada_b, norm_mix, norm_mlp, mlp_w1, mlp_w2,
           ev_w_in, ev_w_s, ev_b_s, ev_vnorm, ev_decay, ev_w_out, od_w_pool, od_pool_scale, final_norm):
    n_prompt, prompt_len, _ = x_prompt.shape
    n_sample, sample_len, _ = x_sample.shape
    assert DEPTH == 2 and n_sample + 1 <= COND_ROWS
    assert prompt_len == UNIT and sample_len % UNIT == 0
    units = _Units(n_prompt, n_sample, sample_len // UNIT)

    mod = _ada_call(c_ctx.reshape(1, D_MODEL), c, ada_w, ada_b)

    row = lambda v: v.reshape(1, -1)
    dec = jnp.broadcast_to(ev_decay[0].reshape(2 * HEADS, 1), (2 * HEADS, CHUNK))
    even_params = (
        row(norm_mix[0]), row(norm_mlp[0]),
        jnp.broadcast_to(ev_b_s[0][:, :, None], (A_GROUPS, CHUNK, A_GROUP_DIM)),
        row(ev_vnorm[0]), dec,
    )
    odd_params = (row(norm_mix[1]), row(norm_mlp[1]), row(final_norm), row(od_pool_scale[0]))
    even_weights = (ev_w_in, ev_w_out, mlp_w1, mlp_w2, ev_w_s)
    odd_weights = (od_w_pool, mlp_w1, mlp_w2)

    xp2 = x_prompt.reshape(n_prompt * prompt_len, D_MODEL)
    xs2 = x_sample.reshape(n_sample * sample_len, D_MODEL)
    cos2, sin2 = _rope_tables(sample_len)
    sp = _ret_state_call(xs2, mod, state_ret, (cos2, sin2), row(norm_mix[0]), ev_w_in, dec,
                         n_seq=n_sample, seq_len=sample_len, layer=0, even_index=0)
    rope = (np.concatenate([np.ones((UNIT, HEAD_DIM), np.float32), cos2]),
            np.concatenate([np.zeros((UNIT, HEAD_DIM), np.float32), sin2]))
    x_all, new_state = _even_call(xp2, xs2, mod, sp, rope, even_params, even_weights, units,
                                  layer=0, even_index=0)
    odd_units = _Units(n_prompt * prompt_len // ODD_UNIT, n_sample, sample_len // ODD_UNIT)
    y_prompt, y_sample = _odd_call(x_all, mod, odd_params, odd_weights, odd_units, unit=ODD_UNIT,
                                   short_len=prompt_len, long_len=sample_len, layer=1, odd_index=0)

    return (y_prompt.reshape(n_prompt, prompt_len, D_MODEL),
            y_sample.reshape(n_sample, sample_len, D_MODEL),
            new_state)
```

```python
import functools

import jax
import jax.numpy as jnp
import numpy as np
from jax import lax
from jax.experimental import pallas as pl
from jax.experimental.pallas import tpu as pltpu

D_MODEL = 1024
DEPTH = 2
GRID_W = 64
A_WIDTH = D_MODEL // 2
A_GROUPS = 4
A_GROUP_DIM = A_WIDTH // A_GROUPS
CHUNK = 128
HEADS = 4
HEAD_DIM = (D_MODEL // 2) // HEADS
ROPE_BASE = 10000.0
ROPE_PAIRS_AXIS = HEAD_DIM // 4
POOL_WINDOWS = (2, 4, 8, 16)
POOL_DIM = D_MODEL // len(POOL_WINDOWS)
POOL_HALO = max(POOL_WINDOWS) // 2
D_FF = 4 * D_MODEL
EPS = 1e-6
QK_W = HEADS * HEAD_DIM
V_W = HEADS * HEAD_DIM
IN_WIDTH = 2 * A_WIDTH + 2 * QK_W + 3 * V_W
OUT_WIDTH = A_WIDTH + V_W
N_MOD = 6 * D_MODEL
K_COL = 2 * A_WIDTH + QK_W
V_COL = K_COL + QK_W

UNIT = 2 * CHUNK
UNIT_CHUNKS = UNIT // CHUNK
ODD_UNIT = 2 * UNIT
FF_CHUNK = 1024
ADA_BLOCK_N = 3072
COND_ROWS = 8
STAGE_ROWS = 512
STAGE_COLS = 1024
STAGE_SLOTS = 3
V7X_VMEM_LIMIT_BYTES = 60 * 1024 * 1024

F32 = jnp.float32
BF16 = jnp.bfloat16


def _dot(a, b):
    return jnp.dot(a, b, preferred_element_type=F32)


def _dot_nt(a, b):
    return lax.dot_general(a, b, (((1,), (1,)), ((), ())), preferred_element_type=F32)


def _silu(x):
    return x * jax.nn.sigmoid(x)


def _gelu_tanh(x):
    return 0.5 * x * (1.0 + jnp.tanh(0.7978845608028654 * (x + 0.044715 * (x * x * x))))


def _rms(x):
    return x * lax.rsqrt(jnp.mean(x * x, axis=-1, keepdims=True) + EPS)


def _rms_mod(x, gain, shift, scale):
    return (_rms(x) * gain) * (1.0 + scale) + shift


def _layer_norm(x):
    mu = jnp.mean(x, axis=-1, keepdims=True)
    d = x - mu
    return d * lax.rsqrt(jnp.mean(d * d, axis=-1, keepdims=True) + EPS)


def _split_mod(mod):
    return [mod[:, i * D_MODEL:(i + 1) * D_MODEL] for i in range(6)]


def _rope(x, cos2, sin2):
    return x * cos2 + pltpu.roll(x, HEAD_DIM // 2, axis=1) * sin2


def _pick_block(take_first, a_ref, b_ref):
    rows = lax.broadcasted_iota(jnp.int32, (a_ref.shape[0], 1), 0)
    return jnp.where(rows < jnp.where(take_first, a_ref.shape[0], 0), a_ref[...], b_ref[...])


def _run(gen, n=None):
    done = 0
    while n is None or done < n:
        try:
            next(gen)
        except StopIteration:
            break
        done += 1
    return done


def _weave(heavy, lights, n_light):
    heavy = list(heavy)
    pending = _roundrobin(lights)
    ran = 0
    for k, piece in enumerate(heavy):
        piece()
        want = ((k + 1) * n_light + len(heavy) - 1) // len(heavy)
        ran += _run(pending, want - ran)
    _run(pending)


def _pieces(gen, n):
    return [functools.partial(next, gen, None) for _ in range(n)]


def _roundrobin(gens):
    live = list(gens)
    while live:
        for g in list(live):
            try:
                next(g)
            except StopIteration:
                live.remove(g)
                continue
            yield


MLP_DOT_PIECES = 2 * (D_FF // FF_CHUNK)


def _mlp_pieces(hb, w1_ref, w2_ref):
    n_ff = D_FF // FF_CHUNK
    up = lambda j: _dot(hb, w1_ref[:, j * FF_CHUNK:(j + 1) * FF_CHUNK])
    a = up(0)
    yield
    acc = None
    for j in range(n_ff):
        a_next = None
        if j + 1 < n_ff:
            a_next = up(j + 1)
            yield
        p = _dot(jnp.square(jnp.maximum(a, 0.0)).astype(BF16), w2_ref[j * FF_CHUNK:(j + 1) * FF_CHUNK, :])
        acc = p if acc is None else acc + p
        a = a_next
        yield
    return acc


def _resident(shape):
    return pl.BlockSpec(shape, lambda i: (0,) * len(shape), pipeline_mode=pl.Buffered(1))


def _weight_chunks(w_hbm, w_vmem):
    rows, cols = w_hbm.shape
    jobs = []
    for r0 in range(0, rows, STAGE_ROWS):
        for c0 in range(0, cols, STAGE_COLS):
            window = (pl.ds(r0, min(STAGE_ROWS, rows - r0)), pl.ds(c0, min(STAGE_COLS, cols - c0)))
            jobs.append((w_hbm.at[window], w_vmem.at[window]))
    return jobs


class _WeightRing:
    def __init__(self, jobs, stage, sem):
        self.jobs, self.stage, self.sem, self.n_slots = jobs, stage, sem, stage.shape[0]

    def _copy(self, k):
        r, c = self.jobs[k][0].shape
        slot = k % self.n_slots
        return pltpu.make_async_copy(self.jobs[k][0], self.stage.at[slot, pl.ds(0, r), pl.ds(0, c)],
                                     self.sem.at[slot])

    def prefill(self):
        for k in range(min(self.n_slots - 1, len(self.jobs))):
            self._copy(k).start()

    def finish(self, first, last):
        for k in range(first, last):
            if k + self.n_slots - 1 < len(self.jobs):
                self._copy(k + self.n_slots - 1).start()
            self._copy(k).wait()
            src, dst = self.jobs[k]
            r, c = src.shape
            dst[...] = self.stage[k % self.n_slots, 0:r, 0:c].astype(BF16)
            yield


def _load_weights_bf16(jobs, stage, sem):
    ring = _WeightRing(jobs, stage, sem)
    ring.prefill()
    _run(ring.finish(0, len(jobs)))


class _Units:
    def __init__(self, n_short, n_long, units_per_long):
        self.n_short, self.n_long, self.units_per_long = n_short, n_long * units_per_long, units_per_long
        self.total = self.n_short + self.n_long

    def clamp(self, unit):
        return jnp.clip(unit, 0, self.total - 1)

    def short_block(self, unit):
        return jnp.minimum(unit, self.n_short - 1)

    def long_block(self, unit):
        return jnp.clip(unit - self.n_short, 0, self.n_long - 1)

    def mod_row(self, unit):
        return jnp.where(unit < self.n_short, 0, 1 + (unit - self.n_short) // self.units_per_long)

    def rope_block(self, unit):
        return jnp.where(unit < self.n_short, 0, 1 + (unit - self.n_short) % self.units_per_long)

    def state_block(self, unit):
        return jnp.where(unit < self.n_short, 0, self.units_per_long + unit - self.n_short)


def _ada_kernel(shared_ref, each_ref, w_ref, b_ref, out_ref):
    row_id = lax.broadcasted_iota(jnp.int32, (COND_ROWS, 1), 0)
    cond = jnp.where(row_id == 0, shared_ref[...], 0.0)
    for r in range(each_ref.shape[0]):
        cond = jnp.where(row_id == 1 + r, each_ref[r:r + 1, :], cond)
    s = _silu(cond).astype(BF16)
    rows = _dot(s, w_ref[0].astype(BF16)) + b_ref[pl.ds(pl.program_id(0), 1), :]
    for r in range(COND_ROWS):
        out_ref[r] = rows[r:r + 1, :]


def _ada_call(cond_shared, cond_each, ada_w, ada_b):
    return pl.pallas_call(
        _ada_kernel,
        grid=(DEPTH, N_MOD // ADA_BLOCK_N),
        in_specs=[
            pl.BlockSpec(cond_shared.shape, lambda i, j: (0, 0)),
            pl.BlockSpec(cond_each.shape, lambda i, j: (0, 0)),
            pl.BlockSpec((1, D_MODEL, ADA_BLOCK_N), lambda i, j: (i, 0, j)),
            pl.BlockSpec((DEPTH, ADA_BLOCK_N), lambda i, j: (0, j)),
        ],
        out_specs=pl.BlockSpec((COND_ROWS, 1, ADA_BLOCK_N), lambda i, j: (i, 0, j)),
        out_shape=jax.ShapeDtypeStruct((DEPTH * COND_ROWS, 1, N_MOD), F32),
        compiler_params=pltpu.CompilerParams(
            dimension_semantics=("arbitrary", "arbitrary"),
            vmem_limit_bytes=V7X_VMEM_LIMIT_BYTES),
        name="ada_mod",
    )(cond_shared, cond_each, ada_w, ada_b)


def _decay_tables(dec_ref, hd):
    row = lax.broadcasted_iota(jnp.int32, (CHUNK, CHUNK), 0).astype(F32)
    col = lax.broadcasted_iota(jnp.int32, (CHUNK, CHUNK), 1).astype(F32)
    lg_f = jnp.broadcast_to(-jnp.exp(dec_ref[hd:hd + 1, :]), (CHUNK, CHUNK))
    lg_b = jnp.broadcast_to(-jnp.exp(dec_ref[HEADS + hd:HEADS + hd + 1, :]), (CHUNK, CHUNK))
    d_f = row - col
    d_b = col - row
    return dict(
        decay_f=jnp.where(d_f >= 0, jnp.exp(lg_f * jnp.maximum(d_f, 0.0)), 0.0),
        decay_b=jnp.where(d_b >= 0, jnp.exp(lg_b * jnp.maximum(d_b, 0.0)), 0.0),
        qd_f=jnp.exp(lg_f * (row + 1.0)),
        qd_b=jnp.exp(lg_b * (CHUNK - row)),
        kd_f=jnp.exp(lg_f * (CHUNK - 1.0 - row)),
        kd_b=jnp.exp(lg_b * row),
        cd_f=jnp.exp(lg_f * CHUNK),
        cd_b=jnp.exp(lg_b * CHUNK),
    )


TABLE_NAMES = ("decay_f", "decay_b", "qd_f", "qd_b", "kd_f", "kd_b", "cd_f", "cd_b")


def _ret_state_kernel(x_ref, mod_ref, s0_ref, cos_ref, sin_ref, nmix_ref, wk_ref, wv_ref, dec_ref, sp_ref,
                      *, n_chunk):
    step = pl.program_id(0)

    @pl.when(step == 0)
    def _():
        sp_ref[...] = jnp.zeros_like(sp_ref)

    @pl.when(step > 0)
    def _():
        sh1, sc1 = _split_mod(mod_ref[0])[:2]
        hb = _rms_mod(x_ref[...], nmix_ref[...], sh1, sc1).astype(BF16)
        zk = _dot(hb, wk_ref[...].astype(BF16)) * (HEAD_DIM ** -0.5)
        zv = _dot(hb, wv_ref[...].astype(BF16)).astype(BF16)
        cos2, sin2 = cos_ref[...], sin_ref[...]
        for hd in range(HEADS):
            lo = hd * HEAD_DIM
            tb = _decay_tables(dec_ref, hd)
            kh = _rope(zk[:, lo:lo + HEAD_DIM], cos2, sin2)
            kv_f, kv_b = [], []
            for c in range(n_chunk):
                kc = kh[c * CHUNK:(c + 1) * CHUNK]
                vc = zv[c * CHUNK:(c + 1) * CHUNK, lo:lo + HEAD_DIM]
                kv_f.append(_dot((kc * tb["kd_f"]).T.astype(BF16), vc))
                kv_b.append(_dot((kc * tb["kd_b"]).T.astype(BF16), vc))
            st = s0_ref[0, 0, 0, hd]
            for c in range(n_chunk):
                if c % UNIT_CHUNKS == 0:
                    sp_ref[c // UNIT_CHUNKS, 0, hd] = st
                st = tb["cd_f"] * st + kv_f[c]
            st = s0_ref[0, 0, 1, hd]
            for c in reversed(range(n_chunk)):
                if c % UNIT_CHUNKS == UNIT_CHUNKS - 1:
                    sp_ref[c // UNIT_CHUNKS, 1, hd] = st
                st = tb["cd_b"] * st + kv_b[c]


def _ret_state_call(x2, mod, s0, rope, nmix, w_in, dec, *, n_seq, seq_len, layer, even_index):
    n_chunk = seq_len // CHUNK
    units = seq_len // UNIT
    seq = lambda i: jnp.maximum(i - 1, 0)
    mod_row = lambda i: layer * COND_ROWS + 1 + seq(i)
    return pl.pallas_call(
        functools.partial(_ret_state_kernel, n_chunk=n_chunk),
        grid=(n_seq + 1,),
        in_specs=[
            pl.BlockSpec((seq_len, D_MODEL), lambda i: (seq(i), 0)),
            pl.BlockSpec((1, 1, N_MOD), lambda i: (mod_row(i), 0, 0)),
            pl.BlockSpec((1, 1, 2, HEADS, HEAD_DIM, HEAD_DIM), lambda i: (seq(i), 0, 0, 0, 0, 0)),
            _resident((seq_len, HEAD_DIM)), _resident((seq_len, HEAD_DIM)),
            _resident((1, D_MODEL)),
            pl.BlockSpec((None, D_MODEL, QK_W), lambda i: (even_index, 0, K_COL // QK_W),
                         pipeline_mode=pl.Buffered(1)),
            pl.BlockSpec((None, D_MODEL, V_W), lambda i: (even_index, 0, V_COL // V_W),
                         pipeline_mode=pl.Buffered(1)),
            _resident(dec.shape),
        ],
        out_specs=pl.BlockSpec((units, 2, HEADS, HEAD_DIM, HEAD_DIM), lambda i: (i, 0, 0, 0, 0)),
        out_shape=jax.ShapeDtypeStruct(((n_seq + 1) * units, 2, HEADS, HEAD_DIM, HEAD_DIM), F32),
        compiler_params=pltpu.CompilerParams(
            dimension_semantics=("arbitrary",),
            vmem_limit_bytes=V7X_VMEM_LIMIT_BYTES),
        name="ret_state",
    )(x2, mod, s0, rope[0], rope[1], nmix, w_in, w_in, dec)


def _even_kernel(xa_short, xa_long, xc_short, xc_long, moda_ref, modc_ref, sp_ref, cos_ref, sin_ref,
                 nmix_ref, nmlp_ref, b_s_ref, vgain_ref, dec_ref, w_in_hbm, w_out_hbm, w1_hbm, w2_hbm, w_s_hbm,
                 y_ref, st_ref, ug_s, vg_s, q_s, k_s, v_s, gf_s, gb_s, cat_s, tab_s,
                 w_in_ref, w_out_ref, w1_ref, w2_ref, w_s_ref, stage, sem, *, units, layer, even_index):
    n_units, n_short = units.total, units.n_short
    step = pl.program_id(0)
    slot_a = lax.rem(step, 2)
    slot_b = 1 - slot_a

    early_weights = (_weight_chunks(w_in_hbm.at[even_index], w_in_ref)
                     + sum((_weight_chunks(w_s_hbm.at[even_index, g], w_s_ref.at[g]) for g in range(A_GROUPS)), []))
    late_weights = _WeightRing(_weight_chunks(w_out_hbm.at[even_index], w_out_ref)
                               + _weight_chunks(w1_hbm.at[layer], w1_ref)
                               + _weight_chunks(w2_hbm.at[layer], w2_ref), stage, sem)
    n_late = len(late_weights.jobs)
    n_late_step0 = min(n_late, 7)

    @pl.when(step == 0)
    def _():
        _load_weights_bf16(early_weights, stage, sem)
        late_weights.prefill()
        for hd in range(HEADS):
            tb = _decay_tables(dec_ref, hd)
            for n, name in enumerate(TABLE_NAMES):
                tab_s[hd, n] = tb[name]

    def stage_in():
        sh1, sc1 = _split_mod(moda_ref[0])[:2]
        x = _pick_block(step < n_short, xa_short, xa_long)
        hb = _rms_mod(x, nmix_ref[...], sh1, sc1).astype(BF16)
        z = _dot(hb, w_in_ref[...])
        yield
        ug_s[slot_a] = z[:, 0:A_WIDTH]
        yield
        zv = _gelu_tanh(z[:, A_WIDTH:2 * A_WIDTH])
        vgn = [_layer_norm(zv[:, g * A_GROUP_DIM:(g + 1) * A_GROUP_DIM]) for g in range(A_GROUPS)]
        vg_s[slot_a] = (jnp.concatenate(vgn, axis=-1) * vgain_ref[...]).astype(BF16)
        yield
        cos2, sin2 = cos_ref[...], sin_ref[...]
        heads = lambda zz: jnp.concatenate(
            [_rope(zz[:, h * HEAD_DIM:(h + 1) * HEAD_DIM], cos2, sin2) for h in range(HEADS)], -1)
        q_s[slot_a] = heads(z[:, 2 * A_WIDTH:K_COL]).astype(BF16)
        yield
        k_s[slot_a] = heads(z[:, K_COL:V_COL] * (HEAD_DIM ** -0.5))
        v_s[slot_a] = z[:, V_COL:V_COL + V_W].astype(BF16)
        yield
        gf_s[slot_a] = z[:, V_COL + V_W:V_COL + 2 * V_W]
        yield
        gb_s[slot_a] = z[:, V_COL + 2 * V_W:V_COL + 3 * V_W]

    IN_LIGHT = 6

    def stage_mix(final_states):
        s = slot_b
        rows = [slice(c * CHUNK, (c + 1) * CHUNK) for c in range(UNIT_CHUNKS)]
        head_cols = [slice(hd * HEAD_DIM, (hd + 1) * HEAD_DIM) for hd in range(HEADS)]
        table = lambda hd, name: tab_s[hd, TABLE_NAMES.index(name)]

        sp2 = []
        for g in range(A_GROUPS):
            lo = g * A_GROUP_DIM
            vg2 = jnp.concatenate([vg_s[s, r, lo:lo + A_GROUP_DIM] for r in rows], axis=-1)
            sp2.append(_dot(w_s_ref[g], vg2))
        yield
        scores, kv_f, kv_b = [], [], []
        for hd in range(HEADS):
            kc = [k_s[s, r, head_cols[hd]] for r in rows]
            vc = [v_s[s, r, head_cols[hd]] for r in rows]
            scores.append([_dot_nt(q_s[s, r, head_cols[hd]], kc[c].astype(BF16)) for c, r in enumerate(rows)])
            kv_f.append([_dot((kc[c] * table(hd, "kd_f")).T.astype(BF16), vc[c]) for c in range(UNIT_CHUNKS)])
            kv_b.append([_dot((kc[c] * table(hd, "kd_b")).T.astype(BF16), vc[c]) for c in range(UNIT_CHUNKS)])
            yield
        for g in range(A_GROUPS):
            lo = g * A_GROUP_DIM
            for c, r in enumerate(rows):
                gate = sp2[g][:, c * CHUNK:(c + 1) * CHUNK] + b_s_ref[g]
                cat_s[s, r, lo:lo + A_GROUP_DIM] = (_gelu_tanh(ug_s[s, r, lo:lo + A_GROUP_DIM]) * gate).astype(BF16)
            if g % 2 == 1:
                yield
        y_raw = []
        for hd in range(HEADS):
            enter_f, st = [], sp_ref[0, 0, hd]
            for c in range(UNIT_CHUNKS):
                enter_f.append(st)
                st = table(hd, "cd_f") * st + kv_f[hd][c]
            final_f = st
            enter_b, st = [None] * UNIT_CHUNKS, sp_ref[0, 1, hd]
            for c in reversed(range(UNIT_CHUNKS)):
                enter_b[c] = st
                st = table(hd, "cd_b") * st + kv_b[hd][c]
            final_states.append((final_f, st))
            per_chunk = []
            for c, r in enumerate(rows):
                vc = v_s[s, r, head_cols[hd]]
                qf = q_s[s, r, head_cols[hd]].astype(F32)

                def direction(decay, qd, enter):
                    p = (scores[hd][c] * table(hd, decay)).astype(BF16)
                    lhs = jnp.concatenate([p, (qf * table(hd, qd)).astype(BF16)], axis=-1)
                    return _dot(lhs, jnp.concatenate([vc, enter.astype(BF16)], axis=0))

                per_chunk.append((direction("decay_f", "qd_f", enter_f[c]), direction("decay_b", "qd_b", enter_b[c])))
            y_raw.append(per_chunk)
            yield
        for hd in range(HEADS):
            for c, r in enumerate(rows):
                y_f, y_b = y_raw[hd][c]
                cols = slice(A_WIDTH + hd * HEAD_DIM, A_WIDTH + (hd + 1) * HEAD_DIM)
                cat_s[s, r, cols] = (_silu(gf_s[s, r, head_cols[hd]]) * _layer_norm(y_f)
                                     + _silu(gb_s[s, r, head_cols[hd]]) * _layer_norm(y_b)).astype(BF16)
            yield

    MIX_LIGHT = 1 + HEADS + A_GROUPS // 2 + 2 * HEADS

    def emit_states(final_states):
        @pl.when(step - 1 < n_short)
        def _():
            for hd, (final_f, final_b) in enumerate(final_states):
                st_ref[0, 0, 0, hd] = final_f
                st_ref[0, 0, 1, hd] = final_b

    def stage_mlp():
        _, _, g1, sh2, sc2, g2 = _split_mod(modc_ref[0])
        x = _pick_block(step - 2 < n_short, xc_short, xc_long) + g1 * _dot(cat_s[slot_a], w_out_ref[...])
        hb = _rms_mod(x, nmlp_ref[...], sh2, sc2).astype(BF16)
        yield
        acc = yield from _mlp_pieces(hb, w1_ref, w2_ref)
        y_ref[...] = x + g2 * acc

    @pl.when(step == 0)
    def _():
        a = stage_in()
        _weave(_pieces(a, 1), [a, late_weights.finish(0, n_late_step0)], IN_LIGHT + n_late_step0)

    @pl.when(step == 1)
    def _():
        a, finals = stage_in(), []
        _weave(_pieces(a, 1), [a, stage_mix(finals), late_weights.finish(n_late_step0, n_late)],
               IN_LIGHT + MIX_LIGHT + n_late - n_late_step0)
        emit_states(finals)

    @pl.when((step >= 2) & (step < n_units))
    def _():
        a, c, finals = stage_in(), stage_mlp(), []
        heavy = _pieces(c, 1) + _pieces(a, 1) + _pieces(c, MLP_DOT_PIECES + 1)
        _weave(heavy, [a, stage_mix(finals)], IN_LIGHT + MIX_LIGHT)
        emit_states(finals)

    @pl.when(step >= n_units)
    def _():
        finals = []
        _weave(_pieces(stage_mlp(), MLP_DOT_PIECES + 2), [stage_mix(finals)], MIX_LIGHT)
        emit_states(finals)


_HBM = pl.BlockSpec(memory_space=pl.ANY)
_WEIGHT_STAGING = [pltpu.VMEM((STAGE_SLOTS, STAGE_ROWS, STAGE_COLS), F32), pltpu.SemaphoreType.DMA((STAGE_SLOTS,))]


def _even_call(x_short, x_long, mod, sp, rope, params, weights, units, *, layer, even_index):
    n_units = units.total
    unit_a = lambda i: units.clamp(i)
    unit_b = lambda i: units.clamp(i - 1)
    unit_c = lambda i: units.clamp(i - 2)
    mod_row = lambda unit: layer * COND_ROWS + units.mod_row(unit)
    state_block = (1, 1, 2, HEADS, HEAD_DIM, HEAD_DIM)

    in_specs = [pl.BlockSpec((UNIT, D_MODEL), lambda i: (units.short_block(unit_a(i)), 0)),
                pl.BlockSpec((UNIT, D_MODEL), lambda i: (units.long_block(unit_a(i)), 0)),
                pl.BlockSpec((UNIT, D_MODEL), lambda i: (units.short_block(unit_c(i)), 0)),
                pl.BlockSpec((UNIT, D_MODEL), lambda i: (units.long_block(unit_c(i)), 0)),
                pl.BlockSpec((1, 1, N_MOD), lambda i: (mod_row(unit_a(i)), 0, 0)),
                pl.BlockSpec((1, 1, N_MOD), lambda i: (mod_row(unit_c(i)), 0, 0)),
                pl.BlockSpec((1, 2, HEADS, HEAD_DIM, HEAD_DIM), lambda i: (units.state_block(unit_b(i)), 0, 0, 0, 0)),
                pl.BlockSpec((UNIT, HEAD_DIM), lambda i: (units.rope_block(unit_a(i)), 0)),
                pl.BlockSpec((UNIT, HEAD_DIM), lambda i: (units.rope_block(unit_a(i)), 0))]
    in_specs += [_resident(p.shape) for p in params] + [_HBM] * len(weights)
    scratch = [
        pltpu.VMEM((2, UNIT, A_WIDTH), F32),
        pltpu.VMEM((2, UNIT, A_WIDTH), BF16),
        pltpu.VMEM((2, UNIT, QK_W), BF16),
        pltpu.VMEM((2, UNIT, QK_W), F32),
        pltpu.VMEM((2, UNIT, V_W), BF16),
        pltpu.VMEM((2, UNIT, V_W), F32),
        pltpu.VMEM((2, UNIT, V_W), F32),
        pltpu.VMEM((2, UNIT, OUT_WIDTH), BF16),
        pltpu.VMEM((HEADS, len(TABLE_NAMES), CHUNK, CHUNK), F32),
    ]
    scratch += [pltpu.VMEM(w.shape[1:], BF16) for w in weights] + _WEIGHT_STAGING
    return pl.pallas_call(
        functools.partial(_even_kernel, units=units, layer=layer, even_index=even_index),
        grid=(n_units + 2,),
        in_specs=in_specs,
        out_specs=[pl.BlockSpec((UNIT, D_MODEL), lambda i: (unit_c(i), 0)),
                   pl.BlockSpec(state_block, lambda i: (units.short_block(unit_b(i)), 0, 0, 0, 0, 0))],
        out_shape=[jax.ShapeDtypeStruct((n_units * UNIT, D_MODEL), F32),
                   jax.ShapeDtypeStruct((units.n_short,) + state_block[1:], F32)],
        scratch_shapes=scratch,
        compiler_params=pltpu.CompilerParams(
            dimension_semantics=("arbitrary",),
            vmem_limit_bytes=V7X_VMEM_LIMIT_BYTES),
        name="even_layer",
    )(x_short, x_long, x_short, x_long, mod, mod, sp, rope[0], rope[1], *params, *weights)


def _odd_kernel(xp_ref, x_ref, xn_ref, moda_ref, modc_ref, nmix_ref, nmlp_ref, fin_ref, pscale_ref,
                w_pool_hbm, w1_hbm, w2_hbm, y_short, y_long, x1_s, hb_s, w_pool_ref, w1_ref, w2_ref, stage, sem,
                *, n_units, n_short, short_len, long_len, layer, odd_index):
    step = pl.program_id(0)
    slot_a = lax.rem(step, 2)
    slot_b = 1 - slot_a
    unit = x_ref.shape[0]
    rows_w = unit + 2 * POOL_HALO

    late_weights = _WeightRing(_weight_chunks(w1_hbm.at[layer], w1_ref) + _weight_chunks(w2_hbm.at[layer], w2_ref),
                               stage, sem)

    @pl.when(step == 0)
    def _():
        _load_weights_bf16(sum((_weight_chunks(w_pool_hbm.at[odd_index, g], w_pool_ref.at[g])
                                for g in range(len(POOL_WINDOWS))), []), stage, sem)
        late_weights.prefill()

    def stage_pool():
        sh1, sc1, g1, sh2, sc2, _ = _split_mod(moda_ref[0])
        xw = jnp.concatenate([xp_ref[...], x_ref[...], xn_ref[...]], axis=0)
        h = _rms_mod(xw, nmix_ref[...], sh1, sc1)
        seq_len = jnp.where(step < n_short, short_len, long_len)
        first = jnp.minimum(step, n_units - 1) * unit - POOL_HALO
        t = (lax.broadcasted_iota(jnp.int32, (rows_w, 1), 0) + first) & (seq_len - 1)
        yield

        def shift_down(a, k):
            return jnp.where(t >= k, pltpu.roll(a, k, axis=0), 0.0)

        def shift_up(a, k):
            return jnp.where(t < seq_len - k, pltpu.roll(a, rows_w - k, axis=0), 0.0)

        outs = []
        for g, w in enumerate(POOL_WINDOWS):
            hg = h[:, g * POOL_DIM:(g + 1) * POOL_DIM]
            half = w // 2
            left = shift_down(hg, 1)
            right = hg
            span = 1
            while span < half:
                left = left + shift_down(left, span)
                right = right + shift_up(right, span)
                span *= 2
            cnt = (jnp.minimum(t + half, seq_len) - jnp.maximum(t - half, 0)).astype(F32)
            pooled = ((left + right) / cnt - hg)[POOL_HALO:POOL_HALO + unit].astype(BF16)
            outs.append(_dot(pooled, w_pool_ref[g]))
            yield
        x1 = x_ref[...] + g1 * (jnp.concatenate(outs, axis=-1) * pscale_ref[...])
        x1_s[slot_a] = x1
        hb_s[slot_a] = _rms_mod(x1, nmlp_ref[...], sh2, sc2).astype(BF16)

    def stage_mlp(result):
        g2 = _split_mod(modc_ref[0])[5]
        acc = yield from _mlp_pieces(hb_s[slot_b], w1_ref, w2_ref)
        result.append(_rms(x1_s[slot_b] + g2 * acc) * fin_ref[...])

    def store(result):
        @pl.when(step - 1 < n_short)
        def _():
            y_short[...] = result[0]

        @pl.when(step - 1 >= n_short)
        def _():
            y_long[...] = result[0]

    pool_light = 2 + len(POOL_WINDOWS)

    @pl.when(step == 0)
    def _():
        _run(_roundrobin([stage_pool(), late_weights.finish(0, len(late_weights.jobs))]))

    @pl.when(step >= 1)
    def _():
        result = []
        _weave(_pieces(stage_mlp(result), MLP_DOT_PIECES + 1), [stage_pool()], pool_light)
        store(result)


def _odd_call(x2, mod, params, weights, units, *, unit, short_len, long_len, layer, odd_index):
    n_units = units.total
    halo_per_unit = unit // POOL_HALO
    last_halo = x2.shape[0] // POOL_HALO - 1
    unit_a = lambda i: units.clamp(i)
    unit_c = lambda i: units.clamp(i - 1)
    mod_row = lambda u: layer * COND_ROWS + units.mod_row(u)
    assert (units.n_short * unit) % long_len == 0
    in_specs = [pl.BlockSpec((POOL_HALO, D_MODEL), lambda i: (jnp.maximum(unit_a(i) * halo_per_unit - 1, 0), 0)),
                pl.BlockSpec((unit, D_MODEL), lambda i: (unit_a(i), 0)),
                pl.BlockSpec((POOL_HALO, D_MODEL),
                             lambda i: (jnp.minimum((unit_a(i) + 1) * halo_per_unit, last_halo), 0)),
                pl.BlockSpec((1, 1, N_MOD), lambda i: (mod_row(unit_a(i)), 0, 0)),
                pl.BlockSpec((1, 1, N_MOD), lambda i: (mod_row(unit_c(i)), 0, 0))]
    in_specs += [_resident(p.shape) for p in params] + [_HBM] * len(weights)
    scratch = [pltpu.VMEM((2, unit, D_MODEL), F32),
               pltpu.VMEM((2, unit, D_MODEL), BF16)]
    scratch += [pltpu.VMEM(w.shape[1:], BF16) for w in weights] + _WEIGHT_STAGING
    return pl.pallas_call(
        functools.partial(_odd_kernel, n_units=n_units, n_short=units.n_short,
                          short_len=short_len, long_len=long_len, layer=layer, odd_index=odd_index),
        grid=(n_units + 1,),
        in_specs=in_specs,
        out_specs=[pl.BlockSpec((unit, D_MODEL), lambda i: (units.short_block(unit_c(i)), 0)),
                   pl.BlockSpec((unit, D_MODEL), lambda i: (units.long_block(unit_c(i)), 0))],
        out_shape=[jax.ShapeDtypeStruct((units.n_short * unit, D_MODEL), F32),
                   jax.ShapeDtypeStruct((units.n_long * unit, D_MODEL), F32)],
        scratch_shapes=scratch,
        compiler_params=pltpu.CompilerParams(
            dimension_semantics=("arbitrary",),
            vmem_limit_bytes=V7X_VMEM_LIMIT_BYTES),
        name="odd_layer",
    )(x2, x2, x2, mod, mod, *params, *weights)


def _rope_tables(n_tokens):
    f32 = np.float32
    rows = n_tokens // GRID_W
    pos_r = np.repeat(np.arange(rows, dtype=f32), GRID_W)
    pos_c = np.tile(np.arange(GRID_W, dtype=f32), rows)
    inv = (f32(ROPE_BASE) ** (-np.arange(ROPE_PAIRS_AXIS, dtype=f32) / f32(ROPE_PAIRS_AXIS))).astype(f32)
    ang = np.concatenate([pos_r[:, None] * inv, pos_c[:, None] * inv], axis=-1).astype(f32)
    cos, sin = np.cos(ang).astype(f32), np.sin(ang).astype(f32)
    return np.concatenate([cos, cos], axis=-1), np.concatenate([-sin, sin], axis=-1)


def kernel(x_prompt, x_sample, state_ret, c, c_ctx, ada_w, ada_b, norm_mix, norm_mlp, mlp_w1, mlp_w2,
           ev_w_in, ev_w_s, ev_b_s, ev_vnorm, ev_decay, ev_w_out, od_w_pool, od_pool_scale, final_norm):
    n_prompt, prompt_len, _ = x_prompt.shape
    n_sample, sample_len, _ = x_sample.shape
    assert DEPTH == 2 and n_sample + 1 <= COND_ROWS
    assert prompt_len == UNIT and sample_len % UNIT == 0
    units = _Units(n_prompt, n_sample, sample_len // UNIT)

    mod = _ada_call(c_ctx.reshape(1, D_MODEL), c, ada_w, ada_b)

    row = lambda v: v.reshape(1, -1)
    dec = jnp.broadcast_to(ev_decay[0].reshape(2 * HEADS, 1), (2 * HEADS, CHUNK))
    even_params = (
        row(norm_mix[0]), row(norm_mlp[0]),
        jnp.broadcast_to(ev_b_s[0][:, :, None], (A_GROUPS, CHUNK, A_GROUP_DIM)),
        row(ev_vnorm[0]), dec,
    )
    odd_params = (row(norm_mix[1]), row(norm_mlp[1]), row(final_norm), row(od_pool_scale[0]))
    even_weights = (ev_w_in, ev_w_out, mlp_w1, mlp_w2, ev_w_s)
    odd_weights = (od_w_pool, mlp_w1, mlp_w2)

    xp2 = x_prompt.reshape(n_prompt * prompt_len, D_MODEL)
    xs2 = x_sample.reshape(n_sample * sample_len, D_MODEL)
    cos2, sin2 = _rope_tables(sample_len)
    sp = _ret_state_call(xs2, mod, state_ret, (cos2, sin2), row(norm_mix[0]), ev_w_in, dec,
                         n_seq=n_sample, seq_len=sample_len, layer=0, even_index=0)
    rope = (np.concatenate([np.ones((UNIT, HEAD_DIM), np.float32), cos2]),
            np.concatenate([np.zeros((UNIT, HEAD_DIM), np.float32), sin2]))
    x_all, new_state = _even_call(xp2, xs2, mod, sp, rope, even_params, even_weights, units,
                                  layer=0, even_index=0)
    odd_units = _Units(n_prompt * prompt_len // ODD_UNIT, n_sample, sample_len // ODD_UNIT)
    y_prompt, y_sample = _odd_call(x_all, mod, odd_params, odd_weights, odd_units, unit=ODD_UNIT,
                                   short_len=prompt_len, long_len=sample_len, layer=1, odd_index=0)

    return (y_prompt.reshape(n_prompt, prompt_len, D_MODEL),
            y_sample.reshape(n_sample, sample_len, D_MODEL),
            new_state)
```
